```python
import math
import jax
import jax.numpy as jnp
from jax import lax
import numpy as np

D_MODEL = 1024
BATCH = 8
SEQ = 2048
DEPTH = 1
DEC_BATCH = 1
DEC_SEQ = 16384
PAST_LEN = 128

D_MIX = D_MODEL
D_POOL = D_MIX // 2
D_HYENA = D_MIX - D_POOL
POOL_WINDOWS = (2, 4, 8, 16)
N_POOL_GROUPS = len(POOL_WINDOWS)
POOL_GROUP = D_POOL // N_POOL_GROUPS
HYENA_ORDER = 2
SHORT_CONV = 3
POS_EMB_DIM = 33
POS_BANDS = (POS_EMB_DIM - 1) // 2
FILTER_HIDDEN = 64
N_FILTER_INNER = 2
N_FILTERS = 2 * HYENA_ORDER
DECAY_TARGET = 1e-2
FAST_DECAY_PCT = 0.3
SLOW_DECAY_PCT = 1.5
MAX_DECAY = math.log(DECAY_TARGET) / FAST_DECAY_PCT
MIN_DECAY = math.log(DECAY_TARGET) / SLOW_DECAY_PCT
D_IN = D_POOL + (HYENA_ORDER + 1) * D_HYENA
D_FF = 2816
EPS = 1e-6

kernel_name = "sandwich_macaron_pool_hyena_encoder"


def rms_norm(x, g):
    xf = x.astype(jnp.float32)
    inv = lax.rsqrt(jnp.mean(xf * xf, axis=-1, keepdims=True) + EPS)
    return (xf * inv).astype(x.dtype) * g


def swiglu_ffn(x, w_gate_up, w_down):
    gate, up = jnp.split(x @ w_gate_up, 2, axis=-1)
    return (jax.nn.silu(gate) * up) @ w_down


def multiscale_pool(u, w_map, scale):
    b, L, _ = u.shape
    uf = u.astype(jnp.float32)
    cs = jnp.concatenate([jnp.zeros((b, 1, D_POOL), jnp.float32), jnp.cumsum(uf, axis=1)], axis=1)
    pos = jnp.arange(L)
    means = []
    for gi, w in enumerate(POOL_WINDOWS):
        lo = jnp.clip(pos - w // 2, 0, L)
        hi = jnp.clip(pos + (w - w // 2), 0, L)
        c = cs[:, :, gi * POOL_GROUP:(gi + 1) * POOL_GROUP]
        cnt = (hi - lo).astype(jnp.float32)[None, :, None]
        means.append((c[:, hi] - c[:, lo]) / cnt)
    d = (jnp.concatenate(means, axis=-1) - uf).astype(u.dtype)
    d = d.reshape(b, L, N_POOL_GROUPS, POOL_GROUP)
    y = jnp.einsum('blgc,gcd->blgd', d, w_map).reshape(b, L, D_POOL)
    return y * scale


def short_conv(z, w, b):
    L = z.shape[1]
    pad = SHORT_CONV // 2
    zp = jnp.pad(z, ((0, 0), (pad, pad), (0, 0)))
    out = zp[:, 0:L] * w[0]
    for k in range(1, SHORT_CONV):
        out = out + zp[:, k:k + L] * w[k]
    return out + b


def hyena_filter_spectrum(L, w_first, b_first, w_hidden, b_hidden, w_last, freq):
    t = jnp.linspace(0.0, 1.0, L, dtype=jnp.float32)[:, None]
    bands = jnp.linspace(1e-4, POS_BANDS - 1, POS_BANDS, dtype=jnp.float32)[None, :]
    ang = bands * (2.0 * math.pi / L) * jnp.arange(L, dtype=jnp.float32)[:, None]
    z = jnp.concatenate([t, jnp.cos(ang), -jnp.sin(ang)], axis=-1)
    h = jnp.sin(freq * (z @ w_first + b_first))
    for i in range(N_FILTER_INNER):
        h = jnp.sin(freq * (h @ w_hidden[i] + b_hidden[i]))
    h = (h @ w_last).astype(jnp.float32).reshape(L, HYENA_ORDER, 2, D_HYENA)
    deltas = jnp.linspace(MIN_DECAY, MAX_DECAY, D_HYENA, dtype=jnp.float32)
    decay = jnp.exp(-t * jnp.abs(deltas))
    h = h * decay[:, None, None, :]
    h_fwd, h_bwd = h[:, :, 0], h[:, :, 1]
    k = jnp.concatenate([h_fwd, jnp.zeros((1, HYENA_ORDER, D_HYENA), jnp.float32), h_bwd[:0:-1]], axis=0)
    return jnp.fft.rfft(k, n=2 * L, axis=0)


def bidir_long_conv(v, k_spec, skip):
    L = v.shape[1]
    vf = v.astype(jnp.float32)
    y = jnp.fft.irfft(jnp.fft.rfft(vf, n=2 * L, axis=1) * k_spec[None], n=2 * L, axis=1)[:, :L]
    return (y + vf * skip.astype(jnp.float32)).astype(v.dtype)


def hyena_mixer(u, conv_w, conv_b, w_first, b_first, w_hidden, b_hidden, w_last, freq, skip):
    L = u.shape[1]
    parts = jnp.split(short_conv(u, conv_w, conv_b), HYENA_ORDER + 1, axis=-1)
    v, gates = parts[0], parts[1:]
    k_spec = hyena_filter_spectrum(L, w_first, b_first, w_hidden, b_hidden, w_last, freq)
    z = v
    for o in range(HYENA_ORDER):
        z = gates[o] * bidir_long_conv(z, k_spec[:, o], skip[o])
    return z


def encoder_layer(x, ffn1_norm_pre, ffn1_norm_post, ffn1_w_gate_up, ffn1_w_down,
                  mix_norm_pre, w_in, pool_w_map, pool_scale,
                  hyena_conv_w, hyena_conv_b, filt_w_first, filt_b_first, filt_w_hidden,
                  filt_b_hidden, filt_w_last, filt_freq, hyena_skip,
                  pool_out_norm, hyena_out_norm, w_out, mix_norm_post,
                  ffn2_norm_pre, ffn2_norm_post, ffn2_w_gate_up, ffn2_w_down):
    h = rms_norm(x, ffn1_norm_pre)
    x = x + 0.5 * rms_norm(swiglu_ffn(h, ffn1_w_gate_up, ffn1_w_down), ffn1_norm_post)
    h = rms_norm(x, mix_norm_pre)
    z = h @ w_in
    y_pool = multiscale_pool(z[..., :D_POOL], pool_w_map, pool_scale)
    y_hy = hyena_mixer(z[..., D_POOL:], hyena_conv_w, hyena_conv_b, filt_w_first, filt_b_first,
                       filt_w_hidden, filt_b_hidden, filt_w_last, filt_freq, hyena_skip)
    y = jnp.concatenate([rms_norm(y_pool, pool_out_norm), rms_norm(y_hy, hyena_out_norm)], axis=-1) @ w_out
    x = x + rms_norm(y, mix_norm_post)
    h = rms_norm(x, ffn2_norm_pre)
    x = x + 0.5 * rms_norm(swiglu_ffn(h, ffn2_w_gate_up, ffn2_w_down), ffn2_norm_post)
    return x


def setup_inputs(seed: int = 0) -> dict:
    key = jax.random.key(seed)
    ks = jax.random.split(key, 32)

    def nrm(k, shape, scale):
        return jax.random.normal(k, shape, jnp.float32) * scale

    def gain(k, shape):
        return 1.0 + 0.05 * jax.random.normal(k, shape, jnp.float32)

    return {
        "x_prompt": nrm(ks[0], (BATCH, SEQ, D_MODEL), 1.0),
        "x_sample": nrm(ks[1], (DEC_BATCH, DEC_SEQ, D_MODEL), 1.0),
        "ffn1_norm_pre": gain(ks[2], (DEPTH, D_MODEL)),
        "ffn1_norm_post": gain(ks[3], (DEPTH, D_MODEL)),
        "ffn1_w_gate_up": nrm(ks[4], (DEPTH, D_MODEL, 2 * D_FF), D_MODEL ** -0.5),
        "ffn1_w_down": nrm(ks[5], (DEPTH, D_FF, D_MODEL), D_FF ** -0.5),
        "mix_norm_pre": gain(ks[6], (DEPTH, D_MODEL)),
        "w_in": nrm(ks[7], (DEPTH, D_MODEL, D_IN), D_MODEL ** -0.5),
        "pool_w_map": nrm(ks[8], (DEPTH, N_POOL_GROUPS, POOL_GROUP, POOL_GROUP), POOL_GROUP ** -0.5),
        "pool_scale": gain(ks[9], (DEPTH, D_POOL)),
        "hyena_conv_w": nrm(ks[10], (DEPTH, SHORT_CONV, (HYENA_ORDER + 1) * D_HYENA), SHORT_CONV ** -0.5),
        "hyena_conv_b": nrm(ks[11], (DEPTH, (HYENA_ORDER + 1) * D_HYENA), 0.02),
        "filt_w_first": nrm(ks[12], (DEPTH, POS_EMB_DIM, FILTER_HIDDEN), POS_EMB_DIM ** -0.5),
        "filt_b_first": nrm(ks[13], (DEPTH, FILTER_HIDDEN), 0.1),
        "filt_w_hidden": nrm(ks[14], (DEPTH, N_FILTER_INNER, FILTER_HIDDEN, FILTER_HIDDEN), FILTER_HIDDEN ** -0.5),
        "filt_b_hidden": nrm(ks[15], (DEPTH, N_FILTER_INNER, FILTER_HIDDEN), 0.1),
        "filt_w_last": nrm(ks[16], (DEPTH, FILTER_HIDDEN, N_FILTERS * D_HYENA), FILTER_HIDDEN ** -0.5),
        "filt_freq": gain(ks[17], (DEPTH, FILTER_HIDDEN)),
        "hyena_skip": nrm(ks[18], (DEPTH, HYENA_ORDER, D_HYENA), 0.5),
        "pool_out_norm": gain(ks[19], (DEPTH, D_POOL)),
        "hyena_out_norm": gain(ks[20], (DEPTH, D_HYENA)),
        "w_out": nrm(ks[21], (DEPTH, D_MIX, D_MODEL), D_MIX ** -0.5),
        "mix_norm_post": gain(ks[22], (DEPTH, D_MODEL)),
        "ffn2_norm_pre": gain(ks[23], (DEPTH, D_MODEL)),
        "ffn2_norm_post": gain(ks[24], (DEPTH, D_MODEL)),
        "ffn2_w_gate_up": nrm(ks[25], (DEPTH, D_MODEL, 2 * D_FF), D_MODEL ** -0.5),
        "ffn2_w_down": nrm(ks[26], (DEPTH, D_FF, D_MODEL), D_FF ** -0.5),
    }


def reference(x_prompt, x_sample, ffn1_norm_pre, ffn1_norm_post, ffn1_w_gate_up, ffn1_w_down,
              mix_norm_pre, w_in, pool_w_map, pool_scale,
              hyena_conv_w, hyena_conv_b, filt_w_first, filt_b_first, filt_w_hidden,
              filt_b_hidden, filt_w_last, filt_freq, hyena_skip,
              pool_out_norm, hyena_out_norm, w_out, mix_norm_post,
              ffn2_norm_pre, ffn2_norm_post, ffn2_w_gate_up, ffn2_w_down):
    def run_trunk(x):
        for l in range(DEPTH):
            x = encoder_layer(
                x, ffn1_norm_pre[l], ffn1_norm_post[l], ffn1_w_gate_up[l], ffn1_w_down[l],
                mix_norm_pre[l], w_in[l], pool_w_map[l], pool_scale[l],
                hyena_conv_w[l], hyena_conv_b[l], filt_w_first[l], filt_b_first[l], filt_w_hidden[l],
                filt_b_hidden[l], filt_w_last[l], filt_freq[l], hyena_skip[l],
                pool_out_norm[l], hyena_out_norm[l], w_out[l], mix_norm_post[l],
                ffn2_norm_pre[l], ffn2_norm_post[l], ffn2_w_gate_up[l], ffn2_w_down[l])
        return x

    y_prompt = run_trunk(x_prompt)
    y_sample = run_trunk(x_sample)
    return (y_prompt, y_sample)
```

```python
import functools
import math

import jax
import jax.numpy as jnp
import numpy as np
from jax import lax
from jax.experimental import pallas as pl
from jax.experimental.pallas import tpu as pltpu

F32 = jnp.float32
BF16 = jnp.bfloat16
EPS = 1e-6
LANES = 128
SUBLANES = 8
POOL_WINDOWS = (2, 4, 8, 16)
HALO = 8
DECAY_TARGET = 1e-2
FAST_DECAY_PCT = 0.3
SLOW_DECAY_PCT = 1.5
MAX_DECAY = math.log(DECAY_TARGET) / FAST_DECAY_PCT
MIN_DECAY = math.log(DECAY_TARGET) / SLOW_DECAY_PCT
CHUNK = 512
HIGHEST = lax.Precision.HIGHEST


def _params(sem, vmem_mib):
    return pltpu.CompilerParams(dimension_semantics=sem, vmem_limit_bytes=vmem_mib << 20)


def _rms(x, g):
    inv = lax.rsqrt(jnp.mean(x * x, axis=-1, keepdims=True) + EPS)
    return (x * inv) * g


def _dot(a, b):
    return jnp.dot(a, b, preferred_element_type=F32)


def _ffn_kernel(x_ref, gpre_ref, gpost_ref, wgu_ref, wd_ref, o_ref, *, d_ff, chunk):
    x = x_ref[...]
    h = _rms(x, gpre_ref[...]).astype(BF16)
    acc = None
    for j in range(d_ff // chunk):
        lo = j * chunk
        gate = _dot(h, wgu_ref[:, lo:lo + chunk])
        up = _dot(h, wgu_ref[:, d_ff + lo:d_ff + lo + chunk])
        act = (gate * jax.nn.sigmoid(gate) * up).astype(BF16)
        part = _dot(act, wd_ref[lo:lo + chunk, :])
        acc = part if acc is None else acc + part
    o_ref[...] = x + 0.5 * _rms(acc, gpost_ref[...])


def _ffn(x, g_pre, g_post, wgu, wd, *, tm, chunk):
    t, d = x.shape
    d_ff = wd.shape[0]
    const = lambda i: (0, 0)
    return pl.pallas_call(
        functools.partial(_ffn_kernel, d_ff=d_ff, chunk=chunk),
        grid=(t // tm,),
        in_specs=[
            pl.BlockSpec((tm, d), lambda i: (i, 0)),
            pl.BlockSpec((1, d), const),
            pl.BlockSpec((1, d), const),
            pl.BlockSpec((d, 2 * d_ff), const, pipeline_mode=pl.Buffered(1)),
            pl.BlockSpec((d_ff, d), const, pipeline_mode=pl.Buffered(1)),
        ],
        out_specs=pl.BlockSpec((tm, d), lambda i: (i, 0)),
        out_shape=jax.ShapeDtypeStruct((t, d), F32),
        compiler_params=_params(("parallel",), 52),
        name="ffn",
    )(x, g_pre, g_post, wgu, wd)


def _mix_in_kernel(x_ref, g_ref, w_ref, z_ref):
    h = _rms(x_ref[...], g_ref[...]).astype(BF16)
    z_ref[...] = _dot(h, w_ref[...])


def _mix_in(x, g, w, *, tm):
    t, d = x.shape
    d_in = w.shape[1]
    const = lambda i: (0, 0)
    return pl.pallas_call(
        _mix_in_kernel,
        grid=(t // tm,),
        in_specs=[
            pl.BlockSpec((tm, d), lambda i: (i, 0)),
            pl.BlockSpec((1, d), const),
            pl.BlockSpec((d, d_in), const, pipeline_mode=pl.Buffered(1)),
        ],
        out_specs=pl.BlockSpec((tm, d_in), lambda i: (i, 0)),
        out_shape=jax.ShapeDtypeStruct((t, d_in), F32),
        compiler_params=_params(("parallel",), 40),
        name="mix_in",
    )(x, g, w)


def _with_halo(prev_ref, main_ref, next_ref, i, n_tiles):
    prev = jnp.where(i > 0, prev_ref[0], 0.0)
    nxt = jnp.where(i < n_tiles - 1, next_ref[0], 0.0)
    return jnp.concatenate([prev, main_ref[0], nxt], axis=0)


def _pool_kernel(prev_ref, main_ref, next_ref, wmap_ref, scale_ref, o_ref, *, seq_len, n_tiles):
    i = pl.program_id(1)
    c = pl.program_id(2)
    rows = main_ref.shape[1]
    ext = _with_halo(prev_ref, main_ref, next_ref, i, n_tiles)
    n_ext = rows + 2 * HALO
    s2 = ext + pltpu.roll(ext, 1, 0)
    s4 = pltpu.roll(s2, 1, 0) + pltpu.roll(s2, n_ext - 1, 0)
    s8 = pltpu.roll(s4, 2, 0) + pltpu.roll(s4, n_ext - 2, 0)
    s16 = pltpu.roll(s8, 4, 0) + pltpu.roll(s8, n_ext - 4, 0)
    ssum = jnp.where(c == 0, s2, jnp.where(c == 1, s4, jnp.where(c == 2, s8, s16)))
    ssum = ssum[HALO:HALO + rows]
    u = main_ref[0]
    half = jnp.left_shift(1, c)
    pos = i * rows + lax.broadcasted_iota(jnp.int32, u.shape, 0)
    lo = jnp.clip(pos - half, 0, seq_len)
    hi = jnp.clip(pos + half, 0, seq_len)
    cnt = (hi - lo).astype(F32)
    d = ssum / cnt - u
    y = _dot(d.astype(BF16), wmap_ref[0])
    o_ref[0] = y * scale_ref[...]


def _halo_specs(rows, seq_len, col_of):
    blocks_per_tile = rows // HALO
    last = seq_len // HALO - 1
    prev = pl.BlockSpec((1, HALO, LANES),
                        lambda b, i, c: (b, jnp.maximum(i * blocks_per_tile - 1, 0), col_of(c)))
    main = pl.BlockSpec((1, rows, LANES), lambda b, i, c: (b, i, col_of(c)))
    nxt = pl.BlockSpec((1, HALO, LANES),
                       lambda b, i, c: (b, jnp.minimum((i + 1) * blocks_per_tile, last), col_of(c)))
    return [prev, main, nxt]


def _pool(z, wmap, scale, *, rows):
    b, seq_len, _ = z.shape
    n_groups = wmap.shape[0]
    n_tiles = seq_len // rows
    return pl.pallas_call(
        functools.partial(_pool_kernel, seq_len=seq_len, n_tiles=n_tiles),
        grid=(b, n_tiles, n_groups),
        in_specs=_halo_specs(rows, seq_len, lambda c: c) + [
            pl.BlockSpec((1, LANES, LANES), lambda b, i, c: (c, 0, 0)),
            pl.BlockSpec((1, LANES), lambda b, i, c: (0, c)),
        ],
        out_specs=pl.BlockSpec((1, rows, LANES), lambda b, i, c: (b, i, c)),
        out_shape=jax.ShapeDtypeStruct((b, seq_len, n_groups * LANES), F32),
        compiler_params=_params(("parallel", "parallel", "parallel"), 40),
        name="pool",
    )(z, z, z, wmap, scale)


def _sconv_kernel(prev_ref, main_ref, next_ref, w_ref, b_ref, o_ref, *, n_tiles):
    i = pl.program_id(1)
    rows = main_ref.shape[1]
    ext = _with_halo(prev_ref, main_ref, next_ref, i, n_tiles)
    n_ext = rows + 2 * HALO
    before = pltpu.roll(ext, 1, 0)[HALO:HALO + rows]
    after = pltpu.roll(ext, n_ext - 1, 0)[HALO:HALO + rows]
    w = w_ref[...]
    out = before * w[0:1] + main_ref[0] * w[1:2] + after * w[2:3]
    o_ref[0, 0] = out + b_ref[...]


def _sconv(z, w, bias, *, rows, col0, n_parts):
    b, seq_len, _ = z.shape
    n_tiles = seq_len // rows
    blocks_per_part = (w.shape[1] // n_parts) // LANES
    cb0 = col0 // LANES
    return pl.pallas_call(
        functools.partial(_sconv_kernel, n_tiles=n_tiles),
        grid=(b, n_tiles, n_parts * blocks_per_part),
        in_specs=_halo_specs(rows, seq_len, lambda c: c + cb0) + [
            pl.BlockSpec((w.shape[0], LANES), lambda b, i, c: (0, c)),
            pl.BlockSpec((1, LANES), lambda b, i, c: (0, c)),
        ],
        out_specs=pl.BlockSpec((1, 1, rows, LANES),
                               lambda b, i, c: (c // blocks_per_part, b, i, c % blocks_per_part)),
        out_shape=jax.ShapeDtypeStruct((n_parts, b, seq_len, blocks_per_part * LANES), F32),
        compiler_params=_params(("parallel", "parallel", "parallel"), 40),
        name="sconv",
    )(z, z, z, w, bias)


def _filt_gen_kernel(bands_ref, wt_ref, wc_ref, ws_ref, b1_ref, wh_ref, bh_ref, freq_ref,
                     wl_ref, delta_ref, o_ref, *, seq_len, d_ch):
    rows = o_ref.shape[0]
    base = pl.program_id(0) * rows
    m_lane = (base + lax.broadcasted_iota(jnp.int32, (1, rows), 1)).astype(F32)
    t_lane = m_lane / (seq_len - 1.0)
    ang = (bands_ref[...] * (2.0 * math.pi / seq_len)) * m_lane
    freq = freq_ref[...]
    pre = (jnp.dot(wc_ref[...], jnp.cos(ang), precision=HIGHEST, preferred_element_type=F32)
           + jnp.dot(ws_ref[...], -jnp.sin(ang), precision=HIGHEST, preferred_element_type=F32)
           + wt_ref[...] * t_lane + b1_ref[...])
    h = jnp.sin(freq * pre)
    for layer in range(wh_ref.shape[0]):
        pre = jnp.dot(wh_ref[layer], h, precision=HIGHEST, preferred_element_type=F32) + bh_ref[layer]
        h = jnp.sin(freq * pre)
    out = jnp.dot(h.T, wl_ref[...], precision=HIGHEST, preferred_element_type=F32)
    m_row = base + lax.broadcasted_iota(jnp.int32, (rows, d_ch), 0)
    t_row = m_row.astype(F32) / (seq_len - 1.0)
    decay = jnp.exp(-t_row * jnp.abs(delta_ref[...]))
    decay_bwd = jnp.where(m_row == 0, 0.0, decay)
    n_cols = out.shape[1] // d_ch
    for q in range(n_cols):
        dq = decay_bwd if q % 2 == 1 else decay
        o_ref[:, q * d_ch:(q + 1) * d_ch] = out[:, q * d_ch:(q + 1) * d_ch] * dq


def _filt_gen(seq_len, w_first, b_first, w_hidden, b_hidden, w_last, freq, *, rows):
    pos_bands = (w_first.shape[0] - 1) // 2
    hidden = w_first.shape[1]
    n_cols = w_last.shape[1]
    d_ch = n_cols // 4
    bands = jnp.linspace(1e-4, pos_bands - 1, pos_bands, dtype=F32).reshape(pos_bands, 1)
    deltas = jnp.linspace(MIN_DECAY, MAX_DECAY, d_ch, dtype=F32).reshape(1, d_ch)
    w1t = w_first.T
    args = (bands, w1t[:, 0:1], w1t[:, 1:1 + pos_bands], w1t[:, 1 + pos_bands:],
            b_first.reshape(hidden, 1), jnp.swapaxes(w_hidden, 1, 2),
            b_hidden.reshape(b_hidden.shape[0], hidden, 1), freq.reshape(hidden, 1),
            w_last, deltas)
    full = lambda a: pl.BlockSpec(a.shape, lambda i, _n=a.ndim: (0,) * _n)
    return pl.pallas_call(
        functools.partial(_filt_gen_kernel, seq_len=seq_len, d_ch=d_ch),
        grid=(seq_len // rows,),
        in_specs=[full(a) for a in args],
        out_specs=pl.BlockSpec((rows, n_cols), lambda i: (i, 0)),
        out_shape=jax.ShapeDtypeStruct((seq_len, n_cols), F32),
        compiler_params=_params(("parallel",), 40),
        name="filt_gen",
    )(*args)


class _Plan:
    def __init__(self, seq_len, n1, n2):
        assert n1 * n2 == 2 * seq_len and n1 % (2 * SUBLANES) == 0 and n2 % SUBLANES == 0
        self.seq_len, self.n1, self.n2 = seq_len, n1, n2
        n = n1 * n2
        self.kh = kh = n1 // 2 + 1
        self.khp = khp = -(-kh // SUBLANES) * SUBLANES
        k1 = np.arange(kh, dtype=np.float64)[:, None]
        m1 = np.arange(n1 // 2, dtype=np.float64)[None, :]
        th1 = 2.0 * np.pi * k1 * m1 / n1
        f1c = np.zeros((khp, n1 // 2)); f1c[:kh] = np.cos(th1)
        f1s = np.zeros((khp, n1 // 2)); f1s[:kh] = -np.sin(th1)
        self.f1c, self.f1s = jnp.asarray(f1c, F32), jnp.asarray(f1s, F32)
        m2 = np.arange(n2, dtype=np.float64)[None, :]
        tht = 2.0 * np.pi * k1 * m2 / n
        twr = np.zeros((khp, n2)); twr[:kh] = np.cos(tht)
        twi = np.zeros((khp, n2)); twi[:kh] = -np.sin(tht)
        rep = lambda a: np.repeat(a[:, :, None], LANES, axis=2)
        self.twr3, self.twi3 = jnp.asarray(rep(twr), F32), jnp.asarray(rep(twi), F32)
        k2 = np.arange(n2, dtype=np.float64)[:, None]
        th2 = 2.0 * np.pi * k2 * m2 / n2
        f2r, f2i = np.cos(th2), -np.sin(th2)
        self.ffa = jnp.asarray(np.concatenate([f2r, f2i], 0), F32)
        self.ffb = jnp.asarray(np.concatenate([-f2i, f2r], 0), F32)
        self.fia = jnp.asarray(np.concatenate([f2r, -f2i], 0), F32)
        self.fib = jnp.asarray(np.concatenate([f2i, f2r], 0), F32)
        wgt = np.where((k1 == 0) | (k1 == n1 // 2), 1.0, 2.0) / n
        gr = np.zeros((n1 // 2, khp)); gr[:, :kh] = (wgt * np.cos(th1)).T
        gi = np.zeros((n1 // 2, khp)); gi[:, :kh] = (-wgt * np.sin(th1)).T
        self.gr, self.gi = jnp.asarray(gr, F32), jnp.asarray(gi, F32)


def _dft_fwd_kernel(x_ref, f1c_ref, f1s_ref, twr_ref, twi_ref, re_ref, im_ref):
    g, ch = x_ref.shape[-2:]
    reps = CHUNK // LANES
    f1c, f1s = f1c_ref[...].astype(BF16), f1s_ref[...].astype(BF16)
    for j in range(g):
        twr = jnp.tile(twr_ref[:, j, :], (1, reps))
        twi = jnp.tile(twi_ref[:, j, :], (1, reps))
        for lo in range(0, ch, CHUNK):
            xb = x_ref[0, 0, :, j, lo:lo + CHUNK].astype(BF16)
            pr = _dot(f1c, xb)
            pi = _dot(f1s, xb)
            re_ref[0, :, j, lo:lo + CHUNK] = pr * twr - pi * twi
            im_ref[0, :, j, lo:lo + CHUNK] = pr * twi + pi * twr


def _dft_fwd(plan, x4, part, *, g):
    _, b, seq_len, ch = x4.shape
    n1h, n2, khp = plan.n1 // 2, plan.n2, plan.khp
    xv = x4.reshape(x4.shape[0], b, n1h, n2, ch)
    const = lambda j, bb: (0, 0)
    out = jax.ShapeDtypeStruct((b, khp, n2, ch), F32)
    return pl.pallas_call(
        _dft_fwd_kernel,
        grid=(n2 // g, b),
        in_specs=[
            pl.BlockSpec((1, 1, n1h, g, ch), lambda j, bb: (part, bb, 0, j, 0)),
            pl.BlockSpec((khp, n1h), const),
            pl.BlockSpec((khp, n1h), const),
            pl.BlockSpec((khp, g, LANES), lambda j, bb: (0, j, 0)),
            pl.BlockSpec((khp, g, LANES), lambda j, bb: (0, j, 0)),
        ],
        out_specs=[pl.BlockSpec((1, khp, g, ch), lambda j, bb: (bb, 0, j, 0))] * 2,
        out_shape=[out, out],
        compiler_params=_params(("parallel", "parallel"), 48),
        name="dft_fwd",
    )(xv, plan.f1c, plan.f1s, plan.twr3, plan.twi3)


def _filt_spec_kernel(re_ref, im_ref, ffa_ref, ffb_ref, kr_ref, ki_ref, *, d_ch):
    n2 = re_ref.shape[2]
    x = (_dot(ffa_ref[...].astype(BF16), re_ref[0, 0].astype(BF16))
         + _dot(ffb_ref[...].astype(BF16), im_ref[0, 0].astype(BF16)))
    xr, xi = x[:n2], x[n2:]
    for o in range(kr_ref.shape[2] // d_ch):
        f = slice(2 * o * d_ch, (2 * o + 1) * d_ch)
        bk = slice((2 * o + 1) * d_ch, (2 * o + 2) * d_ch)
        kr_ref[0, :, o * d_ch:(o + 1) * d_ch] = xr[:, f] + xr[:, bk]
        ki_ref[0, :, o * d_ch:(o + 1) * d_ch] = xi[:, f] - xi[:, bk]


def _filt_spec(plan, fre, fim, d_ch):
    n2, kh = plan.n2, plan.kh
    cols = fre.shape[-1]
    const = lambda k: (0, 0)
    out = jax.ShapeDtypeStruct((kh, n2, cols // 2), F32)
    return pl.pallas_call(
        functools.partial(_filt_spec_kernel, d_ch=d_ch),
        grid=(kh,),
        in_specs=[
            pl.BlockSpec((1, 1, n2, cols), lambda k: (0, k, 0, 0)),
            pl.BlockSpec((1, 1, n2, cols), lambda k: (0, k, 0, 0)),
            pl.BlockSpec((2 * n2, n2), const),
            pl.BlockSpec((2 * n2, n2), const),
        ],
        out_specs=[pl.BlockSpec((1, n2, cols // 2), lambda k: (k, 0, 0))] * 2,
        out_shape=[out, out],
        compiler_params=_params(("parallel",), 40),
        name="filt_spec",
    )(fre, fim, plan.ffa, plan.ffb)


def _conv_mid_kernel(re_ref, im_ref, kr_ref, ki_ref, twr_ref, twi_ref,
                     ffa_ref, ffb_ref, fia_ref, fib_ref, ore_ref, oim_ref, *, kh):
    k1 = pl.program_id(0)
    nb, _, n2, ch = re_ref.shape
    reps = ch // LANES

    @pl.when(k1 < kh)
    def _():
        kr, ki = kr_ref[0], ki_ref[0]
        twr = jnp.tile(twr_ref[0], (1, reps))
        twi = jnp.tile(twi_ref[0], (1, reps))
        ffa, ffb = ffa_ref[...].astype(BF16), ffb_ref[...].astype(BF16)
        fia, fib = fia_ref[...].astype(BF16), fib_ref[...].astype(BF16)

        def body(b, carry):
            x = _dot(ffa, re_ref[b, 0].astype(BF16)) + _dot(ffb, im_ref[b, 0].astype(BF16))
            xr, xi = x[:n2], x[n2:]
            yr = (xr * kr - xi * ki).astype(BF16)
            yi = (xr * ki + xi * kr).astype(BF16)
            y = _dot(fia, yr) + _dot(fib, yi)
            br, bi = y[:n2], y[n2:]
            ore_ref[b, 0] = br * twr + bi * twi
            oim_ref[b, 0] = bi * twr - br * twi
            return carry

        lax.fori_loop(0, nb, body, 0)

    @pl.when(k1 >= kh)
    def _():
        ore_ref[...] = jnp.zeros(ore_ref.shape, ore_ref.dtype)
        oim_ref[...] = jnp.zeros(oim_ref.shape, oim_ref.dtype)


def _conv_mid(plan, are, aim, kr, ki, order):
    b, khp, n2, ch = are.shape
    kh = plan.kh
    const = lambda k: (0, 0)
    kmap = lambda k: (jnp.minimum(k, kh - 1), 0, order)
    out = jax.ShapeDtypeStruct((b, khp, n2, ch), F32)
    return pl.pallas_call(
        functools.partial(_conv_mid_kernel, kh=kh),
        grid=(khp,),
        in_specs=[
            pl.BlockSpec((b, 1, n2, ch), lambda k: (0, k, 0, 0)),
            pl.BlockSpec((b, 1, n2, ch), lambda k: (0, k, 0, 0)),
            pl.BlockSpec((1, n2, ch), kmap),
            pl.BlockSpec((1, n2, ch), kmap),
            pl.BlockSpec((1, n2, LANES), lambda k: (k, 0, 0)),
            pl.BlockSpec((1, n2, LANES), lambda k: (k, 0, 0)),
            pl.BlockSpec((2 * n2, n2), const),
            pl.BlockSpec((2 * n2, n2), const),
            pl.BlockSpec((2 * n2, n2), const),
            pl.BlockSpec((2 * n2, n2), const),
        ],
        out_specs=[pl.BlockSpec((b, 1, n2, ch), lambda k: (0, k, 0, 0))] * 2,
        out_shape=[out, out],
        compiler_params=_params(("parallel",), 40),
        name="conv_mid",
    )(are, aim, kr, ki, plan.twr3, plan.twi3, plan.ffa, plan.ffb, plan.fia, plan.fib)


def _dft_inv_kernel(bre_ref, bim_ref, x_ref, gate_ref, skip_ref, gr_ref, gi_ref, o_ref):
    g = o_ref.shape[-2]
    skip = skip_ref[0]
    gr, gi = gr_ref[...].astype(BF16), gi_ref[...].astype(BF16)
    for j in range(g):
        y = (_dot(gr, bre_ref[0, :, j, :].astype(BF16))
             + _dot(gi, bim_ref[0, :, j, :].astype(BF16)))
        x = x_ref[0, 0, :, j, :]
        o_ref[0, 0, :, j, :] = gate_ref[0, 0, :, j, :] * (y + x * skip)


def _dft_inv(plan, bre, bim, x4, x_part, gate4, gate_part, skip, order, *, g):
    b, seq_len, ch = x4.shape[1:]
    n1h, n2, khp = plan.n1 // 2, plan.n2, plan.khp
    xv = x4.reshape(x4.shape[0], b, n1h, n2, ch)
    gv = gate4.reshape(gate4.shape[0], b, n1h, n2, ch)
    const = lambda j, bb: (0, 0)
    out = pl.pallas_call(
        _dft_inv_kernel,
        grid=(n2 // g, b),
        in_specs=[
            pl.BlockSpec((1, khp, g, ch), lambda j, bb: (bb, 0, j, 0)),
            pl.BlockSpec((1, khp, g, ch), lambda j, bb: (bb, 0, j, 0)),
            pl.BlockSpec((1, 1, n1h, g, ch), lambda j, bb: (x_part, bb, 0, j, 0)),
            pl.BlockSpec((1, 1, n1h, g, ch), lambda j, bb: (gate_part, bb, 0, j, 0)),
            pl.BlockSpec((1, 1, ch), lambda j, bb: (order, 0, 0)),
            pl.BlockSpec((n1h, khp), const),
            pl.BlockSpec((n1h, khp), const),
        ],
        out_specs=pl.BlockSpec((1, 1, n1h, g, ch), lambda j, bb: (0, bb, 0, j, 0)),
        out_shape=jax.ShapeDtypeStruct((1, b, n1h, n2, ch), F32),
        compiler_params=_params(("parallel", "parallel"), 48),
        name="dft_inv",
    )(bre, bim, xv, gv, skip, plan.gr, plan.gi)
    return out.reshape(1, b, seq_len, ch)


def _mix_out_kernel(x_ref, yp_ref, yh_ref, gp_ref, gh_ref, w_ref, gpost_ref, o_ref):
    d_pool = yp_ref.shape[1]
    yp = _rms(yp_ref[...], gp_ref[...]).astype(BF16)
    yh = _rms(yh_ref[...], gh_ref[...]).astype(BF16)
    y = _dot(yp, w_ref[:d_pool, :]) + _dot(yh, w_ref[d_pool:, :])
    o_ref[...] = x_ref[...] + _rms(y, gpost_ref[...])


def _mix_out(x, yp, yh, gp, gh, w, gpost, *, tm):
    t, d = x.shape
    dp, dh = yp.shape[1], yh.shape[1]
    const = lambda i: (0, 0)
    row = lambda i: (i, 0)
    return pl.pallas_call(
        _mix_out_kernel,
        grid=(t // tm,),
        in_specs=[
            pl.BlockSpec((tm, d), row),
            pl.BlockSpec((tm, dp), row),
            pl.BlockSpec((tm, dh), row),
            pl.BlockSpec((1, dp), const),
            pl.BlockSpec((1, dh), const),
            pl.BlockSpec((dp + dh, d), const, pipeline_mode=pl.Buffered(1)),
            pl.BlockSpec((1, d), const),
        ],
        out_specs=pl.BlockSpec((tm, d), row),
        out_shape=jax.ShapeDtypeStruct((t, d), F32),
        compiler_params=_params(("parallel",), 40),
        name="mix_out",
    )(x, yp, yh, gp, gh, w, gpost)


def _dft_shape(seq_len):
    n2 = 256 if seq_len >= 8192 else 64
    return 2 * seq_len // n2, n2


def _block_cols(n2, ch, rows, target_bytes=1 << 20):
    g = max(SUBLANES, target_bytes // (rows * ch * 4) // SUBLANES * SUBLANES)
    while n2 % g:
        g -= SUBLANES
    return g


def _hyena(plan, uc, filt, skip):
    _, b, seq_len, ch = uc.shape
    n_orders = skip.shape[0]
    g_f = _block_cols(plan.n2, filt.shape[1], plan.n1 // 2)
    fre, fim = _dft_fwd(plan, filt.reshape(1, 1, seq_len, filt.shape[1]), 0, g=g_f)
    kr, ki = _filt_spec(plan, fre, fim, ch)
    g = _block_cols(plan.n2, ch, plan.n1 // 2)
    skip3 = skip.reshape(n_orders, 1, ch)
    cur, cur_part = uc, 0
    for order in range(n_orders):
        are, aim = _dft_fwd(plan, cur, cur_part, g=g)
        bre, bim = _conv_mid(plan, are, aim, kr, ki, order)
        cur = _dft_inv(plan, bre, bim, cur, cur_part, uc, order + 1, skip3, order, g=g)
        cur_part = 0
    return cur[0]


def _trunk(x3, p, plan, filt):
    b, seq_len, d = x3.shape
    t = b * seq_len
    x = x3.reshape(t, d)
    x = _ffn(x, p["g1pre"], p["g1post"], p["w1gu"], p["w1d"], tm=512, chunk=p["ff_chunk"])
    z = _mix_in(x, p["gmix"], p["w_in"], tm=512).reshape(b, seq_len, -1)
    d_pool = p["pool_w"].shape[0] * LANES
    rows = min(seq_len, 2048)
    y_pool = _pool(z, p["pool_w"], p["pool_scale"], rows=rows)
    uc = _sconv(z, p["conv_w"], p["conv_b"], rows=rows, col0=d_pool, n_parts=3)
    y_hy = _hyena(plan, uc, filt, p["skip"])
    x = _mix_out(x, y_pool.reshape(t, -1), y_hy.reshape(t, -1), p["gpool"], p["ghy"],
                 p["w_out"], p["gmixpost"], tm=512)
    x = _ffn(x, p["g2pre"], p["g2post"], p["w2gu"], p["w2d"], tm=512, chunk=p["ff_chunk"])
    return x.reshape(b, seq_len, d)


def kernel(x_prompt, x_sample, ffn1_norm_pre, ffn1_norm_post, ffn1_w_gate_up, ffn1_w_down, mix_norm_pre, w_in, pool_w_map, pool_scale, hyena_conv_w, hyena_conv_b, filt_w_first, filt_b_first, filt_w_hidden, filt_b_hidden, filt_w_last, filt_freq, hyena_skip, pool_out_norm, hyena_out_norm, w_out, mix_norm_post, ffn2_norm_pre, ffn2_norm_post, ffn2_w_gate_up, ffn2_w_down):
    assert ffn1_norm_pre.shape[0] == 1, "single-layer trunk"
    row = lambda a: a[0].reshape(1, -1)
    d_ff = ffn1_w_down.shape[1]
    p = dict(
        g1pre=row(ffn1_norm_pre), g1post=row(ffn1_norm_post),
        w1gu=ffn1_w_gate_up[0].astype(BF16), w1d=ffn1_w_down[0].astype(BF16),
        gmix=row(mix_norm_pre), w_in=w_in[0].astype(BF16),
        pool_w=pool_w_map[0].astype(BF16), pool_scale=row(pool_scale),
        conv_w=hyena_conv_w[0], conv_b=row(hyena_conv_b),
        skip=hyena_skip[0], gpool=row(pool_out_norm), ghy=row(hyena_out_norm),
        w_out=w_out[0].astype(BF16), gmixpost=row(mix_norm_post),
        g2pre=row(ffn2_norm_pre), g2post=row(ffn2_norm_post),
        w2gu=ffn2_w_gate_up[0].astype(BF16), w2d=ffn2_w_down[0].astype(BF16),
        ff_chunk=d_ff // 2 if (d_ff // 2) % LANES == 0 else d_ff,
    )
    outs = []
    for x3 in (x_prompt, x_sample):
        seq_len = x3.shape[1]
        plan = _Plan(seq_len, *_dft_shape(seq_len))
        filt = _filt_gen(seq_len, filt_w_first[0], filt_b_first[0], filt_w_hidden[0],
                         filt_b_hidden[0], filt_w_last[0], filt_freq[0], rows=min(seq_len, 512))
        outs.append(_trunk(x3, p, plan, filt))
    return tuple(outs)
```

```python
import functools
import math

import jax
import jax.numpy as jnp
import numpy as np
from jax import lax
from jax.experimental import pallas as pl
from jax.experimental.pallas import tpu as pltpu

F32 = jnp.float32
BF16 = jnp.bfloat16
EPS = 1e-6
LANES = 128
SUBLANES = 8
POOL_WINDOWS = (2, 4, 8, 16)
HALO = 8
DECAY_TARGET = 1e-2
FAST_DECAY_PCT = 0.3
SLOW_DECAY_PCT = 1.5
MAX_DECAY = math.log(DECAY_TARGET) / FAST_DECAY_PCT
MIN_DECAY = math.log(DECAY_TARGET) / SLOW_DECAY_PCT
HIGHEST = lax.Precision.HIGHEST


def _params(sem, vmem_mib):
    return pltpu.CompilerParams(dimension_semantics=sem, vmem_limit_bytes=vmem_mib << 20)


def _rms(x, g):
    inv = lax.rsqrt(jnp.mean(x * x, axis=-1, keepdims=True) + EPS)
    return (x * inv) * g


def _dot(a, b):
    return jnp.dot(a, b, preferred_element_type=F32)


def _ffn_kernel(x_ref, gpre_ref, gpost_ref, wgu_ref, wd_ref, o_ref, *, d_ff, chunk):
    x = x_ref[...]
    h = _rms(x, gpre_ref[...]).astype(BF16)
    acc = None
    for j in range(d_ff // chunk):
        lo = j * chunk
        gate = _dot(h, wgu_ref[:, lo:lo + chunk])
        up = _dot(h, wgu_ref[:, d_ff + lo:d_ff + lo + chunk])
        act = (gate * jax.nn.sigmoid(gate) * up).astype(BF16)
        part = _dot(act, wd_ref[lo:lo + chunk, :])
        acc = part if acc is None else acc + part
    o_ref[...] = x + 0.5 * _rms(acc, gpost_ref[...])


def _ffn(x, g_pre, g_post, wgu, wd, *, tm, chunk):
    t, d = x.shape
    d_ff = wd.shape[0]
    const = lambda i: (0, 0)
    return pl.pallas_call(
        functools.partial(_ffn_kernel, d_ff=d_ff, chunk=chunk),
        grid=(t // tm,),
        in_specs=[
            pl.BlockSpec((tm, d), lambda i: (i, 0)),
            pl.BlockSpec((1, d), const),
            pl.BlockSpec((1, d), const),
            pl.BlockSpec((d, 2 * d_ff), const, pipeline_mode=pl.Buffered(1)),
            pl.BlockSpec((d_ff, d), const, pipeline_mode=pl.Buffered(1)),
        ],
        out_specs=pl.BlockSpec((tm, d), lambda i: (i, 0)),
        out_shape=jax.ShapeDtypeStruct((t, d), F32),
        compiler_params=_params(("parallel",), 52),
        name="ffn",
    )(x, g_pre, g_post, wgu, wd)


def _mix_in_kernel(x_ref, g_ref, w_ref, z_ref):
    h = _rms(x_ref[...], g_ref[...]).astype(BF16)
    z_ref[...] = _dot(h, w_ref[...])


def _mix_in(x, g, w, *, tm):
    t, d = x.shape
    d_in = w.shape[1]
    const = lambda i: (0, 0)
    return pl.pallas_call(
        _mix_in_kernel,
        grid=(t // tm,),
        in_specs=[
            pl.BlockSpec((tm, d), lambda i: (i, 0)),
            pl.BlockSpec((1, d), const),
            pl.BlockSpec((d, d_in), const, pipeline_mode=pl.Buffered(1)),
        ],
        out_specs=pl.BlockSpec((tm, d_in), lambda i: (i, 0)),
        out_shape=jax.ShapeDtypeStruct((t, d_in), F32),
        compiler_params=_params(("parallel",), 40),
        name="mix_in",
    )(x, g, w)


def _with_halo(prev_ref, main_ref, next_ref, i, n_tiles):
    prev = jnp.where(i > 0, prev_ref[0], 0.0)
    nxt = jnp.where(i < n_tiles - 1, next_ref[0], 0.0)
    return jnp.concatenate([prev, main_ref[0], nxt], axis=0)


def _pool_kernel(prev_ref, main_ref, next_ref, wmap_ref, scale_ref, o_ref, *, seq_len, n_tiles):
    i = pl.program_id(1)
    c = pl.program_id(2)
    rows = main_ref.shape[1]
    ext = _with_halo(prev_ref, main_ref, next_ref, i, n_tiles)
    n_ext = rows + 2 * HALO
    s2 = ext + pltpu.roll(ext, 1, 0)
    s4 = pltpu.roll(s2, 1, 0) + pltpu.roll(s2, n_ext - 1, 0)
    s8 = pltpu.roll(s4, 2, 0) + pltpu.roll(s4, n_ext - 2, 0)
    s16 = pltpu.roll(s8, 4, 0) + pltpu.roll(s8, n_ext - 4, 0)
    ssum = jnp.where(c == 0, s2, jnp.where(c == 1, s4, jnp.where(c == 2, s8, s16)))
    ssum = ssum[HALO:HALO + rows]
    u = main_ref[0]
    half = jnp.left_shift(1, c)
    pos = i * rows + lax.broadcasted_iota(jnp.int32, u.shape, 0)
    lo = jnp.clip(pos - half, 0, seq_len)
    hi = jnp.clip(pos + half, 0, seq_len)
    cnt = (hi - lo).astype(F32)
    d = ssum / cnt - u
    y = _dot(d.astype(BF16), wmap_ref[0])
    o_ref[0] = y * scale_ref[...]


def _halo_specs(rows, seq_len, col_of):
    blocks_per_tile = rows // HALO
    last = seq_len // HALO - 1
    prev = pl.BlockSpec((1, HALO, LANES),
                        lambda b, i, c: (b, jnp.maximum(i * blocks_per_tile - 1, 0), col_of(c)))
    main = pl.BlockSpec((1, rows, LANES), lambda b, i, c: (b, i, col_of(c)))
    nxt = pl.BlockSpec((1, HALO, LANES),
                       lambda b, i, c: (b, jnp.minimum((i + 1) * blocks_per_tile, last), col_of(c)))
    return [prev, main, nxt]


def _pool(z, wmap, scale, *, rows):
    b, seq_len, _ = z.shape
    n_groups = wmap.shape[0]
    n_tiles = seq_len // rows
    return pl.pallas_call(
        functools.partial(_pool_kernel, seq_len=seq_len, n_tiles=n_tiles),
        grid=(b, n_tiles, n_groups),
        in_specs=_halo_specs(rows, seq_len, lambda c: c) + [
            pl.BlockSpec((1, LANES, LANES), lambda b, i, c: (c, 0, 0)),
            pl.BlockSpec((1, LANES), lambda b, i, c: (0, c)),
        ],
        out_specs=pl.BlockSpec((1, rows, LANES), lambda b, i, c: (b, i, c)),
        out_shape=jax.ShapeDtypeStruct((b, seq_len, n_groups * LANES), F32),
        compiler_params=_params(("parallel", "parallel", "parallel"), 40),
        name="pool",
    )(z, z, z, wmap, scale)


def _sconv_kernel(prev_ref, main_ref, next_ref, w_ref, b_ref, o_ref, *, n_tiles):
    i = pl.program_id(1)
    rows = main_ref.shape[1]
    ext = _with_halo(prev_ref, main_ref, next_ref, i, n_tiles)
    n_ext = rows + 2 * HALO
    before = pltpu.roll(ext, 1, 0)[HALO:HALO + rows]
    after = pltpu.roll(ext, n_ext - 1, 0)[HALO:HALO + rows]
    w = w_ref[...]
    out = before * w[0:1] + main_ref[0] * w[1:2] + after * w[2:3]
    o_ref[0, 0] = out + b_ref[...]


def _sconv(z, w, bias, *, rows, col0, n_parts):
    b, seq_len, _ = z.shape
    n_tiles = seq_len // rows
    blocks_per_part = (w.shape[1] // n_parts) // LANES
    cb0 = col0 // LANES
    return pl.pallas_call(
        functools.partial(_sconv_kernel, n_tiles=n_tiles),
        grid=(b, n_tiles, n_parts * blocks_per_part),
        in_specs=_halo_specs(rows, seq_len, lambda c: c + cb0) + [
            pl.BlockSpec((w.shape[0], LANES), lambda b, i, c: (0, c)),
            pl.BlockSpec((1, LANES), lambda b, i, c: (0, c)),
        ],
        out_specs=pl.BlockSpec((1, 1, rows, LANES),
                               lambda b, i, c: (c // blocks_per_part, b, i, c % blocks_per_part)),
        out_shape=jax.ShapeDtypeStruct((n_parts, b, seq_len, blocks_per_part * LANES), F32),
        compiler_params=_params(("parallel", "parallel", "parallel"), 40),
        name="sconv",
    )(z, z, z, w, bias)


def _filt_gen_kernel(bands_ref, wt_ref, wc_ref, ws_ref, b1_ref, wh_ref, bh_ref, freq_ref,
                     wl_ref, delta_ref, o_ref, *, seq_len, d_ch):
    rows = o_ref.shape[0]
    base = pl.program_id(0) * rows
    m_lane = (base + lax.broadcasted_iota(jnp.int32, (1, rows), 1)).astype(F32)
    t_lane = m_lane / (seq_len - 1.0)
    ang = (bands_ref[...] * (2.0 * math.pi / seq_len)) * m_lane
    freq = freq_ref[...]
    pre = (jnp.dot(wc_ref[...], jnp.cos(ang), precision=HIGHEST, preferred_element_type=F32)
           + jnp.dot(ws_ref[...], -jnp.sin(ang), precision=HIGHEST, preferred_element_type=F32)
           + wt_ref[...] * t_lane + b1_ref[...])
    h = jnp.sin(freq * pre)
    for layer in range(wh_ref.shape[0]):
        pre = jnp.dot(wh_ref[layer], h, precision=HIGHEST, preferred_element_type=F32) + bh_ref[layer]
        h = jnp.sin(freq * pre)
    out = jnp.dot(h.T, wl_ref[...], precision=HIGHEST, preferred_element_type=F32)
    m_row = base + lax.broadcasted_iota(jnp.int32, (rows, d_ch), 0)
    t_row = m_row.astype(F32) / (seq_len - 1.0)
    decay = jnp.exp(-t_row * jnp.abs(delta_ref[...]))
    decay_bwd = jnp.where(m_row == 0, 0.0, decay)
    n_cols = out.shape[1] // d_ch
    for q in range(n_cols):
        dq = decay_bwd if q % 2 == 1 else decay
        o_ref[:, q * d_ch:(q + 1) * d_ch] = out[:, q * d_ch:(q + 1) * d_ch] * dq


def _filt_gen(seq_len, w_first, b_first, w_hidden, b_hidden, w_last, freq, *, rows):
    pos_bands = (w_first.shape[0] - 1) // 2
    hidden = w_first.shape[1]
    n_cols = w_last.shape[1]
    d_ch = n_cols // 4
    bands = jnp.linspace(1e-4, pos_bands - 1, pos_bands, dtype=F32).reshape(pos_bands, 1)
    deltas = jnp.linspace(MIN_DECAY, MAX_DECAY, d_ch, dtype=F32).reshape(1, d_ch)
    w1t = w_first.T
    args = (bands, w1t[:, 0:1], w1t[:, 1:1 + pos_bands], w1t[:, 1 + pos_bands:],
            b_first.reshape(hidden, 1), jnp.swapaxes(w_hidden, 1, 2),
            b_hidden.reshape(b_hidden.shape[0], hidden, 1), freq.reshape(hidden, 1),
            w_last, deltas)
    full = lambda a: pl.BlockSpec(a.shape, lambda i, _n=a.ndim: (0,) * _n)
    return pl.pallas_call(
        functools.partial(_filt_gen_kernel, seq_len=seq_len, d_ch=d_ch),
        grid=(seq_len // rows,),
        in_specs=[full(a) for a in args],
        out_specs=pl.BlockSpec((rows, n_cols), lambda i: (i, 0)),
        out_shape=jax.ShapeDtypeStruct((seq_len, n_cols), F32),
        compiler_params=_params(("parallel",), 40),
        name="filt_gen",
    )(*args)


class _Plan:
    def __init__(self, seq_len, n1, n2):
        assert n1 * n2 == 2 * seq_len and n1 % (2 * SUBLANES) == 0 and n2 % SUBLANES == 0
        self.seq_len, self.n1, self.n2 = seq_len, n1, n2
        n = n1 * n2
        n1h = n1 // 2
        self.kh = kh = n1h + 1
        k1 = np.arange(kh, dtype=np.float64)[:, None]
        m1 = np.arange(n1h, dtype=np.float64)[None, :]
        th1 = 2.0 * np.pi * k1 * m1 / n1
        f1 = np.zeros((n1 + SUBLANES, n1h))
        f1[:n1h] = np.cos(th1[:n1h]); f1[n1h:n1] = -np.sin(th1[:n1h]); f1[n1] = np.cos(th1[n1h])
        self.f1 = jnp.asarray(f1, F32)
        m2 = np.arange(n2, dtype=np.float64)[None, :]
        tht = 2.0 * np.pi * k1 * m2 / n
        twr, twi = np.cos(tht), -np.sin(tht)
        rep = lambda a: np.repeat(a[:, :, None], LANES, axis=2)
        self.twr_inv, self.twi_inv = jnp.asarray(rep(twr), F32), jnp.asarray(rep(twi), F32)
        pad = np.zeros((n2, SUBLANES - 1))
        self.twr_fwd = jnp.asarray(rep(np.concatenate([twr.T, pad], 1)), F32)
        self.twi_fwd = jnp.asarray(rep(np.concatenate([twi.T, pad], 1)), F32)
        k2 = np.arange(n2, dtype=np.float64)[:, None]
        th2 = 2.0 * np.pi * k2 * m2 / n2
        f2r, f2i = np.cos(th2), -np.sin(th2)
        self.f2 = jnp.asarray(np.block([[f2r, -f2i], [f2i, f2r]]), F32)
        self.f2inv = jnp.asarray(np.block([[f2r, f2i], [-f2i, f2r]]), F32)
        wgt = np.where(k1[:n1h] == 0, 1.0, 2.0) / n
        ginv = np.concatenate([(wgt * np.cos(th1[:n1h])).T, (-wgt * np.sin(th1[:n1h])).T], axis=1)
        self.ginv = jnp.asarray(ginv, F32)


def _dft_fwd_kernel(x_ref, f1_ref, twr_ref, twi_ref, re_ref, im_ref, *, unroll):
    _, _, n1h, g, _ = x_ref.shape
    x2 = x_ref.at[0, 0].reshape(n1h * g, LANES)
    re2 = re_ref.at[0].reshape((n1h + 1) * g, LANES)
    im2 = im_ref.at[0].reshape((n1h + 1) * g, LANES)
    f1 = f1_ref[...].astype(BF16)

    def body(j, carry):
        xb = x2[pl.ds(j, n1h, stride=g), :].astype(BF16)
        p = _dot(f1, xb)
        twr, twi = twr_ref[j], twi_ref[j]
        pr, pi, pn = p[:n1h], p[n1h:2 * n1h], p[2 * n1h:2 * n1h + 1]
        re2[pl.ds(j, n1h, stride=g), :] = pr * twr[:n1h] - pi * twi[:n1h]
        im2[pl.ds(j, n1h, stride=g), :] = pr * twi[:n1h] + pi * twr[:n1h]
        re2[pl.ds(n1h * g + j, 1), :] = pn * twr[n1h:n1h + 1]
        im2[pl.ds(n1h * g + j, 1), :] = pn * twi[n1h:n1h + 1]
        return carry

    lax.fori_loop(0, g, body, 0, unroll=unroll)


def _dft_fwd(plan, x4, part, *, g):
    _, b, seq_len, ch = x4.shape
    n1h, n2, kh = plan.n1 // 2, plan.n2, plan.kh
    xv = x4.reshape(x4.shape[0], b, n1h, n2, ch)
    out = jax.ShapeDtypeStruct((b, kh, n2, ch), F32)
    tw_spec = pl.BlockSpec((g, n1h + SUBLANES, LANES), lambda j, bb, c: (j, 0, 0))
    return pl.pallas_call(
        functools.partial(_dft_fwd_kernel, unroll=min(g, 8)),
        grid=(n2 // g, b, ch // LANES),
        in_specs=[
            pl.BlockSpec((1, 1, n1h, g, LANES), lambda j, bb, c: (part, bb, 0, j, c)),
            pl.BlockSpec(plan.f1.shape, lambda j, bb, c: (0, 0)),
            tw_spec, tw_spec,
        ],
        out_specs=[pl.BlockSpec((1, kh, g, LANES), lambda j, bb, c: (bb, 0, j, c))] * 2,
        out_shape=[out, out],
        compiler_params=_params(("parallel", "parallel", "parallel"), 48),
        name="dft_fwd",
    )(xv, plan.f1, plan.twr_fwd, plan.twi_fwd)


def _stack_bf16(re, im):
    return jnp.concatenate([re.astype(BF16), im.astype(BF16)], axis=0)


def _filt_spec_kernel(re_ref, im_ref, f2_ref, kr_ref, ki_ref, *, d_ch):
    n2 = re_ref.shape[2]
    x = _dot(f2_ref[...].astype(BF16), _stack_bf16(re_ref[0, 0], im_ref[0, 0]))
    xr, xi = x[:n2], x[n2:]
    for o in range(kr_ref.shape[2] // d_ch):
        f = slice(2 * o * d_ch, (2 * o + 1) * d_ch)
        bk = slice((2 * o + 1) * d_ch, (2 * o + 2) * d_ch)
        kr_ref[0, :, o * d_ch:(o + 1) * d_ch] = xr[:, f] + xr[:, bk]
        ki_ref[0, :, o * d_ch:(o + 1) * d_ch] = xi[:, f] - xi[:, bk]


def _filt_spec(plan, fre, fim, d_ch):
    n2, kh = plan.n2, plan.kh
    cols = fre.shape[-1]
    out = jax.ShapeDtypeStruct((kh, n2, cols // 2), F32)
    return pl.pallas_call(
        functools.partial(_filt_spec_kernel, d_ch=d_ch),
        grid=(kh,),
        in_specs=[
            pl.BlockSpec((1, 1, n2, cols), lambda k: (0, k, 0, 0)),
            pl.BlockSpec((1, 1, n2, cols), lambda k: (0, k, 0, 0)),
            pl.BlockSpec((2 * n2, 2 * n2), lambda k: (0, 0)),
        ],
        out_specs=[pl.BlockSpec((1, n2, cols // 2), lambda k: (k, 0, 0))] * 2,
        out_shape=[out, out],
        compiler_params=_params(("parallel",), 40),
        name="filt_spec",
    )(fre, fim, plan.f2)


def _conv_mid_kernel(re_ref, im_ref, kr_ref, ki_ref, twr_ref, twi_ref, f2_ref, f2inv_ref,
                     ore_ref, oim_ref):
    nb, ks, n2, ch = re_ref.shape
    reps = ch // LANES
    f2, f2inv = f2_ref[...].astype(BF16), f2inv_ref[...].astype(BF16)
    for s in range(ks):
        kr, ki = kr_ref[s], ki_ref[s]
        twr = jnp.tile(twr_ref[s], (1, reps))
        twi = jnp.tile(twi_ref[s], (1, reps))

        def body(b, carry):
            x = _dot(f2, _stack_bf16(re_ref[b, s], im_ref[b, s]))
            xr, xi = x[:n2], x[n2:]
            y = _dot(f2inv, _stack_bf16(xr * kr - xi * ki, xr * ki + xi * kr))
            br, bi = y[:n2], y[n2:]
            ore_ref[b, s] = br * twr + bi * twi
            oim_ref[b, s] = bi * twr - br * twi
            return carry

        lax.fori_loop(0, nb, body, 0)


def _conv_mid(plan, are, aim, kr, ki, order, *, ks):
    b, kh, n2, ch = are.shape
    data = pl.BlockSpec((b, ks, n2, ch), lambda k: (0, k, 0, 0))
    filt = pl.BlockSpec((ks, n2, ch), lambda k: (k, 0, order))
    tw = pl.BlockSpec((ks, n2, LANES), lambda k: (k, 0, 0))
    mat = pl.BlockSpec((2 * n2, 2 * n2), lambda k: (0, 0))
    out = jax.ShapeDtypeStruct((b, kh, n2, ch), F32)
    return pl.pallas_call(
        _conv_mid_kernel,
        grid=(kh // ks,),
        in_specs=[data, data, filt, filt, tw, tw, mat, mat],
        out_specs=[data, data],
        out_shape=[out, out],
        compiler_params=_params(("parallel",), 48),
        name="conv_mid",
    )(are, aim, kr, ki, plan.twr_inv, plan.twi_inv, plan.f2, plan.f2inv)


def _dft_inv_kernel(bre_ref, bim_ref, x_ref, gate_ref, skip_ref, ginv_ref, o_ref, *, unroll, scale):
    _, _, n1h, g, _ = o_ref.shape
    b_re = bre_ref.at[0].reshape((n1h + 1) * g, LANES)
    b_im = bim_ref.at[0].reshape((n1h + 1) * g, LANES)
    x2 = x_ref.at[0, 0].reshape(n1h * g, LANES)
    gate2 = gate_ref.at[0, 0].reshape(n1h * g, LANES)
    o2 = o_ref.at[0, 0].reshape(n1h * g, LANES)
    ginv = ginv_ref[...].astype(BF16)
    skip = skip_ref[0]
    n1 = lax.broadcasted_iota(jnp.int32, (n1h, LANES), 0)
    nyq_w = jnp.where(jnp.bitwise_and(n1, 1) == 0, scale, -scale)

    def body(j, carry):
        rows = pl.ds(j, n1h, stride=g)
        spec = _stack_bf16(b_re[rows, :], b_im[rows, :])
        y = _dot(ginv, spec) + nyq_w * b_re[pl.ds(n1h * g + j, 1), :]
        o2[rows, :] = gate2[rows, :] * (y + x2[rows, :] * skip)
        return carry

    lax.fori_loop(0, g, body, 0, unroll=unroll)


def _dft_inv(plan, bre, bim, x4, x_part, gate4, gate_part, skip, order, *, g):
    b, seq_len, ch = x4.shape[1:]
    n1h, n2, kh = plan.n1 // 2, plan.n2, plan.kh
    xv = x4.reshape(x4.shape[0], b, n1h, n2, ch)
    gv = gate4.reshape(gate4.shape[0], b, n1h, n2, ch)
    spec = pl.BlockSpec((1, kh, g, LANES), lambda j, bb, c: (bb, 0, j, c))
    time = lambda part: pl.BlockSpec((1, 1, n1h, g, LANES), lambda j, bb, c: (part, bb, 0, j, c))
    out = pl.pallas_call(
        functools.partial(_dft_inv_kernel, unroll=min(g, 8), scale=1.0 / (plan.n1 * n2)),
        grid=(n2 // g, b, ch // LANES),
        in_specs=[
            spec, spec, time(x_part), time(gate_part),
            pl.BlockSpec((1, 1, LANES), lambda j, bb, c: (order, 0, c)),
            pl.BlockSpec(plan.ginv.shape, lambda j, bb, c: (0, 0)),
        ],
        out_specs=time(0),
        out_shape=jax.ShapeDtypeStruct((1, b, n1h, n2, ch), F32),
        compiler_params=_params(("parallel", "parallel", "parallel"), 48),
        name="dft_inv",
    )(bre, bim, xv, gv, skip, plan.ginv)
    return out.reshape(1, b, seq_len, ch)


def _mix_out_kernel(x_ref, yp_ref, yh_ref, gp_ref, gh_ref, w_ref, gpost_ref, o_ref):
    d_pool = yp_ref.shape[1]
    yp = _rms(yp_ref[...], gp_ref[...]).astype(BF16)
    yh = _rms(yh_ref[...], gh_ref[...]).astype(BF16)
    y = _dot(yp, w_ref[:d_pool, :]) + _dot(yh, w_ref[d_pool:, :])
    o_ref[...] = x_ref[...] + _rms(y, gpost_ref[...])


def _mix_out(x, yp, yh, gp, gh, w, gpost, *, tm):
    t, d = x.shape
    dp, dh = yp.shape[1], yh.shape[1]
    const = lambda i: (0, 0)
    row = lambda i: (i, 0)
    return pl.pallas_call(
        _mix_out_kernel,
        grid=(t // tm,),
        in_specs=[
            pl.BlockSpec((tm, d), row),
            pl.BlockSpec((tm, dp), row),
            pl.BlockSpec((tm, dh), row),
            pl.BlockSpec((1, dp), const),
            pl.BlockSpec((1, dh), const),
            pl.BlockSpec((dp + dh, d), const, pipeline_mode=pl.Buffered(1)),
            pl.BlockSpec((1, d), const),
        ],
        out_specs=pl.BlockSpec((tm, d), row),
        out_shape=jax.ShapeDtypeStruct((t, d), F32),
        compiler_params=_params(("parallel",), 40),
        name="mix_out",
    )(x, yp, yh, gp, gh, w, gpost)


def _dft_shape(seq_len):
    n2 = 128
    return 2 * seq_len // n2, n2


def _block_cols(n2, rows, target_bytes=1 << 20):
    g = max(SUBLANES, min(n2, target_bytes // (rows * LANES * 4)) // SUBLANES * SUBLANES)
    while n2 % g:
        g -= SUBLANES
    return g


def _slabs_per_step(kh, slab_bytes, target_bytes=3 << 20):
    ks = max(1, min(kh, target_bytes // slab_bytes))
    while kh % ks:
        ks -= 1
    return ks


def _hyena(plan, uc, filt, skip):
    _, b, seq_len, ch = uc.shape
    n_orders = skip.shape[0]
    g = _block_cols(plan.n2, plan.n1 // 2)
    fre, fim = _dft_fwd(plan, filt.reshape(1, 1, seq_len, filt.shape[1]), 0, g=g)
    kr, ki = _filt_spec(plan, fre, fim, ch)
    ks = _slabs_per_step(plan.kh, b * plan.n2 * ch * 4)
    skip3 = skip.reshape(n_orders, 1, ch)
    cur, cur_part = uc, 0
    for order in range(n_orders):
        are, aim = _dft_fwd(plan, cur, cur_part, g=g)
        bre, bim = _conv_mid(plan, are, aim, kr, ki, order, ks=ks)
        cur = _dft_inv(plan, bre, bim, cur, cur_part, uc, order + 1, skip3, order, g=g)
        cur_part = 0
    return cur[0]


def _trunk(x3, p, plan, filt):
    b, seq_len, d = x3.shape
    t = b * seq_len
    x = x3.reshape(t, d)
    x = _ffn(x, p["g1pre"], p["g1post"], p["w1gu"], p["w1d"], tm=512, chunk=p["ff_chunk"])
    z = _mix_in(x, p["gmix"], p["w_in"], tm=512).reshape(b, seq_len, -1)
    d_pool = p["pool_w"].shape[0] * LANES
    rows = min(seq_len, 2048)
    y_pool = _pool(z, p["pool_w"], p["pool_scale"], rows=rows)
    uc = _sconv(z, p["conv_w"], p["conv_b"], rows=rows, col0=d_pool, n_parts=3)
    y_hy = _hyena(plan, uc, filt, p["skip"])
    x = _mix_out(x, y_pool.reshape(t, -1), y_hy.reshape(t, -1), p["gpool"], p["ghy"],
                 p["w_out"], p["gmixpost"], tm=512)
    x = _ffn(x, p["g2pre"], p["g2post"], p["w2gu"], p["w2d"], tm=512, chunk=p["ff_chunk"])
    return x.reshape(b, seq_len, d)


def kernel(x_prompt, x_sample, ffn1_norm_pre, ffn1_norm_post, ffn1_w_gate_up, ffn1_w_down, mix_norm_pre, w_in, pool_w_map, pool_scale, hyena_conv_w, hyena_conv_b, filt_w_first, filt_b_first, filt_w_hidden, filt_b_hidden, filt_w_last, filt_freq, hyena_skip, pool_out_norm, hyena_out_norm, w_out, mix_norm_post, ffn2_norm_pre, ffn2_norm_post, ffn2_w_gate_up, ffn2_w_down):
    assert ffn1_norm_pre.shape[0] == 1, "single-layer trunk"
    row = lambda a: a[0].reshape(1, -1)
    d_ff = ffn1_w_down.shape[1]
    p = dict(
        g1pre=row(ffn1_norm_pre), g1post=row(ffn1_norm_post),
        w1gu=ffn1_w_gate_up[0].astype(BF16), w1d=ffn1_w_down[0].astype(BF16),
        gmix=row(mix_norm_pre), w_in=w_in[0].astype(BF16),
        pool_w=pool_w_map[0].astype(BF16), pool_scale=row(pool_scale),
        conv_w=hyena_conv_w[0], conv_b=row(hyena_conv_b),
        skip=hyena_skip[0], gpool=row(pool_out_norm), ghy=row(hyena_out_norm),
        w_out=w_out[0].astype(BF16), gmixpost=row(mix_norm_post),
        g2pre=row(ffn2_norm_pre), g2post=row(ffn2_norm_post),
        w2gu=ffn2_w_gate_up[0].astype(BF16), w2d=ffn2_w_down[0].astype(BF16),
        ff_chunk=d_ff // 2 if (d_ff // 2) % LANES == 0 else d_ff,
    )
    outs = []
    for x3 in (x_prompt, x_sample):
        seq_len = x3.shape[1]
        plan = _Plan(seq_len, *_dft_shape(seq_len))
        filt = _filt_gen(seq_len, filt_w_first[0], filt_b_first[0], filt_w_hidden[0],
                         filt_b_hidden[0], filt_w_last[0], filt_freq[0], rows=min(seq_len, 512))
        outs.append(_trunk(x3, p, plan, filt))
    return tuple(outs)
```

```python
import functools
import math

import jax
import jax.numpy as jnp
import numpy as np
from jax import lax
from jax.experimental import pallas as pl
from jax.experimental.pallas import tpu as pltpu

F32 = jnp.float32
BF16 = jnp.bfloat16
EPS = 1e-6
LANES = 128
SUBLANES = 8
POOL_WINDOWS = (2, 4, 8, 16)
HALO = 8
DECAY_TARGET = 1e-2
FAST_DECAY_PCT = 0.3
SLOW_DECAY_PCT = 1.5
MAX_DECAY = math.log(DECAY_TARGET) / FAST_DECAY_PCT
MIN_DECAY = math.log(DECAY_TARGET) / SLOW_DECAY_PCT
HIGHEST = lax.Precision.HIGHEST
FUSED_VMEM_BUDGET = 32 << 20


def _params(sem, vmem_mib):
    return pltpu.CompilerParams(dimension_semantics=sem, vmem_limit_bytes=vmem_mib << 20)


def _rms(x, g):
    inv = lax.rsqrt(jnp.mean(x * x, axis=-1, keepdims=True) + EPS)
    return (x * inv) * g


def _dot(a, b):
    return jnp.dot(a, b, preferred_element_type=F32)


def _ffn_kernel(x_ref, gpre_ref, gpost_ref, wgu_ref, wd_ref, o_ref, *, d_ff, chunk):
    x = x_ref[...]
    h = _rms(x, gpre_ref[...]).astype(BF16)
    acc = None
    for j in range(d_ff // chunk):
        lo = j * chunk
        gate = _dot(h, wgu_ref[:, lo:lo + chunk])
        up = _dot(h, wgu_ref[:, d_ff + lo:d_ff + lo + chunk])
        act = (gate * jax.nn.sigmoid(gate) * up).astype(BF16)
        part = _dot(act, wd_ref[lo:lo + chunk, :])
        acc = part if acc is None else acc + part
    o_ref[...] = x + 0.5 * _rms(acc, gpost_ref[...])


def _ffn(x, g_pre, g_post, wgu, wd, *, tm, chunk):
    t, d = x.shape
    d_ff = wd.shape[0]
    const = lambda i: (0, 0)
    return pl.pallas_call(
        functools.partial(_ffn_kernel, d_ff=d_ff, chunk=chunk),
        grid=(t // tm,),
        in_specs=[
            pl.BlockSpec((tm, d), lambda i: (i, 0)),
            pl.BlockSpec((1, d), const),
            pl.BlockSpec((1, d), const),
            pl.BlockSpec((d, 2 * d_ff), const, pipeline_mode=pl.Buffered(1)),
            pl.BlockSpec((d_ff, d), const, pipeline_mode=pl.Buffered(1)),
        ],
        out_specs=pl.BlockSpec((tm, d), lambda i: (i, 0)),
        out_shape=jax.ShapeDtypeStruct((t, d), F32),
        compiler_params=_params(("parallel",), 52),
        name="ffn",
    )(x, g_pre, g_post, wgu, wd)


def _mix_in_kernel(x_ref, g_ref, w_ref, z_ref):
    h = _rms(x_ref[...], g_ref[...]).astype(BF16)
    z_ref[...] = _dot(h, w_ref[...])


def _mix_in(x, g, w, *, tm):
    t, d = x.shape
    d_in = w.shape[1]
    const = lambda i: (0, 0)
    return pl.pallas_call(
        _mix_in_kernel,
        grid=(t // tm,),
        in_specs=[
            pl.BlockSpec((tm, d), lambda i: (i, 0)),
            pl.BlockSpec((1, d), const),
            pl.BlockSpec((d, d_in), const, pipeline_mode=pl.Buffered(1)),
        ],
        out_specs=pl.BlockSpec((tm, d_in), lambda i: (i, 0)),
        out_shape=jax.ShapeDtypeStruct((t, d_in), F32),
        compiler_params=_params(("parallel",), 40),
        name="mix_in",
    )(x, g, w)


def _with_halo(prev_ref, main_ref, next_ref, i, n_tiles):
    prev = jnp.where(i > 0, prev_ref[0], 0.0)
    nxt = jnp.where(i < n_tiles - 1, next_ref[0], 0.0)
    return jnp.concatenate([prev, main_ref[0], nxt], axis=0)


def _pool_kernel(prev_ref, main_ref, next_ref, wmap_ref, scale_ref, o_ref, *, seq_len, n_tiles):
    i = pl.program_id(1)
    c = pl.program_id(2)
    rows = main_ref.shape[1]
    ext = _with_halo(prev_ref, main_ref, next_ref, i, n_tiles)
    n_ext = rows + 2 * HALO
    s2 = ext + pltpu.roll(ext, 1, 0)
    s4 = pltpu.roll(s2, 1, 0) + pltpu.roll(s2, n_ext - 1, 0)
    s8 = pltpu.roll(s4, 2, 0) + pltpu.roll(s4, n_ext - 2, 0)
    s16 = pltpu.roll(s8, 4, 0) + pltpu.roll(s8, n_ext - 4, 0)
    ssum = jnp.where(c == 0, s2, jnp.where(c == 1, s4, jnp.where(c == 2, s8, s16)))
    ssum = ssum[HALO:HALO + rows]
    u = main_ref[0]
    half = jnp.left_shift(1, c)
    pos = i * rows + lax.broadcasted_iota(jnp.int32, u.shape, 0)
    lo = jnp.clip(pos - half, 0, seq_len)
    hi = jnp.clip(pos + half, 0, seq_len)
    cnt = (hi - lo).astype(F32)
    d = ssum / cnt - u
    y = _dot(d.astype(BF16), wmap_ref[0])
    o_ref[0] = y * scale_ref[...]


def _halo_specs(rows, seq_len, col_of):
    blocks_per_tile = rows // HALO
    last = seq_len // HALO - 1
    prev = pl.BlockSpec((1, HALO, LANES),
                        lambda b, i, c: (b, jnp.maximum(i * blocks_per_tile - 1, 0), col_of(c)))
    main = pl.BlockSpec((1, rows, LANES), lambda b, i, c: (b, i, col_of(c)))
    nxt = pl.BlockSpec((1, HALO, LANES),
                       lambda b, i, c: (b, jnp.minimum((i + 1) * blocks_per_tile, last), col_of(c)))
    return [prev, main, nxt]


def _pool(z, wmap, scale, *, rows):
    b, seq_len, _ = z.shape
    n_groups = wmap.shape[0]
    n_tiles = seq_len // rows
    return pl.pallas_call(
        functools.partial(_pool_kernel, seq_len=seq_len, n_tiles=n_tiles),
        grid=(b, n_tiles, n_groups),
        in_specs=_halo_specs(rows, seq_len, lambda c: c) + [
            pl.BlockSpec((1, LANES, LANES), lambda b, i, c: (c, 0, 0)),
            pl.BlockSpec((1, LANES), lambda b, i, c: (0, c)),
        ],
        out_specs=pl.BlockSpec((1, rows, LANES), lambda b, i, c: (b, i, c)),
        out_shape=jax.ShapeDtypeStruct((b, seq_len, n_groups * LANES), F32),
        compiler_params=_params(("parallel", "parallel", "parallel"), 40),
        name="pool",
    )(z, z, z, wmap, scale)


def _sconv_kernel(prev_ref, main_ref, next_ref, w_ref, b_ref, o_ref, *, n_tiles):
    i = pl.program_id(1)
    rows = main_ref.shape[1]
    ext = _with_halo(prev_ref, main_ref, next_ref, i, n_tiles)
    n_ext = rows + 2 * HALO
    before = pltpu.roll(ext, 1, 0)[HALO:HALO + rows]
    after = pltpu.roll(ext, n_ext - 1, 0)[HALO:HALO + rows]
    w = w_ref[...]
    out = before * w[0:1] + main_ref[0] * w[1:2] + after * w[2:3]
    o_ref[0, 0] = out + b_ref[...]


def _sconv(z, w, bias, *, rows, col0, n_parts):
    b, seq_len, _ = z.shape
    n_tiles = seq_len // rows
    blocks_per_part = (w.shape[1] // n_parts) // LANES
    cb0 = col0 // LANES
    return pl.pallas_call(
        functools.partial(_sconv_kernel, n_tiles=n_tiles),
        grid=(b, n_tiles, n_parts * blocks_per_part),
        in_specs=_halo_specs(rows, seq_len, lambda c: c + cb0) + [
            pl.BlockSpec((w.shape[0], LANES), lambda b, i, c: (0, c)),
            pl.BlockSpec((1, LANES), lambda b, i, c: (0, c)),
        ],
        out_specs=pl.BlockSpec((1, 1, rows, LANES),
                               lambda b, i, c: (c // blocks_per_part, b, i, c % blocks_per_part)),
        out_shape=jax.ShapeDtypeStruct((n_parts, b, seq_len, blocks_per_part * LANES), F32),
        compiler_params=_params(("parallel", "parallel", "parallel"), 40),
        name="sconv",
    )(z, z, z, w, bias)


def _filt_gen_kernel(bands_ref, wt_ref, wc_ref, ws_ref, b1_ref, wh_ref, bh_ref, freq_ref,
                     wl_ref, delta_ref, o_ref, *, seq_len, d_ch):
    rows = o_ref.shape[0]
    base = pl.program_id(0) * rows
    m_lane = (base + lax.broadcasted_iota(jnp.int32, (1, rows), 1)).astype(F32)
    t_lane = m_lane / (seq_len - 1.0)
    ang = (bands_ref[...] * (2.0 * math.pi / seq_len)) * m_lane
    freq = freq_ref[...]
    pre = (jnp.dot(wc_ref[...], jnp.cos(ang), precision=HIGHEST, preferred_element_type=F32)
           + jnp.dot(ws_ref[...], -jnp.sin(ang), precision=HIGHEST, preferred_element_type=F32)
           + wt_ref[...] * t_lane + b1_ref[...])
    h = jnp.sin(freq * pre)
    for layer in range(wh_ref.shape[0]):
        pre = jnp.dot(wh_ref[layer], h, precision=HIGHEST, preferred_element_type=F32) + bh_ref[layer]
        h = jnp.sin(freq * pre)
    out = jnp.dot(h.T, wl_ref[...], precision=HIGHEST, preferred_element_type=F32)
    m_row = base + lax.broadcasted_iota(jnp.int32, (rows, d_ch), 0)
    t_row = m_row.astype(F32) / (seq_len - 1.0)
    decay = jnp.exp(-t_row * jnp.abs(delta_ref[...]))
    decay_bwd = jnp.where(m_row == 0, 0.0, decay)
    n_cols = out.shape[1] // d_ch
    for q in range(n_cols):
        dq = decay_bwd if q % 2 == 1 else decay
        o_ref[:, q * d_ch:(q + 1) * d_ch] = out[:, q * d_ch:(q + 1) * d_ch] * dq


def _filt_gen(seq_len, w_first, b_first, w_hidden, b_hidden, w_last, freq, *, rows):
    pos_bands = (w_first.shape[0] - 1) // 2
    hidden = w_first.shape[1]
    n_cols = w_last.shape[1]
    d_ch = n_cols // 4
    bands = jnp.linspace(1e-4, pos_bands - 1, pos_bands, dtype=F32).reshape(pos_bands, 1)
    deltas = jnp.linspace(MIN_DECAY, MAX_DECAY, d_ch, dtype=F32).reshape(1, d_ch)
    w1t = w_first.T
    args = (bands, w1t[:, 0:1], w1t[:, 1:1 + pos_bands], w1t[:, 1 + pos_bands:],
            b_first.reshape(hidden, 1), jnp.swapaxes(w_hidden, 1, 2),
            b_hidden.reshape(b_hidden.shape[0], hidden, 1), freq.reshape(hidden, 1),
            w_last, deltas)
    full = lambda a: pl.BlockSpec(a.shape, lambda i, _n=a.ndim: (0,) * _n)
    return pl.pallas_call(
        functools.partial(_filt_gen_kernel, seq_len=seq_len, d_ch=d_ch),
        grid=(seq_len // rows,),
        in_specs=[full(a) for a in args],
        out_specs=pl.BlockSpec((rows, n_cols), lambda i: (i, 0)),
        out_shape=jax.ShapeDtypeStruct((seq_len, n_cols), F32),
        compiler_params=_params(("parallel",), 40),
        name="filt_gen",
    )(*args)


class _Plan:
    def __init__(self, seq_len, n1, n2):
        assert n1 * n2 == 2 * seq_len and n1 % (2 * SUBLANES) == 0 and n2 % SUBLANES == 0
        self.seq_len, self.n1, self.n2 = seq_len, n1, n2
        n = n1 * n2
        n1h = n1 // 2
        self.kh = kh = n1h + 1
        k1 = np.arange(kh, dtype=np.float64)[:, None]
        m1 = np.arange(n1h, dtype=np.float64)[None, :]
        th1 = 2.0 * np.pi * k1 * m1 / n1
        f1 = np.zeros((n1 + SUBLANES, n1h))
        f1[:n1h] = np.cos(th1[:n1h]); f1[n1h:n1] = -np.sin(th1[:n1h]); f1[n1] = np.cos(th1[n1h])
        self.f1 = jnp.asarray(f1, F32)
        m2 = np.arange(n2, dtype=np.float64)[None, :]
        tht = 2.0 * np.pi * k1 * m2 / n
        twr, twi = np.cos(tht), -np.sin(tht)
        rep = lambda a: np.repeat(a[:, :, None], LANES, axis=2)
        self.twr_inv, self.twi_inv = jnp.asarray(rep(twr), F32), jnp.asarray(rep(twi), F32)
        pad = np.zeros((n2, SUBLANES - 1))
        self.twr_fwd = jnp.asarray(rep(np.concatenate([twr.T, pad], 1)), F32)
        self.twi_fwd = jnp.asarray(rep(np.concatenate([twi.T, pad], 1)), F32)
        k2 = np.arange(n2, dtype=np.float64)[:, None]
        th2 = 2.0 * np.pi * k2 * m2 / n2
        f2r, f2i = np.cos(th2), -np.sin(th2)
        self.f2 = jnp.asarray(np.block([[f2r, -f2i], [f2i, f2r]]), F32)
        self.f2inv = jnp.asarray(np.block([[f2r, f2i], [-f2i, f2r]]), F32)
        wgt = np.where(k1[:n1h] == 0, 1.0, 2.0) / n
        ginv = np.concatenate([(wgt * np.cos(th1[:n1h])).T, (-wgt * np.sin(th1[:n1h])).T], axis=1)
        self.ginv = jnp.asarray(ginv, F32)


def _dft_fwd_kernel(x_ref, f1_ref, twr_ref, twi_ref, re_ref, im_ref, *, unroll):
    _, _, n1h, g, _ = x_ref.shape
    x2 = x_ref.at[0, 0].reshape(n1h * g, LANES)
    re2 = re_ref.at[0].reshape((n1h + 1) * g, LANES)
    im2 = im_ref.at[0].reshape((n1h + 1) * g, LANES)
    f1 = f1_ref[...].astype(BF16)

    def body(j, carry):
        xb = x2[pl.ds(j, n1h, stride=g), :].astype(BF16)
        p = _dot(f1, xb)
        twr, twi = twr_ref[j], twi_ref[j]
        pr, pi, pn = p[:n1h], p[n1h:2 * n1h], p[2 * n1h:2 * n1h + 1]
        re2[pl.ds(j, n1h, stride=g), :] = pr * twr[:n1h] - pi * twi[:n1h]
        im2[pl.ds(j, n1h, stride=g), :] = pr * twi[:n1h] + pi * twr[:n1h]
        re2[pl.ds(n1h * g + j, 1), :] = pn * twr[n1h:n1h + 1]
        im2[pl.ds(n1h * g + j, 1), :] = pn * twi[n1h:n1h + 1]
        return carry

    lax.fori_loop(0, g, body, 0, unroll=unroll)


def _dft_fwd(plan, x4, part, *, g):
    _, b, seq_len, ch = x4.shape
    n1h, n2, kh = plan.n1 // 2, plan.n2, plan.kh
    xv = x4.reshape(x4.shape[0], b, n1h, n2, ch)
    out = jax.ShapeDtypeStruct((b, kh, n2, ch), F32)
    tw_spec = pl.BlockSpec((g, n1h + SUBLANES, LANES), lambda j, bb, c: (j, 0, 0))
    return pl.pallas_call(
        functools.partial(_dft_fwd_kernel, unroll=min(g, 16)),
        grid=(n2 // g, b, ch // LANES),
        in_specs=[
            pl.BlockSpec((1, 1, n1h, g, LANES), lambda j, bb, c: (part, bb, 0, j, c)),
            pl.BlockSpec(plan.f1.shape, lambda j, bb, c: (0, 0)),
            tw_spec, tw_spec,
        ],
        out_specs=[pl.BlockSpec((1, kh, g, LANES), lambda j, bb, c: (bb, 0, j, c))] * 2,
        out_shape=[out, out],
        compiler_params=_params(("parallel", "parallel", "parallel"), 48),
        name="dft_fwd",
    )(xv, plan.f1, plan.twr_fwd, plan.twi_fwd)


def _stack_bf16(re, im):
    return jnp.concatenate([re.astype(BF16), im.astype(BF16)], axis=0)


def _filt_spec_kernel(re_ref, im_ref, f2_ref, kr_ref, ki_ref, *, d_ch):
    n2 = re_ref.shape[2]
    x = _dot(f2_ref[...].astype(BF16), _stack_bf16(re_ref[0, 0], im_ref[0, 0]))
    xr, xi = x[:n2], x[n2:]
    for o in range(kr_ref.shape[2] // d_ch):
        f = slice(2 * o * d_ch, (2 * o + 1) * d_ch)
        bk = slice((2 * o + 1) * d_ch, (2 * o + 2) * d_ch)
        kr_ref[0, :, o * d_ch:(o + 1) * d_ch] = xr[:, f] + xr[:, bk]
        ki_ref[0, :, o * d_ch:(o + 1) * d_ch] = xi[:, f] - xi[:, bk]


def _filt_spec(plan, fre, fim, d_ch):
    n2, kh = plan.n2, plan.kh
    cols = fre.shape[-1]
    out = jax.ShapeDtypeStruct((kh, n2, cols // 2), F32)
    return pl.pallas_call(
        functools.partial(_filt_spec_kernel, d_ch=d_ch),
        grid=(kh,),
        in_specs=[
            pl.BlockSpec((1, 1, n2, cols), lambda k: (0, k, 0, 0)),
            pl.BlockSpec((1, 1, n2, cols), lambda k: (0, k, 0, 0)),
            pl.BlockSpec((2 * n2, 2 * n2), lambda k: (0, 0)),
        ],
        out_specs=[pl.BlockSpec((1, n2, cols // 2), lambda k: (k, 0, 0))] * 2,
        out_shape=[out, out],
        compiler_params=_params(("parallel",), 40),
        name="filt_spec",
    )(fre, fim, plan.f2)


def _conv_mid_kernel(re_ref, im_ref, kr_ref, ki_ref, twr_ref, twi_ref, f2_ref, f2inv_ref,
                     ore_ref, oim_ref):
    nb, ks, n2, ch = re_ref.shape
    reps = ch // LANES
    f2, f2inv = f2_ref[...].astype(BF16), f2inv_ref[...].astype(BF16)
    for s in range(ks):
        kr, ki = kr_ref[s], ki_ref[s]
        twr = jnp.tile(twr_ref[s], (1, reps))
        twi = jnp.tile(twi_ref[s], (1, reps))

        def body(b, carry):
            x = _dot(f2, _stack_bf16(re_ref[b, s], im_ref[b, s]))
            xr, xi = x[:n2], x[n2:]
            y = _dot(f2inv, _stack_bf16(xr * kr - xi * ki, xr * ki + xi * kr))
            br, bi = y[:n2], y[n2:]
            ore_ref[b, s] = br * twr + bi * twi
            oim_ref[b, s] = bi * twr - br * twi
            return carry

        lax.fori_loop(0, nb, body, 0)


def _conv_mid(plan, are, aim, kr, ki, order, *, ks):
    b, kh, n2, ch = are.shape
    data = pl.BlockSpec((b, ks, n2, ch), lambda k: (0, k, 0, 0))
    filt = pl.BlockSpec((ks, n2, ch), lambda k: (k, 0, order))
    tw = pl.BlockSpec((ks, n2, LANES), lambda k: (k, 0, 0))
    mat = pl.BlockSpec((2 * n2, 2 * n2), lambda k: (0, 0))
    out = jax.ShapeDtypeStruct((b, kh, n2, ch), F32)
    return pl.pallas_call(
        _conv_mid_kernel,
        grid=(kh // ks,),
        in_specs=[data, data, filt, filt, tw, tw, mat, mat],
        out_specs=[data, data],
        out_shape=[out, out],
        compiler_params=_params(("parallel",), 48),
        name="conv_mid",
    )(are, aim, kr, ki, plan.twr_inv, plan.twi_inv, plan.f2, plan.f2inv)


def _dft_inv_kernel(bre_ref, bim_ref, x_ref, gate_ref, skip_ref, ginv_ref, o_ref, *, unroll, scale):
    _, _, n1h, g, _ = o_ref.shape
    b_re = bre_ref.at[0].reshape((n1h + 1) * g, LANES)
    b_im = bim_ref.at[0].reshape((n1h + 1) * g, LANES)
    x2 = x_ref.at[0, 0].reshape(n1h * g, LANES)
    gate2 = gate_ref.at[0, 0].reshape(n1h * g, LANES)
    o2 = o_ref.at[0, 0].reshape(n1h * g, LANES)
    ginv = ginv_ref[...].astype(BF16)
    skip = skip_ref[0]
    n1 = lax.broadcasted_iota(jnp.int32, (n1h, LANES), 0)
    nyq_w = jnp.where(jnp.bitwise_and(n1, 1) == 0, scale, -scale)

    def body(j, carry):
        rows = pl.ds(j, n1h, stride=g)
        spec = _stack_bf16(b_re[rows, :], b_im[rows, :])
        y = _dot(ginv, spec) + nyq_w * b_re[pl.ds(n1h * g + j, 1), :]
        o2[rows, :] = gate2[rows, :] * (y + x2[rows, :] * skip)
        return carry

    lax.fori_loop(0, g, body, 0, unroll=unroll)


def _dft_inv(plan, bre, bim, x4, x_part, gate4, gate_part, skip, order, *, g):
    b, seq_len, ch = x4.shape[1:]
    n1h, n2, kh = plan.n1 // 2, plan.n2, plan.kh
    xv = x4.reshape(x4.shape[0], b, n1h, n2, ch)
    gv = gate4.reshape(gate4.shape[0], b, n1h, n2, ch)
    spec = pl.BlockSpec((1, kh, g, LANES), lambda j, bb, c: (bb, 0, j, c))
    time = lambda part: pl.BlockSpec((1, 1, n1h, g, LANES), lambda j, bb, c: (part, bb, 0, j, c))
    out = pl.pallas_call(
        functools.partial(_dft_inv_kernel, unroll=min(g, 16), scale=1.0 / (plan.n1 * n2)),
        grid=(n2 // g, b, ch // LANES),
        in_specs=[
            spec, spec, time(x_part), time(gate_part),
            pl.BlockSpec((1, 1, LANES), lambda j, bb, c: (order, 0, c)),
            pl.BlockSpec(plan.ginv.shape, lambda j, bb, c: (0, 0)),
        ],
        out_specs=time(0),
        out_shape=jax.ShapeDtypeStruct((1, b, n1h, n2, ch), F32),
        compiler_params=_params(("parallel", "parallel", "parallel"), 48),
        name="dft_inv",
    )(bre, bim, xv, gv, skip, plan.ginv)
    return out.reshape(1, b, seq_len, ch)


def _hyena_fused_kernel(v_ref, x1_ref, x2_ref, skip_ref, kr0_ref, ki0_ref, kr1_ref, ki1_ref,
                        f1_ref, twrf_ref, twif_ref, twri_ref, twii_ref, f2_ref, f2inv_ref, ginv_ref,
                        o_ref, are_ref, aim_ref, cur_ref, tin_ref, *, n1h, n2, unroll):
    nb = o_ref.shape[0]
    pitch = n2 + SUBLANES
    f1 = f1_ref[...].astype(BF16)
    f2, f2inv = f2_ref[...].astype(BF16), f2inv_ref[...].astype(BF16)
    ginv = ginv_ref[...].astype(BF16)
    scale = 1.0 / (2 * n1h * n2)
    n1 = lax.broadcasted_iota(jnp.int32, (n1h, nb * LANES), 0)
    nyq_w = jnp.where(jnp.bitwise_and(n1, 1) == 0, scale, -scale)
    lanes = lambda parts: parts[0] if len(parts) == 1 else jnp.concatenate(parts, axis=1)
    wide = lambda t, reps: t if reps == 1 else jnp.tile(t, (1, reps))
    lane_block = lambda a, i: a[:, i * LANES:(i + 1) * LANES]

    def repitch(n, carry):
        for part, ref in enumerate((v_ref, x1_ref, x2_ref)):
            for i in range(nb):
                tin_ref[part, i, pl.ds(pl.multiple_of(n * pitch, SUBLANES), n2), :] = (
                    ref[0, i, pl.ds(pl.multiple_of(n * n2, n2), n2), :])
        return carry

    lax.fori_loop(0, n1h, repitch, 0)
    stages = ((tin_ref.at[0], tin_ref.at[1], cur_ref, kr0_ref, ki0_ref),
              (cur_ref, tin_ref.at[2], tin_ref.at[0], kr1_ref, ki1_ref))
    for order, (src, gate, dst, kr_ref, ki_ref) in enumerate(stages):
        skip = wide(skip_ref[order], nb)

        def stage1(j, carry):
            xb = lanes([src[i, pl.ds(j, n1h, stride=pitch), :] for i in range(nb)]).astype(BF16)
            p = _dot(f1, xb)
            twr, twi = wide(twrf_ref[j], nb), wide(twif_ref[j], nb)
            pr, pi, pn = p[:n1h], p[n1h:2 * n1h], p[2 * n1h:2 * n1h + 1]
            re, im = pr * twr[:n1h] - pi * twi[:n1h], pr * twi[:n1h] + pi * twr[:n1h]
            re_n, im_n = pn * twr[n1h:n1h + 1], pn * twi[n1h:n1h + 1]
            for i in range(nb):
                are_ref[i, pl.ds(j, n1h, stride=pitch), :] = lane_block(re, i)
                aim_ref[i, pl.ds(j, n1h, stride=pitch), :] = lane_block(im, i)
                are_ref[i, pl.ds(n1h * pitch + j, 1), :] = lane_block(re_n, i)
                aim_ref[i, pl.ds(n1h * pitch + j, 1), :] = lane_block(im_n, i)
            return carry

        lax.fori_loop(0, n2, stage1, 0, unroll=unroll)

        def slabs(ks):
            rows = [pl.ds(pl.multiple_of(k * pitch, SUBLANES), n2) for k in ks]
            a = _stack_bf16(lanes([are_ref[i, r, :] for r in rows for i in range(nb)]),
                            lanes([aim_ref[i, r, :] for r in rows for i in range(nb)]))
            x = _dot(f2, a)
            xr, xi = x[:n2], x[n2:]
            per_slab = lambda ref: lanes([wide(ref[k], nb) for k in ks])
            kr, ki = per_slab(kr_ref), per_slab(ki_ref)
            y = _dot(f2inv, _stack_bf16(xr * kr - xi * ki, xr * ki + xi * kr))
            br, bi = y[:n2], y[n2:]
            twr, twi = per_slab(twri_ref), per_slab(twii_ref)
            ore, oim = br * twr + bi * twi, bi * twr - br * twi
            for s, r in enumerate(rows):
                for i in range(nb):
                    are_ref[i, r, :] = lane_block(ore, s * nb + i)
                    aim_ref[i, r, :] = lane_block(oim, s * nb + i)

        def slab(k, carry):
            slabs((k,))
            return carry

        lax.fori_loop(0, n1h + 1, slab, 0, unroll=4)

        def stage1_inv(j, carry):
            rows = pl.ds(j, n1h, stride=pitch)
            spec = _stack_bf16(lanes([are_ref[i, rows, :] for i in range(nb)]),
                               lanes([aim_ref[i, rows, :] for i in range(nb)]))
            nyq = lanes([are_ref[i, pl.ds(n1h * pitch + j, 1), :] for i in range(nb)])
            y = _dot(ginv, spec) + nyq_w * nyq
            x = lanes([src[i, rows, :] for i in range(nb)])
            g = lanes([gate[i, rows, :] for i in range(nb)])
            out = g * (y + x * skip)
            for i in range(nb):
                dst[i, rows, :] = lane_block(out, i)
            return carry

        lax.fori_loop(0, n2, stage1_inv, 0, unroll=unroll)

    def unpitch(n, carry):
        for i in range(nb):
            o_ref[i, pl.ds(pl.multiple_of(n * n2, n2), n2), :] = (
                tin_ref[0, i, pl.ds(pl.multiple_of(n * pitch, SUBLANES), n2), :])
        return carry

    lax.fori_loop(0, n1h, unpitch, 0)


def _hyena_fused(plan, uc, kr, ki, skip, *, nb):
    _, b, seq_len, ch = uc.shape
    n1h, n2, kh = plan.n1 // 2, plan.n2, plan.kh
    n_cb = ch // LANES
    part = lambda p: pl.BlockSpec((1, nb, seq_len, LANES), lambda c, bb: (p, bb, 0, c))
    once = pl.Buffered(1)
    filt = lambda o: pl.BlockSpec((kh, n2, LANES), lambda c, bb: (0, 0, o * n_cb + c), pipeline_mode=once)
    full = lambda a: pl.BlockSpec(a.shape, lambda c, bb, _n=a.ndim: (0,) * _n, pipeline_mode=once)
    tables = (plan.f1, plan.twr_fwd, plan.twi_fwd, plan.twr_inv, plan.twi_inv, plan.f2, plan.f2inv, plan.ginv)
    pitch = n2 + SUBLANES
    spec_scratch = pltpu.VMEM((nb, kh * pitch, LANES), F32)
    return pl.pallas_call(
        functools.partial(_hyena_fused_kernel, n1h=n1h, n2=n2, unroll=min(n2, 16)),
        grid=(n_cb, b // nb),
        in_specs=[part(0), part(1), part(2),
                  pl.BlockSpec((skip.shape[0], 1, LANES), lambda c, bb: (0, 0, c)),
                  filt(0), filt(0), filt(1), filt(1)] + [full(a) for a in tables],
        out_specs=pl.BlockSpec((nb, seq_len, LANES), lambda c, bb: (bb, 0, c)),
        out_shape=jax.ShapeDtypeStruct((b, seq_len, ch), F32),
        scratch_shapes=[spec_scratch, spec_scratch, pltpu.VMEM((nb, n1h * pitch, LANES), F32),
                        pltpu.VMEM((3, nb, n1h * pitch, LANES), F32)],
        compiler_params=_params(("parallel", "arbitrary"), 56),
        name="hyena_fused",
    )(uc, uc, uc, skip.reshape(skip.shape[0], 1, ch), kr, ki, kr, ki, *tables)


def _mix_out_kernel(x_ref, yp_ref, yh_ref, gp_ref, gh_ref, w_ref, gpost_ref, o_ref):
    d_pool = yp_ref.shape[1]
    yp = _rms(yp_ref[...], gp_ref[...]).astype(BF16)
    yh = _rms(yh_ref[...], gh_ref[...]).astype(BF16)
    y = _dot(yp, w_ref[:d_pool, :]) + _dot(yh, w_ref[d_pool:, :])
    o_ref[...] = x_ref[...] + _rms(y, gpost_ref[...])


def _mix_out(x, yp, yh, gp, gh, w, gpost, *, tm):
    t, d = x.shape
    dp, dh = yp.shape[1], yh.shape[1]
    const = lambda i: (0, 0)
    row = lambda i: (i, 0)
    return pl.pallas_call(
        _mix_out_kernel,
        grid=(t // tm,),
        in_specs=[
            pl.BlockSpec((tm, d), row),
            pl.BlockSpec((tm, dp), row),
            pl.BlockSpec((tm, dh), row),
            pl.BlockSpec((1, dp), const),
            pl.BlockSpec((1, dh), const),
            pl.BlockSpec((dp + dh, d), const, pipeline_mode=pl.Buffered(1)),
            pl.BlockSpec((1, d), const),
        ],
        out_specs=pl.BlockSpec((tm, d), row),
        out_shape=jax.ShapeDtypeStruct((t, d), F32),
        compiler_params=_params(("parallel",), 40),
        name="mix_out",
    )(x, yp, yh, gp, gh, w, gpost)


def _dft_shape(seq_len):
    n2 = 128
    return 2 * seq_len // n2, n2


def _block_cols(n2, rows, target_bytes=1 << 20):
    g = max(SUBLANES, min(n2, target_bytes // (rows * LANES * 4)) // SUBLANES * SUBLANES)
    while n2 % g:
        g -= SUBLANES
    return g


def _slabs_per_step(kh, slab_bytes, target_bytes=3 << 20):
    ks = max(1, min(kh, target_bytes // slab_bytes))
    while kh % ks:
        ks -= 1
    return ks


def _hyena(plan, uc, filt, skip):
    _, b, seq_len, ch = uc.shape
    n_orders = skip.shape[0]
    g = _block_cols(plan.n2, plan.n1 // 2)
    fre, fim = _dft_fwd(plan, filt.reshape(1, 1, seq_len, filt.shape[1]), 0, g=g)
    kr, ki = _filt_spec(plan, fre, fim, ch)
    nb = 2 if b % 2 == 0 else 1
    fused_bytes = 15 * nb * seq_len * LANES * 4
    if n_orders == 2 and fused_bytes <= FUSED_VMEM_BUDGET:
        return _hyena_fused(plan, uc, kr, ki, skip, nb=nb)
    ks = _slabs_per_step(plan.kh, b * plan.n2 * ch * 4)
    skip3 = skip.reshape(n_orders, 1, ch)
    cur, cur_part = uc, 0
    for order in range(n_orders):
        are, aim = _dft_fwd(plan, cur, cur_part, g=g)
        bre, bim = _conv_mid(plan, are, aim, kr, ki, order, ks=ks)
        cur = _dft_inv(plan, bre, bim, cur, cur_part, uc, order + 1, skip3, order, g=g)
        cur_part = 0
    return cur[0]


def _trunk(x3, p, plan, filt):
    b, seq_len, d = x3.shape
    t = b * seq_len
    x = x3.reshape(t, d)
    x = _ffn(x, p["g1pre"], p["g1post"], p["w1gu"], p["w1d"], tm=512, chunk=p["ff_chunk"])
    z = _mix_in(x, p["gmix"], p["w_in"], tm=512).reshape(b, seq_len, -1)
    d_pool = p["pool_w"].shape[0] * LANES
    rows = min(seq_len, 2048)
    y_pool = _pool(z, p["pool_w"], p["pool_scale"], rows=rows)
    uc = _sconv(z, p["conv_w"], p["conv_b"], rows=rows, col0=d_pool, n_parts=3)
    y_hy = _hyena(plan, uc, filt, p["skip"])
    x = _mix_out(x, y_pool.reshape(t, -1), y_hy.reshape(t, -1), p["gpool"], p["ghy"],
                 p["w_out"], p["gmixpost"], tm=512)
    x = _ffn(x, p["g2pre"], p["g2post"], p["w2gu"], p["w2d"], tm=512, chunk=p["ff_chunk"])
    return x.reshape(b, seq_len, d)


def kernel(x_prompt, x_sample, ffn1_norm_pre, ffn1_norm_post, ffn1_w_gate_up, ffn1_w_down, mix_norm_pre, w_in, pool_w_map, pool_scale, hyena_conv_w, hyena_conv_b, filt_w_first, filt_b_first, filt_w_hidden, filt_b_hidden, filt_w_last, filt_freq, hyena_skip, pool_out_norm, hyena_out_norm, w_out, mix_norm_post, ffn2_norm_pre, ffn2_norm_post, ffn2_w_gate_up, ffn2_w_down):
    assert ffn1_norm_pre.shape[0] == 1, "single-layer trunk"
    row = lambda a: a[0].reshape(1, -1)
    d_ff = ffn1_w_down.shape[1]
    p = dict(
        g1pre=row(ffn1_norm_pre), g1post=row(ffn1_norm_post),
        w1gu=ffn1_w_gate_up[0].astype(BF16), w1d=ffn1_w_down[0].astype(BF16),
        gmix=row(mix_norm_pre), w_in=w_in[0].astype(BF16),
        pool_w=pool_w_map[0].astype(BF16), pool_scale=row(pool_scale),
        conv_w=hyena_conv_w[0], conv_b=row(hyena_conv_b),
        skip=hyena_skip[0], gpool=row(pool_out_norm), ghy=row(hyena_out_norm),
        w_out=w_out[0].astype(BF16), gmixpost=row(mix_norm_post),
        g2pre=row(ffn2_norm_pre), g2post=row(ffn2_norm_post),
        w2gu=ffn2_w_gate_up[0].astype(BF16), w2d=ffn2_w_down[0].astype(BF16),
        ff_chunk=d_ff // 2 if (d_ff // 2) % LANES == 0 else d_ff,
    )
    outs = []
    for x3 in (x_prompt, x_sample):
        seq_len = x3.shape[1]
        plan = _Plan(seq_len, *_dft_shape(seq_len))
        filt = _filt_gen(seq_len, filt_w_first[0], filt_b_first[0], filt_w_hidden[0],
                         filt_b_hidden[0], filt_w_last[0], filt_freq[0], rows=min(seq_len, 512))
        outs.append(_trunk(x3, p, plan, filt))
    return tuple(outs)
```

```python
import functools
import math

import jax
import jax.numpy as jnp
import numpy as np
from jax import lax
from jax.experimental import pallas as pl
from jax.experimental.pallas import tpu as pltpu

F32 = jnp.float32
BF16 = jnp.bfloat16
EPS = 1e-6
LANES = 128
SUBLANES = 8
POOL_WINDOWS = (2, 4, 8, 16)
HALO = 8
DECAY_TARGET = 1e-2
FAST_DECAY_PCT = 0.3
SLOW_DECAY_PCT = 1.5
MAX_DECAY = math.log(DECAY_TARGET) / FAST_DECAY_PCT
MIN_DECAY = math.log(DECAY_TARGET) / SLOW_DECAY_PCT
HIGHEST = lax.Precision.HIGHEST
FUSED_VMEM_BUDGET = 32 << 20


def _params(sem, vmem_mib):
    return pltpu.CompilerParams(dimension_semantics=sem, vmem_limit_bytes=vmem_mib << 20)


def _rms(x, g):
    inv = lax.rsqrt(jnp.mean(x * x, axis=-1, keepdims=True) + EPS)
    return (x * inv) * g


def _dot(a, b):
    return jnp.dot(a, b, preferred_element_type=F32)


def _ffn_kernel(x_ref, gpre_ref, gpost_ref, wgu_ref, wd_ref, o_ref, *, d_ff, chunk):
    x = x_ref[...]
    h = _rms(x, gpre_ref[...]).astype(BF16)
    acc = None
    for j in range(d_ff // chunk):
        lo = j * chunk
        gate = _dot(h, wgu_ref[:, lo:lo + chunk])
        up = _dot(h, wgu_ref[:, d_ff + lo:d_ff + lo + chunk])
        act = (gate * jax.nn.sigmoid(gate) * up).astype(BF16)
        part = _dot(act, wd_ref[lo:lo + chunk, :])
        acc = part if acc is None else acc + part
    o_ref[...] = x + 0.5 * _rms(acc, gpost_ref[...])


def _ffn(x, g_pre, g_post, wgu, wd, *, tm, chunk):
    t, d = x.shape
    d_ff = wd.shape[0]
    const = lambda i: (0, 0)
    return pl.pallas_call(
        functools.partial(_ffn_kernel, d_ff=d_ff, chunk=chunk),
        grid=(t // tm,),
        in_specs=[
            pl.BlockSpec((tm, d), lambda i: (i, 0)),
            pl.BlockSpec((1, d), const),
            pl.BlockSpec((1, d), const),
            pl.BlockSpec((d, 2 * d_ff), const, pipeline_mode=pl.Buffered(1)),
            pl.BlockSpec((d_ff, d), const, pipeline_mode=pl.Buffered(1)),
        ],
        out_specs=pl.BlockSpec((tm, d), lambda i: (i, 0)),
        out_shape=jax.ShapeDtypeStruct((t, d), F32),
        compiler_params=_params(("parallel",), 52),
        name="ffn",
    )(x, g_pre, g_post, wgu, wd)


def _mix_in_kernel(x_ref, g_ref, w_ref, z_ref):
    h = _rms(x_ref[...], g_ref[...]).astype(BF16)
    z_ref[...] = _dot(h, w_ref[...])


def _mix_in(x, g, w, *, tm):
    t, d = x.shape
    d_in = w.shape[1]
    const = lambda i: (0, 0)
    return pl.pallas_call(
        _mix_in_kernel,
        grid=(t // tm,),
        in_specs=[
            pl.BlockSpec((tm, d), lambda i: (i, 0)),
            pl.BlockSpec((1, d), const),
            pl.BlockSpec((d, d_in), const, pipeline_mode=pl.Buffered(1)),
        ],
        out_specs=pl.BlockSpec((tm, d_in), lambda i: (i, 0)),
        out_shape=jax.ShapeDtypeStruct((t, d_in), F32),
        compiler_params=_params(("parallel",), 40),
        name="mix_in",
    )(x, g, w)


def _with_halo(prev_ref, main_ref, next_ref, i, n_tiles):
    prev = jnp.where(i > 0, prev_ref[0], 0.0)
    nxt = jnp.where(i < n_tiles - 1, next_ref[0], 0.0)
    return jnp.concatenate([prev, main_ref[0], nxt], axis=0)


def _pool_kernel(prev_ref, main_ref, next_ref, wmap_ref, scale_ref, o_ref, *, seq_len, n_tiles):
    i = pl.program_id(1)
    c = pl.program_id(2)
    rows = main_ref.shape[1]
    ext = _with_halo(prev_ref, main_ref, next_ref, i, n_tiles)
    n_ext = rows + 2 * HALO
    s2 = ext + pltpu.roll(ext, 1, 0)
    s4 = pltpu.roll(s2, 1, 0) + pltpu.roll(s2, n_ext - 1, 0)
    s8 = pltpu.roll(s4, 2, 0) + pltpu.roll(s4, n_ext - 2, 0)
    s16 = pltpu.roll(s8, 4, 0) + pltpu.roll(s8, n_ext - 4, 0)
    ssum = jnp.where(c == 0, s2, jnp.where(c == 1, s4, jnp.where(c == 2, s8, s16)))
    ssum = ssum[HALO:HALO + rows]
    u = main_ref[0]
    half = jnp.left_shift(1, c)
    pos = i * rows + lax.broadcasted_iota(jnp.int32, u.shape, 0)
    lo = jnp.clip(pos - half, 0, seq_len)
    hi = jnp.clip(pos + half, 0, seq_len)
    cnt = (hi - lo).astype(F32)
    d = ssum / cnt - u
    y = _dot(d.astype(BF16), wmap_ref[0])
    o_ref[0] = y * scale_ref[...]


def _halo_specs(rows, seq_len, col_of):
    blocks_per_tile = rows // HALO
    last = seq_len // HALO - 1
    prev = pl.BlockSpec((1, HALO, LANES),
                        lambda b, i, c: (b, jnp.maximum(i * blocks_per_tile - 1, 0), col_of(c)))
    main = pl.BlockSpec((1, rows, LANES), lambda b, i, c: (b, i, col_of(c)))
    nxt = pl.BlockSpec((1, HALO, LANES),
                       lambda b, i, c: (b, jnp.minimum((i + 1) * blocks_per_tile, last), col_of(c)))
    return [prev, main, nxt]


def _pool(z, wmap, scale, *, rows):
    b, seq_len, _ = z.shape
    n_groups = wmap.shape[0]
    n_tiles = seq_len // rows
    return pl.pallas_call(
        functools.partial(_pool_kernel, seq_len=seq_len, n_tiles=n_tiles),
        grid=(b, n_tiles, n_groups),
        in_specs=_halo_specs(rows, seq_len, lambda c: c) + [
            pl.BlockSpec((1, LANES, LANES), lambda b, i, c: (c, 0, 0)),
            pl.BlockSpec((1, LANES), lambda b, i, c: (0, c)),
        ],
        out_specs=pl.BlockSpec((1, rows, LANES), lambda b, i, c: (b, i, c)),
        out_shape=jax.ShapeDtypeStruct((b, seq_len, n_groups * LANES), F32),
        compiler_params=_params(("parallel", "parallel", "parallel"), 40),
        name="pool",
    )(z, z, z, wmap, scale)


def _sconv_kernel(prev_ref, main_ref, next_ref, w_ref, b_ref, o_ref, *, n_tiles):
    i = pl.program_id(1)
    rows = main_ref.shape[1]
    ext = _with_halo(prev_ref, main_ref, next_ref, i, n_tiles)
    n_ext = rows + 2 * HALO
    before = pltpu.roll(ext, 1, 0)[HALO:HALO + rows]
    after = pltpu.roll(ext, n_ext - 1, 0)[HALO:HALO + rows]
    w = w_ref[...]
    out = before * w[0:1] + main_ref[0] * w[1:2] + after * w[2:3]
    o_ref[0, 0] = out + b_ref[...]


def _sconv(z, w, bias, *, rows, col0, n_parts):
    b, seq_len, _ = z.shape
    n_tiles = seq_len // rows
    blocks_per_part = (w.shape[1] // n_parts) // LANES
    cb0 = col0 // LANES
    return pl.pallas_call(
        functools.partial(_sconv_kernel, n_tiles=n_tiles),
        grid=(b, n_tiles, n_parts * blocks_per_part),
        in_specs=_halo_specs(rows, seq_len, lambda c: c + cb0) + [
            pl.BlockSpec((w.shape[0], LANES), lambda b, i, c: (0, c)),
            pl.BlockSpec((1, LANES), lambda b, i, c: (0, c)),
        ],
        out_specs=pl.BlockSpec((1, 1, rows, LANES),
                               lambda b, i, c: (c // blocks_per_part, b, i, c % blocks_per_part)),
        out_shape=jax.ShapeDtypeStruct((n_parts, b, seq_len, blocks_per_part * LANES), F32),
        compiler_params=_params(("parallel", "parallel", "parallel"), 40),
        name="sconv",
    )(z, z, z, w, bias)


def _filt_gen_kernel(bands_ref, wt_ref, wc_ref, ws_ref, b1_ref, wh_ref, bh_ref, freq_ref,
                     wl_ref, delta_ref, o_ref, *, seq_len, d_ch):
    rows = o_ref.shape[0]
    base = pl.program_id(0) * rows
    m_lane = (base + lax.broadcasted_iota(jnp.int32, (1, rows), 1)).astype(F32)
    t_lane = m_lane / (seq_len - 1.0)
    ang = (bands_ref[...] * (2.0 * math.pi / seq_len)) * m_lane
    freq = freq_ref[...]
    pre = (jnp.dot(wc_ref[...], jnp.cos(ang), precision=HIGHEST, preferred_element_type=F32)
           + jnp.dot(ws_ref[...], -jnp.sin(ang), precision=HIGHEST, preferred_element_type=F32)
           + wt_ref[...] * t_lane + b1_ref[...])
    h = jnp.sin(freq * pre)
    for layer in range(wh_ref.shape[0]):
        pre = jnp.dot(wh_ref[layer], h, precision=HIGHEST, preferred_element_type=F32) + bh_ref[layer]
        h = jnp.sin(freq * pre)
    out = jnp.dot(h.T, wl_ref[...], precision=HIGHEST, preferred_element_type=F32)
    m_row = base + lax.broadcasted_iota(jnp.int32, (rows, d_ch), 0)
    t_row = m_row.astype(F32) / (seq_len - 1.0)
    decay = jnp.exp(-t_row * jnp.abs(delta_ref[...]))
    decay_bwd = jnp.where(m_row == 0, 0.0, decay)
    n_cols = out.shape[1] // d_ch
    for q in range(n_cols):
        dq = decay_bwd if q % 2 == 1 else decay
        o_ref[:, q * d_ch:(q + 1) * d_ch] = out[:, q * d_ch:(q + 1) * d_ch] * dq


def _filt_gen(seq_len, w_first, b_first, w_hidden, b_hidden, w_last, freq, *, rows):
    pos_bands = (w_first.shape[0] - 1) // 2
    hidden = w_first.shape[1]
    n_cols = w_last.shape[1]
    d_ch = n_cols // 4
    bands = jnp.linspace(1e-4, pos_bands - 1, pos_bands, dtype=F32).reshape(pos_bands, 1)
    deltas = jnp.linspace(MIN_DECAY, MAX_DECAY, d_ch, dtype=F32).reshape(1, d_ch)
    w1t = w_first.T
    args = (bands, w1t[:, 0:1], w1t[:, 1:1 + pos_bands], w1t[:, 1 + pos_bands:],
            b_first.reshape(hidden, 1), jnp.swapaxes(w_hidden, 1, 2),
            b_hidden.reshape(b_hidden.shape[0], hidden, 1), freq.reshape(hidden, 1),
            w_last, deltas)
    full = lambda a: pl.BlockSpec(a.shape, lambda i, _n=a.ndim: (0,) * _n)
    return pl.pallas_call(
        functools.partial(_filt_gen_kernel, seq_len=seq_len, d_ch=d_ch),
        grid=(seq_len // rows,),
        in_specs=[full(a) for a in args],
        out_specs=pl.BlockSpec((rows, n_cols), lambda i: (i, 0)),
        out_shape=jax.ShapeDtypeStruct((seq_len, n_cols), F32),
        compiler_params=_params(("parallel",), 40),
        name="filt_gen",
    )(*args)


class _Plan:
    def __init__(self, seq_len, n1, n2):
        assert n1 * n2 == 2 * seq_len and n1 % (2 * SUBLANES) == 0 and n2 % SUBLANES == 0
        self.seq_len, self.n1, self.n2 = seq_len, n1, n2
        n = n1 * n2
        n1h = n1 // 2
        self.kh = kh = n1h + 1
        k1 = np.arange(kh, dtype=np.float64)[:, None]
        m1 = np.arange(n1h, dtype=np.float64)[None, :]
        th1 = 2.0 * np.pi * k1 * m1 / n1
        f1 = np.zeros((n1 + SUBLANES, n1h))
        f1[:n1h] = np.cos(th1[:n1h]); f1[n1h:n1] = -np.sin(th1[:n1h]); f1[n1] = np.cos(th1[n1h])
        self.f1 = jnp.asarray(f1, F32)
        m2 = np.arange(n2, dtype=np.float64)[None, :]
        tht = 2.0 * np.pi * k1 * m2 / n
        twr, twi = np.cos(tht), -np.sin(tht)
        rep = lambda a: np.repeat(a[:, :, None], LANES, axis=2)
        self.twr_inv, self.twi_inv = jnp.asarray(rep(twr), F32), jnp.asarray(rep(twi), F32)
        pad = np.zeros((n2, SUBLANES - 1))
        self.twr_fwd = jnp.asarray(rep(np.concatenate([twr.T, pad], 1)), F32)
        self.twi_fwd = jnp.asarray(rep(np.concatenate([twi.T, pad], 1)), F32)
        k2 = np.arange(n2, dtype=np.float64)[:, None]
        th2 = 2.0 * np.pi * k2 * m2 / n2
        f2r, f2i = np.cos(th2), -np.sin(th2)
        self.f2 = jnp.asarray(np.block([[f2r, -f2i], [f2i, f2r]]), F32)
        self.f2inv = jnp.asarray(np.block([[f2r, f2i], [-f2i, f2r]]), F32)
        wgt = np.where(k1[:n1h] == 0, 1.0, 2.0) / n
        ginv = np.concatenate([(wgt * np.cos(th1[:n1h])).T, (-wgt * np.sin(th1[:n1h])).T], axis=1)
        self.ginv = jnp.asarray(ginv, F32)
        k1f = np.arange(kh, dtype=np.float64)[:, None]
        full = lambda cols: 2.0 * np.pi * k1f * cols[None, :] / n1
        stage1 = lambda th: np.concatenate(
            [np.cos(th[:n1h]), -np.sin(th[:n1h]), np.cos(th[n1h:]), np.zeros((SUBLANES - 1, th.shape[1]))], 0)
        fwd_cols = np.arange(n1h, dtype=np.float64)
        f1a = stage1(full(np.concatenate([fwd_cols, n1 - 1 - fwd_cols])))
        f1b = stage1(full(np.concatenate([fwd_cols, (n1 - fwd_cols) % n1])))
        f1b[:, n1h] = 0.0
        self.f1_filt, self.f1_filt0 = jnp.asarray(f1a, F32), jnp.asarray(f1b, F32)
        rows = lambda t: rep(np.concatenate([t.T, np.zeros((t.shape[1], SUBLANES - 1))], 1))
        coarse, fine = tht[:, ::SUBLANES], tht[:, :SUBLANES]
        self.tw_coarse = (jnp.asarray(rows(np.cos(coarse)), F32), jnp.asarray(rows(-np.sin(coarse)), F32))
        self.tw_fine = (jnp.asarray(rows(np.cos(fine)), F32), jnp.asarray(rows(-np.sin(fine)), F32))


def _dft_fwd_kernel(x_ref, f1_ref, twr_ref, twi_ref, re_ref, im_ref, *, unroll):
    _, _, n1h, g, _ = x_ref.shape
    x2 = x_ref.at[0, 0].reshape(n1h * g, LANES)
    re2 = re_ref.at[0].reshape((n1h + 1) * g, LANES)
    im2 = im_ref.at[0].reshape((n1h + 1) * g, LANES)
    f1 = f1_ref[...].astype(BF16)

    def body(j, carry):
        xb = x2[pl.ds(j, n1h, stride=g), :].astype(BF16)
        p = _dot(f1, xb)
        twr, twi = twr_ref[j], twi_ref[j]
        pr, pi, pn = p[:n1h], p[n1h:2 * n1h], p[2 * n1h:2 * n1h + 1]
        re2[pl.ds(j, n1h, stride=g), :] = pr * twr[:n1h] - pi * twi[:n1h]
        im2[pl.ds(j, n1h, stride=g), :] = pr * twi[:n1h] + pi * twr[:n1h]
        re2[pl.ds(n1h * g + j, 1), :] = pn * twr[n1h:n1h + 1]
        im2[pl.ds(n1h * g + j, 1), :] = pn * twi[n1h:n1h + 1]
        return carry

    lax.fori_loop(0, g, body, 0, unroll=unroll)


def _dft_fwd(plan, x4, part, *, g):
    _, b, seq_len, ch = x4.shape
    n1h, n2, kh = plan.n1 // 2, plan.n2, plan.kh
    xv = x4.reshape(x4.shape[0], b, n1h, n2, ch)
    out = jax.ShapeDtypeStruct((b, kh, n2, ch), F32)
    tw_spec = pl.BlockSpec((g, n1h + SUBLANES, LANES), lambda j, bb, c: (j, 0, 0))
    return pl.pallas_call(
        functools.partial(_dft_fwd_kernel, unroll=min(g, 16)),
        grid=(n2 // g, b, ch // LANES),
        in_specs=[
            pl.BlockSpec((1, 1, n1h, g, LANES), lambda j, bb, c: (part, bb, 0, j, c)),
            pl.BlockSpec(plan.f1.shape, lambda j, bb, c: (0, 0)),
            tw_spec, tw_spec,
        ],
        out_specs=[pl.BlockSpec((1, kh, g, LANES), lambda j, bb, c: (bb, 0, j, c))] * 2,
        out_shape=[out, out],
        compiler_params=_params(("parallel", "parallel", "parallel"), 48),
        name="dft_fwd",
    )(xv, plan.f1, plan.twr_fwd, plan.twi_fwd)


def _stack_bf16(re, im):
    return jnp.concatenate([re.astype(BF16), im.astype(BF16)], axis=0)


def _filt_spec_kernel(hf_ref, hb_ref, f1_ref, f1j0_ref, twcr_ref, twci_ref, twfr_ref, twfi_ref, f2_ref,
                      kr_ref, ki_ref, are_ref, aim_ref, *, n1h, n2, unroll):
    pitch = n2 + SUBLANES
    ks = kr_ref.shape[0]
    kc = pl.program_id(1)

    @pl.when(kc == 0)
    def _():
        def stage1(j, f1, hb_start, a, b):
            x = jnp.concatenate([hf_ref[pl.ds(j, n1h, stride=n2), :],
                                 hb_ref[pl.ds(hb_start, n1h, stride=n2), :]], axis=0)
            p = _dot(f1, x.astype(BF16))
            cr, ci, fr, fi = twcr_ref[a], twci_ref[a], twfr_ref[b], twfi_ref[b]
            twr, twi = cr * fr - ci * fi, cr * fi + ci * fr
            pr, pi, pn = p[:n1h], p[n1h:2 * n1h], p[2 * n1h:2 * n1h + 1]
            are_ref[pl.ds(j, n1h, stride=pitch), :] = pr * twr[:n1h] - pi * twi[:n1h]
            aim_ref[pl.ds(j, n1h, stride=pitch), :] = pr * twi[:n1h] + pi * twr[:n1h]
            are_ref[pl.ds(n1h * pitch + j, 1), :] = pn * twr[n1h:n1h + 1]
            aim_ref[pl.ds(n1h * pitch + j, 1), :] = pn * twi[n1h:n1h + 1]

        stage1(0, f1j0_ref[...].astype(BF16), 0, 0, 0)
        f1 = f1_ref[...].astype(BF16)

        def body(j, carry):
            stage1(j, f1, n2 - j, lax.shift_right_logical(j, 3), jnp.bitwise_and(j, SUBLANES - 1))
            return carry

        lax.fori_loop(1, n2, body, 0, unroll=unroll)

    f2 = f2_ref[...].astype(BF16)

    def slab(s, carry):
        rows = pl.ds(pl.multiple_of((kc * ks + s) * pitch, SUBLANES), n2)
        x = _dot(f2, _stack_bf16(are_ref[rows, :], aim_ref[rows, :]))
        kr_ref[s] = x[:n2]
        ki_ref[s] = x[n2:]
        return carry

    lax.fori_loop(0, ks, slab, 0, unroll=2)


def _filt_spec(plan, filt, n_orders, *, ks):
    seq_len, cols = filt.shape
    n1h, n2, kh = plan.n1 // 2, plan.n2, plan.kh
    n_cb = cols // (2 * n_orders * LANES)
    pitch = n2 + SUBLANES
    once = pl.Buffered(1)
    col = lambda c, back: (c // n_cb) * 2 * n_cb + back * n_cb + c % n_cb
    seq = lambda back: pl.BlockSpec((seq_len, LANES), lambda c, k: (0, col(c, back)), pipeline_mode=once)
    full = lambda a: pl.BlockSpec(a.shape, lambda c, k, _n=a.ndim: (0,) * _n, pipeline_mode=once)
    tables = (plan.f1_filt, plan.f1_filt0, *plan.tw_coarse, *plan.tw_fine, plan.f2)
    out = jax.ShapeDtypeStruct((kh, n2, cols // 2), F32)
    scratch = pltpu.VMEM((kh * pitch, LANES), F32)
    return pl.pallas_call(
        functools.partial(_filt_spec_kernel, n1h=n1h, n2=n2, unroll=min(n2 - 1, 8)),
        grid=(n_orders * n_cb, kh // ks),
        in_specs=[seq(0), seq(1)] + [full(a) for a in tables],
        out_specs=[pl.BlockSpec((ks, n2, LANES), lambda c, k: (k, 0, c))] * 2,
        out_shape=[out, out],
        scratch_shapes=[scratch, scratch],
        compiler_params=_params(("parallel", "arbitrary"), 56),
        name="filt_spec",
    )(filt, filt, *tables)


def _conv_mid_kernel(re_ref, im_ref, kr_ref, ki_ref, twr_ref, twi_ref, f2_ref, f2inv_ref,
                     ore_ref, oim_ref):
    nb, ks, n2, ch = re_ref.shape
    reps = ch // LANES
    f2, f2inv = f2_ref[...].astype(BF16), f2inv_ref[...].astype(BF16)
    for s in range(ks):
        kr, ki = kr_ref[s], ki_ref[s]
        twr = jnp.tile(twr_ref[s], (1, reps))
        twi = jnp.tile(twi_ref[s], (1, reps))

        def body(b, carry):
            x = _dot(f2, _stack_bf16(re_ref[b, s], im_ref[b, s]))
            xr, xi = x[:n2], x[n2:]
            y = _dot(f2inv, _stack_bf16(xr * kr - xi * ki, xr * ki + xi * kr))
            br, bi = y[:n2], y[n2:]
            ore_ref[b, s] = br * twr + bi * twi
            oim_ref[b, s] = bi * twr - br * twi
            return carry

        lax.fori_loop(0, nb, body, 0)


def _conv_mid(plan, are, aim, kr, ki, order, *, ks):
    b, kh, n2, ch = are.shape
    data = pl.BlockSpec((b, ks, n2, ch), lambda k: (0, k, 0, 0))
    filt = pl.BlockSpec((ks, n2, ch), lambda k: (k, 0, order))
    tw = pl.BlockSpec((ks, n2, LANES), lambda k: (k, 0, 0))
    mat = pl.BlockSpec((2 * n2, 2 * n2), lambda k: (0, 0))
    out = jax.ShapeDtypeStruct((b, kh, n2, ch), F32)
    return pl.pallas_call(
        _conv_mid_kernel,
        grid=(kh // ks,),
        in_specs=[data, data, filt, filt, tw, tw, mat, mat],
        out_specs=[data, data],
        out_shape=[out, out],
        compiler_params=_params(("parallel",), 48),
        name="conv_mid",
    )(are, aim, kr, ki, plan.twr_inv, plan.twi_inv, plan.f2, plan.f2inv)


def _dft_inv_kernel(bre_ref, bim_ref, x_ref, gate_ref, skip_ref, ginv_ref, o_ref, *, unroll, scale):
    _, _, n1h, g, _ = o_ref.shape
    b_re = bre_ref.at[0].reshape((n1h + 1) * g, LANES)
    b_im = bim_ref.at[0].reshape((n1h + 1) * g, LANES)
    x2 = x_ref.at[0, 0].reshape(n1h * g, LANES)
    gate2 = gate_ref.at[0, 0].reshape(n1h * g, LANES)
    o2 = o_ref.at[0, 0].reshape(n1h * g, LANES)
    ginv = ginv_ref[...].astype(BF16)
    skip = skip_ref[0]
    n1 = lax.broadcasted_iota(jnp.int32, (n1h, LANES), 0)
    nyq_w = jnp.where(jnp.bitwise_and(n1, 1) == 0, scale, -scale)

    def body(j, carry):
        rows = pl.ds(j, n1h, stride=g)
        spec = _stack_bf16(b_re[rows, :], b_im[rows, :])
        y = _dot(ginv, spec) + nyq_w * b_re[pl.ds(n1h * g + j, 1), :]
        o2[rows, :] = gate2[rows, :] * (y + x2[rows, :] * skip)
        return carry

    lax.fori_loop(0, g, body, 0, unroll=unroll)


def _dft_inv(plan, bre, bim, x4, x_part, gate4, gate_part, skip, order, *, g):
    b, seq_len, ch = x4.shape[1:]
    n1h, n2, kh = plan.n1 // 2, plan.n2, plan.kh
    xv = x4.reshape(x4.shape[0], b, n1h, n2, ch)
    gv = gate4.reshape(gate4.shape[0], b, n1h, n2, ch)
    spec = pl.BlockSpec((1, kh, g, LANES), lambda j, bb, c: (bb, 0, j, c))
    time = lambda part: pl.BlockSpec((1, 1, n1h, g, LANES), lambda j, bb, c: (part, bb, 0, j, c))
    out = pl.pallas_call(
        functools.partial(_dft_inv_kernel, unroll=min(g, 16), scale=1.0 / (plan.n1 * n2)),
        grid=(n2 // g, b, ch // LANES),
        in_specs=[
            spec, spec, time(x_part), time(gate_part),
            pl.BlockSpec((1, 1, LANES), lambda j, bb, c: (order, 0, c)),
            pl.BlockSpec(plan.ginv.shape, lambda j, bb, c: (0, 0)),
        ],
        out_specs=time(0),
        out_shape=jax.ShapeDtypeStruct((1, b, n1h, n2, ch), F32),
        compiler_params=_params(("parallel", "parallel", "parallel"), 48),
        name="dft_inv",
    )(bre, bim, xv, gv, skip, plan.ginv)
    return out.reshape(1, b, seq_len, ch)


def _hyena_fused_kernel(zv_ref, zx1_ref, zx2_ref, wv_ref, wx1_ref, wx2_ref, bv_ref, bx1_ref, bx2_ref,
                        skip_ref, kr0_ref, ki0_ref, kr1_ref, ki1_ref,
                        f1_ref, twrf_ref, twif_ref, twri_ref, twii_ref, f2_ref, f2inv_ref, ginv_ref,
                        o_ref, are_ref, aim_ref, cur_ref, tin_ref, *, n1h, n2, unroll):
    nb = o_ref.shape[0]
    pitch = n2 + SUBLANES
    f1 = f1_ref[...].astype(BF16)
    f2, f2inv = f2_ref[...].astype(BF16), f2inv_ref[...].astype(BF16)
    ginv = ginv_ref[...].astype(BF16)
    scale = 1.0 / (2 * n1h * n2)
    n1 = lax.broadcasted_iota(jnp.int32, (n1h, nb * LANES), 0)
    nyq_w = jnp.where(jnp.bitwise_and(n1, 1) == 0, scale, -scale)
    lanes = lambda parts: parts[0] if len(parts) == 1 else jnp.concatenate(parts, axis=1)
    wide = lambda t, reps: t if reps == 1 else jnp.tile(t, (1, reps))
    lane_block = lambda a, i: a[:, i * LANES:(i + 1) * LANES]

    row = lax.broadcasted_iota(jnp.int32, (n2, LANES), 0)
    for part, (z_ref, w_ref, b_ref) in enumerate(((zv_ref, wv_ref, bv_ref), (zx1_ref, wx1_ref, bx1_ref),
                                                   (zx2_ref, wx2_ref, bx2_ref))):
        w, bias = w_ref[...], b_ref[...]
        for i in range(nb):
            for n in range(n1h):
                lo = n * n2
                mid = z_ref[i, lo:lo + n2, :]
                if n > 0:
                    before = z_ref[i, lo - 1:lo + n2 - 1, :]
                else:
                    before = jnp.where(row == 0, 0.0, pltpu.roll(mid, 1, 0))
                if n < n1h - 1:
                    after = z_ref[i, lo + 1:lo + n2 + 1, :]
                else:
                    after = jnp.where(row == n2 - 1, 0.0, pltpu.roll(mid, n2 - 1, 0))
                tin_ref[part, i, n * pitch:n * pitch + n2, :] = (
                    before * w[0:1] + mid * w[1:2] + after * w[2:3] + bias)
    stages = ((tin_ref.at[0], tin_ref.at[1], cur_ref, kr0_ref, ki0_ref),
              (cur_ref, tin_ref.at[2], tin_ref.at[0], kr1_ref, ki1_ref))
    for order, (src, gate, dst, kr_ref, ki_ref) in enumerate(stages):
        skip = wide(skip_ref[order], nb)

        def stage1(j, carry):
            xb = lanes([src[i, pl.ds(j, n1h, stride=pitch), :] for i in range(nb)]).astype(BF16)
            p = _dot(f1, xb)
            twr, twi = wide(twrf_ref[j], nb), wide(twif_ref[j], nb)
            pr, pi, pn = p[:n1h], p[n1h:2 * n1h], p[2 * n1h:2 * n1h + 1]
            re, im = pr * twr[:n1h] - pi * twi[:n1h], pr * twi[:n1h] + pi * twr[:n1h]
            re_n, im_n = pn * twr[n1h:n1h + 1], pn * twi[n1h:n1h + 1]
            for i in range(nb):
                are_ref[i, pl.ds(j, n1h, stride=pitch), :] = lane_block(re, i)
                aim_ref[i, pl.ds(j, n1h, stride=pitch), :] = lane_block(im, i)
                are_ref[i, pl.ds(n1h * pitch + j, 1), :] = lane_block(re_n, i)
                aim_ref[i, pl.ds(n1h * pitch + j, 1), :] = lane_block(im_n, i)
            return carry

        lax.fori_loop(0, n2, stage1, 0, unroll=unroll)

        def slabs(ks):
            rows = [pl.ds(pl.multiple_of(k * pitch, SUBLANES), n2) for k in ks]
            a = _stack_bf16(lanes([are_ref[i, r, :] for r in rows for i in range(nb)]),
                            lanes([aim_ref[i, r, :] for r in rows for i in range(nb)]))
            x = _dot(f2, a)
            xr, xi = x[:n2], x[n2:]
            per_slab = lambda ref: lanes([wide(ref[k], nb) for k in ks])
            kr, ki = per_slab(kr_ref), per_slab(ki_ref)
            y = _dot(f2inv, _stack_bf16(xr * kr - xi * ki, xr * ki + xi * kr))
            br, bi = y[:n2], y[n2:]
            twr, twi = per_slab(twri_ref), per_slab(twii_ref)
            ore, oim = br * twr + bi * twi, bi * twr - br * twi
            for s, r in enumerate(rows):
                for i in range(nb):
                    are_ref[i, r, :] = lane_block(ore, s * nb + i)
                    aim_ref[i, r, :] = lane_block(oim, s * nb + i)

        def slab(k, carry):
            slabs((k,))
            return carry

        lax.fori_loop(0, n1h + 1, slab, 0, unroll=4)

        def stage1_inv(j, carry):
            rows = pl.ds(j, n1h, stride=pitch)
            spec = _stack_bf16(lanes([are_ref[i, rows, :] for i in range(nb)]),
                               lanes([aim_ref[i, rows, :] for i in range(nb)]))
            nyq = lanes([are_ref[i, pl.ds(n1h * pitch + j, 1), :] for i in range(nb)])
            y = _dot(ginv, spec) + nyq_w * nyq
            x = lanes([src[i, rows, :] for i in range(nb)])
            g = lanes([gate[i, rows, :] for i in range(nb)])
            out = g * (y + x * skip)
            for i in range(nb):
                dst[i, rows, :] = lane_block(out, i)
            return carry

        lax.fori_loop(0, n2, stage1_inv, 0, unroll=unroll)

    def unpitch(n, carry):
        for i in range(nb):
            o_ref[i, pl.ds(pl.multiple_of(n * n2, n2), n2), :] = (
                tin_ref[0, i, pl.ds(pl.multiple_of(n * pitch, SUBLANES), n2), :])
        return carry

    lax.fori_loop(0, n1h, unpitch, 0)


def _hyena_fused(plan, z, col0, conv_w, conv_b, kr, ki, skip, *, nb):
    b, seq_len, _ = z.shape
    ch = skip.shape[1]
    n1h, n2, kh = plan.n1 // 2, plan.n2, plan.kh
    n_cb = ch // LANES
    cb0 = col0 // LANES
    part = lambda p: pl.BlockSpec((nb, seq_len, LANES), lambda c, bb: (bb, 0, cb0 + p * n_cb + c))
    taps = lambda p: pl.BlockSpec((conv_w.shape[0], LANES), lambda c, bb: (0, p * n_cb + c))
    bias = lambda p: pl.BlockSpec((1, LANES), lambda c, bb: (0, p * n_cb + c))
    once = pl.Buffered(1)
    filt = lambda o: pl.BlockSpec((kh, n2, LANES), lambda c, bb: (0, 0, o * n_cb + c), pipeline_mode=once)
    full = lambda a: pl.BlockSpec(a.shape, lambda c, bb, _n=a.ndim: (0,) * _n, pipeline_mode=once)
    tables = (plan.f1, plan.twr_fwd, plan.twi_fwd, plan.twr_inv, plan.twi_inv, plan.f2, plan.f2inv, plan.ginv)
    pitch = n2 + SUBLANES
    spec_scratch = pltpu.VMEM((nb, kh * pitch, LANES), F32)
    return pl.pallas_call(
        functools.partial(_hyena_fused_kernel, n1h=n1h, n2=n2, unroll=min(n2, 16)),
        grid=(n_cb, b // nb),
        in_specs=[part(0), part(1), part(2), taps(0), taps(1), taps(2), bias(0), bias(1), bias(2),
                  pl.BlockSpec((skip.shape[0], 1, LANES), lambda c, bb: (0, 0, c)),
                  filt(0), filt(0), filt(1), filt(1)] + [full(a) for a in tables],
        out_specs=pl.BlockSpec((nb, seq_len, LANES), lambda c, bb: (bb, 0, c)),
        out_shape=jax.ShapeDtypeStruct((b, seq_len, ch), F32),
        scratch_shapes=[spec_scratch, spec_scratch, pltpu.VMEM((nb, n1h * pitch, LANES), F32),
                        pltpu.VMEM((3, nb, n1h * pitch, LANES), F32)],
        compiler_params=_params(("parallel", "arbitrary"), 56),
        name="hyena_fused",
    )(z, z, z, conv_w, conv_w, conv_w, conv_b, conv_b, conv_b,
      skip.reshape(skip.shape[0], 1, ch), kr, ki, kr, ki, *tables)


def _mix_out_kernel(x_ref, yp_ref, yh_ref, gp_ref, gh_ref, w_ref, gpost_ref, o_ref):
    d_pool = yp_ref.shape[1]
    yp = _rms(yp_ref[...], gp_ref[...]).astype(BF16)
    yh = _rms(yh_ref[...], gh_ref[...]).astype(BF16)
    y = _dot(yp, w_ref[:d_pool, :]) + _dot(yh, w_ref[d_pool:, :])
    o_ref[...] = x_ref[...] + _rms(y, gpost_ref[...])


def _mix_out(x, yp, yh, gp, gh, w, gpost, *, tm):
    t, d = x.shape
    dp, dh = yp.shape[1], yh.shape[1]
    const = lambda i: (0, 0)
    row = lambda i: (i, 0)
    return pl.pallas_call(
        _mix_out_kernel,
        grid=(t // tm,),
        in_specs=[
            pl.BlockSpec((tm, d), row),
            pl.BlockSpec((tm, dp), row),
            pl.BlockSpec((tm, dh), row),
            pl.BlockSpec((1, dp), const),
            pl.BlockSpec((1, dh), const),
            pl.BlockSpec((dp + dh, d), const, pipeline_mode=pl.Buffered(1)),
            pl.BlockSpec((1, d), const),
        ],
        out_specs=pl.BlockSpec((tm, d), row),
        out_shape=jax.ShapeDtypeStruct((t, d), F32),
        compiler_params=_params(("parallel",), 40),
        name="mix_out",
    )(x, yp, yh, gp, gh, w, gpost)


def _dft_shape(seq_len):
    n2 = 128
    return 2 * seq_len // n2, n2


def _block_cols(n2, rows, target_bytes=1 << 20):
    g = max(SUBLANES, min(n2, target_bytes // (rows * LANES * 4)) // SUBLANES * SUBLANES)
    while n2 % g:
        g -= SUBLANES
    return g


def _slabs_per_step(kh, slab_bytes, target_bytes=3 << 20):
    ks = max(1, min(kh, target_bytes // slab_bytes))
    while kh % ks:
        ks -= 1
    return ks


def _hyena(plan, z, col0, conv_w, conv_b, filt, skip):
    b, seq_len, _ = z.shape
    n_orders, ch = skip.shape
    g = _block_cols(plan.n2, plan.n1 // 2)
    ks_filt = _slabs_per_step(plan.kh, 2 * plan.n2 * LANES * 4, target_bytes=6 << 20)
    kr, ki = _filt_spec(plan, filt, n_orders, ks=ks_filt)
    nb = 2 if b % 2 == 0 else 1
    fused_bytes = 15 * nb * seq_len * LANES * 4
    if n_orders == 2 and fused_bytes <= FUSED_VMEM_BUDGET:
        return _hyena_fused(plan, z, col0, conv_w, conv_b, kr, ki, skip, nb=nb)
    uc = _sconv(z, conv_w, conv_b, rows=min(seq_len, 2048), col0=col0, n_parts=n_orders + 1)
    ks = _slabs_per_step(plan.kh, b * plan.n2 * ch * 4)
    skip3 = skip.reshape(n_orders, 1, ch)
    cur, cur_part = uc, 0
    for order in range(n_orders):
        are, aim = _dft_fwd(plan, cur, cur_part, g=g)
        bre, bim = _conv_mid(plan, are, aim, kr, ki, order, ks=ks)
        cur = _dft_inv(plan, bre, bim, cur, cur_part, uc, order + 1, skip3, order, g=g)
        cur_part = 0
    return cur[0]


def _trunk(x3, p, plan, filt):
    b, seq_len, d = x3.shape
    t = b * seq_len
    x = x3.reshape(t, d)
    x = _ffn(x, p["g1pre"], p["g1post"], p["w1gu"], p["w1d"], tm=512, chunk=p["ff_chunk"])
    z = _mix_in(x, p["gmix"], p["w_in"], tm=512).reshape(b, seq_len, -1)
    d_pool = p["pool_w"].shape[0] * LANES
    rows = min(seq_len, 2048)
    y_pool = _pool(z, p["pool_w"], p["pool_scale"], rows=rows)
    y_hy = _hyena(plan, z, d_pool, p["conv_w"], p["conv_b"], filt, p["skip"])
    x = _mix_out(x, y_pool.reshape(t, -1), y_hy.reshape(t, -1), p["gpool"], p["ghy"],
                 p["w_out"], p["gmixpost"], tm=512)
    x = _ffn(x, p["g2pre"], p["g2post"], p["w2gu"], p["w2d"], tm=512, chunk=p["ff_chunk"])
    return x.reshape(b, seq_len, d)


def kernel(x_prompt, x_sample, ffn1_norm_pre, ffn1_norm_post, ffn1_w_gate_up, ffn1_w_down, mix_norm_pre, w_in, pool_w_map, pool_scale, hyena_conv_w, hyena_conv_b, filt_w_first, filt_b_first, filt_w_hidden, filt_b_hidden, filt_w_last, filt_freq, hyena_skip, pool_out_norm, hyena_out_norm, w_out, mix_norm_post, ffn2_norm_pre, ffn2_norm_post, ffn2_w_gate_up, ffn2_w_down):
    assert ffn1_norm_pre.shape[0] == 1, "single-layer trunk"
    row = lambda a: a[0].reshape(1, -1)
    d_ff = ffn1_w_down.shape[1]
    p = dict(
        g1pre=row(ffn1_norm_pre), g1post=row(ffn1_norm_post),
        w1gu=ffn1_w_gate_up[0].astype(BF16), w1d=ffn1_w_down[0].astype(BF16),
        gmix=row(mix_norm_pre), w_in=w_in[0].astype(BF16),
        pool_w=pool_w_map[0].astype(BF16), pool_scale=row(pool_scale),
        conv_w=hyena_conv_w[0], conv_b=row(hyena_conv_b),
        skip=hyena_skip[0], gpool=row(pool_out_norm), ghy=row(hyena_out_norm),
        w_out=w_out[0].astype(BF16), gmixpost=row(mix_norm_post),
        g2pre=row(ffn2_norm_pre), g2post=row(ffn2_norm_post),
        w2gu=ffn2_w_gate_up[0].astype(BF16), w2d=ffn2_w_down[0].astype(BF16),
        ff_chunk=d_ff // 2 if (d_ff // 2) % LANES == 0 else d_ff,
    )
    outs = []
    for x3 in (x_prompt, x_sample):
        seq_len = x3.shape[1]
        plan = _Plan(seq_len, *_dft_shape(seq_len))
        filt = _filt_gen(seq_len, filt_w_first[0], filt_b_first[0], filt_w_hidden[0],
                         filt_b_hidden[0], filt_w_last[0], filt_freq[0], rows=min(seq_len, 512))
        outs.append(_trunk(x3, p, plan, filt))
    return tuple(outs)
```

```python
import functools
import math

import jax
import jax.numpy as jnp
import numpy as np
from jax import lax
from jax.experimental import pallas as pl
from jax.experimental.pallas import tpu as pltpu

F32 = jnp.float32
BF16 = jnp.bfloat16
EPS = 1e-6
LANES = 128
SUBLANES = 8
POOL_WINDOWS = (2, 4, 8, 16)
HALO = 8
DECAY_TARGET = 1e-2
FAST_DECAY_PCT = 0.3
SLOW_DECAY_PCT = 1.5
MAX_DECAY = math.log(DECAY_TARGET) / FAST_DECAY_PCT
MIN_DECAY = math.log(DECAY_TARGET) / SLOW_DECAY_PCT
HIGHEST = lax.Precision.HIGHEST
FUSED_VMEM_BUDGET = 32 << 20


def _params(sem, vmem_mib):
    return pltpu.CompilerParams(dimension_semantics=sem, vmem_limit_bytes=vmem_mib << 20)


def _rms(x, g):
    inv = lax.rsqrt(jnp.mean(x * x, axis=-1, keepdims=True) + EPS)
    return (x * inv) * g


def _dot(a, b):
    return jnp.dot(a, b, preferred_element_type=F32)


def _half_step_ffn(x, gpre, gpost, wgu_ref, wd_ref, chunk):
    d_ff = wd_ref.shape[0]
    h = _rms(x, gpre).astype(BF16)
    acc = None
    for j in range(d_ff // chunk):
        lo = j * chunk
        gate = _dot(h, wgu_ref[:, lo:lo + chunk])
        up = _dot(h, wgu_ref[:, d_ff + lo:d_ff + lo + chunk])
        act = (gate * jax.nn.sigmoid(gate) * up).astype(BF16)
        part = _dot(act, wd_ref[lo:lo + chunk, :])
        acc = part if acc is None else acc + part
    return x + 0.5 * _rms(acc, gpost)


def _ffn_in_kernel(x_ref, gpre_ref, gpost_ref, wgu_ref, wd_ref, gmix_ref, win_ref, o_ref, z_ref, *, chunk):
    x = _half_step_ffn(x_ref[...], gpre_ref[...], gpost_ref[...], wgu_ref, wd_ref, chunk)
    o_ref[...] = x
    z_ref[...] = _dot(_rms(x, gmix_ref[...]).astype(BF16), win_ref[...])


def _ffn_in(x, g_pre, g_post, wgu, wd, g_mix, w_in, *, tm, chunk):
    t, d = x.shape
    d_ff, d_in = wd.shape[0], w_in.shape[1]
    const = lambda i: (0, 0)
    row = lambda i: (i, 0)
    once = pl.Buffered(1)
    vec = pl.BlockSpec((1, d), const)
    return pl.pallas_call(
        functools.partial(_ffn_in_kernel, chunk=chunk),
        grid=(t // tm,),
        in_specs=[
            pl.BlockSpec((tm, d), row), vec, vec,
            pl.BlockSpec((d, 2 * d_ff), const, pipeline_mode=once),
            pl.BlockSpec((d_ff, d), const, pipeline_mode=once),
            vec,
            pl.BlockSpec((d, d_in), const, pipeline_mode=once),
        ],
        out_specs=[pl.BlockSpec((tm, d), row), pl.BlockSpec((tm, d_in), row)],
        out_shape=[jax.ShapeDtypeStruct((t, d), F32), jax.ShapeDtypeStruct((t, d_in), F32)],
        compiler_params=_params(("parallel",), 56),
        name="ffn_in",
    )(x, g_pre, g_post, wgu, wd, g_mix, w_in)


def _ffn_out_kernel(x_ref, yp_ref, yh_ref, gp_ref, gh_ref, wout_ref, gmix_ref,
                    gpre_ref, gpost_ref, wgu_ref, wd_ref, o_ref, *, chunk):
    d_pool = yp_ref.shape[1]
    yp = _rms(yp_ref[...], gp_ref[...]).astype(BF16)
    yh = _rms(yh_ref[...], gh_ref[...]).astype(BF16)
    y = _dot(yp, wout_ref[:d_pool, :]) + _dot(yh, wout_ref[d_pool:, :])
    x = x_ref[...] + _rms(y, gmix_ref[...])
    o_ref[...] = _half_step_ffn(x, gpre_ref[...], gpost_ref[...], wgu_ref, wd_ref, chunk)


def _ffn_out(x, yp, yh, gp, gh, w_out, g_mix, g_pre, g_post, wgu, wd, *, tm, chunk):
    t, d = x.shape
    d_ff, dp, dh = wd.shape[0], yp.shape[1], yh.shape[1]
    const = lambda i: (0, 0)
    row = lambda i: (i, 0)
    once = pl.Buffered(1)
    vec = lambda n: pl.BlockSpec((1, n), const)
    return pl.pallas_call(
        functools.partial(_ffn_out_kernel, chunk=chunk),
        grid=(t // tm,),
        in_specs=[
            pl.BlockSpec((tm, d), row), pl.BlockSpec((tm, dp), row), pl.BlockSpec((tm, dh), row),
            vec(dp), vec(dh),
            pl.BlockSpec((dp + dh, d), const, pipeline_mode=once),
            vec(d), vec(d), vec(d),
            pl.BlockSpec((d, 2 * d_ff), const, pipeline_mode=once),
            pl.BlockSpec((d_ff, d), const, pipeline_mode=once),
        ],
        out_specs=pl.BlockSpec((tm, d), row),
        out_shape=jax.ShapeDtypeStruct((t, d), F32),
        compiler_params=_params(("parallel",), 56),
        name="ffn_out",
    )(x, yp, yh, gp, gh, w_out, g_mix, g_pre, g_post, wgu, wd)


def _with_halo(prev_ref, main_ref, next_ref, i, n_tiles):
    prev = jnp.where(i > 0, prev_ref[0], 0.0)
    nxt = jnp.where(i < n_tiles - 1, next_ref[0], 0.0)
    return jnp.concatenate([prev, main_ref[0], nxt], axis=0)


def _pool_kernel(prev_ref, main_ref, next_ref, wmap_ref, scale_ref, o_ref, *, seq_len, n_tiles):
    i = pl.program_id(1)
    c = pl.program_id(2)
    rows = main_ref.shape[1]
    ext = _with_halo(prev_ref, main_ref, next_ref, i, n_tiles)
    n_ext = rows + 2 * HALO
    s2 = ext + pltpu.roll(ext, 1, 0)
    s4 = pltpu.roll(s2, 1, 0) + pltpu.roll(s2, n_ext - 1, 0)
    s8 = pltpu.roll(s4, 2, 0) + pltpu.roll(s4, n_ext - 2, 0)
    s16 = pltpu.roll(s8, 4, 0) + pltpu.roll(s8, n_ext - 4, 0)
    ssum = jnp.where(c == 0, s2, jnp.where(c == 1, s4, jnp.where(c == 2, s8, s16)))
    ssum = ssum[HALO:HALO + rows]
    u = main_ref[0]
    half = jnp.left_shift(1, c)
    pos = i * rows + lax.broadcasted_iota(jnp.int32, u.shape, 0)
    lo = jnp.clip(pos - half, 0, seq_len)
    hi = jnp.clip(pos + half, 0, seq_len)
    cnt = (hi - lo).astype(F32)
    d = ssum / cnt - u
    y = _dot(d.astype(BF16), wmap_ref[0])
    o_ref[0] = y * scale_ref[...]


def _halo_specs(rows, seq_len, col_of):
    blocks_per_tile = rows // HALO
    last = seq_len // HALO - 1
    prev = pl.BlockSpec((1, HALO, LANES),
                        lambda b, i, c: (b, jnp.maximum(i * blocks_per_tile - 1, 0), col_of(c)))
    main = pl.BlockSpec((1, rows, LANES), lambda b, i, c: (b, i, col_of(c)))
    nxt = pl.BlockSpec((1, HALO, LANES),
                       lambda b, i, c: (b, jnp.minimum((i + 1) * blocks_per_tile, last), col_of(c)))
    return [prev, main, nxt]


def _pool(z, wmap, scale, *, rows):
    b, seq_len, _ = z.shape
    n_groups = wmap.shape[0]
    n_tiles = seq_len // rows
    return pl.pallas_call(
        functools.partial(_pool_kernel, seq_len=seq_len, n_tiles=n_tiles),
        grid=(b, n_tiles, n_groups),
        in_specs=_halo_specs(rows, seq_len, lambda c: c) + [
            pl.BlockSpec((1, LANES, LANES), lambda b, i, c: (c, 0, 0)),
            pl.BlockSpec((1, LANES), lambda b, i, c: (0, c)),
        ],
        out_specs=pl.BlockSpec((1, rows, LANES), lambda b, i, c: (b, i, c)),
        out_shape=jax.ShapeDtypeStruct((b, seq_len, n_groups * LANES), F32),
        compiler_params=_params(("parallel", "parallel", "parallel"), 40),
        name="pool",
    )(z, z, z, wmap, scale)


def _sconv_kernel(prev_ref, main_ref, next_ref, w_ref, b_ref, o_ref, *, n_tiles):
    i = pl.program_id(1)
    rows = main_ref.shape[1]
    ext = _with_halo(prev_ref, main_ref, next_ref, i, n_tiles)
    n_ext = rows + 2 * HALO
    before = pltpu.roll(ext, 1, 0)[HALO:HALO + rows]
    after = pltpu.roll(ext, n_ext - 1, 0)[HALO:HALO + rows]
    w = w_ref[...]
    out = before * w[0:1] + main_ref[0] * w[1:2] + after * w[2:3]
    o_ref[0, 0] = out + b_ref[...]


def _sconv(z, w, bias, *, rows, col0, n_parts):
    b, seq_len, _ = z.shape
    n_tiles = seq_len // rows
    blocks_per_part = (w.shape[1] // n_parts) // LANES
    cb0 = col0 // LANES
    return pl.pallas_call(
        functools.partial(_sconv_kernel, n_tiles=n_tiles),
        grid=(b, n_tiles, n_parts * blocks_per_part),
        in_specs=_halo_specs(rows, seq_len, lambda c: c + cb0) + [
            pl.BlockSpec((w.shape[0], LANES), lambda b, i, c: (0, c)),
            pl.BlockSpec((1, LANES), lambda b, i, c: (0, c)),
        ],
        out_specs=pl.BlockSpec((1, 1, rows, LANES),
                               lambda b, i, c: (c // blocks_per_part, b, i, c % blocks_per_part)),
        out_shape=jax.ShapeDtypeStruct((n_parts, b, seq_len, blocks_per_part * LANES), F32),
        compiler_params=_params(("parallel", "parallel", "parallel"), 40),
        name="sconv",
    )(z, z, z, w, bias)


def _filt_gen_kernel(bands_ref, wt_ref, wc_ref, ws_ref, b1_ref, wh_ref, bh_ref, freq_ref,
                     wl_ref, delta_ref, o_ref, *, seq_len, d_ch):
    rows = o_ref.shape[0] * (o_ref.shape[1] - SUBLANES)
    base = pl.program_id(0) * rows
    m_lane = (base + lax.broadcasted_iota(jnp.int32, (1, rows), 1)).astype(F32)
    t_lane = m_lane / (seq_len - 1.0)
    ang = (bands_ref[...] * (2.0 * math.pi / seq_len)) * m_lane
    freq = freq_ref[...]
    pre = (jnp.dot(wc_ref[...], jnp.cos(ang), precision=HIGHEST, preferred_element_type=F32)
           + jnp.dot(ws_ref[...], -jnp.sin(ang), precision=HIGHEST, preferred_element_type=F32)
           + wt_ref[...] * t_lane + b1_ref[...])
    h = jnp.sin(freq * pre)
    for layer in range(wh_ref.shape[0]):
        pre = jnp.dot(wh_ref[layer], h, precision=HIGHEST, preferred_element_type=F32) + bh_ref[layer]
        h = jnp.sin(freq * pre)
    out = jnp.dot(h.T, wl_ref[...], precision=HIGHEST, preferred_element_type=F32)
    m_row = base + lax.broadcasted_iota(jnp.int32, (rows, d_ch), 0)
    t_row = m_row.astype(F32) / (seq_len - 1.0)
    decay = jnp.exp(-t_row * jnp.abs(delta_ref[...]))
    decay_bwd = jnp.where(m_row == 0, 0.0, decay)
    n_chunks, pitch, _ = o_ref.shape
    n2 = rows // n_chunks
    for q in range(out.shape[1] // d_ch):
        dq = decay_bwd if q % 2 == 1 else decay
        val = out[:, q * d_ch:(q + 1) * d_ch] * dq
        for n in range(n_chunks):
            o_ref[n, :n2, q * d_ch:(q + 1) * d_ch] = val[n * n2:(n + 1) * n2]
    o_ref[:, n2:, :] = jnp.zeros((n_chunks, pitch - n2, o_ref.shape[2]), F32)


def _filt_gen(seq_len, w_first, b_first, w_hidden, b_hidden, w_last, freq, *, rows, n2):
    pos_bands = (w_first.shape[0] - 1) // 2
    hidden = w_first.shape[1]
    n_cols = w_last.shape[1]
    d_ch = n_cols // 4
    bands = jnp.linspace(1e-4, pos_bands - 1, pos_bands, dtype=F32).reshape(pos_bands, 1)
    deltas = jnp.linspace(MIN_DECAY, MAX_DECAY, d_ch, dtype=F32).reshape(1, d_ch)
    w1t = w_first.T
    args = (bands, w1t[:, 0:1], w1t[:, 1:1 + pos_bands], w1t[:, 1 + pos_bands:],
            b_first.reshape(hidden, 1), jnp.swapaxes(w_hidden, 1, 2),
            b_hidden.reshape(b_hidden.shape[0], hidden, 1), freq.reshape(hidden, 1),
            w_last, deltas)
    full = lambda a: pl.BlockSpec(a.shape, lambda i, _n=a.ndim: (0,) * _n)
    return pl.pallas_call(
        functools.partial(_filt_gen_kernel, seq_len=seq_len, d_ch=d_ch),
        grid=(seq_len // rows,),
        in_specs=[full(a) for a in args],
        out_specs=pl.BlockSpec((rows // n2, n2 + SUBLANES, n_cols), lambda i: (i, 0, 0)),
        out_shape=jax.ShapeDtypeStruct((seq_len // n2, n2 + SUBLANES, n_cols), F32),
        compiler_params=_params(("parallel",), 40),
        name="filt_gen",
    )(*args)


class _Plan:
    def __init__(self, seq_len, n1, n2):
        assert n1 * n2 == 2 * seq_len and n1 % (2 * SUBLANES) == 0 and n2 % SUBLANES == 0
        self.seq_len, self.n1, self.n2 = seq_len, n1, n2
        n = n1 * n2
        n1h = n1 // 2
        self.kh = kh = n1h + 1
        k1 = np.arange(kh, dtype=np.float64)[:, None]
        m1 = np.arange(n1h, dtype=np.float64)[None, :]
        th1 = 2.0 * np.pi * k1 * m1 / n1
        f1 = np.zeros((n1 + SUBLANES, n1h))
        f1[:n1h] = np.cos(th1[:n1h]); f1[n1h:n1] = -np.sin(th1[:n1h]); f1[n1] = np.cos(th1[n1h])
        self.f1 = jnp.asarray(f1, F32)
        m2 = np.arange(n2, dtype=np.float64)[None, :]
        tht = 2.0 * np.pi * k1 * m2 / n
        twr, twi = np.cos(tht), -np.sin(tht)
        rep = lambda a: np.repeat(a[:, :, None], LANES, axis=2)
        self.twr_inv, self.twi_inv = jnp.asarray(rep(twr), F32), jnp.asarray(rep(twi), F32)
        pad = np.zeros((n2, SUBLANES - 1))
        self.twr_fwd = jnp.asarray(rep(np.concatenate([twr.T, pad], 1)), F32)
        self.twi_fwd = jnp.asarray(rep(np.concatenate([twi.T, pad], 1)), F32)
        k2 = np.arange(n2, dtype=np.float64)[:, None]
        th2 = 2.0 * np.pi * k2 * m2 / n2
        f2r, f2i = np.cos(th2), -np.sin(th2)
        self.f2 = jnp.asarray(np.block([[f2r, -f2i], [f2i, f2r]]), F32)
        self.f2inv = jnp.asarray(np.block([[f2r, f2i], [-f2i, f2r]]), F32)
        wgt = np.where(k1[:n1h] == 0, 1.0, 2.0) / n
        ginv = np.concatenate([(wgt * np.cos(th1[:n1h])).T, (-wgt * np.sin(th1[:n1h])).T], axis=1)
        self.ginv = jnp.asarray(ginv, F32)
        k1f = np.arange(kh, dtype=np.float64)[:, None]
        full = lambda cols: 2.0 * np.pi * k1f * cols[None, :] / n1
        stage1 = lambda th: np.concatenate(
            [np.cos(th[:n1h]), -np.sin(th[:n1h]), np.cos(th[n1h:]), np.zeros((SUBLANES - 1, th.shape[1]))], 0)
        fwd_cols = np.arange(n1h, dtype=np.float64)
        f1a = stage1(full(np.concatenate([fwd_cols, n1 - 1 - fwd_cols])))
        f1b = stage1(full(np.concatenate([fwd_cols, (n1 - fwd_cols) % n1])))
        f1b[:, n1h] = 0.0
        self.f1_filt, self.f1_filt0 = jnp.asarray(f1a, F32), jnp.asarray(f1b, F32)
        rows = lambda t: rep(np.concatenate([t.T, np.zeros((t.shape[1], SUBLANES - 1))], 1))
        coarse, fine = tht[:, ::SUBLANES], tht[:, :SUBLANES]
        self.tw_coarse = (jnp.asarray(rows(np.cos(coarse)), F32), jnp.asarray(rows(-np.sin(coarse)), F32))
        self.tw_fine = (jnp.asarray(rows(np.cos(fine)), F32), jnp.asarray(rows(-np.sin(fine)), F32))


def _dft_fwd_kernel(x_ref, f1_ref, twr_ref, twi_ref, re_ref, im_ref, *, unroll):
    _, _, n1h, g, _ = x_ref.shape
    x2 = x_ref.at[0, 0].reshape(n1h * g, LANES)
    re2 = re_ref.at[0].reshape((n1h + 1) * g, LANES)
    im2 = im_ref.at[0].reshape((n1h + 1) * g, LANES)
    f1 = f1_ref[...].astype(BF16)

    def body(j, carry):
        xb = x2[pl.ds(j, n1h, stride=g), :].astype(BF16)
        p = _dot(f1, xb)
        twr, twi = twr_ref[j], twi_ref[j]
        pr, pi, pn = p[:n1h], p[n1h:2 * n1h], p[2 * n1h:2 * n1h + 1]
        re2[pl.ds(j, n1h, stride=g), :] = pr * twr[:n1h] - pi * twi[:n1h]
        im2[pl.ds(j, n1h, stride=g), :] = pr * twi[:n1h] + pi * twr[:n1h]
        re2[pl.ds(n1h * g + j, 1), :] = pn * twr[n1h:n1h + 1]
        im2[pl.ds(n1h * g + j, 1), :] = pn * twi[n1h:n1h + 1]
        return carry

    lax.fori_loop(0, g, body, 0, unroll=unroll)


def _dft_fwd(plan, x4, part, *, g):
    _, b, seq_len, ch = x4.shape
    n1h, n2, kh = plan.n1 // 2, plan.n2, plan.kh
    xv = x4.reshape(x4.shape[0], b, n1h, n2, ch)
    out = jax.ShapeDtypeStruct((b, kh, n2, ch), F32)
    tw_spec = pl.BlockSpec((g, n1h + SUBLANES, LANES), lambda j, bb, c: (j, 0, 0))
    return pl.pallas_call(
        functools.partial(_dft_fwd_kernel, unroll=min(g, 16)),
        grid=(n2 // g, b, ch // LANES),
        in_specs=[
            pl.BlockSpec((1, 1, n1h, g, LANES), lambda j, bb, c: (part, bb, 0, j, c)),
            pl.BlockSpec(plan.f1.shape, lambda j, bb, c: (0, 0)),
            tw_spec, tw_spec,
        ],
        out_specs=[pl.BlockSpec((1, kh, g, LANES), lambda j, bb, c: (bb, 0, j, c))] * 2,
        out_shape=[out, out],
        compiler_params=_params(("parallel", "parallel", "parallel"), 48),
        name="dft_fwd",
    )(xv, plan.f1, plan.twr_fwd, plan.twi_fwd)


def _stack_bf16(re, im):
    return jnp.concatenate([re.astype(BF16), im.astype(BF16)], axis=0)


def _filt_spec_kernel(hf_ref, hb_ref, f1_ref, f1j0_ref, twcr_ref, twci_ref, twfr_ref, twfi_ref, f2_ref,
                      kr_ref, ki_ref, are_ref, aim_ref, *, n1h, n2, unroll):
    pitch = n2 + SUBLANES
    ks = kr_ref.shape[0]
    kc = pl.program_id(1)

    @pl.when(kc == 0)
    def _():
        def stage1(j, f1, hb_start, a, b):
            x = jnp.concatenate([hf_ref[pl.ds(j, n1h, stride=pitch), :],
                                 hb_ref[pl.ds(hb_start, n1h, stride=pitch), :]], axis=0)
            p = _dot(f1, x.astype(BF16))
            cr, ci, fr, fi = twcr_ref[a], twci_ref[a], twfr_ref[b], twfi_ref[b]
            twr, twi = cr * fr - ci * fi, cr * fi + ci * fr
            pr, pi, pn = p[:n1h], p[n1h:2 * n1h], p[2 * n1h:2 * n1h + 1]
            are_ref[pl.ds(j, n1h, stride=pitch), :] = pr * twr[:n1h] - pi * twi[:n1h]
            aim_ref[pl.ds(j, n1h, stride=pitch), :] = pr * twi[:n1h] + pi * twr[:n1h]
            are_ref[pl.ds(n1h * pitch + j, 1), :] = pn * twr[n1h:n1h + 1]
            aim_ref[pl.ds(n1h * pitch + j, 1), :] = pn * twi[n1h:n1h + 1]

        stage1(0, f1j0_ref[...].astype(BF16), 0, 0, 0)
        f1 = f1_ref[...].astype(BF16)

        def body(j, carry):
            stage1(j, f1, n2 - j, lax.shift_right_logical(j, 3), jnp.bitwise_and(j, SUBLANES - 1))
            return carry

        lax.fori_loop(1, n2, body, 0, unroll=unroll)

    f2 = f2_ref[...].astype(BF16)

    def slabs(first, count):
        rows = [pl.ds(pl.multiple_of((kc * ks + first + i) * pitch, SUBLANES), n2) for i in range(count)]
        lanes = lambda parts: parts[0] if count == 1 else jnp.concatenate(parts, axis=1)
        x = _dot(f2, _stack_bf16(lanes([are_ref[r, :] for r in rows]), lanes([aim_ref[r, :] for r in rows])))
        for i in range(count):
            kr_ref[first + i] = x[:n2, i * LANES:(i + 1) * LANES]
            ki_ref[first + i] = x[n2:, i * LANES:(i + 1) * LANES]

    def pair(p, carry):
        slabs(2 * p, 2)
        return carry

    lax.fori_loop(0, ks // 2, pair, 0, unroll=2)
    if ks % 2:
        slabs(ks - 1, 1)


def _filt_spec(plan, filt, n_orders, *, ks):
    n1h, n2, kh = plan.n1 // 2, plan.n2, plan.kh
    cols = filt.shape[2]
    filt = filt.reshape(n1h * (n2 + SUBLANES), cols)
    n_cb = cols // (2 * n_orders * LANES)
    pitch = n2 + SUBLANES
    once = pl.Buffered(1)
    col = lambda c, back: (c // n_cb) * 2 * n_cb + back * n_cb + c % n_cb
    seq = lambda back: pl.BlockSpec((n1h * pitch, LANES), lambda c, k: (0, col(c, back)), pipeline_mode=once)
    full = lambda a: pl.BlockSpec(a.shape, lambda c, k, _n=a.ndim: (0,) * _n, pipeline_mode=once)
    tables = (plan.f1_filt, plan.f1_filt0, *plan.tw_coarse, *plan.tw_fine, plan.f2)
    out = jax.ShapeDtypeStruct((kh, n2, cols // 2), F32)
    scratch = pltpu.VMEM((kh * pitch, LANES), F32)
    return pl.pallas_call(
        functools.partial(_filt_spec_kernel, n1h=n1h, n2=n2, unroll=min(n2 - 1, 8)),
        grid=(n_orders * n_cb, kh // ks),
        in_specs=[seq(0), seq(1)] + [full(a) for a in tables],
        out_specs=[pl.BlockSpec((ks, n2, LANES), lambda c, k: (k, 0, c))] * 2,
        out_shape=[out, out],
        scratch_shapes=[scratch, scratch],
        compiler_params=_params(("parallel", "arbitrary"), 56),
        name="filt_spec",
    )(filt, filt, *tables)


def _conv_mid_kernel(re_ref, im_ref, kr_ref, ki_ref, twr_ref, twi_ref, f2_ref, f2inv_ref,
                     ore_ref, oim_ref):
    nb, ks, n2, ch = re_ref.shape
    reps = ch // LANES
    f2, f2inv = f2_ref[...].astype(BF16), f2inv_ref[...].astype(BF16)
    for s in range(ks):
        kr, ki = kr_ref[s], ki_ref[s]
        twr = jnp.tile(twr_ref[s], (1, reps))
        twi = jnp.tile(twi_ref[s], (1, reps))

        def body(b, carry):
            x = _dot(f2, _stack_bf16(re_ref[b, s], im_ref[b, s]))
            xr, xi = x[:n2], x[n2:]
            y = _dot(f2inv, _stack_bf16(xr * kr - xi * ki, xr * ki + xi * kr))
            br, bi = y[:n2], y[n2:]
            ore_ref[b, s] = br * twr + bi * twi
            oim_ref[b, s] = bi * twr - br * twi
            return carry

        lax.fori_loop(0, nb, body, 0)


def _conv_mid(plan, are, aim, kr, ki, order, *, ks):
    b, kh, n2, ch = are.shape
    data = pl.BlockSpec((b, ks, n2, ch), lambda k: (0, k, 0, 0))
    filt = pl.BlockSpec((ks, n2, ch), lambda k: (k, 0, order))
    tw = pl.BlockSpec((ks, n2, LANES), lambda k: (k, 0, 0))
    mat = pl.BlockSpec((2 * n2, 2 * n2), lambda k: (0, 0))
    out = jax.ShapeDtypeStruct((b, kh, n2, ch), F32)
    return pl.pallas_call(
        _conv_mid_kernel,
        grid=(kh // ks,),
        in_specs=[data, data, filt, filt, tw, tw, mat, mat],
        out_specs=[data, data],
        out_shape=[out, out],
        compiler_params=_params(("parallel",), 48),
        name="conv_mid",
    )(are, aim, kr, ki, plan.twr_inv, plan.twi_inv, plan.f2, plan.f2inv)


def _dft_inv_kernel(bre_ref, bim_ref, x_ref, gate_ref, skip_ref, ginv_ref, o_ref, *, unroll, scale):
    _, _, n1h, g, _ = o_ref.shape
    b_re = bre_ref.at[0].reshape((n1h + 1) * g, LANES)
    b_im = bim_ref.at[0].reshape((n1h + 1) * g, LANES)
    x2 = x_ref.at[0, 0].reshape(n1h * g, LANES)
    gate2 = gate_ref.at[0, 0].reshape(n1h * g, LANES)
    o2 = o_ref.at[0, 0].reshape(n1h * g, LANES)
    ginv = ginv_ref[...].astype(BF16)
    skip = skip_ref[0]
    n1 = lax.broadcasted_iota(jnp.int32, (n1h, LANES), 0)
    nyq_w = jnp.where(jnp.bitwise_and(n1, 1) == 0, scale, -scale)

    def body(j, carry):
        rows = pl.ds(j, n1h, stride=g)
        spec = _stack_bf16(b_re[rows, :], b_im[rows, :])
        y = _dot(ginv, spec) + nyq_w * b_re[pl.ds(n1h * g + j, 1), :]
        o2[rows, :] = gate2[rows, :] * (y + x2[rows, :] * skip)
        return carry

    lax.fori_loop(0, g, body, 0, unroll=unroll)


def _dft_inv(plan, bre, bim, x4, x_part, gate4, gate_part, skip, order, *, g):
    b, seq_len, ch = x4.shape[1:]
    n1h, n2, kh = plan.n1 // 2, plan.n2, plan.kh
    xv = x4.reshape(x4.shape[0], b, n1h, n2, ch)
    gv = gate4.reshape(gate4.shape[0], b, n1h, n2, ch)
    spec = pl.BlockSpec((1, kh, g, LANES), lambda j, bb, c: (bb, 0, j, c))
    time = lambda part: pl.BlockSpec((1, 1, n1h, g, LANES), lambda j, bb, c: (part, bb, 0, j, c))
    out = pl.pallas_call(
        functools.partial(_dft_inv_kernel, unroll=min(g, 16), scale=1.0 / (plan.n1 * n2)),
        grid=(n2 // g, b, ch // LANES),
        in_specs=[
            spec, spec, time(x_part), time(gate_part),
            pl.BlockSpec((1, 1, LANES), lambda j, bb, c: (order, 0, c)),
            pl.BlockSpec(plan.ginv.shape, lambda j, bb, c: (0, 0)),
        ],
        out_specs=time(0),
        out_shape=jax.ShapeDtypeStruct((1, b, n1h, n2, ch), F32),
        compiler_params=_params(("parallel", "parallel", "parallel"), 48),
        name="dft_inv",
    )(bre, bim, xv, gv, skip, plan.ginv)
    return out.reshape(1, b, seq_len, ch)


def _hyena_fused_kernel(zv_ref, zx1_ref, zx2_ref, wv_ref, wx1_ref, wx2_ref, bv_ref, bx1_ref, bx2_ref,
                        skip_ref, kr0_ref, ki0_ref, kr1_ref, ki1_ref,
                        f1_ref, twrf_ref, twif_ref, twri_ref, twii_ref, f2_ref, f2inv_ref, ginv_ref,
                        o_ref, are_ref, aim_ref, cur_ref, tin_ref, *, n1h, n2, unroll):
    nb = o_ref.shape[0]
    pitch = n2 + SUBLANES
    f1 = f1_ref[...].astype(BF16)
    f2, f2inv = f2_ref[...].astype(BF16), f2inv_ref[...].astype(BF16)
    ginv = ginv_ref[...].astype(BF16)
    scale = 1.0 / (2 * n1h * n2)
    n1 = lax.broadcasted_iota(jnp.int32, (n1h, nb * LANES), 0)
    nyq_w = jnp.where(jnp.bitwise_and(n1, 1) == 0, scale, -scale)
    lanes = lambda parts: parts[0] if len(parts) == 1 else jnp.concatenate(parts, axis=1)
    wide = lambda t, reps: t if reps == 1 else jnp.tile(t, (1, reps))
    lane_block = lambda a, i: a[:, i * LANES:(i + 1) * LANES]

    row = lax.broadcasted_iota(jnp.int32, (n2, LANES), 0)
    for part, (z_ref, w_ref, b_ref) in enumerate(((zv_ref, wv_ref, bv_ref), (zx1_ref, wx1_ref, bx1_ref),
                                                   (zx2_ref, wx2_ref, bx2_ref))):
        w, bias = w_ref[...], b_ref[...]
        for i in range(nb):
            for n in range(n1h):
                lo = n * n2
                mid = z_ref[i, lo:lo + n2, :]
                if n > 0:
                    before = z_ref[i, lo - 1:lo + n2 - 1, :]
                else:
                    before = jnp.where(row == 0, 0.0, pltpu.roll(mid, 1, 0))
                if n < n1h - 1:
                    after = z_ref[i, lo + 1:lo + n2 + 1, :]
                else:
                    after = jnp.where(row == n2 - 1, 0.0, pltpu.roll(mid, n2 - 1, 0))
                tin_ref[part, i, n * pitch:n * pitch + n2, :] = (
                    before * w[0:1] + mid * w[1:2] + after * w[2:3] + bias)
    stages = ((tin_ref.at[0], tin_ref.at[1], cur_ref, kr0_ref, ki0_ref),
              (cur_ref, tin_ref.at[2], tin_ref.at[0], kr1_ref, ki1_ref))
    for order, (src, gate, dst, kr_ref, ki_ref) in enumerate(stages):
        skip = wide(skip_ref[order], nb)

        def stage1(j, carry):
            xb = lanes([src[i, pl.ds(j, n1h, stride=pitch), :] for i in range(nb)]).astype(BF16)
            p = _dot(f1, xb)
            twr, twi = wide(twrf_ref[j], nb), wide(twif_ref[j], nb)
            pr, pi, pn = p[:n1h], p[n1h:2 * n1h], p[2 * n1h:2 * n1h + 1]
            re, im = pr * twr[:n1h] - pi * twi[:n1h], pr * twi[:n1h] + pi * twr[:n1h]
            re_n, im_n = pn * twr[n1h:n1h + 1], pn * twi[n1h:n1h + 1]
            for i in range(nb):
                are_ref[i, pl.ds(j, n1h, stride=pitch), :] = lane_block(re, i)
                aim_ref[i, pl.ds(j, n1h, stride=pitch), :] = lane_block(im, i)
                are_ref[i, pl.ds(n1h * pitch + j, 1), :] = lane_block(re_n, i)
                aim_ref[i, pl.ds(n1h * pitch + j, 1), :] = lane_block(im_n, i)
            return carry

        lax.fori_loop(0, n2, stage1, 0, unroll=unroll)

        def slabs(ks):
            rows = [pl.ds(pl.multiple_of(k * pitch, SUBLANES), n2) for k in ks]
            a = _stack_bf16(lanes([are_ref[i, r, :] for r in rows for i in range(nb)]),
                            lanes([aim_ref[i, r, :] for r in rows for i in range(nb)]))
            x = _dot(f2, a)
            xr, xi = x[:n2], x[n2:]
            per_slab = lambda ref: lanes([wide(ref[k], nb) for k in ks])
            kr, ki = per_slab(kr_ref), per_slab(ki_ref)
            y = _dot(f2inv, _stack_bf16(xr * kr - xi * ki, xr * ki + xi * kr))
            br, bi = y[:n2], y[n2:]
            twr, twi = per_slab(twri_ref), per_slab(twii_ref)
            ore, oim = br * twr + bi * twi, bi * twr - br * twi
            for s, r in enumerate(rows):
                for i in range(nb):
                    are_ref[i, r, :] = lane_block(ore, s * nb + i)
                    aim_ref[i, r, :] = lane_block(oim, s * nb + i)

        def slab(k, carry):
            slabs((k,))
            return carry

        lax.fori_loop(0, n1h + 1, slab, 0, unroll=4)

        def stage1_inv(j, carry):
            rows = pl.ds(j, n1h, stride=pitch)
            spec = _stack_bf16(lanes([are_ref[i, rows, :] for i in range(nb)]),
                               lanes([aim_ref[i, rows, :] for i in range(nb)]))
            nyq = lanes([are_ref[i, pl.ds(n1h * pitch + j, 1), :] for i in range(nb)])
            y = _dot(ginv, spec) + nyq_w * nyq
            x = lanes([src[i, rows, :] for i in range(nb)])
            g = lanes([gate[i, rows, :] for i in range(nb)])
            out = g * (y + x * skip)
            for i in range(nb):
                dst[i, rows, :] = lane_block(out, i)
            return carry

        lax.fori_loop(0, n2, stage1_inv, 0, unroll=unroll)

    def unpitch(n, carry):
        for i in range(nb):
            o_ref[i, pl.ds(pl.multiple_of(n * n2, n2), n2), :] = (
                tin_ref[0, i, pl.ds(pl.multiple_of(n * pitch, SUBLANES), n2), :])
        return carry

    lax.fori_loop(0, n1h, unpitch, 0)


def _hyena_fused(plan, z, col0, conv_w, conv_b, kr, ki, skip, *, nb):
    b, seq_len, _ = z.shape
    ch = skip.shape[1]
    n1h, n2, kh = plan.n1 // 2, plan.n2, plan.kh
    n_cb = ch // LANES
    cb0 = col0 // LANES
    part = lambda p: pl.BlockSpec((nb, seq_len, LANES), lambda c, bb: (bb, 0, cb0 + p * n_cb + c))
    taps = lambda p: pl.BlockSpec((conv_w.shape[0], LANES), lambda c, bb: (0, p * n_cb + c))
    bias = lambda p: pl.BlockSpec((1, LANES), lambda c, bb: (0, p * n_cb + c))
    once = pl.Buffered(1)
    filt = lambda o: pl.BlockSpec((kh, n2, LANES), lambda c, bb: (0, 0, o * n_cb + c), pipeline_mode=once)
    full = lambda a: pl.BlockSpec(a.shape, lambda c, bb, _n=a.ndim: (0,) * _n, pipeline_mode=once)
    tables = (plan.f1, plan.twr_fwd, plan.twi_fwd, plan.twr_inv, plan.twi_inv, plan.f2, plan.f2inv, plan.ginv)
    pitch = n2 + SUBLANES
    spec_scratch = pltpu.VMEM((nb, kh * pitch, LANES), F32)
    return pl.pallas_call(
        functools.partial(_hyena_fused_kernel, n1h=n1h, n2=n2, unroll=min(n2, 16)),
        grid=(n_cb, b // nb),
        in_specs=[part(0), part(1), part(2), taps(0), taps(1), taps(2), bias(0), bias(1), bias(2),
                  pl.BlockSpec((skip.shape[0], 1, LANES), lambda c, bb: (0, 0, c)),
                  filt(0), filt(0), filt(1), filt(1)] + [full(a) for a in tables],
        out_specs=pl.BlockSpec((nb, seq_len, LANES), lambda c, bb: (bb, 0, c)),
        out_shape=jax.ShapeDtypeStruct((b, seq_len, ch), F32),
        scratch_shapes=[spec_scratch, spec_scratch, pltpu.VMEM((nb, n1h * pitch, LANES), F32),
                        pltpu.VMEM((3, nb, n1h * pitch, LANES), F32)],
        compiler_params=_params(("parallel", "arbitrary"), 56),
        name="hyena_fused",
    )(z, z, z, conv_w, conv_w, conv_w, conv_b, conv_b, conv_b,
      skip.reshape(skip.shape[0], 1, ch), kr, ki, kr, ki, *tables)


def _dft_shape(seq_len):
    n2 = 128
    return 2 * seq_len // n2, n2


def _block_cols(n2, rows, target_bytes=1 << 20):
    g = max(SUBLANES, min(n2, target_bytes // (rows * LANES * 4)) // SUBLANES * SUBLANES)
    while n2 % g:
        g -= SUBLANES
    return g


def _slabs_per_step(kh, slab_bytes, target_bytes=3 << 20):
    ks = max(1, min(kh, target_bytes // slab_bytes))
    while kh % ks:
        ks -= 1
    return ks


def _hyena(plan, z, col0, conv_w, conv_b, filt, skip):
    b, seq_len, _ = z.shape
    n_orders, ch = skip.shape
    g = _block_cols(plan.n2, plan.n1 // 2)
    ks_filt = _slabs_per_step(plan.kh, 2 * plan.n2 * LANES * 4, target_bytes=6 << 20)
    kr, ki = _filt_spec(plan, filt, n_orders, ks=ks_filt)
    nb = 2 if b % 2 == 0 else 1
    fused_bytes = 15 * nb * seq_len * LANES * 4
    if n_orders == 2 and fused_bytes <= FUSED_VMEM_BUDGET:
        return _hyena_fused(plan, z, col0, conv_w, conv_b, kr, ki, skip, nb=nb)
    uc = _sconv(z, conv_w, conv_b, rows=min(seq_len, 2048), col0=col0, n_parts=n_orders + 1)
    ks = _slabs_per_step(plan.kh, b * plan.n2 * ch * 4)
    skip3 = skip.reshape(n_orders, 1, ch)
    cur, cur_part = uc, 0
    for order in range(n_orders):
        are, aim = _dft_fwd(plan, cur, cur_part, g=g)
        bre, bim = _conv_mid(plan, are, aim, kr, ki, order, ks=ks)
        cur = _dft_inv(plan, bre, bim, cur, cur_part, uc, order + 1, skip3, order, g=g)
        cur_part = 0
    return cur[0]


def _trunk(x3, p, plan, filt):
    b, seq_len, d = x3.shape
    t = b * seq_len
    x = x3.reshape(t, d)
    x, z = _ffn_in(x, p["g1pre"], p["g1post"], p["w1gu"], p["w1d"], p["gmix"], p["w_in"],
                   tm=512, chunk=p["ff_chunk"])
    z = z.reshape(b, seq_len, -1)
    d_pool = p["pool_w"].shape[0] * LANES
    y_pool = _pool(z, p["pool_w"], p["pool_scale"], rows=min(seq_len, 2048))
    y_hy = _hyena(plan, z, d_pool, p["conv_w"], p["conv_b"], filt, p["skip"])
    x = _ffn_out(x, y_pool.reshape(t, -1), y_hy.reshape(t, -1), p["gpool"], p["ghy"], p["w_out"],
                 p["gmixpost"], p["g2pre"], p["g2post"], p["w2gu"], p["w2d"], tm=512, chunk=p["ff_chunk"])
    return x.reshape(b, seq_len, d)


def kernel(x_prompt, x_sample, ffn1_norm_pre, ffn1_norm_post, ffn1_w_gate_up, ffn1_w_down, mix_norm_pre, w_in, pool_w_map, pool_scale, hyena_conv_w, hyena_conv_b, filt_w_first, filt_b_first, filt_w_hidden, filt_b_hidden, filt_w_last, filt_freq, hyena_skip, pool_out_norm, hyena_out_norm, w_out, mix_norm_post, ffn2_norm_pre, ffn2_norm_post, ffn2_w_gate_up, ffn2_w_down):
    assert ffn1_norm_pre.shape[0] == 1, "single-layer trunk"
    row = lambda a: a[0].reshape(1, -1)
    d_ff = ffn1_w_down.shape[1]
    p = dict(
        g1pre=row(ffn1_norm_pre), g1post=row(ffn1_norm_post),
        w1gu=ffn1_w_gate_up[0].astype(BF16), w1d=ffn1_w_down[0].astype(BF16),
        gmix=row(mix_norm_pre), w_in=w_in[0].astype(BF16),
        pool_w=pool_w_map[0].astype(BF16), pool_scale=row(pool_scale),
        conv_w=hyena_conv_w[0], conv_b=row(hyena_conv_b),
        skip=hyena_skip[0], gpool=row(pool_out_norm), ghy=row(hyena_out_norm),
        w_out=w_out[0].astype(BF16), gmixpost=row(mix_norm_post),
        g2pre=row(ffn2_norm_pre), g2post=row(ffn2_norm_post),
        w2gu=ffn2_w_gate_up[0].astype(BF16), w2d=ffn2_w_down[0].astype(BF16),
        ff_chunk=d_ff // 2 if (d_ff // 2) % LANES == 0 else d_ff,
    )
    outs = []
    for x3 in (x_prompt, x_sample):
        seq_len = x3.shape[1]
        plan = _Plan(seq_len, *_dft_shape(seq_len))
        filt = _filt_gen(seq_len, filt_w_first[0], filt_b_first[0], filt_w_hidden[0],
                         filt_b_hidden[0], filt_w_last[0], filt_freq[0], rows=min(seq_len, 512),
                         n2=plan.n2)
        outs.append(_trunk(x3, p, plan, filt))
    return tuple(outs)
```

```python
import functools
import math

import jax
import jax.numpy as jnp
import numpy as np
from jax import lax
from jax.experimental import pallas as pl
from jax.experimental.pallas import tpu as pltpu

F32 = jnp.float32
BF16 = jnp.bfloat16
EPS = 1e-6
LANES = 128
SUBLANES = 8
MXU_DIM = 256
POOL_WINDOWS = (2, 4, 8, 16)
HALO = 8
DECAY_TARGET = 1e-2
FAST_DECAY_PCT = 0.3
SLOW_DECAY_PCT = 1.5
MAX_DECAY = math.log(DECAY_TARGET) / FAST_DECAY_PCT
MIN_DECAY = math.log(DECAY_TARGET) / SLOW_DECAY_PCT
HIGHEST = lax.Precision.HIGHEST
FUSED_VMEM_BUDGET = 32 << 20


def _params(sem, vmem_mib):
    return pltpu.CompilerParams(dimension_semantics=sem, vmem_limit_bytes=vmem_mib << 20)


def _rms(x, g):
    inv = lax.rsqrt(jnp.mean(x * x, axis=-1, keepdims=True) + EPS)
    return (x * inv) * g


def _dot(a, b):
    return jnp.dot(a, b, preferred_element_type=F32)


def _ff_chunks(d_ff, n_chunks=2):
    if d_ff % MXU_DIM:
        return (0, d_ff)
    tiles = d_ff // MXU_DIM
    return tuple(MXU_DIM * ((tiles * k + n_chunks - 1) // n_chunks) for k in range(n_chunks + 1))


def _half_step_ffn(x, gpre, gpost, wgu_ref, wd_ref):
    d_ff = wd_ref.shape[0]
    h = _rms(x, gpre).astype(BF16)
    acc = None
    bounds = _ff_chunks(d_ff)
    for lo, hi in zip(bounds[:-1], bounds[1:]):
        gate = _dot(h, wgu_ref[:, lo:hi])
        up = _dot(h, wgu_ref[:, d_ff + lo:d_ff + hi])
        act = (gate * jax.nn.sigmoid(gate) * up).astype(BF16)
        part = _dot(act, wd_ref[lo:hi, :])
        acc = part if acc is None else acc + part
    return x + 0.5 * _rms(acc, gpost)


def _ffn_in_kernel(x_ref, gpre_ref, gpost_ref, wgu_ref, wd_ref, gmix_ref, win_ref, o_ref, z_ref):
    x = _half_step_ffn(x_ref[...], gpre_ref[...], gpost_ref[...], wgu_ref, wd_ref)
    o_ref[...] = x
    z_ref[...] = _dot(_rms(x, gmix_ref[...]).astype(BF16), win_ref[...])


def _ffn_in(x, g_pre, g_post, wgu, wd, g_mix, w_in, *, tm):
    t, d = x.shape
    d_ff, d_in = wd.shape[0], w_in.shape[1]
    const = lambda i: (0, 0)
    row = lambda i: (i, 0)
    once = pl.Buffered(1)
    vec = pl.BlockSpec((1, d), const)
    return pl.pallas_call(
        _ffn_in_kernel,
        grid=(t // tm,),
        in_specs=[
            pl.BlockSpec((tm, d), row), vec, vec,
            pl.BlockSpec((d, 2 * d_ff), const, pipeline_mode=once),
            pl.BlockSpec((d_ff, d), const, pipeline_mode=once),
            vec,
            pl.BlockSpec((d, d_in), const, pipeline_mode=once),
        ],
        out_specs=[pl.BlockSpec((tm, d), row), pl.BlockSpec((tm, d_in), row)],
        out_shape=[jax.ShapeDtypeStruct((t, d), F32), jax.ShapeDtypeStruct((t, d_in), F32)],
        compiler_params=_params(("parallel",), 56),
        name="ffn_in",
    )(x, g_pre, g_post, wgu, wd, g_mix, w_in)


def _ffn_out_kernel(x_ref, yp_ref, yh_ref, gp_ref, gh_ref, wout_ref, gmix_ref,
                    gpre_ref, gpost_ref, wgu_ref, wd_ref, o_ref):
    d_pool = yp_ref.shape[1]
    yp = _rms(yp_ref[...], gp_ref[...]).astype(BF16)
    yh = _rms(yh_ref[...], gh_ref[...]).astype(BF16)
    y = _dot(yp, wout_ref[:d_pool, :]) + _dot(yh, wout_ref[d_pool:, :])
    x = x_ref[...] + _rms(y, gmix_ref[...])
    o_ref[...] = _half_step_ffn(x, gpre_ref[...], gpost_ref[...], wgu_ref, wd_ref)


def _ffn_out(x, yp, yh, gp, gh, w_out, g_mix, g_pre, g_post, wgu, wd, *, tm):
    t, d = x.shape
    d_ff, dp, dh = wd.shape[0], yp.shape[1], yh.shape[1]
    const = lambda i: (0, 0)
    row = lambda i: (i, 0)
    once = pl.Buffered(1)
    vec = lambda n: pl.BlockSpec((1, n), const)
    return pl.pallas_call(
        _ffn_out_kernel,
        grid=(t // tm,),
        in_specs=[
            pl.BlockSpec((tm, d), row), pl.BlockSpec((tm, dp), row), pl.BlockSpec((tm, dh), row),
            vec(dp), vec(dh),
            pl.BlockSpec((dp + dh, d), const, pipeline_mode=once),
            vec(d), vec(d), vec(d),
            pl.BlockSpec((d, 2 * d_ff), const, pipeline_mode=once),
            pl.BlockSpec((d_ff, d), const, pipeline_mode=once),
        ],
        out_specs=pl.BlockSpec((tm, d), row),
        out_shape=jax.ShapeDtypeStruct((t, d), F32),
        compiler_params=_params(("parallel",), 56),
        name="ffn_out",
    )(x, yp, yh, gp, gh, w_out, g_mix, g_pre, g_post, wgu, wd)


def _with_halo(prev_ref, main_ref, next_ref, i, n_tiles):
    prev = jnp.where(i > 0, prev_ref[0], 0.0)
    nxt = jnp.where(i < n_tiles - 1, next_ref[0], 0.0)
    return jnp.concatenate([prev, main_ref[0], nxt], axis=0)


def _pool_kernel(prev_ref, main_ref, next_ref, wmap_ref, scale_ref, o_ref, *, seq_len, n_tiles):
    i = pl.program_id(1)
    rows = main_ref.shape[1]
    n_ext = rows + 2 * HALO
    ext_all = _with_halo(prev_ref, main_ref, next_ref, i, n_tiles)
    pos = i * rows + lax.broadcasted_iota(jnp.int32, (rows, LANES), 0)
    for grp, window in enumerate(POOL_WINDOWS):
        lanes = slice(grp * LANES, (grp + 1) * LANES)
        ext = ext_all[:, lanes]
        ssum, w = ext + pltpu.roll(ext, 1, 0), 2
        while w < window:
            ssum, w = pltpu.roll(ssum, w // 2, 0) + pltpu.roll(ssum, n_ext - w // 2, 0), 2 * w
        u = main_ref[0, :, lanes]
        lo = jnp.clip(pos - window // 2, 0, seq_len)
        hi = jnp.clip(pos + (window - window // 2), 0, seq_len)
        d = ssum[HALO:HALO + rows] / (hi - lo).astype(F32) - u
        y = _dot(d.astype(BF16), wmap_ref[grp])
        o_ref[0, :, lanes] = y * scale_ref[:, lanes]


def _halo_specs(rows, width, seq_len, col_of):
    blocks_per_tile = rows // HALO
    last = seq_len // HALO - 1
    prev = pl.BlockSpec((1, HALO, width),
                        lambda b, i, c: (b, jnp.maximum(i * blocks_per_tile - 1, 0), col_of(c)))
    main = pl.BlockSpec((1, rows, width), lambda b, i, c: (b, i, col_of(c)))
    nxt = pl.BlockSpec((1, HALO, width),
                       lambda b, i, c: (b, jnp.minimum((i + 1) * blocks_per_tile, last), col_of(c)))
    return [prev, main, nxt]


def _pool(z, wmap, scale, *, rows):
    b, seq_len, _ = z.shape
    assert wmap.shape[0] == len(POOL_WINDOWS) and max(POOL_WINDOWS) <= 2 * HALO
    width = wmap.shape[0] * LANES
    n_tiles = seq_len // rows
    return pl.pallas_call(
        functools.partial(_pool_kernel, seq_len=seq_len, n_tiles=n_tiles),
        grid=(b, n_tiles, 1),
        in_specs=_halo_specs(rows, width, seq_len, lambda c: 0) + [
            pl.BlockSpec(wmap.shape, lambda b, i, c: (0, 0, 0)),
            pl.BlockSpec((1, width), lambda b, i, c: (0, 0)),
        ],
        out_specs=pl.BlockSpec((1, rows, width), lambda b, i, c: (b, i, 0)),
        out_shape=jax.ShapeDtypeStruct((b, seq_len, width), F32),
        compiler_params=_params(("parallel", "parallel", "parallel"), 40),
        name="pool",
    )(z, z, z, wmap, scale)


def _sconv_kernel(prev_ref, main_ref, next_ref, w_ref, b_ref, o_ref, *, n_tiles):
    i = pl.program_id(1)
    rows = main_ref.shape[1]
    ext = _with_halo(prev_ref, main_ref, next_ref, i, n_tiles)
    n_ext = rows + 2 * HALO
    before = pltpu.roll(ext, 1, 0)[HALO:HALO + rows]
    after = pltpu.roll(ext, n_ext - 1, 0)[HALO:HALO + rows]
    w = w_ref[...]
    out = before * w[0:1] + main_ref[0] * w[1:2] + after * w[2:3]
    o_ref[0, 0] = out + b_ref[...]


def _sconv(z, w, bias, *, rows, col0, n_parts):
    b, seq_len, _ = z.shape
    n_tiles = seq_len // rows
    width = w.shape[1] // n_parts
    assert col0 % width == 0
    return pl.pallas_call(
        functools.partial(_sconv_kernel, n_tiles=n_tiles),
        grid=(b, n_tiles, n_parts),
        in_specs=_halo_specs(rows, width, seq_len, lambda c: c + col0 // width) + [
            pl.BlockSpec((w.shape[0], width), lambda b, i, c: (0, c)),
            pl.BlockSpec((1, width), lambda b, i, c: (0, c)),
        ],
        out_specs=pl.BlockSpec((1, 1, rows, width), lambda b, i, c: (c, b, i, 0)),
        out_shape=jax.ShapeDtypeStruct((n_parts, b, seq_len, width), F32),
        compiler_params=_params(("parallel", "parallel", "parallel"), 40),
        name="sconv",
    )(z, z, z, w, bias)


def _filt_gen_kernel(bands_ref, wt_ref, wc_ref, ws_ref, b1_ref, wh_ref, bh_ref, freq_ref,
                     wl_ref, delta_ref, o_ref, *, seq_len, d_ch):
    rows = o_ref.shape[0] * (o_ref.shape[1] - SUBLANES)
    base = pl.program_id(0) * rows
    m_lane = (base + lax.broadcasted_iota(jnp.int32, (1, rows), 1)).astype(F32)
    t_lane = m_lane / (seq_len - 1.0)
    ang = (bands_ref[...] * (2.0 * math.pi / seq_len)) * m_lane
    freq = freq_ref[...]
    pre = (jnp.dot(wc_ref[...], jnp.cos(ang), precision=HIGHEST, preferred_element_type=F32)
           + jnp.dot(ws_ref[...], -jnp.sin(ang), precision=HIGHEST, preferred_element_type=F32)
           + wt_ref[...] * t_lane + b1_ref[...])
    h = jnp.sin(freq * pre)
    for layer in range(wh_ref.shape[0]):
        pre = jnp.dot(wh_ref[layer], h, precision=HIGHEST, preferred_element_type=F32) + bh_ref[layer]
        h = jnp.sin(freq * pre)
    out = jnp.dot(h.T, wl_ref[...], precision=HIGHEST, preferred_element_type=F32)
    m_row = base + lax.broadcasted_iota(jnp.int32, (rows, d_ch), 0)
    t_row = m_row.astype(F32) / (seq_len - 1.0)
    decay = jnp.exp(-t_row * jnp.abs(delta_ref[...]))
    decay_bwd = jnp.where(m_row == 0, 0.0, decay)
    n_chunks, pitch, _ = o_ref.shape
    n2 = rows // n_chunks
    for q in range(out.shape[1] // d_ch):
        dq = decay_bwd if q % 2 == 1 else decay
        val = out[:, q * d_ch:(q + 1) * d_ch] * dq
        for n in range(n_chunks):
            o_ref[n, :n2, q * d_ch:(q + 1) * d_ch] = val[n * n2:(n + 1) * n2]
    o_ref[:, n2:, :] = jnp.zeros((n_chunks, pitch - n2, o_ref.shape[2]), F32)


def _filt_gen(seq_len, w_first, b_first, w_hidden, b_hidden, w_last, freq, *, rows, n2):
    pos_bands = (w_first.shape[0] - 1) // 2
    hidden = w_first.shape[1]
    n_cols = w_last.shape[1]
    d_ch = n_cols // 4
    bands = jnp.linspace(1e-4, pos_bands - 1, pos_bands, dtype=F32).reshape(pos_bands, 1)
    deltas = jnp.linspace(MIN_DECAY, MAX_DECAY, d_ch, dtype=F32).reshape(1, d_ch)
    w1t = w_first.T
    args = (bands, w1t[:, 0:1], w1t[:, 1:1 + pos_bands], w1t[:, 1 + pos_bands:],
            b_first.reshape(hidden, 1), jnp.swapaxes(w_hidden, 1, 2),
            b_hidden.reshape(b_hidden.shape[0], hidden, 1), freq.reshape(hidden, 1),
            w_last, deltas)
    full = lambda a: pl.BlockSpec(a.shape, lambda i, _n=a.ndim: (0,) * _n)
    return pl.pallas_call(
        functools.partial(_filt_gen_kernel, seq_len=seq_len, d_ch=d_ch),
        grid=(seq_len // rows,),
        in_specs=[full(a) for a in args],
        out_specs=pl.BlockSpec((rows // n2, n2 + SUBLANES, n_cols), lambda i: (i, 0, 0)),
        out_shape=jax.ShapeDtypeStruct((seq_len // n2, n2 + SUBLANES, n_cols), F32),
        compiler_params=_params(("parallel",), 40),
        name="filt_gen",
    )(*args)


class _Plan:
    def __init__(self, seq_len, n1, n2):
        assert n1 * n2 == 2 * seq_len and n1 % (2 * SUBLANES) == 0 and n2 % SUBLANES == 0
        self.seq_len, self.n1, self.n2 = seq_len, n1, n2
        n = n1 * n2
        n1h = n1 // 2
        self.kh = kh = n1h + 1
        k1 = np.arange(kh, dtype=np.float64)[:, None]
        m1 = np.arange(n1h, dtype=np.float64)[None, :]
        th1 = 2.0 * np.pi * k1 * m1 / n1
        f1 = np.zeros((n1 + SUBLANES, n1h))
        f1[:n1h] = np.cos(th1[:n1h]); f1[n1h:n1] = -np.sin(th1[:n1h]); f1[n1] = np.cos(th1[n1h])
        self.f1 = jnp.asarray(f1, F32)
        m2 = np.arange(n2, dtype=np.float64)[None, :]
        tht = 2.0 * np.pi * k1 * m2 / n
        twr, twi = np.cos(tht), -np.sin(tht)
        rep = lambda a: np.repeat(a[:, :, None], LANES, axis=2)
        self.twr_inv, self.twi_inv = jnp.asarray(rep(twr), F32), jnp.asarray(rep(twi), F32)
        pad = np.zeros((n2, SUBLANES - 1))
        self.twr_fwd = jnp.asarray(rep(np.concatenate([twr.T, pad], 1)), F32)
        self.twi_fwd = jnp.asarray(rep(np.concatenate([twi.T, pad], 1)), F32)
        k2 = np.arange(n2, dtype=np.float64)[:, None]
        th2 = 2.0 * np.pi * k2 * m2 / n2
        f2r, f2i = np.cos(th2), -np.sin(th2)
        self.f2 = jnp.asarray(np.block([[f2r, -f2i], [f2i, f2r]]), F32)
        self.f2inv = jnp.asarray(np.block([[f2r, f2i], [-f2i, f2r]]), F32)
        wgt = np.where(k1[:n1h] == 0, 1.0, 2.0) / n
        ginv = np.concatenate([(wgt * np.cos(th1[:n1h])).T, (-wgt * np.sin(th1[:n1h])).T], axis=1)
        self.ginv = jnp.asarray(ginv, F32)
        k1f = np.arange(kh, dtype=np.float64)[:, None]
        full = lambda cols: 2.0 * np.pi * k1f * cols[None, :] / n1
        stage1 = lambda th: np.concatenate(
            [np.cos(th[:n1h]), -np.sin(th[:n1h]), np.cos(th[n1h:]), np.zeros((SUBLANES - 1, th.shape[1]))], 0)
        fwd_cols = np.arange(n1h, dtype=np.float64)
        f1a = stage1(full(np.concatenate([fwd_cols, n1 - 1 - fwd_cols])))
        f1b = stage1(full(np.concatenate([fwd_cols, (n1 - fwd_cols) % n1])))
        f1b[:, n1h] = 0.0
        self.f1_filt, self.f1_filt0 = jnp.asarray(f1a, F32), jnp.asarray(f1b, F32)
        rows = lambda t: rep(np.concatenate([t.T, np.zeros((t.shape[1], SUBLANES - 1))], 1))
        coarse, fine = tht[:, ::SUBLANES], tht[:, :SUBLANES]
        self.tw_coarse = (jnp.asarray(rows(np.cos(coarse)), F32), jnp.asarray(rows(-np.sin(coarse)), F32))
        self.tw_fine = (jnp.asarray(rows(np.cos(fine)), F32), jnp.asarray(rows(-np.sin(fine)), F32))


def _dft_fwd_kernel(x_ref, f1_ref, twr_ref, twi_ref, re_ref, im_ref, *, unroll):
    _, _, n1h, g, _ = x_ref.shape
    x2 = x_ref.at[0, 0].reshape(n1h * g, LANES)
    re2 = re_ref.at[0].reshape((n1h + 1) * g, LANES)
    im2 = im_ref.at[0].reshape((n1h + 1) * g, LANES)
    f1 = f1_ref[...].astype(BF16)

    def body(j, carry):
        xb = x2[pl.ds(j, n1h, stride=g), :].astype(BF16)
        p = _dot(f1, xb)
        twr, twi = twr_ref[j], twi_ref[j]
        pr, pi, pn = p[:n1h], p[n1h:2 * n1h], p[2 * n1h:2 * n1h + 1]
        re2[pl.ds(j, n1h, stride=g), :] = pr * twr[:n1h] - pi * twi[:n1h]
        im2[pl.ds(j, n1h, stride=g), :] = pr * twi[:n1h] + pi * twr[:n1h]
        re2[pl.ds(n1h * g + j, 1), :] = pn * twr[n1h:n1h + 1]
        im2[pl.ds(n1h * g + j, 1), :] = pn * twi[n1h:n1h + 1]
        return carry

    lax.fori_loop(0, g, body, 0, unroll=unroll)


def _dft_fwd(plan, x4, part, *, g):
    _, b, seq_len, ch = x4.shape
    n1h, n2, kh = plan.n1 // 2, plan.n2, plan.kh
    xv = x4.reshape(x4.shape[0], b, n1h, n2, ch)
    out = jax.ShapeDtypeStruct((b, kh, n2, ch), F32)
    tw_spec = pl.BlockSpec((g, n1h + SUBLANES, LANES), lambda j, bb, c: (j, 0, 0))
    return pl.pallas_call(
        functools.partial(_dft_fwd_kernel, unroll=min(g, 16)),
        grid=(n2 // g, b, ch // LANES),
        in_specs=[
            pl.BlockSpec((1, 1, n1h, g, LANES), lambda j, bb, c: (part, bb, 0, j, c)),
            pl.BlockSpec(plan.f1.shape, lambda j, bb, c: (0, 0)),
            tw_spec, tw_spec,
        ],
        out_specs=[pl.BlockSpec((1, kh, g, LANES), lambda j, bb, c: (bb, 0, j, c))] * 2,
        out_shape=[out, out],
        compiler_params=_params(("parallel", "parallel", "parallel"), 48),
        name="dft_fwd",
    )(xv, plan.f1, plan.twr_fwd, plan.twi_fwd)


def _stack_bf16(re, im):
    return jnp.concatenate([re.astype(BF16), im.astype(BF16)], axis=0)


def _filt_spec_kernel(hf_ref, hb_ref, f1_ref, f1j0_ref, twcr_ref, twci_ref, twfr_ref, twfi_ref, f2_ref,
                      kr_ref, ki_ref, are_ref, aim_ref, *, n1h, n2, unroll):
    pitch = n2 + SUBLANES
    ks = kr_ref.shape[0]
    kc = pl.program_id(1)

    @pl.when(kc == 0)
    def _():
        def stage1(j, f1, hb_start, a, b):
            x = jnp.concatenate([hf_ref[pl.ds(j, n1h, stride=pitch), :],
                                 hb_ref[pl.ds(hb_start, n1h, stride=pitch), :]], axis=0)
            p = _dot(f1, x.astype(BF16))
            cr, ci, fr, fi = twcr_ref[a], twci_ref[a], twfr_ref[b], twfi_ref[b]
            twr, twi = cr * fr - ci * fi, cr * fi + ci * fr
            pr, pi, pn = p[:n1h], p[n1h:2 * n1h], p[2 * n1h:2 * n1h + 1]
            are_ref[pl.ds(j, n1h, stride=pitch), :] = pr * twr[:n1h] - pi * twi[:n1h]
            aim_ref[pl.ds(j, n1h, stride=pitch), :] = pr * twi[:n1h] + pi * twr[:n1h]
            are_ref[pl.ds(n1h * pitch + j, 1), :] = pn * twr[n1h:n1h + 1]
            aim_ref[pl.ds(n1h * pitch + j, 1), :] = pn * twi[n1h:n1h + 1]

        stage1(0, f1j0_ref[...].astype(BF16), 0, 0, 0)
        f1 = f1_ref[...].astype(BF16)

        def body(j, carry):
            stage1(j, f1, n2 - j, lax.shift_right_logical(j, 3), jnp.bitwise_and(j, SUBLANES - 1))
            return carry

        lax.fori_loop(1, n2, body, 0, unroll=unroll)

    f2 = f2_ref[...].astype(BF16)

    def slabs(first, count):
        rows = [pl.ds(pl.multiple_of((kc * ks + first + i) * pitch, SUBLANES), n2) for i in range(count)]
        lanes = lambda parts: parts[0] if count == 1 else jnp.concatenate(parts, axis=1)
        x = _dot(f2, _stack_bf16(lanes([are_ref[r, :] for r in rows]), lanes([aim_ref[r, :] for r in rows])))
        for i in range(count):
            kr_ref[first + i] = x[:n2, i * LANES:(i + 1) * LANES]
            ki_ref[first + i] = x[n2:, i * LANES:(i + 1) * LANES]

    def pair(p, carry):
        slabs(2 * p, 2)
        return carry

    lax.fori_loop(0, ks // 2, pair, 0, unroll=2)
    if ks % 2:
        slabs(ks - 1, 1)


def _filt_spec(plan, filt, n_orders, *, ks):
    n1h, n2, kh = plan.n1 // 2, plan.n2, plan.kh
    cols = filt.shape[2]
    filt = filt.reshape(n1h * (n2 + SUBLANES), cols)
    n_cb = cols // (2 * n_orders * LANES)
    pitch = n2 + SUBLANES
    once = pl.Buffered(1)
    col = lambda c, back: (c // n_cb) * 2 * n_cb + back * n_cb + c % n_cb
    seq = lambda back: pl.BlockSpec((n1h * pitch, LANES), lambda c, k: (0, col(c, back)), pipeline_mode=once)
    full = lambda a: pl.BlockSpec(a.shape, lambda c, k, _n=a.ndim: (0,) * _n, pipeline_mode=once)
    tables = (plan.f1_filt, plan.f1_filt0, *plan.tw_coarse, *plan.tw_fine, plan.f2)
    out = jax.ShapeDtypeStruct((kh, n2, cols // 2), F32)
    scratch = pltpu.VMEM((kh * pitch, LANES), F32)
    return pl.pallas_call(
        functools.partial(_filt_spec_kernel, n1h=n1h, n2=n2, unroll=min(n2 - 1, 8)),
        grid=(n_orders * n_cb, kh // ks),
        in_specs=[seq(0), seq(1)] + [full(a) for a in tables],
        out_specs=[pl.BlockSpec((ks, n2, LANES), lambda c, k: (k, 0, c))] * 2,
        out_shape=[out, out],
        scratch_shapes=[scratch, scratch],
        compiler_params=_params(("parallel", "arbitrary"), 56),
        name="filt_spec",
    )(filt, filt, *tables)


def _conv_mid_kernel(re_ref, im_ref, kr_ref, ki_ref, twr_ref, twi_ref, f2_ref, f2inv_ref,
                     ore_ref, oim_ref):
    nb, ks, n2, ch = re_ref.shape
    reps = ch // LANES
    f2, f2inv = f2_ref[...].astype(BF16), f2inv_ref[...].astype(BF16)
    for s in range(ks):
        kr, ki = kr_ref[s], ki_ref[s]
        twr = jnp.tile(twr_ref[s], (1, reps))
        twi = jnp.tile(twi_ref[s], (1, reps))

        def body(b, carry):
            x = _dot(f2, _stack_bf16(re_ref[b, s], im_ref[b, s]))
            xr, xi = x[:n2], x[n2:]
            y = _dot(f2inv, _stack_bf16(xr * kr - xi * ki, xr * ki + xi * kr))
            br, bi = y[:n2], y[n2:]
            ore_ref[b, s] = br * twr + bi * twi
            oim_ref[b, s] = bi * twr - br * twi
            return carry

        lax.fori_loop(0, nb, body, 0)


def _conv_mid(plan, are, aim, kr, ki, order, *, ks):
    b, kh, n2, ch = are.shape
    data = pl.BlockSpec((b, ks, n2, ch), lambda k: (0, k, 0, 0))
    filt = pl.BlockSpec((ks, n2, ch), lambda k: (k, 0, order))
    tw = pl.BlockSpec((ks, n2, LANES), lambda k: (k, 0, 0))
    mat = pl.BlockSpec((2 * n2, 2 * n2), lambda k: (0, 0))
    out = jax.ShapeDtypeStruct((b, kh, n2, ch), F32)
    return pl.pallas_call(
        _conv_mid_kernel,
        grid=(kh // ks,),
        in_specs=[data, data, filt, filt, tw, tw, mat, mat],
        out_specs=[data, data],
        out_shape=[out, out],
        compiler_params=_params(("parallel",), 48),
        name="conv_mid",
    )(are, aim, kr, ki, plan.twr_inv, plan.twi_inv, plan.f2, plan.f2inv)


def _dft_inv_kernel(bre_ref, bim_ref, x_ref, gate_ref, skip_ref, ginv_ref, o_ref, *, unroll, scale):
    _, _, n1h, g, _ = o_ref.shape
    b_re = bre_ref.at[0].reshape((n1h + 1) * g, LANES)
    b_im = bim_ref.at[0].reshape((n1h + 1) * g, LANES)
    x2 = x_ref.at[0, 0].reshape(n1h * g, LANES)
    gate2 = gate_ref.at[0, 0].reshape(n1h * g, LANES)
    o2 = o_ref.at[0, 0].reshape(n1h * g, LANES)
    ginv = ginv_ref[...].astype(BF16)
    skip = skip_ref[0]
    n1 = lax.broadcasted_iota(jnp.int32, (n1h, LANES), 0)
    nyq_w = jnp.where(jnp.bitwise_and(n1, 1) == 0, scale, -scale)

    def body(j, carry):
        rows = pl.ds(j, n1h, stride=g)
        spec = _stack_bf16(b_re[rows, :], b_im[rows, :])
        y = _dot(ginv, spec) + nyq_w * b_re[pl.ds(n1h * g + j, 1), :]
        o2[rows, :] = gate2[rows, :] * (y + x2[rows, :] * skip)
        return carry

    lax.fori_loop(0, g, body, 0, unroll=unroll)


def _dft_inv(plan, bre, bim, x4, x_part, gate4, gate_part, skip, order, *, g):
    b, seq_len, ch = x4.shape[1:]
    n1h, n2, kh = plan.n1 // 2, plan.n2, plan.kh
    xv = x4.reshape(x4.shape[0], b, n1h, n2, ch)
    gv = gate4.reshape(gate4.shape[0], b, n1h, n2, ch)
    spec = pl.BlockSpec((1, kh, g, LANES), lambda j, bb, c: (bb, 0, j, c))
    time = lambda part: pl.BlockSpec((1, 1, n1h, g, LANES), lambda j, bb, c: (part, bb, 0, j, c))
    out = pl.pallas_call(
        functools.partial(_dft_inv_kernel, unroll=min(g, 16), scale=1.0 / (plan.n1 * n2)),
        grid=(n2 // g, b, ch // LANES),
        in_specs=[
            spec, spec, time(x_part), time(gate_part),
            pl.BlockSpec((1, 1, LANES), lambda j, bb, c: (order, 0, c)),
            pl.BlockSpec(plan.ginv.shape, lambda j, bb, c: (0, 0)),
        ],
        out_specs=time(0),
        out_shape=jax.ShapeDtypeStruct((1, b, n1h, n2, ch), F32),
        compiler_params=_params(("parallel", "parallel", "parallel"), 48),
        name="dft_inv",
    )(bre, bim, xv, gv, skip, plan.ginv)
    return out.reshape(1, b, seq_len, ch)


def _hyena_fused_kernel(zv_ref, zx1_ref, zx2_ref, wv_ref, wx1_ref, wx2_ref, bv_ref, bx1_ref, bx2_ref,
                        skip_ref, kr0_ref, ki0_ref, kr1_ref, ki1_ref,
                        f1_ref, twrf_ref, twif_ref, twri_ref, twii_ref, f2_ref, f2inv_ref, ginv_ref,
                        o_ref, are_ref, aim_ref, cur_ref, tin_ref, *, n1h, n2, unroll):
    nb = o_ref.shape[0]
    pitch = n2 + SUBLANES
    f1 = f1_ref[...].astype(BF16)
    f2, f2inv = f2_ref[...].astype(BF16), f2inv_ref[...].astype(BF16)
    ginv = ginv_ref[...].astype(BF16)
    scale = 1.0 / (2 * n1h * n2)
    n1 = lax.broadcasted_iota(jnp.int32, (n1h, nb * LANES), 0)
    nyq_w = jnp.where(jnp.bitwise_and(n1, 1) == 0, scale, -scale)
    lanes = lambda parts: parts[0] if len(parts) == 1 else jnp.concatenate(parts, axis=1)
    wide = lambda t, reps: t if reps == 1 else jnp.tile(t, (1, reps))
    lane_block = lambda a, i: a[:, i * LANES:(i + 1) * LANES]

    row = lax.broadcasted_iota(jnp.int32, (n2, LANES), 0)
    for part, (z_ref, w_ref, b_ref) in enumerate(((zv_ref, wv_ref, bv_ref), (zx1_ref, wx1_ref, bx1_ref),
                                                   (zx2_ref, wx2_ref, bx2_ref))):
        w, bias = w_ref[...], b_ref[...]
        for i in range(nb):
            for n in range(n1h):
                lo = n * n2
                mid = z_ref[i, lo:lo + n2, :]
                if n > 0:
                    before = z_ref[i, lo - 1:lo + n2 - 1, :]
                else:
                    before = jnp.where(row == 0, 0.0, pltpu.roll(mid, 1, 0))
                if n < n1h - 1:
                    after = z_ref[i, lo + 1:lo + n2 + 1, :]
                else:
                    after = jnp.where(row == n2 - 1, 0.0, pltpu.roll(mid, n2 - 1, 0))
                tin_ref[part, i, n * pitch:n * pitch + n2, :] = (
                    before * w[0:1] + mid * w[1:2] + after * w[2:3] + bias)
    stages = ((tin_ref.at[0], tin_ref.at[1], cur_ref, kr0_ref, ki0_ref),
              (cur_ref, tin_ref.at[2], tin_ref.at[0], kr1_ref, ki1_ref))
    for order, (src, gate, dst, kr_ref, ki_ref) in enumerate(stages):
        skip = wide(skip_ref[order], nb)

        def stage1(j, carry):
            xb = lanes([src[i, pl.ds(j, n1h, stride=pitch), :] for i in range(nb)]).astype(BF16)
            p = _dot(f1, xb)
            twr, twi = wide(twrf_ref[j], nb), wide(twif_ref[j], nb)
            pr, pi, pn = p[:n1h], p[n1h:2 * n1h], p[2 * n1h:2 * n1h + 1]
            re, im = pr * twr[:n1h] - pi * twi[:n1h], pr * twi[:n1h] + pi * twr[:n1h]
            re_n, im_n = pn * twr[n1h:n1h + 1], pn * twi[n1h:n1h + 1]
            for i in range(nb):
                are_ref[i, pl.ds(j, n1h, stride=pitch), :] = lane_block(re, i)
                aim_ref[i, pl.ds(j, n1h, stride=pitch), :] = lane_block(im, i)
                are_ref[i, pl.ds(n1h * pitch + j, 1), :] = lane_block(re_n, i)
                aim_ref[i, pl.ds(n1h * pitch + j, 1), :] = lane_block(im_n, i)
            return carry

        lax.fori_loop(0, n2, stage1, 0, unroll=unroll)

        def slabs(ks):
            rows = [pl.ds(pl.multiple_of(k * pitch, SUBLANES), n2) for k in ks]
            a = _stack_bf16(lanes([are_ref[i, r, :] for r in rows for i in range(nb)]),
                            lanes([aim_ref[i, r, :] for r in rows for i in range(nb)]))
            x = _dot(f2, a)
            xr, xi = x[:n2], x[n2:]
            per_slab = lambda ref: lanes([wide(ref[k], nb) for k in ks])
            kr, ki = per_slab(kr_ref), per_slab(ki_ref)
            y = _dot(f2inv, _stack_bf16(xr * kr - xi * ki, xr * ki + xi * kr))
            br, bi = y[:n2], y[n2:]
            twr, twi = per_slab(twri_ref), per_slab(twii_ref)
            ore, oim = br * twr + bi * twi, bi * twr - br * twi
            for s, r in enumerate(rows):
                for i in range(nb):
                    are_ref[i, r, :] = lane_block(ore, s * nb + i)
                    aim_ref[i, r, :] = lane_block(oim, s * nb + i)

        def slab(k, carry):
            slabs((k,))
            return carry

        lax.fori_loop(0, n1h + 1, slab, 0, unroll=4)

        def stage1_inv(j, carry):
            rows = pl.ds(j, n1h, stride=pitch)
            spec = _stack_bf16(lanes([are_ref[i, rows, :] for i in range(nb)]),
                               lanes([aim_ref[i, rows, :] for i in range(nb)]))
            nyq = lanes([are_ref[i, pl.ds(n1h * pitch + j, 1), :] for i in range(nb)])
            y = _dot(ginv, spec) + nyq_w * nyq
            x = lanes([src[i, rows, :] for i in range(nb)])
            g = lanes([gate[i, rows, :] for i in range(nb)])
            out = g * (y + x * skip)
            for i in range(nb):
                dst[i, rows, :] = lane_block(out, i)
            return carry

        lax.fori_loop(0, n2, stage1_inv, 0, unroll=unroll)

    def unpitch(n, carry):
        for i in range(nb):
            o_ref[i, pl.ds(pl.multiple_of(n * n2, n2), n2), :] = (
                tin_ref[0, i, pl.ds(pl.multiple_of(n * pitch, SUBLANES), n2), :])
        return carry

    lax.fori_loop(0, n1h, unpitch, 0)


def _hyena_fused(plan, z, col0, conv_w, conv_b, kr, ki, skip, *, nb):
    b, seq_len, _ = z.shape
    ch = skip.shape[1]
    n1h, n2, kh = plan.n1 // 2, plan.n2, plan.kh
    n_cb = ch // LANES
    cb0 = col0 // LANES
    part = lambda p: pl.BlockSpec((nb, seq_len, LANES), lambda c, bb: (bb, 0, cb0 + p * n_cb + c))
    taps = lambda p: pl.BlockSpec((conv_w.shape[0], LANES), lambda c, bb: (0, p * n_cb + c))
    bias = lambda p: pl.BlockSpec((1, LANES), lambda c, bb: (0, p * n_cb + c))
    once = pl.Buffered(1)
    filt = lambda o: pl.BlockSpec((kh, n2, LANES), lambda c, bb: (0, 0, o * n_cb + c), pipeline_mode=once)
    full = lambda a: pl.BlockSpec(a.shape, lambda c, bb, _n=a.ndim: (0,) * _n, pipeline_mode=once)
    tables = (plan.f1, plan.twr_fwd, plan.twi_fwd, plan.twr_inv, plan.twi_inv, plan.f2, plan.f2inv, plan.ginv)
    pitch = n2 + SUBLANES
    spec_scratch = pltpu.VMEM((nb, kh * pitch, LANES), F32)
    return pl.pallas_call(
        functools.partial(_hyena_fused_kernel, n1h=n1h, n2=n2, unroll=min(n2, 16)),
        grid=(n_cb, b // nb),
        in_specs=[part(0), part(1), part(2), taps(0), taps(1), taps(2), bias(0), bias(1), bias(2),
                  pl.BlockSpec((skip.shape[0], 1, LANES), lambda c, bb: (0, 0, c)),
                  filt(0), filt(0), filt(1), filt(1)] + [full(a) for a in tables],
        out_specs=pl.BlockSpec((nb, seq_len, LANES), lambda c, bb: (bb, 0, c)),
        out_shape=jax.ShapeDtypeStruct((b, seq_len, ch), F32),
        scratch_shapes=[spec_scratch, spec_scratch, pltpu.VMEM((nb, n1h * pitch, LANES), F32),
                        pltpu.VMEM((3, nb, n1h * pitch, LANES), F32)],
        compiler_params=_params(("parallel", "arbitrary"), 56),
        name="hyena_fused",
    )(z, z, z, conv_w, conv_w, conv_w, conv_b, conv_b, conv_b,
      skip.reshape(skip.shape[0], 1, ch), kr, ki, kr, ki, *tables)


def _dft_shape(seq_len):
    n2 = 128
    return 2 * seq_len // n2, n2


def _block_cols(n2, rows, target_bytes=1 << 20):
    g = max(SUBLANES, min(n2, target_bytes // (rows * LANES * 4)) // SUBLANES * SUBLANES)
    while n2 % g:
        g -= SUBLANES
    return g


def _slabs_per_step(kh, slab_bytes, target_bytes=3 << 20):
    ks = max(1, min(kh, target_bytes // slab_bytes))
    while kh % ks:
        ks -= 1
    return ks


def _hyena(plan, z, col0, conv_w, conv_b, filt, skip):
    b, seq_len, _ = z.shape
    n_orders, ch = skip.shape
    g = _block_cols(plan.n2, plan.n1 // 2)
    ks_filt = _slabs_per_step(plan.kh, 2 * plan.n2 * LANES * 4, target_bytes=6 << 20)
    kr, ki = _filt_spec(plan, filt, n_orders, ks=ks_filt)
    nb = 2 if b % 2 == 0 else 1
    fused_bytes = 15 * nb * seq_len * LANES * 4
    if n_orders == 2 and fused_bytes <= FUSED_VMEM_BUDGET:
        return _hyena_fused(plan, z, col0, conv_w, conv_b, kr, ki, skip, nb=nb)
    uc = _sconv(z, conv_w, conv_b, rows=min(seq_len, 512), col0=col0, n_parts=n_orders + 1)
    ks = _slabs_per_step(plan.kh, b * plan.n2 * ch * 4)
    skip3 = skip.reshape(n_orders, 1, ch)
    cur, cur_part = uc, 0
    for order in range(n_orders):
        are, aim = _dft_fwd(plan, cur, cur_part, g=g)
        bre, bim = _conv_mid(plan, are, aim, kr, ki, order, ks=ks)
        cur = _dft_inv(plan, bre, bim, cur, cur_part, uc, order + 1, skip3, order, g=g)
        cur_part = 0
    return cur[0]


def _trunk(x3, p, plan, filt):
    b, seq_len, d = x3.shape
    t = b * seq_len
    x = x3.reshape(t, d)
    x, z = _ffn_in(x, p["g1pre"], p["g1post"], p["w1gu"], p["w1d"], p["gmix"], p["w_in"], tm=512)
    z = z.reshape(b, seq_len, -1)
    d_pool = p["pool_w"].shape[0] * LANES
    y_pool = _pool(z, p["pool_w"], p["pool_scale"], rows=min(seq_len, 512))
    y_hy = _hyena(plan, z, d_pool, p["conv_w"], p["conv_b"], filt, p["skip"])
    x = _ffn_out(x, y_pool.reshape(t, -1), y_hy.reshape(t, -1), p["gpool"], p["ghy"], p["w_out"],
                 p["gmixpost"], p["g2pre"], p["g2post"], p["w2gu"], p["w2d"], tm=512)
    return x.reshape(b, seq_len, d)


def kernel(x_prompt, x_sample, ffn1_norm_pre, ffn1_norm_post, ffn1_w_gate_up, ffn1_w_down, mix_norm_pre, w_in, pool_w_map, pool_scale, hyena_conv_w, hyena_conv_b, filt_w_first, filt_b_first, filt_w_hidden, filt_b_hidden, filt_w_last, filt_freq, hyena_skip, pool_out_norm, hyena_out_norm, w_out, mix_norm_post, ffn2_norm_pre, ffn2_norm_post, ffn2_w_gate_up, ffn2_w_down):
    assert ffn1_norm_pre.shape[0] == 1, "single-layer trunk"
    row = lambda a: a[0].reshape(1, -1)
    p = dict(
        g1pre=row(ffn1_norm_pre), g1post=row(ffn1_norm_post),
        w1gu=ffn1_w_gate_up[0].astype(BF16), w1d=ffn1_w_down[0].astype(BF16),
        gmix=row(mix_norm_pre), w_in=w_in[0].astype(BF16),
        pool_w=pool_w_map[0].astype(BF16), pool_scale=row(pool_scale),
        conv_w=hyena_conv_w[0], conv_b=row(hyena_conv_b),
        skip=hyena_skip[0], gpool=row(pool_out_norm), ghy=row(hyena_out_norm),
        w_out=w_out[0].astype(BF16), gmixpost=row(mix_norm_post),
        g2pre=row(ffn2_norm_pre), g2post=row(ffn2_norm_post),
        w2gu=ffn2_w_gate_up[0].astype(BF16), w2d=ffn2_w_down[0].astype(BF16),
    )
    outs = []
    for x3 in (x_prompt, x_sample):
        seq_len = x3.shape[1]
        plan = _Plan(seq_len, *_dft_shape(seq_len))
        filt = _filt_gen(seq_len, filt_w_first[0], filt_b_first[0], filt_w_hidden[0],
                         filt_b_hidden[0], filt_w_last[0], filt_freq[0], rows=min(seq_len, 512),
                         n2=plan.n2)
        outs.append(_trunk(x3, p, plan, filt))
    return tuple(outs)
```

```python
import functools
import math

import jax
import jax.numpy as jnp
import numpy as np
from jax import lax
from jax.experimental import pallas as pl
from jax.experimental.pallas import tpu as pltpu

F32 = jnp.float32
BF16 = jnp.bfloat16
EPS = 1e-6
LANES = 128
SUBLANES = 8
MXU_DIM = 256
POOL_WINDOWS = (2, 4, 8, 16)
HALO = 8
DECAY_TARGET = 1e-2
FAST_DECAY_PCT = 0.3
SLOW_DECAY_PCT = 1.5
MAX_DECAY = math.log(DECAY_TARGET) / FAST_DECAY_PCT
MIN_DECAY = math.log(DECAY_TARGET) / SLOW_DECAY_PCT
HIGHEST = lax.Precision.HIGHEST
FUSED_VMEM_BUDGET = 32 << 20


def _params(sem, vmem_mib):
    return pltpu.CompilerParams(dimension_semantics=sem, vmem_limit_bytes=vmem_mib << 20)


def _rms(x, g):
    inv = lax.rsqrt(jnp.mean(x * x, axis=-1, keepdims=True) + EPS)
    return (x * inv) * g


def _dot(a, b):
    return jnp.dot(a, b, preferred_element_type=F32)


def _dot3(a, b):
    a_hi, b_hi = a.astype(BF16), b.astype(BF16)
    a_lo = (a - a_hi.astype(F32)).astype(BF16)
    b_lo = (b - b_hi.astype(F32)).astype(BF16)
    return _dot(a_hi, b_hi) + _dot(a_hi, b_lo) + _dot(a_lo, b_hi)


def _lanes(parts):
    return parts[0] if len(parts) == 1 else jnp.concatenate(parts, axis=1)


def _lane_block(a, i):
    return a[:, i * LANES:(i + 1) * LANES]


def _ff_chunks(d_ff, n_chunks=2):
    if d_ff % MXU_DIM:
        return (0, d_ff)
    tiles = d_ff // MXU_DIM
    return tuple(MXU_DIM * ((tiles * k + n_chunks - 1) // n_chunks) for k in range(n_chunks + 1))


def _half_step_ffn(x, gpre, gpost, wgu_ref, wd_ref):
    d_ff = wd_ref.shape[0]
    h = _rms(x, gpre).astype(BF16)
    acc = None
    bounds = _ff_chunks(d_ff)
    for lo, hi in zip(bounds[:-1], bounds[1:]):
        gate = _dot(h, wgu_ref[:, lo:hi])
        up = _dot(h, wgu_ref[:, d_ff + lo:d_ff + hi])
        act = (gate * jax.nn.sigmoid(gate) * up).astype(BF16)
        part = _dot(act, wd_ref[lo:hi, :])
        acc = part if acc is None else acc + part
    return x + 0.5 * _rms(acc, gpost)


def _ffn_in_kernel(x_ref, gpre_ref, gpost_ref, wgu_ref, wd_ref, gmix_ref, win_ref, o_ref, z_ref):
    x = _half_step_ffn(x_ref[...], gpre_ref[...], gpost_ref[...], wgu_ref, wd_ref)
    o_ref[...] = x
    z_ref[...] = _dot(_rms(x, gmix_ref[...]).astype(BF16), win_ref[...])


def _ffn_in(x, g_pre, g_post, wgu, wd, g_mix, w_in, *, tm):
    t, d = x.shape
    d_ff, d_in = wd.shape[0], w_in.shape[1]
    const = lambda i: (0, 0)
    row = lambda i: (i, 0)
    once = pl.Buffered(1)
    vec = pl.BlockSpec((1, d), const)
    return pl.pallas_call(
        _ffn_in_kernel,
        grid=(t // tm,),
        in_specs=[
            pl.BlockSpec((tm, d), row), vec, vec,
            pl.BlockSpec((d, 2 * d_ff), const, pipeline_mode=once),
            pl.BlockSpec((d_ff, d), const, pipeline_mode=once),
            vec,
            pl.BlockSpec((d, d_in), const, pipeline_mode=once),
        ],
        out_specs=[pl.BlockSpec((tm, d), row), pl.BlockSpec((tm, d_in), row)],
        out_shape=[jax.ShapeDtypeStruct((t, d), F32), jax.ShapeDtypeStruct((t, d_in), F32)],
        compiler_params=_params(("parallel",), 56),
        name="ffn_in",
    )(x, g_pre, g_post, wgu, wd, g_mix, w_in)


def _ffn_out_kernel(x_ref, yp_ref, yh_ref, gp_ref, gh_ref, wout_ref, gmix_ref,
                    gpre_ref, gpost_ref, wgu_ref, wd_ref, o_ref):
    d_pool = yp_ref.shape[1]
    yp = _rms(yp_ref[...], gp_ref[...]).astype(BF16)
    yh = _rms(yh_ref[...], gh_ref[...]).astype(BF16)
    y = _dot(yp, wout_ref[:d_pool, :]) + _dot(yh, wout_ref[d_pool:, :])
    x = x_ref[...] + _rms(y, gmix_ref[...])
    o_ref[...] = _half_step_ffn(x, gpre_ref[...], gpost_ref[...], wgu_ref, wd_ref)


def _ffn_out(x, yp, yh, gp, gh, w_out, g_mix, g_pre, g_post, wgu, wd, *, tm):
    t, d = x.shape
    d_ff, dp, dh = wd.shape[0], yp.shape[1], yh.shape[1]
    const = lambda i: (0, 0)
    row = lambda i: (i, 0)
    once = pl.Buffered(1)
    vec = lambda n: pl.BlockSpec((1, n), const)
    return pl.pallas_call(
        _ffn_out_kernel,
        grid=(t // tm,),
        in_specs=[
            pl.BlockSpec((tm, d), row), pl.BlockSpec((tm, dp), row), pl.BlockSpec((tm, dh), row),
            vec(dp), vec(dh),
            pl.BlockSpec((dp + dh, d), const, pipeline_mode=once),
            vec(d), vec(d), vec(d),
            pl.BlockSpec((d, 2 * d_ff), const, pipeline_mode=once),
            pl.BlockSpec((d_ff, d), const, pipeline_mode=once),
        ],
        out_specs=pl.BlockSpec((tm, d), row),
        out_shape=jax.ShapeDtypeStruct((t, d), F32),
        compiler_params=_params(("parallel",), 56),
        name="ffn_out",
    )(x, yp, yh, gp, gh, w_out, g_mix, g_pre, g_post, wgu, wd)


def _with_halo(prev_ref, main_ref, next_ref, i, n_tiles):
    prev = jnp.where(i > 0, prev_ref[0], 0.0)
    nxt = jnp.where(i < n_tiles - 1, next_ref[0], 0.0)
    return jnp.concatenate([prev, main_ref[0], nxt], axis=0)


def _pool_kernel(prev_ref, main_ref, next_ref, wmap_ref, scale_ref, o_ref, *, seq_len, n_tiles):
    i = pl.program_id(1)
    rows = main_ref.shape[1]
    n_ext = rows + 2 * HALO
    ext_all = _with_halo(prev_ref, main_ref, next_ref, i, n_tiles)
    pos = i * rows + lax.broadcasted_iota(jnp.int32, (rows, LANES), 0)
    for grp, window in enumerate(POOL_WINDOWS):
        lanes = slice(grp * LANES, (grp + 1) * LANES)
        ext = ext_all[:, lanes]
        ssum, w = ext + pltpu.roll(ext, 1, 0), 2
        while w < window:
            ssum, w = pltpu.roll(ssum, w // 2, 0) + pltpu.roll(ssum, n_ext - w // 2, 0), 2 * w
        u = main_ref[0, :, lanes]
        lo = jnp.clip(pos - window // 2, 0, seq_len)
        hi = jnp.clip(pos + (window - window // 2), 0, seq_len)
        d = ssum[HALO:HALO + rows] / (hi - lo).astype(F32) - u
        y = _dot(d.astype(BF16), wmap_ref[grp])
        o_ref[0, :, lanes] = y * scale_ref[:, lanes]


def _halo_specs(rows, width, seq_len, col_of):
    blocks_per_tile = rows // HALO
    last = seq_len // HALO - 1
    prev = pl.BlockSpec((1, HALO, width),
                        lambda b, i, c: (b, jnp.maximum(i * blocks_per_tile - 1, 0), col_of(c)))
    main = pl.BlockSpec((1, rows, width), lambda b, i, c: (b, i, col_of(c)))
    nxt = pl.BlockSpec((1, HALO, width),
                       lambda b, i, c: (b, jnp.minimum((i + 1) * blocks_per_tile, last), col_of(c)))
    return [prev, main, nxt]


def _pool(z, wmap, scale, *, rows):
    b, seq_len, _ = z.shape
    assert wmap.shape[0] == len(POOL_WINDOWS) and max(POOL_WINDOWS) <= 2 * HALO
    width = wmap.shape[0] * LANES
    n_tiles = seq_len // rows
    return pl.pallas_call(
        functools.partial(_pool_kernel, seq_len=seq_len, n_tiles=n_tiles),
        grid=(b, n_tiles, 1),
        in_specs=_halo_specs(rows, width, seq_len, lambda c: 0) + [
            pl.BlockSpec(wmap.shape, lambda b, i, c: (0, 0, 0)),
            pl.BlockSpec((1, width), lambda b, i, c: (0, 0)),
        ],
        out_specs=pl.BlockSpec((1, rows, width), lambda b, i, c: (b, i, 0)),
        out_shape=jax.ShapeDtypeStruct((b, seq_len, width), F32),
        compiler_params=_params(("parallel", "parallel", "parallel"), 40),
        name="pool",
    )(z, z, z, wmap, scale)


def _sconv_kernel(prev_ref, main_ref, next_ref, w_ref, b_ref, o_ref, *, n_tiles):
    i = pl.program_id(1)
    rows = main_ref.shape[1]
    ext = _with_halo(prev_ref, main_ref, next_ref, i, n_tiles)
    n_ext = rows + 2 * HALO
    before = pltpu.roll(ext, 1, 0)[HALO:HALO + rows]
    after = pltpu.roll(ext, n_ext - 1, 0)[HALO:HALO + rows]
    w = w_ref[...]
    out = before * w[0:1] + main_ref[0] * w[1:2] + after * w[2:3]
    o_ref[0, 0] = out + b_ref[...]


def _sconv(z, w, bias, *, rows, col0, n_parts):
    b, seq_len, _ = z.shape
    n_tiles = seq_len // rows
    width = w.shape[1] // n_parts
    assert col0 % width == 0
    return pl.pallas_call(
        functools.partial(_sconv_kernel, n_tiles=n_tiles),
        grid=(b, n_tiles, n_parts),
        in_specs=_halo_specs(rows, width, seq_len, lambda c: c + col0 // width) + [
            pl.BlockSpec((w.shape[0], width), lambda b, i, c: (0, c)),
            pl.BlockSpec((1, width), lambda b, i, c: (0, c)),
        ],
        out_specs=pl.BlockSpec((1, 1, rows, width), lambda b, i, c: (c, b, i, 0)),
        out_shape=jax.ShapeDtypeStruct((n_parts, b, seq_len, width), F32),
        compiler_params=_params(("parallel", "parallel", "parallel"), 40),
        name="sconv",
    )(z, z, z, w, bias)


def _filt_gen_kernel(bands_ref, wt_ref, wc_ref, ws_ref, b1_ref, wh_ref, bh_ref, freq_ref,
                     wl_ref, delta_ref, o_ref, *, seq_len, d_ch):
    rows = o_ref.shape[0] * (o_ref.shape[1] - SUBLANES)
    base = pl.program_id(0) * rows
    m_lane = (base + lax.broadcasted_iota(jnp.int32, (1, rows), 1)).astype(F32)
    t_lane = m_lane / (seq_len - 1.0)
    ang = (bands_ref[...] * (2.0 * math.pi / seq_len)) * m_lane
    freq = freq_ref[...]
    pre = (jnp.dot(wc_ref[...], jnp.cos(ang), precision=HIGHEST, preferred_element_type=F32)
           + jnp.dot(ws_ref[...], -jnp.sin(ang), precision=HIGHEST, preferred_element_type=F32)
           + wt_ref[...] * t_lane + b1_ref[...])
    h = jnp.sin(freq * pre)
    for layer in range(wh_ref.shape[0]):
        pre = jnp.dot(wh_ref[layer], h, precision=HIGHEST, preferred_element_type=F32) + bh_ref[layer]
        h = jnp.sin(freq * pre)
    out = _dot3(h.T, wl_ref[...])
    m_row = base + lax.broadcasted_iota(jnp.int32, (rows, d_ch), 0)
    t_row = m_row.astype(F32) / (seq_len - 1.0)
    decay = jnp.exp(-t_row * jnp.abs(delta_ref[...]))
    decay_bwd = jnp.where(m_row == 0, 0.0, decay)
    n_chunks, pitch, _ = o_ref.shape
    n2 = rows // n_chunks
    for q in range(out.shape[1] // d_ch):
        dq = decay_bwd if q % 2 == 1 else decay
        val = out[:, q * d_ch:(q + 1) * d_ch] * dq
        for n in range(n_chunks):
            o_ref[n, :n2, q * d_ch:(q + 1) * d_ch] = val[n * n2:(n + 1) * n2]
    o_ref[:, n2:, :] = jnp.zeros((n_chunks, pitch - n2, o_ref.shape[2]), F32)


def _filt_gen(seq_len, w_first, b_first, w_hidden, b_hidden, w_last, freq, *, rows, n2):
    pos_bands = (w_first.shape[0] - 1) // 2
    hidden = w_first.shape[1]
    n_cols = w_last.shape[1]
    d_ch = n_cols // 4
    bands = jnp.linspace(1e-4, pos_bands - 1, pos_bands, dtype=F32).reshape(pos_bands, 1)
    deltas = jnp.linspace(MIN_DECAY, MAX_DECAY, d_ch, dtype=F32).reshape(1, d_ch)
    w1t = w_first.T
    args = (bands, w1t[:, 0:1], w1t[:, 1:1 + pos_bands], w1t[:, 1 + pos_bands:],
            b_first.reshape(hidden, 1), jnp.swapaxes(w_hidden, 1, 2),
            b_hidden.reshape(b_hidden.shape[0], hidden, 1), freq.reshape(hidden, 1),
            w_last, deltas)
    full = lambda a: pl.BlockSpec(a.shape, lambda i, _n=a.ndim: (0,) * _n)
    return pl.pallas_call(
        functools.partial(_filt_gen_kernel, seq_len=seq_len, d_ch=d_ch),
        grid=(seq_len // rows,),
        in_specs=[full(a) for a in args],
        out_specs=pl.BlockSpec((rows // n2, n2 + SUBLANES, n_cols), lambda i: (i, 0, 0)),
        out_shape=jax.ShapeDtypeStruct((seq_len // n2, n2 + SUBLANES, n_cols), F32),
        compiler_params=_params(("parallel",), 40),
        name="filt_gen",
    )(*args)


class _Plan:
    def __init__(self, seq_len, n1, n2):
        assert n1 * n2 == 2 * seq_len and n1 % (2 * SUBLANES) == 0 and n2 % SUBLANES == 0
        self.seq_len, self.n1, self.n2 = seq_len, n1, n2
        n = n1 * n2
        n1h = n1 // 2
        self.kh = kh = n1h + 1
        k1 = np.arange(kh, dtype=np.float64)[:, None]
        m1 = np.arange(n1h, dtype=np.float64)[None, :]
        th1 = 2.0 * np.pi * k1 * m1 / n1
        f1 = np.zeros((n1 + SUBLANES, n1h))
        f1[:n1h] = np.cos(th1[:n1h]); f1[n1h:n1] = -np.sin(th1[:n1h]); f1[n1] = np.cos(th1[n1h])
        self.f1 = jnp.asarray(f1, F32)
        m2 = np.arange(n2, dtype=np.float64)[None, :]
        tht = 2.0 * np.pi * k1 * m2 / n
        twr, twi = np.cos(tht), -np.sin(tht)
        rep = lambda a: np.repeat(a[:, :, None], LANES, axis=2)
        self.twr_inv, self.twi_inv = jnp.asarray(rep(twr), F32), jnp.asarray(rep(twi), F32)
        pad = np.zeros((n2, SUBLANES - 1))
        self.twr_fwd = jnp.asarray(rep(np.concatenate([twr.T, pad], 1)), F32)
        self.twi_fwd = jnp.asarray(rep(np.concatenate([twi.T, pad], 1)), F32)
        k2 = np.arange(n2, dtype=np.float64)[:, None]
        th2 = 2.0 * np.pi * k2 * m2 / n2
        f2r, f2i = np.cos(th2), -np.sin(th2)
        self.f2 = jnp.asarray(np.block([[f2r, -f2i], [f2i, f2r]]), F32)
        self.f2inv = jnp.asarray(np.block([[f2r, f2i], [-f2i, f2r]]), F32)
        wgt = np.where(k1[:n1h] == 0, 1.0, 2.0) / n
        ginv = np.concatenate([(wgt * np.cos(th1[:n1h])).T, (-wgt * np.sin(th1[:n1h])).T], axis=1)
        self.ginv = jnp.asarray(ginv, F32)
        k1f = np.arange(kh, dtype=np.float64)[:, None]
        full = lambda cols: 2.0 * np.pi * k1f * cols[None, :] / n1
        stage1 = lambda th: np.concatenate(
            [np.cos(th[:n1h]), -np.sin(th[:n1h]), np.cos(th[n1h:]), np.zeros((SUBLANES - 1, th.shape[1]))], 0)
        fwd_cols = np.arange(n1h, dtype=np.float64)
        f1a = stage1(full(np.concatenate([fwd_cols, n1 - 1 - fwd_cols])))
        f1b = stage1(full(np.concatenate([fwd_cols, (n1 - fwd_cols) % n1])))
        f1b[:, n1h] = 0.0
        self.f1_filt, self.f1_filt0 = jnp.asarray(f1a, F32), jnp.asarray(f1b, F32)
        rows = lambda t: rep(np.concatenate([t.T, np.zeros((t.shape[1], SUBLANES - 1))], 1))
        coarse, fine = tht[:, ::SUBLANES], tht[:, :SUBLANES]
        self.tw_coarse = (jnp.asarray(rows(np.cos(coarse)), F32), jnp.asarray(rows(-np.sin(coarse)), F32))
        self.tw_fine = (jnp.asarray(rows(np.cos(fine)), F32), jnp.asarray(rows(-np.sin(fine)), F32))


def _dft_fwd_kernel(x_ref, f1_ref, twr_ref, twi_ref, re_ref, im_ref, *, unroll):
    _, _, n1h, g, _ = x_ref.shape
    x2 = x_ref.at[0, 0].reshape(n1h * g, LANES)
    re2 = re_ref.at[0].reshape((n1h + 1) * g, LANES)
    im2 = im_ref.at[0].reshape((n1h + 1) * g, LANES)
    f1 = f1_ref[...].astype(BF16)

    def body(jp, carry):
        js = (2 * jp, 2 * jp + 1)
        xb = _lanes([x2[pl.ds(j, n1h, stride=g), :] for j in js]).astype(BF16)
        p = _dot(f1, xb)
        twr, twi = _lanes([twr_ref[j] for j in js]), _lanes([twi_ref[j] for j in js])
        pr, pi, pn = p[:n1h], p[n1h:2 * n1h], p[2 * n1h:2 * n1h + 1]
        re, im = pr * twr[:n1h] - pi * twi[:n1h], pr * twi[:n1h] + pi * twr[:n1h]
        re_n, im_n = pn * twr[n1h:n1h + 1], pn * twi[n1h:n1h + 1]
        for i, j in enumerate(js):
            re2[pl.ds(j, n1h, stride=g), :] = _lane_block(re, i)
            im2[pl.ds(j, n1h, stride=g), :] = _lane_block(im, i)
            re2[pl.ds(n1h * g + j, 1), :] = _lane_block(re_n, i)
            im2[pl.ds(n1h * g + j, 1), :] = _lane_block(im_n, i)
        return carry

    lax.fori_loop(0, g // 2, body, 0, unroll=unroll)


def _dft_fwd(plan, x4, part, *, g):
    _, b, seq_len, ch = x4.shape
    n1h, n2, kh = plan.n1 // 2, plan.n2, plan.kh
    xv = x4.reshape(x4.shape[0], b, n1h, n2, ch)
    out = jax.ShapeDtypeStruct((b, kh, n2, ch), F32)
    tw_spec = pl.BlockSpec((g, n1h + SUBLANES, LANES), lambda j, bb, c: (j, 0, 0))
    return pl.pallas_call(
        functools.partial(_dft_fwd_kernel, unroll=min(g // 2, 8)),
        grid=(n2 // g, b, ch // LANES),
        in_specs=[
            pl.BlockSpec((1, 1, n1h, g, LANES), lambda j, bb, c: (part, bb, 0, j, c)),
            pl.BlockSpec(plan.f1.shape, lambda j, bb, c: (0, 0)),
            tw_spec, tw_spec,
        ],
        out_specs=[pl.BlockSpec((1, kh, g, LANES), lambda j, bb, c: (bb, 0, j, c))] * 2,
        out_shape=[out, out],
        compiler_params=_params(("parallel", "parallel", "parallel"), 48),
        name="dft_fwd",
    )(xv, plan.f1, plan.twr_fwd, plan.twi_fwd)


def _stack_bf16(re, im):
    return jnp.concatenate([re.astype(BF16), im.astype(BF16)], axis=0)


def _filt_spec_kernel(h_ref, f1_ref, f1j0_ref, twcr_ref, twci_ref, twfr_ref, twfi_ref, f2_ref,
                      kr_ref, ki_ref, are_ref, aim_ref, *, n1h, n2, unroll):
    pitch = n2 + SUBLANES
    ks = kr_ref.shape[0]
    phase = pl.program_id(1)

    def stage1(f1, cols, accumulate):
        x = _lanes([h_ref[pl.ds(start, n1h, stride=pitch), :] for _, start, _, _ in cols])
        p = _dot(f1, x.astype(BF16))
        cr, ci = _lanes([twcr_ref[a] for _, _, a, _ in cols]), _lanes([twci_ref[a] for _, _, a, _ in cols])
        fr, fi = _lanes([twfr_ref[b] for _, _, _, b in cols]), _lanes([twfi_ref[b] for _, _, _, b in cols])
        twr, twi = cr * fr - ci * fi, cr * fi + ci * fr
        pr, pi, pn = p[:n1h], p[n1h:2 * n1h], p[2 * n1h:2 * n1h + 1]
        re, im = pr * twr[:n1h] - pi * twi[:n1h], pr * twi[:n1h] + pi * twr[:n1h]
        re_n, im_n = pn * twr[n1h:n1h + 1], pn * twi[n1h:n1h + 1]
        for i, (j, _, _, _) in enumerate(cols):
            main, nyq = pl.ds(j, n1h, stride=pitch), pl.ds(n1h * pitch + j, 1)
            for ref, val, rows in ((are_ref, re, main), (aim_ref, im, main), (are_ref, re_n, nyq), (aim_ref, im_n, nyq)):
                blk = _lane_block(val, i)
                ref[rows, :] = ref[rows, :] + blk if accumulate else blk

    def run_stage1(f1_j0, f1, start_of, accumulate):
        stage1(f1_j0, [(0, 0, 0, 0)], accumulate)
        stage1(f1, [(1, start_of(1), 0, 1)], accumulate)

        def body(jp, carry):
            a = lax.shift_right_logical(jp, 2)
            b = 2 * jnp.bitwise_and(jp, SUBLANES // 2 - 1)
            stage1(f1, [(2 * jp, start_of(2 * jp), a, b), (2 * jp + 1, start_of(2 * jp + 1), a, b + 1)], accumulate)
            return carry

        lax.fori_loop(1, n2 // 2, body, 0, unroll=unroll)

    @pl.when(phase == 0)
    def _():
        f1 = f1_ref[:, :n1h].astype(BF16)
        run_stage1(f1, f1, lambda j: j, False)

    @pl.when(phase == 1)
    def _():
        run_stage1(f1j0_ref[:, n1h:].astype(BF16), f1_ref[:, n1h:].astype(BF16), lambda j: n2 - j, True)

    @pl.when(phase >= 2)
    def _():
        f2 = f2_ref[...].astype(BF16)
        kc = phase - 2

        def slabs(first, count):
            rows = [pl.ds(pl.multiple_of((kc * ks + first + i) * pitch, SUBLANES), n2) for i in range(count)]
            x = _dot(f2, _stack_bf16(_lanes([are_ref[r, :] for r in rows]), _lanes([aim_ref[r, :] for r in rows])))
            for i in range(count):
                kr_ref[first + i] = _lane_block(x[:n2], i)
                ki_ref[first + i] = _lane_block(x[n2:], i)

        def pair(p, carry):
            slabs(2 * p, 2)
            return carry

        lax.fori_loop(0, ks // 2, pair, 0, unroll=2)
        if ks % 2:
            slabs(ks - 1, 1)


def _filt_spec(plan, filt, n_orders, *, ks):
    n1h, n2, kh = plan.n1 // 2, plan.n2, plan.kh
    cols = filt.shape[2]
    filt = filt.reshape(n1h * (n2 + SUBLANES), cols)
    n_cb = cols // (2 * n_orders * LANES)
    pitch = n2 + SUBLANES
    col = lambda c, p: (c // n_cb) * 2 * n_cb + jnp.minimum(p, 1) * n_cb + c % n_cb
    full = lambda a: pl.BlockSpec(a.shape, lambda c, p, _n=a.ndim: (0,) * _n, pipeline_mode=pl.Buffered(1))
    tables = (plan.f1_filt, plan.f1_filt0, *plan.tw_coarse, *plan.tw_fine, plan.f2)
    out = jax.ShapeDtypeStruct((kh, n2, cols // 2), F32)
    scratch = pltpu.VMEM((kh * pitch, LANES), F32)
    return pl.pallas_call(
        functools.partial(_filt_spec_kernel, n1h=n1h, n2=n2, unroll=4),
        grid=(n_orders * n_cb, 2 + kh // ks),
        in_specs=[pl.BlockSpec((n1h * pitch, LANES), lambda c, p: (0, col(c, p)))] + [full(a) for a in tables],
        out_specs=[pl.BlockSpec((ks, n2, LANES), lambda c, p: (jnp.maximum(p - 2, 0), 0, c))] * 2,
        out_shape=[out, out],
        scratch_shapes=[scratch, scratch],
        compiler_params=_params(("parallel", "arbitrary"), 56),
        name="filt_spec",
    )(filt, *tables)


def _conv_mid_kernel(re_ref, im_ref, kr_ref, ki_ref, twr_ref, twi_ref, f2_ref, f2inv_ref,
                     ore_ref, oim_ref):
    nb, ks, n2, ch = re_ref.shape
    reps = ch // LANES
    f2, f2inv = f2_ref[...].astype(BF16), f2inv_ref[...].astype(BF16)
    for s in range(ks):
        kr, ki = kr_ref[s], ki_ref[s]
        twr = jnp.tile(twr_ref[s], (1, reps))
        twi = jnp.tile(twi_ref[s], (1, reps))

        def body(b, carry):
            x = _dot(f2, _stack_bf16(re_ref[b, s], im_ref[b, s]))
            xr, xi = x[:n2], x[n2:]
            y = _dot(f2inv, _stack_bf16(xr * kr - xi * ki, xr * ki + xi * kr))
            br, bi = y[:n2], y[n2:]
            ore_ref[b, s] = br * twr + bi * twi
            oim_ref[b, s] = bi * twr - br * twi
            return carry

        lax.fori_loop(0, nb, body, 0)


def _conv_mid(plan, are, aim, kr, ki, order, *, ks):
    b, kh, n2, ch = are.shape
    data = pl.BlockSpec((b, ks, n2, ch), lambda k: (0, k, 0, 0))
    filt = pl.BlockSpec((ks, n2, ch), lambda k: (k, 0, order))
    tw = pl.BlockSpec((ks, n2, LANES), lambda k: (k, 0, 0))
    mat = pl.BlockSpec((2 * n2, 2 * n2), lambda k: (0, 0))
    out = jax.ShapeDtypeStruct((b, kh, n2, ch), F32)
    return pl.pallas_call(
        _conv_mid_kernel,
        grid=(kh // ks,),
        in_specs=[data, data, filt, filt, tw, tw, mat, mat],
        out_specs=[data, data],
        out_shape=[out, out],
        compiler_params=_params(("parallel",), 48),
        name="conv_mid",
    )(are, aim, kr, ki, plan.twr_inv, plan.twi_inv, plan.f2, plan.f2inv)


def _dft_inv_kernel(bre_ref, bim_ref, x_ref, gate_ref, skip_ref, ginv_ref, o_ref, *, unroll, scale):
    _, _, n1h, g, _ = o_ref.shape
    b_re = bre_ref.at[0].reshape((n1h + 1) * g, LANES)
    b_im = bim_ref.at[0].reshape((n1h + 1) * g, LANES)
    x2 = x_ref.at[0, 0].reshape(n1h * g, LANES)
    gate2 = gate_ref.at[0, 0].reshape(n1h * g, LANES)
    o2 = o_ref.at[0, 0].reshape(n1h * g, LANES)
    ginv = ginv_ref[...].astype(BF16)
    skip = skip_ref[0]
    n1 = lax.broadcasted_iota(jnp.int32, (n1h, 2 * LANES), 0)
    nyq_w = jnp.where(jnp.bitwise_and(n1, 1) == 0, scale, -scale)
    skip = jnp.tile(skip, (1, 2))

    def body(jp, carry):
        rows = [pl.ds(2 * jp + i, n1h, stride=g) for i in range(2)]
        spec = _stack_bf16(_lanes([b_re[r, :] for r in rows]), _lanes([b_im[r, :] for r in rows]))
        nyq = _lanes([b_re[pl.ds(n1h * g + 2 * jp + i, 1), :] for i in range(2)])
        y = _dot(ginv, spec) + nyq_w * nyq
        out = _lanes([gate2[r, :] for r in rows]) * (y + _lanes([x2[r, :] for r in rows]) * skip)
        for i, r in enumerate(rows):
            o2[r, :] = _lane_block(out, i)
        return carry

    lax.fori_loop(0, g // 2, body, 0, unroll=unroll)


def _dft_inv(plan, bre, bim, x4, x_part, gate4, gate_part, skip, order, *, g):
    b, seq_len, ch = x4.shape[1:]
    n1h, n2, kh = plan.n1 // 2, plan.n2, plan.kh
    xv = x4.reshape(x4.shape[0], b, n1h, n2, ch)
    gv = gate4.reshape(gate4.shape[0], b, n1h, n2, ch)
    spec = pl.BlockSpec((1, kh, g, LANES), lambda j, bb, c: (bb, 0, j, c))
    time = lambda part: pl.BlockSpec((1, 1, n1h, g, LANES), lambda j, bb, c: (part, bb, 0, j, c))
    out = pl.pallas_call(
        functools.partial(_dft_inv_kernel, unroll=min(g // 2, 8), scale=1.0 / (plan.n1 * n2)),
        grid=(n2 // g, b, ch // LANES),
        in_specs=[
            spec, spec, time(x_part), time(gate_part),
            pl.BlockSpec((1, 1, LANES), lambda j, bb, c: (order, 0, c)),
            pl.BlockSpec(plan.ginv.shape, lambda j, bb, c: (0, 0)),
        ],
        out_specs=time(0),
        out_shape=jax.ShapeDtypeStruct((1, b, n1h, n2, ch), F32),
        compiler_params=_params(("parallel", "parallel", "parallel"), 48),
        name="dft_inv",
    )(bre, bim, xv, gv, skip, plan.ginv)
    return out.reshape(1, b, seq_len, ch)


def _hyena_fused_kernel(zv_ref, zx1_ref, zx2_ref, wv_ref, wx1_ref, wx2_ref, bv_ref, bx1_ref, bx2_ref,
                        skip_ref, kr0_ref, ki0_ref, kr1_ref, ki1_ref,
                        f1_ref, twrf_ref, twif_ref, twri_ref, twii_ref, f2_ref, f2inv_ref, ginv_ref,
                        o_ref, are_ref, aim_ref, cur_ref, tin_ref, *, n1h, n2, unroll):
    nb = o_ref.shape[0]
    pitch = n2 + SUBLANES
    f1 = f1_ref[...].astype(BF16)
    f2, f2inv = f2_ref[...].astype(BF16), f2inv_ref[...].astype(BF16)
    ginv = ginv_ref[...].astype(BF16)
    scale = 1.0 / (2 * n1h * n2)
    n1 = lax.broadcasted_iota(jnp.int32, (n1h, 2 * nb * LANES), 0)
    nyq_w = jnp.where(jnp.bitwise_and(n1, 1) == 0, scale, -scale)
    lanes, lane_block = _lanes, _lane_block
    wide = lambda t, reps: t if reps == 1 else jnp.tile(t, (1, reps))

    row = lax.broadcasted_iota(jnp.int32, (n2, LANES), 0)
    for part, (z_ref, w_ref, b_ref) in enumerate(((zv_ref, wv_ref, bv_ref), (zx1_ref, wx1_ref, bx1_ref),
                                                   (zx2_ref, wx2_ref, bx2_ref))):
        w, bias = w_ref[...], b_ref[...]
        for i in range(nb):
            for n in range(n1h):
                lo = n * n2
                mid = z_ref[i, lo:lo + n2, :]
                if n > 0:
                    before = z_ref[i, lo - 1:lo + n2 - 1, :]
                else:
                    before = jnp.where(row == 0, 0.0, pltpu.roll(mid, 1, 0))
                if n < n1h - 1:
                    after = z_ref[i, lo + 1:lo + n2 + 1, :]
                else:
                    after = jnp.where(row == n2 - 1, 0.0, pltpu.roll(mid, n2 - 1, 0))
                tin_ref[part, i, n * pitch:n * pitch + n2, :] = (
                    before * w[0:1] + mid * w[1:2] + after * w[2:3] + bias)
    stages = ((tin_ref.at[0], tin_ref.at[1], cur_ref, kr0_ref, ki0_ref),
              (cur_ref, tin_ref.at[2], tin_ref.at[0], kr1_ref, ki1_ref))
    for order, (src, gate, dst, kr_ref, ki_ref) in enumerate(stages):
        skip = wide(skip_ref[order], 2 * nb)

        def stage1(jp, carry):
            cols = [(2 * jp + q, i) for q in range(2) for i in range(nb)]
            xb = lanes([src[i, pl.ds(j, n1h, stride=pitch), :] for j, i in cols]).astype(BF16)
            p = _dot(f1, xb)
            twr = lanes([wide(twrf_ref[2 * jp + q], nb) for q in range(2)])
            twi = lanes([wide(twif_ref[2 * jp + q], nb) for q in range(2)])
            pr, pi, pn = p[:n1h], p[n1h:2 * n1h], p[2 * n1h:2 * n1h + 1]
            re, im = pr * twr[:n1h] - pi * twi[:n1h], pr * twi[:n1h] + pi * twr[:n1h]
            re_n, im_n = pn * twr[n1h:n1h + 1], pn * twi[n1h:n1h + 1]
            for c, (j, i) in enumerate(cols):
                are_ref[i, pl.ds(j, n1h, stride=pitch), :] = lane_block(re, c)
                aim_ref[i, pl.ds(j, n1h, stride=pitch), :] = lane_block(im, c)
                are_ref[i, pl.ds(n1h * pitch + j, 1), :] = lane_block(re_n, c)
                aim_ref[i, pl.ds(n1h * pitch + j, 1), :] = lane_block(im_n, c)
            return carry

        lax.fori_loop(0, n2 // 2, stage1, 0, unroll=unroll)

        def slabs(ks):
            rows = [pl.ds(pl.multiple_of(k * pitch, SUBLANES), n2) for k in ks]
            a = _stack_bf16(lanes([are_ref[i, r, :] for r in rows for i in range(nb)]),
                            lanes([aim_ref[i, r, :] for r in rows for i in range(nb)]))
            x = _dot(f2, a)
            xr, xi = x[:n2], x[n2:]
            per_slab = lambda ref: lanes([wide(ref[k], nb) for k in ks])
            kr, ki = per_slab(kr_ref), per_slab(ki_ref)
            y = _dot(f2inv, _stack_bf16(xr * kr - xi * ki, xr * ki + xi * kr))
            br, bi = y[:n2], y[n2:]
            twr, twi = per_slab(twri_ref), per_slab(twii_ref)
            ore, oim = br * twr + bi * twi, bi * twr - br * twi
            for s, r in enumerate(rows):
                for i in range(nb):
                    are_ref[i, r, :] = lane_block(ore, s * nb + i)
                    aim_ref[i, r, :] = lane_block(oim, s * nb + i)

        def slab(k, carry):
            slabs((k,))
            return carry

        lax.fori_loop(0, n1h + 1, slab, 0, unroll=4)

        def stage1_inv(jp, carry):
            cols = [(pl.ds(2 * jp + q, n1h, stride=pitch), 2 * jp + q, i) for q in range(2) for i in range(nb)]
            spec = _stack_bf16(lanes([are_ref[i, r, :] for r, _, i in cols]),
                               lanes([aim_ref[i, r, :] for r, _, i in cols]))
            nyq = lanes([are_ref[i, pl.ds(n1h * pitch + j, 1), :] for _, j, i in cols])
            y = _dot(ginv, spec) + nyq_w * nyq
            x = lanes([src[i, r, :] for r, _, i in cols])
            g = lanes([gate[i, r, :] for r, _, i in cols])
            out = g * (y + x * skip)
            for c, (r, _, i) in enumerate(cols):
                dst[i, r, :] = lane_block(out, c)
            return carry

        lax.fori_loop(0, n2 // 2, stage1_inv, 0, unroll=unroll)

    def unpitch(n, carry):
        for i in range(nb):
            o_ref[i, pl.ds(pl.multiple_of(n * n2, n2), n2), :] = (
                tin_ref[0, i, pl.ds(pl.multiple_of(n * pitch, SUBLANES), n2), :])
        return carry

    lax.fori_loop(0, n1h, unpitch, 0)


def _hyena_fused(plan, z, col0, conv_w, conv_b, kr, ki, skip, *, nb):
    b, seq_len, _ = z.shape
    ch = skip.shape[1]
    n1h, n2, kh = plan.n1 // 2, plan.n2, plan.kh
    n_cb = ch // LANES
    cb0 = col0 // LANES
    part = lambda p: pl.BlockSpec((nb, seq_len, LANES), lambda c, bb: (bb, 0, cb0 + p * n_cb + c))
    taps = lambda p: pl.BlockSpec((conv_w.shape[0], LANES), lambda c, bb: (0, p * n_cb + c))
    bias = lambda p: pl.BlockSpec((1, LANES), lambda c, bb: (0, p * n_cb + c))
    once = pl.Buffered(1)
    filt = lambda o: pl.BlockSpec((kh, n2, LANES), lambda c, bb: (0, 0, o * n_cb + c), pipeline_mode=once)
    full = lambda a: pl.BlockSpec(a.shape, lambda c, bb, _n=a.ndim: (0,) * _n, pipeline_mode=once)
    tables = (plan.f1, plan.twr_fwd, plan.twi_fwd, plan.twr_inv, plan.twi_inv, plan.f2, plan.f2inv, plan.ginv)
    pitch = n2 + SUBLANES
    spec_scratch = pltpu.VMEM((nb, kh * pitch, LANES), F32)
    return pl.pallas_call(
        functools.partial(_hyena_fused_kernel, n1h=n1h, n2=n2, unroll=min(n2 // 2, 8)),
        grid=(n_cb, b // nb),
        in_specs=[part(0), part(1), part(2), taps(0), taps(1), taps(2), bias(0), bias(1), bias(2),
                  pl.BlockSpec((skip.shape[0], 1, LANES), lambda c, bb: (0, 0, c)),
                  filt(0), filt(0), filt(1), filt(1)] + [full(a) for a in tables],
        out_specs=pl.BlockSpec((nb, seq_len, LANES), lambda c, bb: (bb, 0, c)),
        out_shape=jax.ShapeDtypeStruct((b, seq_len, ch), F32),
        scratch_shapes=[spec_scratch, spec_scratch, pltpu.VMEM((nb, n1h * pitch, LANES), F32),
                        pltpu.VMEM((3, nb, n1h * pitch, LANES), F32)],
        compiler_params=_params(("parallel", "arbitrary"), 56),
        name="hyena_fused",
    )(z, z, z, conv_w, conv_w, conv_w, conv_b, conv_b, conv_b,
      skip.reshape(skip.shape[0], 1, ch), kr, ki, kr, ki, *tables)


def _dft_shape(seq_len):
    n2 = 128
    return 2 * seq_len // n2, n2


def _block_cols(n2, rows, target_bytes=1 << 20):
    g = max(SUBLANES, min(n2, target_bytes // (rows * LANES * 4)) // SUBLANES * SUBLANES)
    while n2 % g:
        g -= SUBLANES
    return g


def _slabs_per_step(kh, slab_bytes, target_bytes=3 << 20):
    ks = max(1, min(kh, target_bytes // slab_bytes))
    while kh % ks:
        ks -= 1
    return ks


def _hyena(plan, z, col0, conv_w, conv_b, filt, skip):
    b, seq_len, _ = z.shape
    n_orders, ch = skip.shape
    g = _block_cols(plan.n2, plan.n1 // 2)
    ks_filt = _slabs_per_step(plan.kh, 2 * plan.n2 * LANES * 4, target_bytes=6 << 20)
    kr, ki = _filt_spec(plan, filt, n_orders, ks=ks_filt)
    nb = 2 if b % 2 == 0 else 1
    fused_bytes = 15 * nb * seq_len * LANES * 4
    if n_orders == 2 and fused_bytes <= FUSED_VMEM_BUDGET:
        return _hyena_fused(plan, z, col0, conv_w, conv_b, kr, ki, skip, nb=nb)
    uc = _sconv(z, conv_w, conv_b, rows=min(seq_len, 512), col0=col0, n_parts=n_orders + 1)
    ks = _slabs_per_step(plan.kh, b * plan.n2 * ch * 4)
    skip3 = skip.reshape(n_orders, 1, ch)
    cur, cur_part = uc, 0
    for order in range(n_orders):
        are, aim = _dft_fwd(plan, cur, cur_part, g=g)
        bre, bim = _conv_mid(plan, are, aim, kr, ki, order, ks=ks)
        cur = _dft_inv(plan, bre, bim, cur, cur_part, uc, order + 1, skip3, order, g=g)
        cur_part = 0
    return cur[0]


def _trunk(x3, p, plan, filt):
    b, seq_len, d = x3.shape
    t = b * seq_len
    x = x3.reshape(t, d)
    x, z = _ffn_in(x, p["g1pre"], p["g1post"], p["w1gu"], p["w1d"], p["gmix"], p["w_in"], tm=512)
    z = z.reshape(b, seq_len, -1)
    d_pool = p["pool_w"].shape[0] * LANES
    y_pool = _pool(z, p["pool_w"], p["pool_scale"], rows=min(seq_len, 512))
    y_hy = _hyena(plan, z, d_pool, p["conv_w"], p["conv_b"], filt, p["skip"])
    x = _ffn_out(x, y_pool.reshape(t, -1), y_hy.reshape(t, -1), p["gpool"], p["ghy"], p["w_out"],
                 p["gmixpost"], p["g2pre"], p["g2post"], p["w2gu"], p["w2d"], tm=512)
    return x.reshape(b, seq_len, d)


def kernel(x_prompt, x_sample, ffn1_norm_pre, ffn1_norm_post, ffn1_w_gate_up, ffn1_w_down, mix_norm_pre, w_in, pool_w_map, pool_scale, hyena_conv_w, hyena_conv_b, filt_w_first, filt_b_first, filt_w_hidden, filt_b_hidden, filt_w_last, filt_freq, hyena_skip, pool_out_norm, hyena_out_norm, w_out, mix_norm_post, ffn2_norm_pre, ffn2_norm_post, ffn2_w_gate_up, ffn2_w_down):
    assert ffn1_norm_pre.shape[0] == 1, "single-layer trunk"
    row = lambda a: a[0].reshape(1, -1)
    p = dict(
        g1pre=row(ffn1_norm_pre), g1post=row(ffn1_norm_post),
        w1gu=ffn1_w_gate_up[0].astype(BF16), w1d=ffn1_w_down[0].astype(BF16),
        gmix=row(mix_norm_pre), w_in=w_in[0].astype(BF16),
        pool_w=pool_w_map[0].astype(BF16), pool_scale=row(pool_scale),
        conv_w=hyena_conv_w[0], conv_b=row(hyena_conv_b),
        skip=hyena_skip[0], gpool=row(pool_out_norm), ghy=row(hyena_out_norm),
        w_out=w_out[0].astype(BF16), gmixpost=row(mix_norm_post),
        g2pre=row(ffn2_norm_pre), g2post=row(ffn2_norm_post),
        w2gu=ffn2_w_gate_up[0].astype(BF16), w2d=ffn2_w_down[0].astype(BF16),
    )
    outs = []
    for x3 in (x_prompt, x_sample):
        seq_len = x3.shape[1]
        plan = _Plan(seq_len, *_dft_shape(seq_len))
        filt = _filt_gen(seq_len, filt_w_first[0], filt_b_first[0], filt_w_hidden[0],
                         filt_b_hidden[0], filt_w_last[0], filt_freq[0], rows=min(seq_len, 512),
                         n2=plan.n2)
        outs.append(_trunk(x3, p, plan, filt))
    return tuple(outs)
```

```python
import functools
import math

import jax
import jax.numpy as jnp
import numpy as np
from jax import lax
from jax.experimental import pallas as pl
from jax.experimental.pallas import tpu as pltpu

F32 = jnp.float32
BF16 = jnp.bfloat16
EPS = 1e-6
LANES = 128
SUBLANES = 8
MXU_DIM = 256
POOL_WINDOWS = (2, 4, 8, 16)
HALO = 8
DECAY_TARGET = 1e-2
FAST_DECAY_PCT = 0.3
SLOW_DECAY_PCT = 1.5
MAX_DECAY = math.log(DECAY_TARGET) / FAST_DECAY_PCT
MIN_DECAY = math.log(DECAY_TARGET) / SLOW_DECAY_PCT
HIGHEST = lax.Precision.HIGHEST
FUSED_VMEM_BUDGET = 32 << 20


def _params(sem, vmem_mib):
    return pltpu.CompilerParams(dimension_semantics=sem, vmem_limit_bytes=vmem_mib << 20)


def _rms(x, g):
    inv = lax.rsqrt(jnp.mean(x * x, axis=-1, keepdims=True) + EPS)
    return (x * inv) * g


def _dot(a, b):
    return jnp.dot(a, b, preferred_element_type=F32)


def _dot3(a, b):
    a_hi, b_hi = a.astype(BF16), b.astype(BF16)
    a_lo = (a - a_hi.astype(F32)).astype(BF16)
    b_lo = (b - b_hi.astype(F32)).astype(BF16)
    return _dot(a_hi, b_hi) + _dot(a_hi, b_lo) + _dot(a_lo, b_hi)


def _lanes(parts):
    return parts[0] if len(parts) == 1 else jnp.concatenate(parts, axis=1)


def _lane_block(a, i):
    return a[:, i * LANES:(i + 1) * LANES]


def _ff_chunks(d_ff, n_chunks=2):
    if d_ff % MXU_DIM:
        return (0, d_ff)
    tiles = d_ff // MXU_DIM
    return tuple(MXU_DIM * ((tiles * k + n_chunks - 1) // n_chunks) for k in range(n_chunks + 1))


def _half_step_ffn(x, gpre, gpost, wgu_ref, wd_ref):
    d_ff = wd_ref.shape[0]
    h = _rms(x, gpre).astype(BF16)
    acc = None
    bounds = _ff_chunks(d_ff)
    for lo, hi in zip(bounds[:-1], bounds[1:]):
        gate = _dot(h, wgu_ref[:, lo:hi])
        up = _dot(h, wgu_ref[:, d_ff + lo:d_ff + hi])
        act = (gate * jax.nn.sigmoid(gate) * up).astype(BF16)
        part = _dot(act, wd_ref[lo:hi, :])
        acc = part if acc is None else acc + part
    return x + 0.5 * _rms(acc, gpost)


def _ffn_in_kernel(x_ref, gpre_ref, gpost_ref, wgu_ref, wd_ref, gmix_ref, win_ref, o_ref, z_ref):
    x = _half_step_ffn(x_ref[...], gpre_ref[...], gpost_ref[...], wgu_ref, wd_ref)
    o_ref[...] = x
    z_ref[...] = _dot(_rms(x, gmix_ref[...]).astype(BF16), win_ref[...])


def _ffn_in(x, g_pre, g_post, wgu, wd, g_mix, w_in, *, tm):
    t, d = x.shape
    d_ff, d_in = wd.shape[0], w_in.shape[1]
    const = lambda i: (0, 0)
    row = lambda i: (i, 0)
    once = pl.Buffered(1)
    vec = pl.BlockSpec((1, d), const)
    return pl.pallas_call(
        _ffn_in_kernel,
        grid=(t // tm,),
        in_specs=[
            pl.BlockSpec((tm, d), row), vec, vec,
            pl.BlockSpec((d, 2 * d_ff), const, pipeline_mode=once),
            pl.BlockSpec((d_ff, d), const, pipeline_mode=once),
            vec,
            pl.BlockSpec((d, d_in), const, pipeline_mode=once),
        ],
        out_specs=[pl.BlockSpec((tm, d), row), pl.BlockSpec((tm, d_in), row)],
        out_shape=[jax.ShapeDtypeStruct((t, d), F32), jax.ShapeDtypeStruct((t, d_in), F32)],
        compiler_params=_params(("parallel",), 56),
        name="ffn_in",
    )(x, g_pre, g_post, wgu, wd, g_mix, w_in)


def _ffn_out_kernel(x_ref, yp_ref, yh_ref, gp_ref, gh_ref, wout_ref, gmix_ref,
                    gpre_ref, gpost_ref, wgu_ref, wd_ref, o_ref):
    d_pool = yp_ref.shape[1]
    yp = _rms(yp_ref[...], gp_ref[...]).astype(BF16)
    yh = _rms(yh_ref[...], gh_ref[...]).astype(BF16)
    y = _dot(yp, wout_ref[:d_pool, :]) + _dot(yh, wout_ref[d_pool:, :])
    x = x_ref[...] + _rms(y, gmix_ref[...])
    o_ref[...] = _half_step_ffn(x, gpre_ref[...], gpost_ref[...], wgu_ref, wd_ref)


def _ffn_out(x, yp, yh, gp, gh, w_out, g_mix, g_pre, g_post, wgu, wd, *, tm):
    t, d = x.shape
    d_ff, dp, dh = wd.shape[0], yp.shape[1], yh.shape[1]
    const = lambda i: (0, 0)
    row = lambda i: (i, 0)
    once = pl.Buffered(1)
    vec = lambda n: pl.BlockSpec((1, n), const)
    return pl.pallas_call(
        _ffn_out_kernel,
        grid=(t // tm,),
        in_specs=[
            pl.BlockSpec((tm, d), row), pl.BlockSpec((tm, dp), row), pl.BlockSpec((tm, dh), row),
            vec(dp), vec(dh),
            pl.BlockSpec((dp + dh, d), const, pipeline_mode=once),
            vec(d), vec(d), vec(d),
            pl.BlockSpec((d, 2 * d_ff), const, pipeline_mode=once),
            pl.BlockSpec((d_ff, d), const, pipeline_mode=once),
        ],
        out_specs=pl.BlockSpec((tm, d), row),
        out_shape=jax.ShapeDtypeStruct((t, d), F32),
        compiler_params=_params(("parallel",), 56),
        name="ffn_out",
    )(x, yp, yh, gp, gh, w_out, g_mix, g_pre, g_post, wgu, wd)


def _with_halo(prev_ref, main_ref, next_ref, i, n_tiles):
    prev = jnp.where(i > 0, prev_ref[0], 0.0)
    nxt = jnp.where(i < n_tiles - 1, next_ref[0], 0.0)
    return jnp.concatenate([prev, main_ref[0], nxt], axis=0)


def _pool_kernel(prev_ref, main_ref, next_ref, wmap_ref, scale_ref, o_ref, *, seq_len, n_tiles):
    i = pl.program_id(1)
    rows = main_ref.shape[1]
    n_ext = rows + 2 * HALO
    ext_all = _with_halo(prev_ref, main_ref, next_ref, i, n_tiles)
    pos = i * rows + lax.broadcasted_iota(jnp.int32, (rows, LANES), 0)
    for grp, window in enumerate(POOL_WINDOWS):
        lanes = slice(grp * LANES, (grp + 1) * LANES)
        ext = ext_all[:, lanes]
        ssum, w = ext + pltpu.roll(ext, 1, 0), 2
        while w < window:
            ssum, w = pltpu.roll(ssum, w // 2, 0) + pltpu.roll(ssum, n_ext - w // 2, 0), 2 * w
        u = main_ref[0, :, lanes]
        lo = jnp.clip(pos - window // 2, 0, seq_len)
        hi = jnp.clip(pos + (window - window // 2), 0, seq_len)
        d = ssum[HALO:HALO + rows] / (hi - lo).astype(F32) - u
        y = _dot(d.astype(BF16), wmap_ref[grp])
        o_ref[0, :, lanes] = y * scale_ref[:, lanes]


def _halo_specs(rows, width, seq_len, col_of):
    blocks_per_tile = rows // HALO
    last = seq_len // HALO - 1
    prev = pl.BlockSpec((1, HALO, width),
                        lambda b, i, c: (b, jnp.maximum(i * blocks_per_tile - 1, 0), col_of(c)))
    main = pl.BlockSpec((1, rows, width), lambda b, i, c: (b, i, col_of(c)))
    nxt = pl.BlockSpec((1, HALO, width),
                       lambda b, i, c: (b, jnp.minimum((i + 1) * blocks_per_tile, last), col_of(c)))
    return [prev, main, nxt]


def _pool(z, wmap, scale, *, rows):
    b, seq_len, _ = z.shape
    assert wmap.shape[0] == len(POOL_WINDOWS) and max(POOL_WINDOWS) <= 2 * HALO
    width = wmap.shape[0] * LANES
    n_tiles = seq_len // rows
    return pl.pallas_call(
        functools.partial(_pool_kernel, seq_len=seq_len, n_tiles=n_tiles),
        grid=(b, n_tiles, 1),
        in_specs=_halo_specs(rows, width, seq_len, lambda c: 0) + [
            pl.BlockSpec(wmap.shape, lambda b, i, c: (0, 0, 0)),
            pl.BlockSpec((1, width), lambda b, i, c: (0, 0)),
        ],
        out_specs=pl.BlockSpec((1, rows, width), lambda b, i, c: (b, i, 0)),
        out_shape=jax.ShapeDtypeStruct((b, seq_len, width), F32),
        compiler_params=_params(("parallel", "parallel", "parallel"), 40),
        name="pool",
    )(z, z, z, wmap, scale)


def _sconv_kernel(prev_ref, main_ref, next_ref, w_ref, b_ref, o_ref, *, n_tiles):
    i = pl.program_id(1)
    rows = main_ref.shape[1]
    ext = _with_halo(prev_ref, main_ref, next_ref, i, n_tiles)
    n_ext = rows + 2 * HALO
    before = pltpu.roll(ext, 1, 0)[HALO:HALO + rows]
    after = pltpu.roll(ext, n_ext - 1, 0)[HALO:HALO + rows]
    w = w_ref[...]
    out = before * w[0:1] + main_ref[0] * w[1:2] + after * w[2:3]
    o_ref[0, 0] = out + b_ref[...]


def _sconv(z, w, bias, *, rows, col0, n_parts):
    b, seq_len, _ = z.shape
    n_tiles = seq_len // rows
    width = w.shape[1] // n_parts
    assert col0 % width == 0
    return pl.pallas_call(
        functools.partial(_sconv_kernel, n_tiles=n_tiles),
        grid=(b, n_tiles, n_parts),
        in_specs=_halo_specs(rows, width, seq_len, lambda c: c + col0 // width) + [
            pl.BlockSpec((w.shape[0], width), lambda b, i, c: (0, c)),
            pl.BlockSpec((1, width), lambda b, i, c: (0, c)),
        ],
        out_specs=pl.BlockSpec((1, 1, rows, width), lambda b, i, c: (c, b, i, 0)),
        out_shape=jax.ShapeDtypeStruct((n_parts, b, seq_len, width), F32),
        compiler_params=_params(("parallel", "parallel", "parallel"), 40),
        name="sconv",
    )(z, z, z, w, bias)


def _filt_gen_kernel(bands_ref, wt_ref, wc_ref, ws_ref, b1_ref, wh_ref, bh_ref, freq_ref,
                     wl_ref, delta_ref, o_ref, *, seq_len, d_ch):
    rows = o_ref.shape[0] * (o_ref.shape[1] - SUBLANES)
    base = pl.program_id(0) * rows
    m_lane = (base + lax.broadcasted_iota(jnp.int32, (1, rows), 1)).astype(F32)
    t_lane = m_lane / (seq_len - 1.0)
    ang = (bands_ref[...] * (2.0 * math.pi / seq_len)) * m_lane
    freq = freq_ref[...]
    pre = (jnp.dot(wc_ref[...], jnp.cos(ang), precision=HIGHEST, preferred_element_type=F32)
           + jnp.dot(ws_ref[...], -jnp.sin(ang), precision=HIGHEST, preferred_element_type=F32)
           + wt_ref[...] * t_lane + b1_ref[...])
    h = jnp.sin(freq * pre)
    for layer in range(wh_ref.shape[0]):
        pre = jnp.dot(wh_ref[layer], h, precision=HIGHEST, preferred_element_type=F32) + bh_ref[layer]
        h = jnp.sin(freq * pre)
    out = _dot3(h.T, wl_ref[...])
    m_row = base + lax.broadcasted_iota(jnp.int32, (rows, d_ch), 0)
    t_row = m_row.astype(F32) / (seq_len - 1.0)
    decay = jnp.exp(-t_row * jnp.abs(delta_ref[...]))
    decay_bwd = jnp.where(m_row == 0, 0.0, decay)
    n_chunks, pitch, _ = o_ref.shape
    n2 = rows // n_chunks
    for q in range(out.shape[1] // d_ch):
        dq = decay_bwd if q % 2 == 1 else decay
        val = out[:, q * d_ch:(q + 1) * d_ch] * dq
        for n in range(n_chunks):
            o_ref[n, :n2, q * d_ch:(q + 1) * d_ch] = val[n * n2:(n + 1) * n2]
    o_ref[:, n2:, :] = jnp.zeros((n_chunks, pitch - n2, o_ref.shape[2]), F32)


def _filt_gen(seq_len, w_first, b_first, w_hidden, b_hidden, w_last, freq, *, rows, n2):
    pos_bands = (w_first.shape[0] - 1) // 2
    hidden = w_first.shape[1]
    n_cols = w_last.shape[1]
    d_ch = n_cols // 4
    bands = jnp.linspace(1e-4, pos_bands - 1, pos_bands, dtype=F32).reshape(pos_bands, 1)
    deltas = jnp.linspace(MIN_DECAY, MAX_DECAY, d_ch, dtype=F32).reshape(1, d_ch)
    w1t = w_first.T
    args = (bands, w1t[:, 0:1], w1t[:, 1:1 + pos_bands], w1t[:, 1 + pos_bands:],
            b_first.reshape(hidden, 1), jnp.swapaxes(w_hidden, 1, 2),
            b_hidden.reshape(b_hidden.shape[0], hidden, 1), freq.reshape(hidden, 1),
            w_last, deltas)
    full = lambda a: pl.BlockSpec(a.shape, lambda i, _n=a.ndim: (0,) * _n)
    return pl.pallas_call(
        functools.partial(_filt_gen_kernel, seq_len=seq_len, d_ch=d_ch),
        grid=(seq_len // rows,),
        in_specs=[full(a) for a in args],
        out_specs=pl.BlockSpec((rows // n2, n2 + SUBLANES, n_cols), lambda i: (i, 0, 0)),
        out_shape=jax.ShapeDtypeStruct((seq_len // n2, n2 + SUBLANES, n_cols), F32),
        compiler_params=_params(("parallel",), 40),
        name="filt_gen",
    )(*args)


class _Plan:
    def __init__(self, seq_len, n1, n2):
        assert n1 * n2 == 2 * seq_len and n1 % (2 * SUBLANES) == 0 and n2 % SUBLANES == 0
        self.seq_len, self.n1, self.n2 = seq_len, n1, n2
        n = n1 * n2
        n1h = n1 // 2
        self.kh = kh = n1h + 1
        k1 = np.arange(kh, dtype=np.float64)[:, None]
        m1 = np.arange(n1h, dtype=np.float64)[None, :]
        th1 = 2.0 * np.pi * k1 * m1 / n1
        f1 = np.zeros((n1 + SUBLANES, n1h))
        f1[:n1h] = np.cos(th1[:n1h]); f1[n1h:n1] = -np.sin(th1[:n1h]); f1[n1] = np.cos(th1[n1h])
        self.f1 = jnp.asarray(f1, F32)
        m2 = np.arange(n2, dtype=np.float64)[None, :]
        tht = 2.0 * np.pi * k1 * m2 / n
        twr, twi = np.cos(tht), -np.sin(tht)
        rep = lambda a: np.repeat(a[:, :, None], LANES, axis=2)
        self.twr_inv, self.twi_inv = jnp.asarray(rep(twr), F32), jnp.asarray(rep(twi), F32)
        pad = np.zeros((n2, SUBLANES - 1))
        self.twr_fwd = jnp.asarray(rep(np.concatenate([twr.T, pad], 1)), F32)
        self.twi_fwd = jnp.asarray(rep(np.concatenate([twi.T, pad], 1)), F32)
        k2 = np.arange(n2, dtype=np.float64)[:, None]
        th2 = 2.0 * np.pi * k2 * m2 / n2
        f2r, f2i = np.cos(th2), -np.sin(th2)
        self.f2 = jnp.asarray(np.block([[f2r, -f2i], [f2i, f2r]]), F32)
        self.f2inv = jnp.asarray(np.block([[f2r, f2i], [-f2i, f2r]]), F32)
        wgt = np.where(k1[:n1h] == 0, 1.0, 2.0) / n
        ginv = np.concatenate([(wgt * np.cos(th1[:n1h])).T, (-wgt * np.sin(th1[:n1h])).T], axis=1)
        self.ginv = jnp.asarray(ginv, F32)
        k1f = np.arange(kh, dtype=np.float64)[:, None]
        full = lambda cols: 2.0 * np.pi * k1f * cols[None, :] / n1
        stage1 = lambda th: np.concatenate(
            [np.cos(th[:n1h]), -np.sin(th[:n1h]), np.cos(th[n1h:]), np.zeros((SUBLANES - 1, th.shape[1]))], 0)
        fwd_cols = np.arange(n1h, dtype=np.float64)
        f1a = stage1(full(np.concatenate([fwd_cols, n1 - 1 - fwd_cols])))
        f1b = stage1(full(np.concatenate([fwd_cols, (n1 - fwd_cols) % n1])))
        f1b[:, n1h] = 0.0
        self.f1_filt, self.f1_filt0 = jnp.asarray(f1a, F32), jnp.asarray(f1b, F32)
        rows = lambda t: rep(np.concatenate([t.T, np.zeros((t.shape[1], SUBLANES - 1))], 1))
        coarse, fine = tht[:, ::SUBLANES], tht[:, :SUBLANES]
        self.tw_coarse = (jnp.asarray(rows(np.cos(coarse)), F32), jnp.asarray(rows(-np.sin(coarse)), F32))
        self.tw_fine = (jnp.asarray(rows(np.cos(fine)), F32), jnp.asarray(rows(-np.sin(fine)), F32))
        self.tw_slab_coarse = (jnp.asarray(rep(np.cos(coarse)), F32), jnp.asarray(rep(-np.sin(coarse)), F32))
        self.tw_slab_fine = (jnp.asarray(rep(np.cos(fine)), F32), jnp.asarray(rep(-np.sin(fine)), F32))


def _stack_bf16(re, im):
    return jnp.concatenate([re.astype(BF16), im.astype(BF16)], axis=0)


def _filt_spec_kernel(hf_ref, hb_ref, f1_ref, f1j0_ref, twcr_ref, twci_ref, twfr_ref, twfi_ref, f2_ref,
                      kr_ref, ki_ref, are_ref, aim_ref, *, n1h, n2, unroll):
    pitch = n2 + SUBLANES
    ks = kr_ref.shape[0]
    kc = pl.program_id(1)

    @pl.when(kc == 0)
    def _():
        def stage1(j, f1, hb_start, a, b):
            x = jnp.concatenate([hf_ref[pl.ds(j, n1h, stride=pitch), :],
                                 hb_ref[pl.ds(hb_start, n1h, stride=pitch), :]], axis=0)
            p = _dot(f1, x.astype(BF16))
            cr, ci, fr, fi = twcr_ref[a], twci_ref[a], twfr_ref[b], twfi_ref[b]
            twr, twi = cr * fr - ci * fi, cr * fi + ci * fr
            pr, pi, pn = p[:n1h], p[n1h:2 * n1h], p[2 * n1h:2 * n1h + 1]
            are_ref[pl.ds(j, n1h, stride=pitch), :] = pr * twr[:n1h] - pi * twi[:n1h]
            aim_ref[pl.ds(j, n1h, stride=pitch), :] = pr * twi[:n1h] + pi * twr[:n1h]
            are_ref[pl.ds(n1h * pitch + j, 1), :] = pn * twr[n1h:n1h + 1]
            aim_ref[pl.ds(n1h * pitch + j, 1), :] = pn * twi[n1h:n1h + 1]

        stage1(0, f1j0_ref[...].astype(BF16), 0, 0, 0)
        f1 = f1_ref[...].astype(BF16)

        def body(j, carry):
            stage1(j, f1, n2 - j, lax.shift_right_logical(j, 3), jnp.bitwise_and(j, SUBLANES - 1))
            return carry

        lax.fori_loop(1, n2, body, 0, unroll=unroll)

    f2 = f2_ref[...].astype(BF16)

    def slabs(first, count):
        rows = [pl.ds(pl.multiple_of((kc * ks + first + i) * pitch, SUBLANES), n2) for i in range(count)]
        x = _dot(f2, _stack_bf16(_lanes([are_ref[r, :] for r in rows]), _lanes([aim_ref[r, :] for r in rows])))
        for i in range(count):
            kr_ref[first + i] = _lane_block(x[:n2], i)
            ki_ref[first + i] = _lane_block(x[n2:], i)

    def pair(p, carry):
        slabs(2 * p, 2)
        return carry

    lax.fori_loop(0, ks // 2, pair, 0, unroll=2)
    if ks % 2:
        slabs(ks - 1, 1)


def _filt_spec(plan, filt, n_orders, *, ks):
    n1h, n2, kh = plan.n1 // 2, plan.n2, plan.kh
    cols = filt.shape[2]
    filt = filt.reshape(n1h * (n2 + SUBLANES), cols)
    n_cb = cols // (2 * n_orders * LANES)
    pitch = n2 + SUBLANES
    once = pl.Buffered(1)
    col = lambda c, back: (c // n_cb) * 2 * n_cb + back * n_cb + c % n_cb
    seq = lambda back: pl.BlockSpec((n1h * pitch, LANES), lambda c, k: (0, col(c, back)), pipeline_mode=once)
    full = lambda a: pl.BlockSpec(a.shape, lambda c, k, _n=a.ndim: (0,) * _n, pipeline_mode=once)
    tables = (plan.f1_filt, plan.f1_filt0, *plan.tw_coarse, *plan.tw_fine, plan.f2)
    out = jax.ShapeDtypeStruct((kh, n2, cols // 2), F32)
    scratch = pltpu.VMEM((kh * pitch, LANES), F32)
    return pl.pallas_call(
        functools.partial(_filt_spec_kernel, n1h=n1h, n2=n2, unroll=min(n2 - 1, 8)),
        grid=(n_orders * n_cb, kh // ks),
        in_specs=[seq(0), seq(1)] + [full(a) for a in tables],
        out_specs=[pl.BlockSpec((ks, n2, LANES), lambda c, k: (k, 0, c))] * 2,
        out_shape=[out, out],
        scratch_shapes=[scratch, scratch],
        compiler_params=_params(("parallel", "arbitrary"), 56),
        name="filt_spec",
    )(filt, filt, *tables)


def _conv_front_kernel(x_ref, kr_ref, ki_ref, f1_ref, twcr_ref, twci_ref, twfr_ref, twfi_ref,
                       icr_ref, ici_ref, ifr_ref, ifi_ref, f2_ref, f2inv_ref,
                       ore_ref, oim_ref, are_ref, aim_ref, *, n1h, n2, unroll):
    pitch = n2 + SUBLANES
    ks = kr_ref.shape[0]
    kc = pl.program_id(2)
    x2 = x_ref.at[0, 0]

    @pl.when(kc == 0)
    def _():
        f1 = f1_ref[...].astype(BF16)

        def body(jp, carry):
            js = (2 * jp, 2 * jp + 1)
            a = lax.shift_right_logical(jp, 2)
            bs = [2 * jnp.bitwise_and(jp, SUBLANES // 2 - 1) + q for q in range(2)]
            xb = _lanes([x2[pl.ds(j, n1h, stride=n2), :] for j in js]).astype(BF16)
            p = _dot(f1, xb)
            cr, ci = jnp.tile(twcr_ref[a], (1, 2)), jnp.tile(twci_ref[a], (1, 2))
            fr, fi = _lanes([twfr_ref[b] for b in bs]), _lanes([twfi_ref[b] for b in bs])
            twr, twi = cr * fr - ci * fi, cr * fi + ci * fr
            pr, pi, pn = p[:n1h], p[n1h:2 * n1h], p[2 * n1h:2 * n1h + 1]
            re, im = pr * twr[:n1h] - pi * twi[:n1h], pr * twi[:n1h] + pi * twr[:n1h]
            re_n, im_n = pn * twr[n1h:n1h + 1], pn * twi[n1h:n1h + 1]
            for i, j in enumerate(js):
                are_ref[pl.ds(j, n1h, stride=pitch), :] = _lane_block(re, i)
                aim_ref[pl.ds(j, n1h, stride=pitch), :] = _lane_block(im, i)
                are_ref[pl.ds(n1h * pitch + j, 1), :] = _lane_block(re_n, i)
                aim_ref[pl.ds(n1h * pitch + j, 1), :] = _lane_block(im_n, i)
            return carry

        lax.fori_loop(0, n2 // 2, body, 0, unroll=unroll)

    f2, f2inv = f2_ref[...].astype(BF16), f2inv_ref[...].astype(BF16)

    def slab_twiddle(k):
        cr, ci, fr, fi = icr_ref[k], ici_ref[k], ifr_ref[k], ifi_ref[k]
        pieces = [(cr[a:a + 1] * fr - ci[a:a + 1] * fi, cr[a:a + 1] * fi + ci[a:a + 1] * fr)
                  for a in range(n2 // SUBLANES)]
        return (jnp.concatenate([p[0] for p in pieces], axis=0), jnp.concatenate([p[1] for p in pieces], axis=0))

    def slabs(first, count):
        ks_abs = [kc * ks + first + i for i in range(count)]
        rows = [pl.ds(pl.multiple_of(k * pitch, SUBLANES), n2) for k in ks_abs]
        x = _dot(f2, _stack_bf16(_lanes([are_ref[r, :] for r in rows]), _lanes([aim_ref[r, :] for r in rows])))
        xr, xi = x[:n2], x[n2:]
        kr = _lanes([kr_ref[first + i] for i in range(count)])
        ki = _lanes([ki_ref[first + i] for i in range(count)])
        y = _dot(f2inv, _stack_bf16(xr * kr - xi * ki, xr * ki + xi * kr))
        br, bi = y[:n2], y[n2:]
        tw = [slab_twiddle(k) for k in ks_abs]
        twr, twi = _lanes([t[0] for t in tw]), _lanes([t[1] for t in tw])
        ore, oim = br * twr + bi * twi, bi * twr - br * twi
        for i in range(count):
            ore_ref[0, first + i] = _lane_block(ore, i)
            oim_ref[0, first + i] = _lane_block(oim, i)

    def pair(p, carry):
        slabs(2 * p, 2)
        return carry

    lax.fori_loop(0, ks // 2, pair, 0, unroll=2)
    if ks % 2:
        slabs(ks - 1, 1)


def _conv_front(plan, x4, part, kr, ki, order, *, ks):
    _, b, seq_len, ch = x4.shape
    n1h, n2, kh = plan.n1 // 2, plan.n2, plan.kh
    n_cb = ch // LANES
    pitch = n2 + SUBLANES
    once = pl.Buffered(1)
    full = lambda a: pl.BlockSpec(a.shape, lambda bb, c, k, _n=a.ndim: (0,) * _n, pipeline_mode=once)
    filt = pl.BlockSpec((ks, n2, LANES), lambda bb, c, k: (k, 0, order * n_cb + c))
    data = pl.BlockSpec((1, ks, n2, LANES), lambda bb, c, k: (bb, k, 0, c))
    tables = (plan.f1, *plan.tw_coarse, *plan.tw_fine, *plan.tw_slab_coarse, *plan.tw_slab_fine,
              plan.f2, plan.f2inv)
    out = jax.ShapeDtypeStruct((b, kh, n2, ch), F32)
    scratch = pltpu.VMEM((kh * pitch, LANES), F32)
    return pl.pallas_call(
        functools.partial(_conv_front_kernel, n1h=n1h, n2=n2, unroll=4),
        grid=(b, n_cb, kh // ks),
        in_specs=[pl.BlockSpec((1, 1, seq_len, LANES), lambda bb, c, k: (part, bb, 0, c), pipeline_mode=once),
                  filt, filt] + [full(a) for a in tables],
        out_specs=[data, data],
        out_shape=[out, out],
        scratch_shapes=[scratch, scratch],
        compiler_params=_params(("parallel", "parallel", "arbitrary"), 56),
        name="conv_front",
    )(x4, kr, ki, *tables)


def _dft_inv_kernel(bre_ref, bim_ref, x_ref, gate_ref, skip_ref, ginv_ref, o_ref, *, unroll, scale):
    _, _, n1h, g, _ = o_ref.shape
    b_re = bre_ref.at[0].reshape((n1h + 1) * g, LANES)
    b_im = bim_ref.at[0].reshape((n1h + 1) * g, LANES)
    x2 = x_ref.at[0, 0].reshape(n1h * g, LANES)
    gate2 = gate_ref.at[0, 0].reshape(n1h * g, LANES)
    o2 = o_ref.at[0, 0].reshape(n1h * g, LANES)
    ginv = ginv_ref[...].astype(BF16)
    skip = skip_ref[0]
    n1 = lax.broadcasted_iota(jnp.int32, (n1h, 2 * LANES), 0)
    nyq_w = jnp.where(jnp.bitwise_and(n1, 1) == 0, scale, -scale)
    skip = jnp.tile(skip, (1, 2))

    def body(jp, carry):
        rows = [pl.ds(2 * jp + i, n1h, stride=g) for i in range(2)]
        spec = _stack_bf16(_lanes([b_re[r, :] for r in rows]), _lanes([b_im[r, :] for r in rows]))
        nyq = _lanes([b_re[pl.ds(n1h * g + 2 * jp + i, 1), :] for i in range(2)])
        y = _dot(ginv, spec) + nyq_w * nyq
        out = _lanes([gate2[r, :] for r in rows]) * (y + _lanes([x2[r, :] for r in rows]) * skip)
        for i, r in enumerate(rows):
            o2[r, :] = _lane_block(out, i)
        return carry

    lax.fori_loop(0, g // 2, body, 0, unroll=unroll)


def _dft_inv(plan, bre, bim, x4, x_part, gate4, gate_part, skip, order, *, g):
    b, seq_len, ch = x4.shape[1:]
    n1h, n2, kh = plan.n1 // 2, plan.n2, plan.kh
    xv = x4.reshape(x4.shape[0], b, n1h, n2, ch)
    gv = gate4.reshape(gate4.shape[0], b, n1h, n2, ch)
    spec = pl.BlockSpec((1, kh, g, LANES), lambda j, bb, c: (bb, 0, j, c))
    time = lambda part: pl.BlockSpec((1, 1, n1h, g, LANES), lambda j, bb, c: (part, bb, 0, j, c))
    out = pl.pallas_call(
        functools.partial(_dft_inv_kernel, unroll=min(g // 2, 8), scale=1.0 / (plan.n1 * n2)),
        grid=(n2 // g, b, ch // LANES),
        in_specs=[
            spec, spec, time(x_part), time(gate_part),
            pl.BlockSpec((1, 1, LANES), lambda j, bb, c: (order, 0, c)),
            pl.BlockSpec(plan.ginv.shape, lambda j, bb, c: (0, 0)),
        ],
        out_specs=time(0),
        out_shape=jax.ShapeDtypeStruct((1, b, n1h, n2, ch), F32),
        compiler_params=_params(("parallel", "parallel", "parallel"), 48),
        name="dft_inv",
    )(bre, bim, xv, gv, skip, plan.ginv)
    return out.reshape(1, b, seq_len, ch)


def _hyena_fused_kernel(zv_ref, zx1_ref, zx2_ref, wv_ref, wx1_ref, wx2_ref, bv_ref, bx1_ref, bx2_ref,
                        skip_ref, kr0_ref, ki0_ref, kr1_ref, ki1_ref,
                        f1_ref, twrf_ref, twif_ref, twri_ref, twii_ref, f2_ref, f2inv_ref, ginv_ref,
                        o_ref, are_ref, aim_ref, cur_ref, tin_ref, *, n1h, n2, unroll):
    nb = o_ref.shape[0]
    pitch = n2 + SUBLANES
    f1 = f1_ref[...].astype(BF16)
    f2, f2inv = f2_ref[...].astype(BF16), f2inv_ref[...].astype(BF16)
    ginv = ginv_ref[...].astype(BF16)
    scale = 1.0 / (2 * n1h * n2)
    n1 = lax.broadcasted_iota(jnp.int32, (n1h, 2 * nb * LANES), 0)
    nyq_w = jnp.where(jnp.bitwise_and(n1, 1) == 0, scale, -scale)
    lanes, lane_block = _lanes, _lane_block
    wide = lambda t, reps: t if reps == 1 else jnp.tile(t, (1, reps))

    row = lax.broadcasted_iota(jnp.int32, (n2, LANES), 0)
    for part, (z_ref, w_ref, b_ref) in enumerate(((zv_ref, wv_ref, bv_ref), (zx1_ref, wx1_ref, bx1_ref),
                                                   (zx2_ref, wx2_ref, bx2_ref))):
        w, bias = w_ref[...], b_ref[...]
        for i in range(nb):
            for n in range(n1h):
                lo = n * n2
                mid = z_ref[i, lo:lo + n2, :]
                if n > 0:
                    before = z_ref[i, lo - 1:lo + n2 - 1, :]
                else:
                    before = jnp.where(row == 0, 0.0, pltpu.roll(mid, 1, 0))
                if n < n1h - 1:
                    after = z_ref[i, lo + 1:lo + n2 + 1, :]
                else:
                    after = jnp.where(row == n2 - 1, 0.0, pltpu.roll(mid, n2 - 1, 0))
                tin_ref[part, i, n * pitch:n * pitch + n2, :] = (
                    before * w[0:1] + mid * w[1:2] + after * w[2:3] + bias)
    stages = ((tin_ref.at[0], tin_ref.at[1], cur_ref, kr0_ref, ki0_ref),
              (cur_ref, tin_ref.at[2], tin_ref.at[0], kr1_ref, ki1_ref))
    for order, (src, gate, dst, kr_ref, ki_ref) in enumerate(stages):
        skip = wide(skip_ref[order], 2 * nb)

        def stage1(jp, carry):
            cols = [(2 * jp + q, i) for q in range(2) for i in range(nb)]
            xb = lanes([src[i, pl.ds(j, n1h, stride=pitch), :] for j, i in cols]).astype(BF16)
            p = _dot(f1, xb)
            twr = lanes([wide(twrf_ref[2 * jp + q], nb) for q in range(2)])
            twi = lanes([wide(twif_ref[2 * jp + q], nb) for q in range(2)])
            pr, pi, pn = p[:n1h], p[n1h:2 * n1h], p[2 * n1h:2 * n1h + 1]
            re, im = pr * twr[:n1h] - pi * twi[:n1h], pr * twi[:n1h] + pi * twr[:n1h]
            re_n, im_n = pn * twr[n1h:n1h + 1], pn * twi[n1h:n1h + 1]
            for c, (j, i) in enumerate(cols):
                are_ref[i, pl.ds(j, n1h, stride=pitch), :] = lane_block(re, c)
                aim_ref[i, pl.ds(j, n1h, stride=pitch), :] = lane_block(im, c)
                are_ref[i, pl.ds(n1h * pitch + j, 1), :] = lane_block(re_n, c)
                aim_ref[i, pl.ds(n1h * pitch + j, 1), :] = lane_block(im_n, c)
            return carry

        lax.fori_loop(0, n2 // 2, stage1, 0, unroll=unroll)

        def slabs(ks):
            rows = [pl.ds(pl.multiple_of(k * pitch, SUBLANES), n2) for k in ks]
            a = _stack_bf16(lanes([are_ref[i, r, :] for r in rows for i in range(nb)]),
                            lanes([aim_ref[i, r, :] for r in rows for i in range(nb)]))
            x = _dot(f2, a)
            xr, xi = x[:n2], x[n2:]
            per_slab = lambda ref: lanes([wide(ref[k], nb) for k in ks])
            kr, ki = per_slab(kr_ref), per_slab(ki_ref)
            y = _dot(f2inv, _stack_bf16(xr * kr - xi * ki, xr * ki + xi * kr))
            br, bi = y[:n2], y[n2:]
            twr, twi = per_slab(twri_ref), per_slab(twii_ref)
            ore, oim = br * twr + bi * twi, bi * twr - br * twi
            for s, r in enumerate(rows):
                for i in range(nb):
                    are_ref[i, r, :] = lane_block(ore, s * nb + i)
                    aim_ref[i, r, :] = lane_block(oim, s * nb + i)

        def slab(k, carry):
            slabs((k,))
            return carry

        lax.fori_loop(0, n1h + 1, slab, 0, unroll=4)

        def stage1_inv(jp, carry):
            cols = [(pl.ds(2 * jp + q, n1h, stride=pitch), 2 * jp + q, i) for q in range(2) for i in range(nb)]
            spec = _stack_bf16(lanes([are_ref[i, r, :] for r, _, i in cols]),
                               lanes([aim_ref[i, r, :] for r, _, i in cols]))
            nyq = lanes([are_ref[i, pl.ds(n1h * pitch + j, 1), :] for _, j, i in cols])
            y = _dot(ginv, spec) + nyq_w * nyq
            x = lanes([src[i, r, :] for r, _, i in cols])
            g = lanes([gate[i, r, :] for r, _, i in cols])
            out = g * (y + x * skip)
            for c, (r, _, i) in enumerate(cols):
                dst[i, r, :] = lane_block(out, c)
            return carry

        lax.fori_loop(0, n2 // 2, stage1_inv, 0, unroll=unroll)

    def unpitch(n, carry):
        for i in range(nb):
            o_ref[i, pl.ds(pl.multiple_of(n * n2, n2), n2), :] = (
                tin_ref[0, i, pl.ds(pl.multiple_of(n * pitch, SUBLANES), n2), :])
        return carry

    lax.fori_loop(0, n1h, unpitch, 0)


def _hyena_fused(plan, z, col0, conv_w, conv_b, kr, ki, skip, *, nb):
    b, seq_len, _ = z.shape
    ch = skip.shape[1]
    n1h, n2, kh = plan.n1 // 2, plan.n2, plan.kh
    n_cb = ch // LANES
    cb0 = col0 // LANES
    part = lambda p: pl.BlockSpec((nb, seq_len, LANES), lambda c, bb: (bb, 0, cb0 + p * n_cb + c))
    taps = lambda p: pl.BlockSpec((conv_w.shape[0], LANES), lambda c, bb: (0, p * n_cb + c))
    bias = lambda p: pl.BlockSpec((1, LANES), lambda c, bb: (0, p * n_cb + c))
    once = pl.Buffered(1)
    filt = lambda o: pl.BlockSpec((kh, n2, LANES), lambda c, bb: (0, 0, o * n_cb + c), pipeline_mode=once)
    full = lambda a: pl.BlockSpec(a.shape, lambda c, bb, _n=a.ndim: (0,) * _n, pipeline_mode=once)
    tables = (plan.f1, plan.twr_fwd, plan.twi_fwd, plan.twr_inv, plan.twi_inv, plan.f2, plan.f2inv, plan.ginv)
    pitch = n2 + SUBLANES
    spec_scratch = pltpu.VMEM((nb, kh * pitch, LANES), F32)
    return pl.pallas_call(
        functools.partial(_hyena_fused_kernel, n1h=n1h, n2=n2, unroll=min(n2 // 2, 8)),
        grid=(n_cb, b // nb),
        in_specs=[part(0), part(1), part(2), taps(0), taps(1), taps(2), bias(0), bias(1), bias(2),
                  pl.BlockSpec((skip.shape[0], 1, LANES), lambda c, bb: (0, 0, c)),
                  filt(0), filt(0), filt(1), filt(1)] + [full(a) for a in tables],
        out_specs=pl.BlockSpec((nb, seq_len, LANES), lambda c, bb: (bb, 0, c)),
        out_shape=jax.ShapeDtypeStruct((b, seq_len, ch), F32),
        scratch_shapes=[spec_scratch, spec_scratch, pltpu.VMEM((nb, n1h * pitch, LANES), F32),
                        pltpu.VMEM((3, nb, n1h * pitch, LANES), F32)],
        compiler_params=_params(("parallel", "arbitrary"), 56),
        name="hyena_fused",
    )(z, z, z, conv_w, conv_w, conv_w, conv_b, conv_b, conv_b,
      skip.reshape(skip.shape[0], 1, ch), kr, ki, kr, ki, *tables)


def _dft_shape(seq_len):
    n2 = 128
    return 2 * seq_len // n2, n2


def _block_cols(n2, rows, target_bytes=1 << 20):
    g = max(SUBLANES, min(n2, target_bytes // (rows * LANES * 4)) // SUBLANES * SUBLANES)
    while n2 % g:
        g -= SUBLANES
    return g


def _slabs_per_step(kh, slab_bytes, target_bytes=3 << 20):
    ks = max(1, min(kh, target_bytes // slab_bytes))
    while kh % ks:
        ks -= 1
    return ks


def _hyena(plan, z, col0, conv_w, conv_b, filt, skip):
    b, seq_len, _ = z.shape
    n_orders, ch = skip.shape
    g = _block_cols(plan.n2, plan.n1 // 2)
    ks_filt = _slabs_per_step(plan.kh, 2 * plan.n2 * LANES * 4, target_bytes=6 << 20)
    kr, ki = _filt_spec(plan, filt, n_orders, ks=ks_filt)
    nb = 2 if b % 2 == 0 else 1
    fused_bytes = 15 * nb * seq_len * LANES * 4
    if n_orders == 2 and fused_bytes <= FUSED_VMEM_BUDGET:
        return _hyena_fused(plan, z, col0, conv_w, conv_b, kr, ki, skip, nb=nb)
    uc = _sconv(z, conv_w, conv_b, rows=min(seq_len, 512), col0=col0, n_parts=n_orders + 1)
    skip3 = skip.reshape(n_orders, 1, ch)
    cur, cur_part = uc, 0
    for order in range(n_orders):
        bre, bim = _conv_front(plan, cur, cur_part, kr, ki, order, ks=ks_filt)
        cur = _dft_inv(plan, bre, bim, cur, cur_part, uc, order + 1, skip3, order, g=g)
        cur_part = 0
    return cur[0]


def _trunk(x3, p, plan, filt):
    b, seq_len, d = x3.shape
    t = b * seq_len
    x = x3.reshape(t, d)
    x, z = _ffn_in(x, p["g1pre"], p["g1post"], p["w1gu"], p["w1d"], p["gmix"], p["w_in"], tm=512)
    z = z.reshape(b, seq_len, -1)
    d_pool = p["pool_w"].shape[0] * LANES
    y_pool = _pool(z, p["pool_w"], p["pool_scale"], rows=min(seq_len, 512))
    y_hy = _hyena(plan, z, d_pool, p["conv_w"], p["conv_b"], filt, p["skip"])
    x = _ffn_out(x, y_pool.reshape(t, -1), y_hy.reshape(t, -1), p["gpool"], p["ghy"], p["w_out"],
                 p["gmixpost"], p["g2pre"], p["g2post"], p["w2gu"], p["w2d"], tm=512)
    return x.reshape(b, seq_len, d)


def kernel(x_prompt, x_sample, ffn1_norm_pre, ffn1_norm_post, ffn1_w_gate_up, ffn1_w_down, mix_norm_pre, w_in, pool_w_map, pool_scale, hyena_conv_w, hyena_conv_b, filt_w_first, filt_b_first, filt_w_hidden, filt_b_hidden, filt_w_last, filt_freq, hyena_skip, pool_out_norm, hyena_out_norm, w_out, mix_norm_post, ffn2_norm_pre, ffn2_norm_post, ffn2_w_gate_up, ffn2_w_down):
    assert ffn1_norm_pre.shape[0] == 1, "single-layer trunk"
    row = lambda a: a[0].reshape(1, -1)
    p = dict(
        g1pre=row(ffn1_norm_pre), g1post=row(ffn1_norm_post),
        w1gu=ffn1_w_gate_up[0].astype(BF16), w1d=ffn1_w_down[0].astype(BF16),
        gmix=row(mix_norm_pre), w_in=w_in[0].astype(BF16),
        pool_w=pool_w_map[0].astype(BF16), pool_scale=row(pool_scale),
        conv_w=hyena_conv_w[0], conv_b=row(hyena_conv_b),
        skip=hyena_skip[0], gpool=row(pool_out_norm), ghy=row(hyena_out_norm),
        w_out=w_out[0].astype(BF16), gmixpost=row(mix_norm_post),
        g2pre=row(ffn2_norm_pre), g2post=row(ffn2_norm_post),
        w2gu=ffn2_w_gate_up[0].astype(BF16), w2d=ffn2_w_down[0].astype(BF16),
    )
    outs = []
    for x3 in (x_prompt, x_sample):
        seq_len = x3.shape[1]
        plan = _Plan(seq_len, *_dft_shape(seq_len))
        filt = _filt_gen(seq_len, filt_w_first[0], filt_b_first[0], filt_w_hidden[0],
                         filt_b_hidden[0], filt_w_last[0], filt_freq[0], rows=min(seq_len, 512),
                         n2=plan.n2)
        outs.append(_trunk(x3, p, plan, filt))
    return tuple(outs)
```

```python
import functools
import math

import jax
import jax.numpy as jnp
import numpy as np
from jax import lax
from jax.experimental import pallas as pl
from jax.experimental.pallas import tpu as pltpu

F32 = jnp.float32
BF16 = jnp.bfloat16
EPS = 1e-6
LANES = 128
SUBLANES = 8
MXU_DIM = 256
POOL_WINDOWS = (2, 4, 8, 16)
HALO = 8
DECAY_TARGET = 1e-2
FAST_DECAY_PCT = 0.3
SLOW_DECAY_PCT = 1.5
MAX_DECAY = math.log(DECAY_TARGET) / FAST_DECAY_PCT
MIN_DECAY = math.log(DECAY_TARGET) / SLOW_DECAY_PCT
HIGHEST = lax.Precision.HIGHEST
FUSED_VMEM_BUDGET = 32 << 20


def _params(sem, vmem_mib):
    return pltpu.CompilerParams(dimension_semantics=sem, vmem_limit_bytes=vmem_mib << 20)


def _rms(x, g):
    inv = lax.rsqrt(jnp.mean(x * x, axis=-1, keepdims=True) + EPS)
    return (x * inv) * g


def _dot(a, b):
    return jnp.dot(a, b, preferred_element_type=F32)


def _dot3(a, b):
    a_hi, b_hi = a.astype(BF16), b.astype(BF16)
    a_lo = (a - a_hi.astype(F32)).astype(BF16)
    b_lo = (b - b_hi.astype(F32)).astype(BF16)
    return _dot(a_hi, b_hi) + _dot(a_hi, b_lo) + _dot(a_lo, b_hi)


def _lanes(parts):
    return parts[0] if len(parts) == 1 else jnp.concatenate(parts, axis=1)


def _lane_block(a, i):
    return a[:, i * LANES:(i + 1) * LANES]


def _ff_chunks(d_ff, n_chunks=2):
    if d_ff % MXU_DIM:
        return (0, d_ff)
    tiles = d_ff // MXU_DIM
    return tuple(MXU_DIM * ((tiles * k + n_chunks - 1) // n_chunks) for k in range(n_chunks + 1))


def _half_step_ffn(x, gpre, gpost, wgu_ref, wd_ref):
    d_ff = wd_ref.shape[0]
    h = _rms(x, gpre).astype(BF16)
    acc = None
    bounds = _ff_chunks(d_ff)
    for lo, hi in zip(bounds[:-1], bounds[1:]):
        gate = _dot(h, wgu_ref[:, lo:hi])
        up = _dot(h, wgu_ref[:, d_ff + lo:d_ff + hi])
        act = (gate * jax.nn.sigmoid(gate) * up).astype(BF16)
        part = _dot(act, wd_ref[lo:hi, :])
        acc = part if acc is None else acc + part
    return x + 0.5 * _rms(acc, gpost)


def _ffn_in_kernel(x_ref, gpre_ref, gpost_ref, wgu_ref, wd_ref, gmix_ref, win_ref, o_ref, z_ref):
    x = _half_step_ffn(x_ref[...], gpre_ref[...], gpost_ref[...], wgu_ref, wd_ref)
    o_ref[...] = x
    z_ref[...] = _dot(_rms(x, gmix_ref[...]).astype(BF16), win_ref[...])


def _ffn_in(x, g_pre, g_post, wgu, wd, g_mix, w_in, *, tm):
    t, d = x.shape
    d_ff, d_in = wd.shape[0], w_in.shape[1]
    const = lambda i: (0, 0)
    row = lambda i: (i, 0)
    once = pl.Buffered(1)
    vec = pl.BlockSpec((1, d), const)
    return pl.pallas_call(
        _ffn_in_kernel,
        grid=(t // tm,),
        in_specs=[
            pl.BlockSpec((tm, d), row), vec, vec,
            pl.BlockSpec((d, 2 * d_ff), const, pipeline_mode=once),
            pl.BlockSpec((d_ff, d), const, pipeline_mode=once),
            vec,
            pl.BlockSpec((d, d_in), const, pipeline_mode=once),
        ],
        out_specs=[pl.BlockSpec((tm, d), row), pl.BlockSpec((tm, d_in), row)],
        out_shape=[jax.ShapeDtypeStruct((t, d), F32), jax.ShapeDtypeStruct((t, d_in), F32)],
        compiler_params=_params(("parallel",), 56),
        name="ffn_in",
    )(x, g_pre, g_post, wgu, wd, g_mix, w_in)


def _ffn_out_kernel(x_ref, yp_ref, yh_ref, gp_ref, gh_ref, wout_ref, gmix_ref,
                    gpre_ref, gpost_ref, wgu_ref, wd_ref, o_ref):
    d_pool = yp_ref.shape[1]
    yp = _rms(yp_ref[...], gp_ref[...]).astype(BF16)
    yh = _rms(yh_ref[...], gh_ref[...]).astype(BF16)
    y = _dot(yp, wout_ref[:d_pool, :]) + _dot(yh, wout_ref[d_pool:, :])
    x = x_ref[...] + _rms(y, gmix_ref[...])
    o_ref[...] = _half_step_ffn(x, gpre_ref[...], gpost_ref[...], wgu_ref, wd_ref)


def _ffn_out(x, yp, yh, gp, gh, w_out, g_mix, g_pre, g_post, wgu, wd, *, tm):
    t, d = x.shape
    d_ff, dp, dh = wd.shape[0], yp.shape[1], yh.shape[1]
    const = lambda i: (0, 0)
    row = lambda i: (i, 0)
    once = pl.Buffered(1)
    vec = lambda n: pl.BlockSpec((1, n), const)
    return pl.pallas_call(
        _ffn_out_kernel,
        grid=(t // tm,),
        in_specs=[
            pl.BlockSpec((tm, d), row), pl.BlockSpec((tm, dp), row), pl.BlockSpec((tm, dh), row),
            vec(dp), vec(dh),
            pl.BlockSpec((dp + dh, d), const, pipeline_mode=once),
            vec(d), vec(d), vec(d),
            pl.BlockSpec((d, 2 * d_ff), const, pipeline_mode=once),
            pl.BlockSpec((d_ff, d), const, pipeline_mode=once),
        ],
        out_specs=pl.BlockSpec((tm, d), row),
        out_shape=jax.ShapeDtypeStruct((t, d), F32),
        compiler_params=_params(("parallel",), 56),
        name="ffn_out",
    )(x, yp, yh, gp, gh, w_out, g_mix, g_pre, g_post, wgu, wd)


def _with_halo(prev_ref, main_ref, next_ref, i, n_tiles):
    prev = jnp.where(i > 0, prev_ref[0], 0.0)
    nxt = jnp.where(i < n_tiles - 1, next_ref[0], 0.0)
    return jnp.concatenate([prev, main_ref[0], nxt], axis=0)


def _pool_kernel(prev_ref, main_ref, next_ref, wmap_ref, scale_ref, o_ref, *, seq_len, n_tiles):
    i = pl.program_id(1)
    rows = main_ref.shape[1]
    n_ext = rows + 2 * HALO
    ext_all = _with_halo(prev_ref, main_ref, next_ref, i, n_tiles)
    pos = i * rows + lax.broadcasted_iota(jnp.int32, (rows, LANES), 0)
    for grp, window in enumerate(POOL_WINDOWS):
        lanes = slice(grp * LANES, (grp + 1) * LANES)
        ext = ext_all[:, lanes]
        ssum, w = ext + pltpu.roll(ext, 1, 0), 2
        while w < window:
            ssum, w = pltpu.roll(ssum, w // 2, 0) + pltpu.roll(ssum, n_ext - w // 2, 0), 2 * w
        u = main_ref[0, :, lanes]
        lo = jnp.clip(pos - window // 2, 0, seq_len)
        hi = jnp.clip(pos + (window - window // 2), 0, seq_len)
        d = ssum[HALO:HALO + rows] / (hi - lo).astype(F32) - u
        y = _dot(d.astype(BF16), wmap_ref[grp])
        o_ref[0, :, lanes] = y * scale_ref[:, lanes]


def _halo_specs(rows, width, seq_len, col_of):
    blocks_per_tile = rows // HALO
    last = seq_len // HALO - 1
    prev = pl.BlockSpec((1, HALO, width),
                        lambda b, i, c: (b, jnp.maximum(i * blocks_per_tile - 1, 0), col_of(c)))
    main = pl.BlockSpec((1, rows, width), lambda b, i, c: (b, i, col_of(c)))
    nxt = pl.BlockSpec((1, HALO, width),
                       lambda b, i, c: (b, jnp.minimum((i + 1) * blocks_per_tile, last), col_of(c)))
    return [prev, main, nxt]


def _pool(z, wmap, scale, *, rows):
    b, seq_len, _ = z.shape
    assert wmap.shape[0] == len(POOL_WINDOWS) and max(POOL_WINDOWS) <= 2 * HALO
    width = wmap.shape[0] * LANES
    n_tiles = seq_len // rows
    return pl.pallas_call(
        functools.partial(_pool_kernel, seq_len=seq_len, n_tiles=n_tiles),
        grid=(b, n_tiles, 1),
        in_specs=_halo_specs(rows, width, seq_len, lambda c: 0) + [
            pl.BlockSpec(wmap.shape, lambda b, i, c: (0, 0, 0)),
            pl.BlockSpec((1, width), lambda b, i, c: (0, 0)),
        ],
        out_specs=pl.BlockSpec((1, rows, width), lambda b, i, c: (b, i, 0)),
        out_shape=jax.ShapeDtypeStruct((b, seq_len, width), F32),
        compiler_params=_params(("parallel", "parallel", "parallel"), 40),
        name="pool",
    )(z, z, z, wmap, scale)


def _sconv_kernel(prev_ref, main_ref, next_ref, w_ref, b_ref, o_ref, *, n_tiles):
    i = pl.program_id(1)
    rows = main_ref.shape[1]
    ext = _with_halo(prev_ref, main_ref, next_ref, i, n_tiles)
    n_ext = rows + 2 * HALO
    before = pltpu.roll(ext, 1, 0)[HALO:HALO + rows]
    after = pltpu.roll(ext, n_ext - 1, 0)[HALO:HALO + rows]
    w = w_ref[...]
    out = before * w[0:1] + main_ref[0] * w[1:2] + after * w[2:3]
    o_ref[0, 0] = out + b_ref[...]


def _sconv(z, w, bias, *, rows, col0, n_parts):
    b, seq_len, _ = z.shape
    n_tiles = seq_len // rows
    width = w.shape[1] // n_parts
    assert col0 % width == 0
    return pl.pallas_call(
        functools.partial(_sconv_kernel, n_tiles=n_tiles),
        grid=(b, n_tiles, n_parts),
        in_specs=_halo_specs(rows, width, seq_len, lambda c: c + col0 // width) + [
            pl.BlockSpec((w.shape[0], width), lambda b, i, c: (0, c)),
            pl.BlockSpec((1, width), lambda b, i, c: (0, c)),
        ],
        out_specs=pl.BlockSpec((1, 1, rows, width), lambda b, i, c: (c, b, i, 0)),
        out_shape=jax.ShapeDtypeStruct((n_parts, b, seq_len, width), F32),
        compiler_params=_params(("parallel", "parallel", "parallel"), 40),
        name="sconv",
    )(z, z, z, w, bias)


def _filt_gen_kernel(bands_ref, wt_ref, wc_ref, ws_ref, b1_ref, wh_ref, bh_ref, freq_ref,
                     wl_ref, delta_ref, o_ref, *, seq_len, d_ch):
    rows = o_ref.shape[0] * (o_ref.shape[1] - SUBLANES)
    base = pl.program_id(0) * rows
    m_lane = (base + lax.broadcasted_iota(jnp.int32, (1, rows), 1)).astype(F32)
    t_lane = m_lane / (seq_len - 1.0)
    ang = (bands_ref[...] * (2.0 * math.pi / seq_len)) * m_lane
    freq = freq_ref[...]
    pre = (jnp.dot(wc_ref[...], jnp.cos(ang), precision=HIGHEST, preferred_element_type=F32)
           + jnp.dot(ws_ref[...], -jnp.sin(ang), precision=HIGHEST, preferred_element_type=F32)
           + wt_ref[...] * t_lane + b1_ref[...])
    h = jnp.sin(freq * pre)
    for layer in range(wh_ref.shape[0]):
        pre = jnp.dot(wh_ref[layer], h, precision=HIGHEST, preferred_element_type=F32) + bh_ref[layer]
        h = jnp.sin(freq * pre)
    out = _dot3(h.T, wl_ref[...])
    m_row = base + lax.broadcasted_iota(jnp.int32, (rows, d_ch), 0)
    t_row = m_row.astype(F32) / (seq_len - 1.0)
    decay = jnp.exp(-t_row * jnp.abs(delta_ref[...]))
    decay_bwd = jnp.where(m_row == 0, 0.0, decay)
    n_chunks, pitch, _ = o_ref.shape
    n2 = rows // n_chunks
    for q in range(out.shape[1] // d_ch):
        dq = decay_bwd if q % 2 == 1 else decay
        val = out[:, q * d_ch:(q + 1) * d_ch] * dq
        for n in range(n_chunks):
            o_ref[n, :n2, q * d_ch:(q + 1) * d_ch] = val[n * n2:(n + 1) * n2]
    o_ref[:, n2:, :] = jnp.zeros((n_chunks, pitch - n2, o_ref.shape[2]), F32)


def _filt_gen(seq_len, w_first, b_first, w_hidden, b_hidden, w_last, freq, *, rows, n2):
    pos_bands = (w_first.shape[0] - 1) // 2
    hidden = w_first.shape[1]
    n_cols = w_last.shape[1]
    d_ch = n_cols // 4
    bands = jnp.linspace(1e-4, pos_bands - 1, pos_bands, dtype=F32).reshape(pos_bands, 1)
    deltas = jnp.linspace(MIN_DECAY, MAX_DECAY, d_ch, dtype=F32).reshape(1, d_ch)
    w1t = w_first.T
    args = (bands, w1t[:, 0:1], w1t[:, 1:1 + pos_bands], w1t[:, 1 + pos_bands:],
            b_first.reshape(hidden, 1), jnp.swapaxes(w_hidden, 1, 2),
            b_hidden.reshape(b_hidden.shape[0], hidden, 1), freq.reshape(hidden, 1),
            w_last, deltas)
    full = lambda a: pl.BlockSpec(a.shape, lambda i, _n=a.ndim: (0,) * _n)
    return pl.pallas_call(
        functools.partial(_filt_gen_kernel, seq_len=seq_len, d_ch=d_ch),
        grid=(seq_len // rows,),
        in_specs=[full(a) for a in args],
        out_specs=pl.BlockSpec((rows // n2, n2 + SUBLANES, n_cols), lambda i: (i, 0, 0)),
        out_shape=jax.ShapeDtypeStruct((seq_len // n2, n2 + SUBLANES, n_cols), F32),
        compiler_params=_params(("parallel",), 40),
        name="filt_gen",
    )(*args)


class _Plan:
    def __init__(self, seq_len, n1, n2):
        assert n1 * n2 == 2 * seq_len and n1 % (2 * SUBLANES) == 0 and n2 % SUBLANES == 0
        self.seq_len, self.n1, self.n2 = seq_len, n1, n2
        n = n1 * n2
        n1h = n1 // 2
        self.kh = kh = n1h + 1
        k1 = np.arange(kh, dtype=np.float64)[:, None]
        m1 = np.arange(n1h, dtype=np.float64)[None, :]
        th1 = 2.0 * np.pi * k1 * m1 / n1
        f1 = np.zeros((n1 + SUBLANES, n1h))
        f1[:n1h] = np.cos(th1[:n1h]); f1[n1h:n1] = -np.sin(th1[:n1h]); f1[n1] = np.cos(th1[n1h])
        self.f1 = jnp.asarray(f1, F32)
        m2 = np.arange(n2, dtype=np.float64)[None, :]
        tht = 2.0 * np.pi * k1 * m2 / n
        twr, twi = np.cos(tht), -np.sin(tht)
        rep = lambda a: np.repeat(a[:, :, None], LANES, axis=2)
        self.twr_inv, self.twi_inv = jnp.asarray(rep(twr), F32), jnp.asarray(rep(twi), F32)
        pad = np.zeros((n2, SUBLANES - 1))
        self.twr_fwd = jnp.asarray(rep(np.concatenate([twr.T, pad], 1)), F32)
        self.twi_fwd = jnp.asarray(rep(np.concatenate([twi.T, pad], 1)), F32)
        k2 = np.arange(n2, dtype=np.float64)[:, None]
        th2 = 2.0 * np.pi * k2 * m2 / n2
        f2r, f2i = np.cos(th2), -np.sin(th2)
        self.f2 = jnp.asarray(np.block([[f2r, -f2i], [f2i, f2r]]), F32)
        self.f2inv = jnp.asarray(np.block([[f2r, f2i], [-f2i, f2r]]), F32)
        wgt = np.where(k1[:n1h] == 0, 1.0, 2.0) / n
        ginv = np.concatenate([(wgt * np.cos(th1[:n1h])).T, (-wgt * np.sin(th1[:n1h])).T], axis=1)
        self.ginv = jnp.asarray(ginv, F32)
        k1f = np.arange(kh, dtype=np.float64)[:, None]
        full = lambda cols: 2.0 * np.pi * k1f * cols[None, :] / n1
        stage1 = lambda th: np.concatenate(
            [np.cos(th[:n1h]), -np.sin(th[:n1h]), np.cos(th[n1h:]), np.zeros((SUBLANES - 1, th.shape[1]))], 0)
        fwd_cols = np.arange(n1h, dtype=np.float64)
        f1a = stage1(full(np.concatenate([fwd_cols, n1 - 1 - fwd_cols])))
        f1b = stage1(full(np.concatenate([fwd_cols, (n1 - fwd_cols) % n1])))
        f1b[:, n1h] = 0.0
        self.f1_filt, self.f1_filt0 = jnp.asarray(f1a, F32), jnp.asarray(f1b, F32)
        rows = lambda t: rep(np.concatenate([t.T, np.zeros((t.shape[1], SUBLANES - 1))], 1))
        coarse, fine = tht[:, ::SUBLANES], tht[:, :SUBLANES]
        self.tw_coarse = (jnp.asarray(rows(np.cos(coarse)), F32), jnp.asarray(rows(-np.sin(coarse)), F32))
        self.tw_fine = (jnp.asarray(rows(np.cos(fine)), F32), jnp.asarray(rows(-np.sin(fine)), F32))
        self.tw_slab_coarse = (jnp.asarray(rep(np.cos(coarse)), F32), jnp.asarray(rep(-np.sin(coarse)), F32))
        self.tw_slab_fine = (jnp.asarray(rep(np.cos(fine)), F32), jnp.asarray(rep(-np.sin(fine)), F32))


def _stack_bf16(re, im):
    return jnp.concatenate([re.astype(BF16), im.astype(BF16)], axis=0)


def _filt_spec_kernel(hf_ref, hba_ref, hbb_ref, f1_ref, f1j0_ref, twcr_ref, twci_ref, twfr_ref, twfi_ref, f2_ref,
                      kr_ref, ki_ref, are_ref, aim_ref, *, n1h, n2, jg):
    pitch = n2 + SUBLANES
    n_j = n2 // jg
    ks = kr_ref.shape[0]
    phase = pl.program_id(1)
    hf, hba, hbb = (r.reshape(n1h * jg, LANES) for r in (hf_ref, hba_ref, hbb_ref))

    @pl.when(phase < n_j)
    def _():
        f1 = f1_ref[...].astype(BF16)
        f1_first = jnp.where(phase == 0, f1j0_ref[...], f1_ref[...]).astype(BF16)
        j0 = phase * jg
        a0 = lax.shift_right_logical(j0, 3)

        def stage1(mat, ts):
            rows = lambda ref, start: ref[pl.ds(start, n1h, stride=jg), :]
            x = _lanes([jnp.concatenate([rows(hf, t), rows(hbb, 0) if t == 0 else rows(hba, jg - t)], axis=0)
                        for t in ts])
            p = _dot(mat, x.astype(BF16))
            cr = _lanes([twcr_ref[a0 + t // SUBLANES] for t in ts])
            ci = _lanes([twci_ref[a0 + t // SUBLANES] for t in ts])
            fr = _lanes([twfr_ref[t % SUBLANES] for t in ts])
            fi = _lanes([twfi_ref[t % SUBLANES] for t in ts])
            twr, twi = cr * fr - ci * fi, cr * fi + ci * fr
            pr, pi, pn = p[:n1h], p[n1h:2 * n1h], p[2 * n1h:2 * n1h + 1]
            re, im = pr * twr[:n1h] - pi * twi[:n1h], pr * twi[:n1h] + pi * twr[:n1h]
            re_n, im_n = pn * twr[n1h:n1h + 1], pn * twi[n1h:n1h + 1]
            for i, t in enumerate(ts):
                are_ref[pl.ds(j0 + t, n1h, stride=pitch), :] = _lane_block(re, i)
                aim_ref[pl.ds(j0 + t, n1h, stride=pitch), :] = _lane_block(im, i)
                are_ref[pl.ds(n1h * pitch + j0 + t, 1), :] = _lane_block(re_n, i)
                aim_ref[pl.ds(n1h * pitch + j0 + t, 1), :] = _lane_block(im_n, i)

        stage1(f1_first, (0,))
        for t in range(1, jg - 1, 2):
            stage1(f1, (t, t + 1))
        stage1(f1, (jg - 1,))

    @pl.when(phase >= n_j)
    def _():
        f2 = f2_ref[...].astype(BF16)
        kc = phase - n_j

        def slabs(first, count):
            rows = [pl.ds(pl.multiple_of((kc * ks + first + i) * pitch, SUBLANES), n2) for i in range(count)]
            x = _dot(f2, _stack_bf16(_lanes([are_ref[r, :] for r in rows]), _lanes([aim_ref[r, :] for r in rows])))
            for i in range(count):
                kr_ref[first + i] = _lane_block(x[:n2], i).astype(kr_ref.dtype)
                ki_ref[first + i] = _lane_block(x[n2:], i).astype(ki_ref.dtype)

        def pair(p, carry):
            slabs(2 * p, 2)
            return carry

        lax.fori_loop(0, ks // 2, pair, 0, unroll=2)
        if ks % 2:
            slabs(ks - 1, 1)


def _filt_spec(plan, filt, n_orders, *, ks, jg=16):
    n1h, n2, kh = plan.n1 // 2, plan.n2, plan.kh
    cols = filt.shape[2]
    n_cb = cols // (2 * n_orders * LANES)
    n_j = n2 // jg
    pitch = n2 + SUBLANES
    col = lambda c, back: (c // n_cb) * 2 * n_cb + back * n_cb + c % n_cb
    step = lambda s: jnp.minimum(s, n_j - 1)
    seq = lambda back, blk: pl.BlockSpec((n1h, jg, LANES), lambda c, s: (0, blk(step(s)), col(c, back)))
    full = lambda a: pl.BlockSpec(a.shape, lambda c, s, _n=a.ndim: (0,) * _n, pipeline_mode=pl.Buffered(1))
    tables = (plan.f1_filt, plan.f1_filt0, *plan.tw_coarse, *plan.tw_fine, plan.f2)
    out = jax.ShapeDtypeStruct((kh, n2, cols // 2), BF16)
    scratch = pltpu.VMEM((kh * pitch, LANES), F32)
    return pl.pallas_call(
        functools.partial(_filt_spec_kernel, n1h=n1h, n2=n2, jg=jg),
        grid=(n_orders * n_cb, n_j + kh // ks),
        in_specs=[seq(0, lambda s: s), seq(1, lambda s: n_j - 1 - s), seq(1, lambda s: (n_j - s) % n_j)]
        + [full(a) for a in tables],
        out_specs=[pl.BlockSpec((ks, n2, LANES), lambda c, s: (jnp.maximum(s - n_j, 0), 0, c))] * 2,
        out_shape=[out, out],
        scratch_shapes=[scratch, scratch],
        compiler_params=_params(("parallel", "arbitrary"), 48),
        name="filt_spec",
    )(filt, filt, filt, *tables)


def _conv_front_kernel(x_ref, kr_ref, ki_ref, f1_ref, twcr_ref, twci_ref, twfr_ref, twfi_ref,
                       icr_ref, ici_ref, ifr_ref, ifi_ref, f2_ref, f2inv_ref,
                       ore_ref, oim_ref, are_ref, aim_ref, *, n1h, n2, unroll):
    pitch = n2 + SUBLANES
    ks = kr_ref.shape[0]
    kc = pl.program_id(2)
    x2 = x_ref.at[0, 0]

    @pl.when(kc == 0)
    def _():
        f1 = f1_ref[...].astype(BF16)

        def body(jp, carry):
            js = (2 * jp, 2 * jp + 1)
            a = lax.shift_right_logical(jp, 2)
            bs = [2 * jnp.bitwise_and(jp, SUBLANES // 2 - 1) + q for q in range(2)]
            xb = _lanes([x2[pl.ds(j, n1h, stride=n2), :] for j in js]).astype(BF16)
            p = _dot(f1, xb)
            cr, ci = jnp.tile(twcr_ref[a], (1, 2)), jnp.tile(twci_ref[a], (1, 2))
            fr, fi = _lanes([twfr_ref[b] for b in bs]), _lanes([twfi_ref[b] for b in bs])
            twr, twi = cr * fr - ci * fi, cr * fi + ci * fr
            pr, pi, pn = p[:n1h], p[n1h:2 * n1h], p[2 * n1h:2 * n1h + 1]
            re, im = pr * twr[:n1h] - pi * twi[:n1h], pr * twi[:n1h] + pi * twr[:n1h]
            re_n, im_n = pn * twr[n1h:n1h + 1], pn * twi[n1h:n1h + 1]
            for i, j in enumerate(js):
                are_ref[pl.ds(j, n1h, stride=pitch), :] = _lane_block(re, i)
                aim_ref[pl.ds(j, n1h, stride=pitch), :] = _lane_block(im, i)
                are_ref[pl.ds(n1h * pitch + j, 1), :] = _lane_block(re_n, i)
                aim_ref[pl.ds(n1h * pitch + j, 1), :] = _lane_block(im_n, i)
            return carry

        lax.fori_loop(0, n2 // 2, body, 0, unroll=unroll)

    f2, f2inv = f2_ref[...].astype(BF16), f2inv_ref[...].astype(BF16)

    def slab_twiddle(k):
        cr, ci, fr, fi = icr_ref[k], ici_ref[k], ifr_ref[k], ifi_ref[k]
        pieces = [(cr[a:a + 1] * fr - ci[a:a + 1] * fi, cr[a:a + 1] * fi + ci[a:a + 1] * fr)
                  for a in range(n2 // SUBLANES)]
        return (jnp.concatenate([p[0] for p in pieces], axis=0), jnp.concatenate([p[1] for p in pieces], axis=0))

    def slabs(first, count):
        ks_abs = [kc * ks + first + i for i in range(count)]
        rows = [pl.ds(pl.multiple_of(k * pitch, SUBLANES), n2) for k in ks_abs]
        x = _dot(f2, _stack_bf16(_lanes([are_ref[r, :] for r in rows]), _lanes([aim_ref[r, :] for r in rows])))
        xr, xi = x[:n2], x[n2:]
        kr = _lanes([kr_ref[first + i] for i in range(count)]).astype(F32)
        ki = _lanes([ki_ref[first + i] for i in range(count)]).astype(F32)
        y = _dot(f2inv, _stack_bf16(xr * kr - xi * ki, xr * ki + xi * kr))
        br, bi = y[:n2], y[n2:]
        tw = [slab_twiddle(k) for k in ks_abs]
        twr, twi = _lanes([t[0] for t in tw]), _lanes([t[1] for t in tw])
        ore, oim = br * twr + bi * twi, bi * twr - br * twi
        for i in range(count):
            ore_ref[0, first + i] = _lane_block(ore, i)
            oim_ref[0, first + i] = _lane_block(oim, i)

    def pair(p, carry):
        slabs(2 * p, 2)
        return carry

    lax.fori_loop(0, ks // 2, pair, 0, unroll=2)
    if ks % 2:
        slabs(ks - 1, 1)


def _conv_front(plan, x4, part, kr, ki, order, *, ks):
    _, b, seq_len, ch = x4.shape
    n1h, n2, kh = plan.n1 // 2, plan.n2, plan.kh
    n_cb = ch // LANES
    pitch = n2 + SUBLANES
    once = pl.Buffered(1)
    full = lambda a: pl.BlockSpec(a.shape, lambda bb, c, k, _n=a.ndim: (0,) * _n, pipeline_mode=once)
    filt = pl.BlockSpec((ks, n2, LANES), lambda bb, c, k: (k, 0, order * n_cb + c))
    data = pl.BlockSpec((1, ks, n2, LANES), lambda bb, c, k: (bb, k, 0, c))
    tables = (plan.f1, *plan.tw_coarse, *plan.tw_fine, *plan.tw_slab_coarse, *plan.tw_slab_fine,
              plan.f2, plan.f2inv)
    out = jax.ShapeDtypeStruct((b, kh, n2, ch), F32)
    scratch = pltpu.VMEM((kh * pitch, LANES), F32)
    return pl.pallas_call(
        functools.partial(_conv_front_kernel, n1h=n1h, n2=n2, unroll=4),
        grid=(b, n_cb, kh // ks),
        in_specs=[pl.BlockSpec((1, 1, seq_len, LANES), lambda bb, c, k: (part, bb, 0, c), pipeline_mode=once),
                  filt, filt] + [full(a) for a in tables],
        out_specs=[data, data],
        out_shape=[out, out],
        scratch_shapes=[scratch, scratch],
        compiler_params=_params(("parallel", "parallel", "arbitrary"), 56),
        name="conv_front",
    )(x4, kr, ki, *tables)


def _dft_inv_kernel(bre_ref, bim_ref, x_ref, gate_ref, skip_ref, ginv_ref, o_ref, *, unroll, scale):
    _, _, n1h, g, _ = o_ref.shape
    b_re = bre_ref.at[0].reshape((n1h + 1) * g, LANES)
    b_im = bim_ref.at[0].reshape((n1h + 1) * g, LANES)
    x2 = x_ref.at[0, 0].reshape(n1h * g, LANES)
    gate2 = gate_ref.at[0, 0].reshape(n1h * g, LANES)
    o2 = o_ref.at[0, 0].reshape(n1h * g, LANES)
    ginv = ginv_ref[...].astype(BF16)
    skip = skip_ref[0]
    n1 = lax.broadcasted_iota(jnp.int32, (n1h, 2 * LANES), 0)
    nyq_w = jnp.where(jnp.bitwise_and(n1, 1) == 0, scale, -scale)
    skip = jnp.tile(skip, (1, 2))

    def body(jp, carry):
        rows = [pl.ds(2 * jp + i, n1h, stride=g) for i in range(2)]
        spec = _stack_bf16(_lanes([b_re[r, :] for r in rows]), _lanes([b_im[r, :] for r in rows]))
        nyq = _lanes([b_re[pl.ds(n1h * g + 2 * jp + i, 1), :] for i in range(2)])
        y = _dot(ginv, spec) + nyq_w * nyq
        out = _lanes([gate2[r, :] for r in rows]) * (y + _lanes([x2[r, :] for r in rows]) * skip)
        for i, r in enumerate(rows):
            o2[r, :] = _lane_block(out, i)
        return carry

    lax.fori_loop(0, g // 2, body, 0, unroll=unroll)


def _dft_inv(plan, bre, bim, x4, x_part, gate4, gate_part, skip, order, *, g):
    b, seq_len, ch = x4.shape[1:]
    n1h, n2, kh = plan.n1 // 2, plan.n2, plan.kh
    xv = x4.reshape(x4.shape[0], b, n1h, n2, ch)
    gv = gate4.reshape(gate4.shape[0], b, n1h, n2, ch)
    spec = pl.BlockSpec((1, kh, g, LANES), lambda j, bb, c: (bb, 0, j, c))
    time = lambda part: pl.BlockSpec((1, 1, n1h, g, LANES), lambda j, bb, c: (part, bb, 0, j, c))
    out = pl.pallas_call(
        functools.partial(_dft_inv_kernel, unroll=min(g // 2, 8), scale=1.0 / (plan.n1 * n2)),
        grid=(n2 // g, b, ch // LANES),
        in_specs=[
            spec, spec, time(x_part), time(gate_part),
            pl.BlockSpec((1, 1, LANES), lambda j, bb, c: (order, 0, c)),
            pl.BlockSpec(plan.ginv.shape, lambda j, bb, c: (0, 0)),
        ],
        out_specs=time(0),
        out_shape=jax.ShapeDtypeStruct((1, b, n1h, n2, ch), F32),
        compiler_params=_params(("parallel", "parallel", "parallel"), 48),
        name="dft_inv",
    )(bre, bim, xv, gv, skip, plan.ginv)
    return out.reshape(1, b, seq_len, ch)


def _hyena_fused_kernel(zv_ref, zx1_ref, zx2_ref, wv_ref, wx1_ref, wx2_ref, bv_ref, bx1_ref, bx2_ref,
                        skip_ref, kr0_ref, ki0_ref, kr1_ref, ki1_ref,
                        f1_ref, twrf_ref, twif_ref, twri_ref, twii_ref, f2_ref, f2inv_ref, ginv_ref,
                        o_ref, are_ref, aim_ref, cur_ref, tin_ref, *, n1h, n2, unroll):
    nb = o_ref.shape[0]
    pitch = n2 + SUBLANES
    f1 = f1_ref[...].astype(BF16)
    f2, f2inv = f2_ref[...].astype(BF16), f2inv_ref[...].astype(BF16)
    ginv = ginv_ref[...].astype(BF16)
    scale = 1.0 / (2 * n1h * n2)
    n1 = lax.broadcasted_iota(jnp.int32, (n1h, 2 * nb * LANES), 0)
    nyq_w = jnp.where(jnp.bitwise_and(n1, 1) == 0, scale, -scale)
    lanes, lane_block = _lanes, _lane_block
    wide = lambda t, reps: t if reps == 1 else jnp.tile(t, (1, reps))

    row = lax.broadcasted_iota(jnp.int32, (n2, LANES), 0)
    for part, (z_ref, w_ref, b_ref) in enumerate(((zv_ref, wv_ref, bv_ref), (zx1_ref, wx1_ref, bx1_ref),
                                                   (zx2_ref, wx2_ref, bx2_ref))):
        w, bias = w_ref[...], b_ref[...]
        for i in range(nb):
            for n in range(n1h):
                lo = n * n2
                mid = z_ref[i, lo:lo + n2, :]
                if n > 0:
                    before = z_ref[i, lo - 1:lo + n2 - 1, :]
                else:
                    before = jnp.where(row == 0, 0.0, pltpu.roll(mid, 1, 0))
                if n < n1h - 1:
                    after = z_ref[i, lo + 1:lo + n2 + 1, :]
                else:
                    after = jnp.where(row == n2 - 1, 0.0, pltpu.roll(mid, n2 - 1, 0))
                tin_ref[part, i, n * pitch:n * pitch + n2, :] = (
                    before * w[0:1] + mid * w[1:2] + after * w[2:3] + bias)
    stages = ((tin_ref.at[0], tin_ref.at[1], cur_ref, kr0_ref, ki0_ref),
              (cur_ref, tin_ref.at[2], tin_ref.at[0], kr1_ref, ki1_ref))
    for order, (src, gate, dst, kr_ref, ki_ref) in enumerate(stages):
        skip = wide(skip_ref[order], 2 * nb)

        def stage1(jp, carry):
            cols = [(2 * jp + q, i) for q in range(2) for i in range(nb)]
            xb = lanes([src[i, pl.ds(j, n1h, stride=pitch), :] for j, i in cols]).astype(BF16)
            p = _dot(f1, xb)
            twr = lanes([wide(twrf_ref[2 * jp + q], nb) for q in range(2)])
            twi = lanes([wide(twif_ref[2 * jp + q], nb) for q in range(2)])
            pr, pi, pn = p[:n1h], p[n1h:2 * n1h], p[2 * n1h:2 * n1h + 1]
            re, im = pr * twr[:n1h] - pi * twi[:n1h], pr * twi[:n1h] + pi * twr[:n1h]
            re_n, im_n = pn * twr[n1h:n1h + 1], pn * twi[n1h:n1h + 1]
            for c, (j, i) in enumerate(cols):
                are_ref[i, pl.ds(j, n1h, stride=pitch), :] = lane_block(re, c)
                aim_ref[i, pl.ds(j, n1h, stride=pitch), :] = lane_block(im, c)
                are_ref[i, pl.ds(n1h * pitch + j, 1), :] = lane_block(re_n, c)
                aim_ref[i, pl.ds(n1h * pitch + j, 1), :] = lane_block(im_n, c)
            return carry

        lax.fori_loop(0, n2 // 2, stage1, 0, unroll=unroll)

        def slabs(ks):
            rows = [pl.ds(pl.multiple_of(k * pitch, SUBLANES), n2) for k in ks]
            a = _stack_bf16(lanes([are_ref[i, r, :] for r in rows for i in range(nb)]),
                            lanes([aim_ref[i, r, :] for r in rows for i in range(nb)]))
            x = _dot(f2, a)
            xr, xi = x[:n2], x[n2:]
            per_slab = lambda ref: lanes([wide(ref[k], nb) for k in ks])
            kr, ki = per_slab(kr_ref).astype(F32), per_slab(ki_ref).astype(F32)
            y = _dot(f2inv, _stack_bf16(xr * kr - xi * ki, xr * ki + xi * kr))
            br, bi = y[:n2], y[n2:]
            twr, twi = per_slab(twri_ref), per_slab(twii_ref)
            ore, oim = br * twr + bi * twi, bi * twr - br * twi
            for s, r in enumerate(rows):
                for i in range(nb):
                    are_ref[i, r, :] = lane_block(ore, s * nb + i)
                    aim_ref[i, r, :] = lane_block(oim, s * nb + i)

        def slab(k, carry):
            slabs((k,))
            return carry

        lax.fori_loop(0, n1h + 1, slab, 0, unroll=4)

        def stage1_inv(jp, carry):
            cols = [(pl.ds(2 * jp + q, n1h, stride=pitch), 2 * jp + q, i) for q in range(2) for i in range(nb)]
            spec = _stack_bf16(lanes([are_ref[i, r, :] for r, _, i in cols]),
                               lanes([aim_ref[i, r, :] for r, _, i in cols]))
            nyq = lanes([are_ref[i, pl.ds(n1h * pitch + j, 1), :] for _, j, i in cols])
            y = _dot(ginv, spec) + nyq_w * nyq
            x = lanes([src[i, r, :] for r, _, i in cols])
            g = lanes([gate[i, r, :] for r, _, i in cols])
            out = g * (y + x * skip)
            for c, (r, _, i) in enumerate(cols):
                dst[i, r, :] = lane_block(out, c)
            return carry

        lax.fori_loop(0, n2 // 2, stage1_inv, 0, unroll=unroll)

    def unpitch(n, carry):
        for i in range(nb):
            o_ref[i, pl.ds(pl.multiple_of(n * n2, n2), n2), :] = (
                tin_ref[0, i, pl.ds(pl.multiple_of(n * pitch, SUBLANES), n2), :])
        return carry

    lax.fori_loop(0, n1h, unpitch, 0)


def _hyena_fused(plan, z, col0, conv_w, conv_b, kr, ki, skip, *, nb):
    b, seq_len, _ = z.shape
    ch = skip.shape[1]
    n1h, n2, kh = plan.n1 // 2, plan.n2, plan.kh
    n_cb = ch // LANES
    cb0 = col0 // LANES
    part = lambda p: pl.BlockSpec((nb, seq_len, LANES), lambda c, bb: (bb, 0, cb0 + p * n_cb + c))
    taps = lambda p: pl.BlockSpec((conv_w.shape[0], LANES), lambda c, bb: (0, p * n_cb + c))
    bias = lambda p: pl.BlockSpec((1, LANES), lambda c, bb: (0, p * n_cb + c))
    once = pl.Buffered(1)
    filt = lambda o: pl.BlockSpec((kh, n2, LANES), lambda c, bb: (0, 0, o * n_cb + c), pipeline_mode=once)
    full = lambda a: pl.BlockSpec(a.shape, lambda c, bb, _n=a.ndim: (0,) * _n, pipeline_mode=once)
    tables = (plan.f1, plan.twr_fwd, plan.twi_fwd, plan.twr_inv, plan.twi_inv, plan.f2, plan.f2inv, plan.ginv)
    pitch = n2 + SUBLANES
    spec_scratch = pltpu.VMEM((nb, kh * pitch, LANES), F32)
    return pl.pallas_call(
        functools.partial(_hyena_fused_kernel, n1h=n1h, n2=n2, unroll=min(n2 // 2, 8)),
        grid=(n_cb, b // nb),
        in_specs=[part(0), part(1), part(2), taps(0), taps(1), taps(2), bias(0), bias(1), bias(2),
                  pl.BlockSpec((skip.shape[0], 1, LANES), lambda c, bb: (0, 0, c)),
                  filt(0), filt(0), filt(1), filt(1)] + [full(a) for a in tables],
        out_specs=pl.BlockSpec((nb, seq_len, LANES), lambda c, bb: (bb, 0, c)),
        out_shape=jax.ShapeDtypeStruct((b, seq_len, ch), F32),
        scratch_shapes=[spec_scratch, spec_scratch, pltpu.VMEM((nb, n1h * pitch, LANES), F32),
                        pltpu.VMEM((3, nb, n1h * pitch, LANES), F32)],
        compiler_params=_params(("parallel", "arbitrary"), 56),
        name="hyena_fused",
    )(z, z, z, conv_w, conv_w, conv_w, conv_b, conv_b, conv_b,
      skip.reshape(skip.shape[0], 1, ch), kr, ki, kr, ki, *tables)


def _dft_shape(seq_len):
    n2 = 128
    return 2 * seq_len // n2, n2


def _block_cols(n2, rows, target_bytes=1 << 20):
    g = max(SUBLANES, min(n2, target_bytes // (rows * LANES * 4)) // SUBLANES * SUBLANES)
    while n2 % g:
        g -= SUBLANES
    return g


def _slabs_per_step(kh, slab_bytes, target_bytes=3 << 20):
    ks = max(1, min(kh, target_bytes // slab_bytes))
    while kh % ks:
        ks -= 1
    return ks


def _hyena(plan, z, col0, conv_w, conv_b, filt, skip):
    b, seq_len, _ = z.shape
    n_orders, ch = skip.shape
    g = _block_cols(plan.n2, plan.n1 // 2)
    ks_filt = _slabs_per_step(plan.kh, 2 * plan.n2 * LANES * 4, target_bytes=6 << 20)
    kr, ki = _filt_spec(plan, filt, n_orders, ks=ks_filt)
    nb = 2 if b % 2 == 0 else 1
    fused_bytes = 15 * nb * seq_len * LANES * 4
    if n_orders == 2 and fused_bytes <= FUSED_VMEM_BUDGET:
        return _hyena_fused(plan, z, col0, conv_w, conv_b, kr, ki, skip, nb=nb)
    uc = _sconv(z, conv_w, conv_b, rows=min(seq_len, 512), col0=col0, n_parts=n_orders + 1)
    skip3 = skip.reshape(n_orders, 1, ch)
    cur, cur_part = uc, 0
    for order in range(n_orders):
        bre, bim = _conv_front(plan, cur, cur_part, kr, ki, order, ks=ks_filt)
        cur = _dft_inv(plan, bre, bim, cur, cur_part, uc, order + 1, skip3, order, g=g)
        cur_part = 0
    return cur[0]


def _trunk(x3, p, plan, filt):
    b, seq_len, d = x3.shape
    t = b * seq_len
    x = x3.reshape(t, d)
    x, z = _ffn_in(x, p["g1pre"], p["g1post"], p["w1gu"], p["w1d"], p["gmix"], p["w_in"], tm=512)
    z = z.reshape(b, seq_len, -1)
    d_pool = p["pool_w"].shape[0] * LANES
    y_pool = _pool(z, p["pool_w"], p["pool_scale"], rows=min(seq_len, 512))
    y_hy = _hyena(plan, z, d_pool, p["conv_w"], p["conv_b"], filt, p["skip"])
    x = _ffn_out(x, y_pool.reshape(t, -1), y_hy.reshape(t, -1), p["gpool"], p["ghy"], p["w_out"],
                 p["gmixpost"], p["g2pre"], p["g2post"], p["w2gu"], p["w2d"], tm=512)
    return x.reshape(b, seq_len, d)


def kernel(x_prompt, x_sample, ffn1_norm_pre, ffn1_norm_post, ffn1_w_gate_up, ffn1_w_down, mix_norm_pre, w_in, pool_w_map, pool_scale, hyena_conv_w, hyena_conv_b, filt_w_first, filt_b_first, filt_w_hidden, filt_b_hidden, filt_w_last, filt_freq, hyena_skip, pool_out_norm, hyena_out_norm, w_out, mix_norm_post, ffn2_norm_pre, ffn2_norm_post, ffn2_w_gate_up, ffn2_w_down):
    assert ffn1_norm_pre.shape[0] == 1, "single-layer trunk"
    row = lambda a: a[0].reshape(1, -1)
    p = dict(
        g1pre=row(ffn1_norm_pre), g1post=row(ffn1_norm_post),
        w1gu=ffn1_w_gate_up[0].astype(BF16), w1d=ffn1_w_down[0].astype(BF16),
        gmix=row(mix_norm_pre), w_in=w_in[0].astype(BF16),
        pool_w=pool_w_map[0].astype(BF16), pool_scale=row(pool_scale),
        conv_w=hyena_conv_w[0], conv_b=row(hyena_conv_b),
        skip=hyena_skip[0], gpool=row(pool_out_norm), ghy=row(hyena_out_norm),
        w_out=w_out[0].astype(BF16), gmixpost=row(mix_norm_post),
        g2pre=row(ffn2_norm_pre), g2post=row(ffn2_norm_post),
        w2gu=ffn2_w_gate_up[0].astype(BF16), w2d=ffn2_w_down[0].astype(BF16),
    )
    outs = []
    for x3 in (x_prompt, x_sample):
        seq_len = x3.shape[1]
        plan = _Plan(seq_len, *_dft_shape(seq_len))
        filt = _filt_gen(seq_len, filt_w_first[0], filt_b_first[0], filt_w_hidden[0],
                         filt_b_hidden[0], filt_w_last[0], filt_freq[0], rows=min(seq_len, 512),
                         n2=plan.n2)
        outs.append(_trunk(x3, p, plan, filt))
    return tuple(outs)
```

```python
import functools
import math

import jax
import jax.numpy as jnp
import numpy as np
from jax import lax
from jax.experimental import pallas as pl
from jax.experimental.pallas import tpu as pltpu

F32 = jnp.float32
BF16 = jnp.bfloat16
EPS = 1e-6
LANES = 128
SUBLANES = 8
MXU_DIM = 256
POOL_WINDOWS = (2, 4, 8, 16)
HALO = 8
DECAY_TARGET = 1e-2
FAST_DECAY_PCT = 0.3
SLOW_DECAY_PCT = 1.5
MAX_DECAY = math.log(DECAY_TARGET) / FAST_DECAY_PCT
MIN_DECAY = math.log(DECAY_TARGET) / SLOW_DECAY_PCT
HIGHEST = lax.Precision.HIGHEST
FUSED_VMEM_BUDGET = 32 << 20


def _params(sem, vmem_mib):
    return pltpu.CompilerParams(dimension_semantics=sem, vmem_limit_bytes=vmem_mib << 20)


def _rms(x, g):
    inv = lax.rsqrt(jnp.mean(x * x, axis=-1, keepdims=True) + EPS)
    return (x * inv) * g


def _dot(a, b):
    return jnp.dot(a, b, preferred_element_type=F32)


def _dot3(a, b):
    a_hi, b_hi = a.astype(BF16), b.astype(BF16)
    a_lo = (a - a_hi.astype(F32)).astype(BF16)
    b_lo = (b - b_hi.astype(F32)).astype(BF16)
    return _dot(a_hi, b_hi) + _dot(a_hi, b_lo) + _dot(a_lo, b_hi)


def _lanes(parts):
    return parts[0] if len(parts) == 1 else jnp.concatenate(parts, axis=1)


def _lane_block(a, i):
    return a[:, i * LANES:(i + 1) * LANES]


def _ff_chunks(d_ff, n_chunks=2):
    if d_ff % MXU_DIM:
        return (0, d_ff)
    tiles = d_ff // MXU_DIM
    return tuple(MXU_DIM * ((tiles * k + n_chunks - 1) // n_chunks) for k in range(n_chunks + 1))


def _half_step_ffn(x, gpre, gpost, wgu_ref, wd_ref):
    d_ff = wd_ref.shape[0]
    h = _rms(x, gpre).astype(BF16)
    acc = None
    bounds = _ff_chunks(d_ff)
    for lo, hi in zip(bounds[:-1], bounds[1:]):
        gate = _dot(h, wgu_ref[:, lo:hi])
        up = _dot(h, wgu_ref[:, d_ff + lo:d_ff + hi])
        act = (gate * jax.nn.sigmoid(gate) * up).astype(BF16)
        part = _dot(act, wd_ref[lo:hi, :])
        acc = part if acc is None else acc + part
    return x + 0.5 * _rms(acc, gpost)


def _ffn_in_kernel(x_ref, gpre_ref, gpost_ref, wgu_ref, wd_ref, gmix_ref, win_ref, o_ref, z_ref):
    x = _half_step_ffn(x_ref[...], gpre_ref[...], gpost_ref[...], wgu_ref, wd_ref)
    o_ref[...] = x
    z_ref[...] = _dot(_rms(x, gmix_ref[...]).astype(BF16), win_ref[...])


def _ffn_in(x, g_pre, g_post, wgu, wd, g_mix, w_in, *, tm):
    t, d = x.shape
    d_ff, d_in = wd.shape[0], w_in.shape[1]
    const = lambda i: (0, 0)
    row = lambda i: (i, 0)
    once = pl.Buffered(1)
    vec = pl.BlockSpec((1, d), const)
    return pl.pallas_call(
        _ffn_in_kernel,
        grid=(t // tm,),
        in_specs=[
            pl.BlockSpec((tm, d), row), vec, vec,
            pl.BlockSpec((d, 2 * d_ff), const, pipeline_mode=once),
            pl.BlockSpec((d_ff, d), const, pipeline_mode=once),
            vec,
            pl.BlockSpec((d, d_in), const, pipeline_mode=once),
        ],
        out_specs=[pl.BlockSpec((tm, d), row), pl.BlockSpec((tm, d_in), row)],
        out_shape=[jax.ShapeDtypeStruct((t, d), F32), jax.ShapeDtypeStruct((t, d_in), F32)],
        compiler_params=_params(("parallel",), 56),
        name="ffn_in",
    )(x, g_pre, g_post, wgu, wd, g_mix, w_in)


def _ffn_out_kernel(x_ref, yp_ref, yh_ref, gp_ref, gh_ref, wout_ref, gmix_ref,
                    gpre_ref, gpost_ref, wgu_ref, wd_ref, o_ref):
    d_pool = yp_ref.shape[1]
    yp = _rms(yp_ref[...], gp_ref[...]).astype(BF16)
    yh = _rms(yh_ref[...], gh_ref[...]).astype(BF16)
    y = _dot(yp, wout_ref[:d_pool, :]) + _dot(yh, wout_ref[d_pool:, :])
    x = x_ref[...] + _rms(y, gmix_ref[...])
    o_ref[...] = _half_step_ffn(x, gpre_ref[...], gpost_ref[...], wgu_ref, wd_ref)


def _ffn_out(x, yp, yh, gp, gh, w_out, g_mix, g_pre, g_post, wgu, wd, *, tm):
    t, d = x.shape
    d_ff, dp, dh = wd.shape[0], yp.shape[1], yh.shape[1]
    const = lambda i: (0, 0)
    row = lambda i: (i, 0)
    once = pl.Buffered(1)
    vec = lambda n: pl.BlockSpec((1, n), const)
    return pl.pallas_call(
        _ffn_out_kernel,
        grid=(t // tm,),
        in_specs=[
            pl.BlockSpec((tm, d), row), pl.BlockSpec((tm, dp), row), pl.BlockSpec((tm, dh), row),
            vec(dp), vec(dh),
            pl.BlockSpec((dp + dh, d), const, pipeline_mode=once),
            vec(d), vec(d), vec(d),
            pl.BlockSpec((d, 2 * d_ff), const, pipeline_mode=once),
            pl.BlockSpec((d_ff, d), const, pipeline_mode=once),
        ],
        out_specs=pl.BlockSpec((tm, d), row),
        out_shape=jax.ShapeDtypeStruct((t, d), F32),
        compiler_params=_params(("parallel",), 56),
        name="ffn_out",
    )(x, yp, yh, gp, gh, w_out, g_mix, g_pre, g_post, wgu, wd)


def _with_halo(prev_ref, main_ref, next_ref, i, n_tiles):
    prev = jnp.where(i > 0, prev_ref[0], 0.0)
    nxt = jnp.where(i < n_tiles - 1, next_ref[0], 0.0)
    return jnp.concatenate([prev, main_ref[0], nxt], axis=0)


def _pool_kernel(prev_ref, main_ref, next_ref, wmap_ref, scale_ref, o_ref, *, seq_len, n_tiles):
    i = pl.program_id(1)
    rows = main_ref.shape[1]
    n_ext = rows + 2 * HALO
    ext_all = _with_halo(prev_ref, main_ref, next_ref, i, n_tiles)
    pos = i * rows + lax.broadcasted_iota(jnp.int32, (rows, LANES), 0)
    for grp, window in enumerate(POOL_WINDOWS):
        lanes = slice(grp * LANES, (grp + 1) * LANES)
        ext = ext_all[:, lanes]
        ssum, w = ext + pltpu.roll(ext, 1, 0), 2
        while w < window:
            ssum, w = pltpu.roll(ssum, w // 2, 0) + pltpu.roll(ssum, n_ext - w // 2, 0), 2 * w
        u = main_ref[0, :, lanes]
        lo = jnp.clip(pos - window // 2, 0, seq_len)
        hi = jnp.clip(pos + (window - window // 2), 0, seq_len)
        d = ssum[HALO:HALO + rows] / (hi - lo).astype(F32) - u
        y = _dot(d.astype(BF16), wmap_ref[grp])
        o_ref[0, :, lanes] = y * scale_ref[:, lanes]


def _halo_specs(rows, width, seq_len, col_of):
    blocks_per_tile = rows // HALO
    last = seq_len // HALO - 1
    prev = pl.BlockSpec((1, HALO, width),
                        lambda b, i, c: (b, jnp.maximum(i * blocks_per_tile - 1, 0), col_of(c)))
    main = pl.BlockSpec((1, rows, width), lambda b, i, c: (b, i, col_of(c)))
    nxt = pl.BlockSpec((1, HALO, width),
                       lambda b, i, c: (b, jnp.minimum((i + 1) * blocks_per_tile, last), col_of(c)))
    return [prev, main, nxt]


def _pool(z, wmap, scale, *, rows):
    b, seq_len, _ = z.shape
    assert wmap.shape[0] == len(POOL_WINDOWS) and max(POOL_WINDOWS) <= 2 * HALO
    width = wmap.shape[0] * LANES
    n_tiles = seq_len // rows
    return pl.pallas_call(
        functools.partial(_pool_kernel, seq_len=seq_len, n_tiles=n_tiles),
        grid=(b, n_tiles, 1),
        in_specs=_halo_specs(rows, width, seq_len, lambda c: 0) + [
            pl.BlockSpec(wmap.shape, lambda b, i, c: (0, 0, 0)),
            pl.BlockSpec((1, width), lambda b, i, c: (0, 0)),
        ],
        out_specs=pl.BlockSpec((1, rows, width), lambda b, i, c: (b, i, 0)),
        out_shape=jax.ShapeDtypeStruct((b, seq_len, width), F32),
        compiler_params=_params(("parallel", "parallel", "parallel"), 40),
        name="pool",
    )(z, z, z, wmap, scale)


def _sconv_kernel(prev_ref, main_ref, next_ref, w_ref, b_ref, o_ref, *, n_tiles):
    i = pl.program_id(1)
    rows = main_ref.shape[1]
    ext = _with_halo(prev_ref, main_ref, next_ref, i, n_tiles)
    n_ext = rows + 2 * HALO
    before = pltpu.roll(ext, 1, 0)[HALO:HALO + rows]
    after = pltpu.roll(ext, n_ext - 1, 0)[HALO:HALO + rows]
    w = w_ref[...]
    out = before * w[0:1] + main_ref[0] * w[1:2] + after * w[2:3]
    o_ref[0, 0] = out + b_ref[...]


def _sconv(z, w, bias, *, rows, col0, n_parts):
    b, seq_len, _ = z.shape
    n_tiles = seq_len // rows
    width = w.shape[1] // n_parts
    assert col0 % width == 0
    return pl.pallas_call(
        functools.partial(_sconv_kernel, n_tiles=n_tiles),
        grid=(b, n_tiles, n_parts),
        in_specs=_halo_specs(rows, width, seq_len, lambda c: c + col0 // width) + [
            pl.BlockSpec((w.shape[0], width), lambda b, i, c: (0, c)),
            pl.BlockSpec((1, width), lambda b, i, c: (0, c)),
        ],
        out_specs=pl.BlockSpec((1, 1, rows, width), lambda b, i, c: (c, b, i, 0)),
        out_shape=jax.ShapeDtypeStruct((n_parts, b, seq_len, width), F32),
        compiler_params=_params(("parallel", "parallel", "parallel"), 40),
        name="sconv",
    )(z, z, z, w, bias)


def _filt_gen_kernel(bands_ref, wt_ref, wc_ref, ws_ref, b1_ref, wh_ref, bh_ref, freq_ref,
                     wl_ref, delta_ref, o_ref, *, seq_len, d_ch):
    rows = o_ref.shape[0] * (o_ref.shape[1] - SUBLANES)
    base = pl.program_id(0) * rows
    m_lane = (base + lax.broadcasted_iota(jnp.int32, (1, rows), 1)).astype(F32)
    t_lane = m_lane / (seq_len - 1.0)
    ang = (bands_ref[...] * (2.0 * math.pi / seq_len)) * m_lane
    freq = freq_ref[...]
    pre = (jnp.dot(wc_ref[...], jnp.cos(ang), precision=HIGHEST, preferred_element_type=F32)
           + jnp.dot(ws_ref[...], -jnp.sin(ang), precision=HIGHEST, preferred_element_type=F32)
           + wt_ref[...] * t_lane + b1_ref[...])
    h = jnp.sin(freq * pre)
    for layer in range(wh_ref.shape[0]):
        pre = jnp.dot(wh_ref[layer], h, precision=HIGHEST, preferred_element_type=F32) + bh_ref[layer]
        h = jnp.sin(freq * pre)
    out = _dot3(h.T, wl_ref[...])
    m_row = base + lax.broadcasted_iota(jnp.int32, (rows, d_ch), 0)
    t_row = m_row.astype(F32) / (seq_len - 1.0)
    decay = jnp.exp(-t_row * jnp.abs(delta_ref[...]))
    decay_bwd = jnp.where(m_row == 0, 0.0, decay)
    n_chunks, pitch, _ = o_ref.shape
    n2 = rows // n_chunks
    for q in range(out.shape[1] // d_ch):
        dq = decay_bwd if q % 2 == 1 else decay
        val = out[:, q * d_ch:(q + 1) * d_ch] * dq
        for n in range(n_chunks):
            o_ref[n, :n2, q * d_ch:(q + 1) * d_ch] = val[n * n2:(n + 1) * n2]
    o_ref[:, n2:, :] = jnp.zeros((n_chunks, pitch - n2, o_ref.shape[2]), F32)


def _filt_gen(seq_len, w_first, b_first, w_hidden, b_hidden, w_last, freq, *, rows, n2):
    pos_bands = (w_first.shape[0] - 1) // 2
    hidden = w_first.shape[1]
    n_cols = w_last.shape[1]
    d_ch = n_cols // 4
    bands = jnp.linspace(1e-4, pos_bands - 1, pos_bands, dtype=F32).reshape(pos_bands, 1)
    deltas = jnp.linspace(MIN_DECAY, MAX_DECAY, d_ch, dtype=F32).reshape(1, d_ch)
    w1t = w_first.T
    args = (bands, w1t[:, 0:1], w1t[:, 1:1 + pos_bands], w1t[:, 1 + pos_bands:],
            b_first.reshape(hidden, 1), jnp.swapaxes(w_hidden, 1, 2),
            b_hidden.reshape(b_hidden.shape[0], hidden, 1), freq.reshape(hidden, 1),
            w_last, deltas)
    full = lambda a: pl.BlockSpec(a.shape, lambda i, _n=a.ndim: (0,) * _n)
    return pl.pallas_call(
        functools.partial(_filt_gen_kernel, seq_len=seq_len, d_ch=d_ch),
        grid=(seq_len // rows,),
        in_specs=[full(a) for a in args],
        out_specs=pl.BlockSpec((rows // n2, n2 + SUBLANES, n_cols), lambda i: (i, 0, 0)),
        out_shape=jax.ShapeDtypeStruct((seq_len // n2, n2 + SUBLANES, n_cols), F32),
        compiler_params=_params(("parallel",), 40),
        name="filt_gen",
    )(*args)


class _Plan:
    def __init__(self, seq_len, n1, n2):
        assert n1 * n2 == 2 * seq_len and n1 % (2 * SUBLANES) == 0 and n2 % SUBLANES == 0
        self.seq_len, self.n1, self.n2 = seq_len, n1, n2
        n = n1 * n2
        n1h = n1 // 2
        self.kh = kh = n1h + 1
        k1 = np.arange(kh, dtype=np.float64)[:, None]
        m1 = np.arange(n1h, dtype=np.float64)[None, :]
        th1 = 2.0 * np.pi * k1 * m1 / n1
        f1 = np.zeros((n1 + SUBLANES, n1h))
        f1[:n1h] = np.cos(th1[:n1h]); f1[n1h:n1] = -np.sin(th1[:n1h]); f1[n1] = np.cos(th1[n1h])
        self.f1 = jnp.asarray(f1, F32)
        m2 = np.arange(n2, dtype=np.float64)[None, :]
        tht = 2.0 * np.pi * k1 * m2 / n
        twr, twi = np.cos(tht), -np.sin(tht)
        rep = lambda a: np.repeat(a[:, :, None], LANES, axis=2)
        self.twr_inv, self.twi_inv = jnp.asarray(rep(twr), F32), jnp.asarray(rep(twi), F32)
        pad = np.zeros((n2, SUBLANES - 1))
        self.twr_fwd = jnp.asarray(rep(np.concatenate([twr.T, pad], 1)), F32)
        self.twi_fwd = jnp.asarray(rep(np.concatenate([twi.T, pad], 1)), F32)
        k2 = np.arange(n2, dtype=np.float64)[:, None]
        th2 = 2.0 * np.pi * k2 * m2 / n2
        f2r, f2i = np.cos(th2), -np.sin(th2)
        self.f2 = jnp.asarray(np.block([[f2r, -f2i], [f2i, f2r]]), F32)
        self.f2inv = jnp.asarray(np.block([[f2r, f2i], [-f2i, f2r]]), F32)
        wgt = np.where(k1[:n1h] == 0, 1.0, 2.0) / n
        ginv = np.concatenate([(wgt * np.cos(th1[:n1h])).T, (-wgt * np.sin(th1[:n1h])).T], axis=1)
        self.ginv = jnp.asarray(ginv, F32)
        k1f = np.arange(kh, dtype=np.float64)[:, None]
        full = lambda cols: 2.0 * np.pi * k1f * cols[None, :] / n1
        stage1 = lambda th: np.concatenate(
            [np.cos(th[:n1h]), -np.sin(th[:n1h]), np.cos(th[n1h:]), np.zeros((SUBLANES - 1, th.shape[1]))], 0)
        fwd_cols = np.arange(n1h, dtype=np.float64)
        f1a = stage1(full(np.concatenate([fwd_cols, n1 - 1 - fwd_cols])))
        f1b = stage1(full(np.concatenate([fwd_cols, (n1 - fwd_cols) % n1])))
        f1b[:, n1h] = 0.0
        self.f1_filt, self.f1_filt0 = jnp.asarray(f1a, F32), jnp.asarray(f1b, F32)
        rows = lambda t: rep(np.concatenate([t.T, np.zeros((t.shape[1], SUBLANES - 1))], 1))
        coarse, fine = tht[:, ::SUBLANES], tht[:, :SUBLANES]
        self.tw_coarse = (jnp.asarray(rows(np.cos(coarse)), F32), jnp.asarray(rows(-np.sin(coarse)), F32))
        self.tw_fine = (jnp.asarray(rows(np.cos(fine)), F32), jnp.asarray(rows(-np.sin(fine)), F32))
        self.tw_slab_coarse = (jnp.asarray(rep(np.cos(coarse)), F32), jnp.asarray(rep(-np.sin(coarse)), F32))
        self.tw_slab_fine = (jnp.asarray(rep(np.cos(fine)), F32), jnp.asarray(rep(-np.sin(fine)), F32))


def _stack_bf16(re, im):
    return jnp.concatenate([re.astype(BF16), im.astype(BF16)], axis=0)


def _filt_spec_kernel(hf_ref, hba_ref, hbb_ref, f1_ref, f1j0_ref, twcr_ref, twci_ref, twfr_ref, twfi_ref, f2_ref,
                      kr_ref, ki_ref, are_ref, aim_ref, *, n1h, n2, jg):
    pitch = n2 + SUBLANES
    n_j = n2 // jg
    ks = kr_ref.shape[0]
    phase = pl.program_id(1)
    hf, hba, hbb = (r.reshape(n1h * jg, LANES) for r in (hf_ref, hba_ref, hbb_ref))

    @pl.when(phase < n_j)
    def _():
        f1 = f1_ref[...].astype(BF16)
        f1_first = jnp.where(phase == 0, f1j0_ref[...], f1_ref[...]).astype(BF16)
        j0 = phase * jg
        a0 = lax.shift_right_logical(j0, 3)

        def stage1(mat, ts):
            rows = lambda ref, start: ref[pl.ds(start, n1h, stride=jg), :]
            x = _lanes([jnp.concatenate([rows(hf, t), rows(hbb, 0) if t == 0 else rows(hba, jg - t)], axis=0)
                        for t in ts])
            p = _dot(mat, x.astype(BF16))
            cr = _lanes([twcr_ref[a0 + t // SUBLANES] for t in ts])
            ci = _lanes([twci_ref[a0 + t // SUBLANES] for t in ts])
            fr = _lanes([twfr_ref[t % SUBLANES] for t in ts])
            fi = _lanes([twfi_ref[t % SUBLANES] for t in ts])
            twr, twi = cr * fr - ci * fi, cr * fi + ci * fr
            pr, pi, pn = p[:n1h], p[n1h:2 * n1h], p[2 * n1h:2 * n1h + 1]
            re, im = pr * twr[:n1h] - pi * twi[:n1h], pr * twi[:n1h] + pi * twr[:n1h]
            re_n, im_n = pn * twr[n1h:n1h + 1], pn * twi[n1h:n1h + 1]
            for i, t in enumerate(ts):
                are_ref[pl.ds(j0 + t, n1h, stride=pitch), :] = _lane_block(re, i)
                aim_ref[pl.ds(j0 + t, n1h, stride=pitch), :] = _lane_block(im, i)
                are_ref[pl.ds(n1h * pitch + j0 + t, 1), :] = _lane_block(re_n, i)
                aim_ref[pl.ds(n1h * pitch + j0 + t, 1), :] = _lane_block(im_n, i)

        stage1(f1_first, (0,))
        for t in range(1, jg - 1, 2):
            stage1(f1, (t, t + 1))
        stage1(f1, (jg - 1,))

    @pl.when(phase >= n_j)
    def _():
        f2 = f2_ref[...].astype(BF16)
        kc = phase - n_j

        def slabs(first, count):
            rows = [pl.ds(pl.multiple_of((kc * ks + first + i) * pitch, SUBLANES), n2) for i in range(count)]
            x = _dot(f2, _stack_bf16(_lanes([are_ref[r, :] for r in rows]), _lanes([aim_ref[r, :] for r in rows])))
            for i in range(count):
                kr_ref[first + i] = _lane_block(x[:n2], i).astype(kr_ref.dtype)
                ki_ref[first + i] = _lane_block(x[n2:], i).astype(ki_ref.dtype)

        def pair(p, carry):
            slabs(2 * p, 2)
            return carry

        lax.fori_loop(0, ks // 2, pair, 0, unroll=2)
        if ks % 2:
            slabs(ks - 1, 1)


def _filt_spec(plan, filt, n_orders, *, ks, jg=16):
    n1h, n2, kh = plan.n1 // 2, plan.n2, plan.kh
    cols = filt.shape[2]
    n_cb = cols // (2 * n_orders * LANES)
    n_j = n2 // jg
    pitch = n2 + SUBLANES
    col = lambda c, back: (c // n_cb) * 2 * n_cb + back * n_cb + c % n_cb
    step = lambda s: jnp.minimum(s, n_j - 1)
    seq = lambda back, blk: pl.BlockSpec((n1h, jg, LANES), lambda c, s: (0, blk(step(s)), col(c, back)))
    full = lambda a: pl.BlockSpec(a.shape, lambda c, s, _n=a.ndim: (0,) * _n, pipeline_mode=pl.Buffered(1))
    tables = (plan.f1_filt, plan.f1_filt0, *plan.tw_coarse, *plan.tw_fine, plan.f2)
    out = jax.ShapeDtypeStruct((kh, n2, cols // 2), BF16)
    scratch = pltpu.VMEM((kh * pitch, LANES), F32)
    return pl.pallas_call(
        functools.partial(_filt_spec_kernel, n1h=n1h, n2=n2, jg=jg),
        grid=(n_orders * n_cb, n_j + kh // ks),
        in_specs=[seq(0, lambda s: s), seq(1, lambda s: n_j - 1 - s), seq(1, lambda s: (n_j - s) % n_j)]
        + [full(a) for a in tables],
        out_specs=[pl.BlockSpec((ks, n2, LANES), lambda c, s: (jnp.maximum(s - n_j, 0), 0, c))] * 2,
        out_shape=[out, out],
        scratch_shapes=[scratch, scratch],
        compiler_params=_params(("parallel", "arbitrary"), 48),
        name="filt_spec",
    )(filt, filt, filt, *tables)


def _conv_front_kernel(x_ref, kr_ref, ki_ref, f1_ref, twcr_ref, twci_ref, twfr_ref, twfi_ref,
                       icr_ref, ici_ref, ifr_ref, ifi_ref, f2_ref, f2inv_ref,
                       ore_ref, oim_ref, are_ref, aim_ref, *, n1h, n2, unroll):
    pitch = n2 + SUBLANES
    ks = kr_ref.shape[0]
    kc = pl.program_id(2)
    x2 = x_ref.at[0, 0]

    @pl.when(kc == 0)
    def _():
        f1 = f1_ref[...].astype(BF16)

        def body(jp, carry):
            js = (2 * jp, 2 * jp + 1)
            a = lax.shift_right_logical(jp, 2)
            bs = [2 * jnp.bitwise_and(jp, SUBLANES // 2 - 1) + q for q in range(2)]
            xb = _lanes([x2[pl.ds(j, n1h, stride=n2), :] for j in js]).astype(BF16)
            p = _dot(f1, xb)
            cr, ci = jnp.tile(twcr_ref[a], (1, 2)), jnp.tile(twci_ref[a], (1, 2))
            fr, fi = _lanes([twfr_ref[b] for b in bs]), _lanes([twfi_ref[b] for b in bs])
            twr, twi = cr * fr - ci * fi, cr * fi + ci * fr
            pr, pi, pn = p[:n1h], p[n1h:2 * n1h], p[2 * n1h:2 * n1h + 1]
            re, im = pr * twr[:n1h] - pi * twi[:n1h], pr * twi[:n1h] + pi * twr[:n1h]
            re_n, im_n = pn * twr[n1h:n1h + 1], pn * twi[n1h:n1h + 1]
            for i, j in enumerate(js):
                are_ref[pl.ds(j, n1h, stride=pitch), :] = _lane_block(re, i)
                aim_ref[pl.ds(j, n1h, stride=pitch), :] = _lane_block(im, i)
                are_ref[pl.ds(n1h * pitch + j, 1), :] = _lane_block(re_n, i)
                aim_ref[pl.ds(n1h * pitch + j, 1), :] = _lane_block(im_n, i)
            return carry

        lax.fori_loop(0, n2 // 2, body, 0, unroll=unroll)

    f2, f2inv = f2_ref[...].astype(BF16), f2inv_ref[...].astype(BF16)

    def slab_twiddle(k):
        cr, ci, fr, fi = icr_ref[k], ici_ref[k], ifr_ref[k], ifi_ref[k]
        pieces = [(cr[a:a + 1] * fr - ci[a:a + 1] * fi, cr[a:a + 1] * fi + ci[a:a + 1] * fr)
                  for a in range(n2 // SUBLANES)]
        return (jnp.concatenate([p[0] for p in pieces], axis=0), jnp.concatenate([p[1] for p in pieces], axis=0))

    def slabs(first, count):
        ks_abs = [kc * ks + first + i for i in range(count)]
        rows = [pl.ds(pl.multiple_of(k * pitch, SUBLANES), n2) for k in ks_abs]
        x = _dot(f2, _stack_bf16(_lanes([are_ref[r, :] for r in rows]), _lanes([aim_ref[r, :] for r in rows])))
        xr, xi = x[:n2], x[n2:]
        kr = _lanes([kr_ref[first + i] for i in range(count)]).astype(F32)
        ki = _lanes([ki_ref[first + i] for i in range(count)]).astype(F32)
        y = _dot(f2inv, _stack_bf16(xr * kr - xi * ki, xr * ki + xi * kr))
        br, bi = y[:n2], y[n2:]
        tw = [slab_twiddle(k) for k in ks_abs]
        twr, twi = _lanes([t[0] for t in tw]), _lanes([t[1] for t in tw])
        ore, oim = br * twr + bi * twi, bi * twr - br * twi
        for i in range(count):
            ore_ref[0, first + i] = _lane_block(ore, i)
            oim_ref[0, first + i] = _lane_block(oim, i)

    def pair(p, carry):
        slabs(2 * p, 2)
        return carry

    lax.fori_loop(0, ks // 2, pair, 0, unroll=2)
    if ks % 2:
        slabs(ks - 1, 1)


def _conv_front(plan, x4, part, kr, ki, order, *, ks):
    _, b, seq_len, ch = x4.shape
    n1h, n2, kh = plan.n1 // 2, plan.n2, plan.kh
    n_cb = ch // LANES
    pitch = n2 + SUBLANES
    once = pl.Buffered(1)
    full = lambda a: pl.BlockSpec(a.shape, lambda bb, c, k, _n=a.ndim: (0,) * _n, pipeline_mode=once)
    filt = pl.BlockSpec((ks, n2, LANES), lambda bb, c, k: (k, 0, order * n_cb + c))
    data = pl.BlockSpec((1, ks, n2, LANES), lambda bb, c, k: (bb, k, 0, c))
    tables = (plan.f1, *plan.tw_coarse, *plan.tw_fine, *plan.tw_slab_coarse, *plan.tw_slab_fine,
              plan.f2, plan.f2inv)
    out = jax.ShapeDtypeStruct((b, kh, n2, ch), F32)
    scratch = pltpu.VMEM((kh * pitch, LANES), F32)
    return pl.pallas_call(
        functools.partial(_conv_front_kernel, n1h=n1h, n2=n2, unroll=4),
        grid=(b, n_cb, kh // ks),
        in_specs=[pl.BlockSpec((1, 1, seq_len, LANES), lambda bb, c, k: (part, bb, 0, c), pipeline_mode=once),
                  filt, filt] + [full(a) for a in tables],
        out_specs=[data, data],
        out_shape=[out, out],
        scratch_shapes=[scratch, scratch],
        compiler_params=_params(("parallel", "parallel", "arbitrary"), 56),
        name="conv_front",
    )(x4, kr, ki, *tables)


def _dft_inv_kernel(bre_ref, bim_ref, x_ref, gate_ref, skip_ref, ginv_ref, o_ref, *, unroll, scale):
    _, _, n1h, g, _ = o_ref.shape
    b_re = bre_ref.at[0].reshape((n1h + 1) * g, LANES)
    b_im = bim_ref.at[0].reshape((n1h + 1) * g, LANES)
    x2 = x_ref.at[0, 0].reshape(n1h * g, LANES)
    gate2 = gate_ref.at[0, 0].reshape(n1h * g, LANES)
    o2 = o_ref.at[0, 0].reshape(n1h * g, LANES)
    ginv = ginv_ref[...].astype(BF16)
    skip = skip_ref[0]
    n1 = lax.broadcasted_iota(jnp.int32, (n1h, 2 * LANES), 0)
    nyq_w = jnp.where(jnp.bitwise_and(n1, 1) == 0, scale, -scale)
    skip = jnp.tile(skip, (1, 2))

    def body(jp, carry):
        rows = [pl.ds(2 * jp + i, n1h, stride=g) for i in range(2)]
        spec = _stack_bf16(_lanes([b_re[r, :] for r in rows]), _lanes([b_im[r, :] for r in rows]))
        nyq = _lanes([b_re[pl.ds(n1h * g + 2 * jp + i, 1), :] for i in range(2)])
        y = _dot(ginv, spec) + nyq_w * nyq
        out = _lanes([gate2[r, :] for r in rows]) * (y + _lanes([x2[r, :] for r in rows]) * skip)
        for i, r in enumerate(rows):
            o2[r, :] = _lane_block(out, i)
        return carry

    lax.fori_loop(0, g // 2, body, 0, unroll=unroll)


def _dft_inv(plan, bre, bim, x4, x_part, gate4, gate_part, skip, order, *, g):
    b, seq_len, ch = x4.shape[1:]
    n1h, n2, kh = plan.n1 // 2, plan.n2, plan.kh
    xv = x4.reshape(x4.shape[0], b, n1h, n2, ch)
    gv = gate4.reshape(gate4.shape[0], b, n1h, n2, ch)
    spec = pl.BlockSpec((1, kh, g, LANES), lambda j, bb, c: (bb, 0, j, c))
    time = lambda part: pl.BlockSpec((1, 1, n1h, g, LANES), lambda j, bb, c: (part, bb, 0, j, c))
    out = pl.pallas_call(
        functools.partial(_dft_inv_kernel, unroll=min(g // 2, 8), scale=1.0 / (plan.n1 * n2)),
        grid=(n2 // g, b, ch // LANES),
        in_specs=[
            spec, spec, time(x_part), time(gate_part),
            pl.BlockSpec((1, 1, LANES), lambda j, bb, c: (order, 0, c)),
            pl.BlockSpec(plan.ginv.shape, lambda j, bb, c: (0, 0)),
        ],
        out_specs=time(0),
        out_shape=jax.ShapeDtypeStruct((1, b, n1h, n2, ch), F32),
        compiler_params=_params(("parallel", "parallel", "parallel"), 48),
        name="dft_inv",
    )(bre, bim, xv, gv, skip, plan.ginv)
    return out.reshape(1, b, seq_len, ch)


def _hyena_fused_kernel(zv_ref, zx1_ref, zx2_ref, wv_ref, wx1_ref, wx2_ref, bv_ref, bx1_ref, bx2_ref,
                        skip_ref, kr0_ref, ki0_ref, kr1_ref, ki1_ref,
                        f1_ref, twrf_ref, twif_ref, twri_ref, twii_ref, f2_ref, f2inv_ref, ginv_ref,
                        o_ref, are_ref, aim_ref, cur_ref, tin_ref, *, n1h, n2, unroll):
    nb = o_ref.shape[0]
    pitch = n2 + SUBLANES
    f1 = f1_ref[...].astype(BF16)
    f2, f2inv = f2_ref[...].astype(BF16), f2inv_ref[...].astype(BF16)
    ginv = ginv_ref[...].astype(BF16)
    scale = 1.0 / (2 * n1h * n2)
    n1 = lax.broadcasted_iota(jnp.int32, (n1h, 2 * nb * LANES), 0)
    nyq_w = jnp.where(jnp.bitwise_and(n1, 1) == 0, scale, -scale)
    lanes, lane_block = _lanes, _lane_block
    wide = lambda t, reps: t if reps == 1 else jnp.tile(t, (1, reps))

    row = lax.broadcasted_iota(jnp.int32, (n2, LANES), 0)
    for part, (z_ref, w_ref, b_ref) in enumerate(((zv_ref, wv_ref, bv_ref), (zx1_ref, wx1_ref, bx1_ref),
                                                   (zx2_ref, wx2_ref, bx2_ref))):
        w, bias = w_ref[...], b_ref[...]
        for i in range(nb):
            for n in range(n1h):
                lo = n * n2
                mid = z_ref[i, lo:lo + n2, :]
                if n > 0:
                    before = z_ref[i, lo - 1:lo + n2 - 1, :]
                else:
                    before = jnp.where(row == 0, 0.0, pltpu.roll(mid, 1, 0))
                if n < n1h - 1:
                    after = z_ref[i, lo + 1:lo + n2 + 1, :]
                else:
                    after = jnp.where(row == n2 - 1, 0.0, pltpu.roll(mid, n2 - 1, 0))
                tin_ref[part, i, n * pitch:n * pitch + n2, :] = (
                    before * w[0:1] + mid * w[1:2] + after * w[2:3] + bias)
    stages = ((tin_ref.at[0], tin_ref.at[1], cur_ref, kr0_ref, ki0_ref),
              (cur_ref, tin_ref.at[2], tin_ref.at[0], kr1_ref, ki1_ref))
    for order, (src, gate, dst, kr_ref, ki_ref) in enumerate(stages):
        skip = wide(skip_ref[order], 2 * nb)

        def stage1(jp, carry):
            cols = [(2 * jp + q, i) for q in range(2) for i in range(nb)]
            xb = lanes([src[i, pl.ds(j, n1h, stride=pitch), :] for j, i in cols]).astype(BF16)
            p = _dot(f1, xb)
            twr = lanes([wide(twrf_ref[2 * jp + q], nb) for q in range(2)])
            twi = lanes([wide(twif_ref[2 * jp + q], nb) for q in range(2)])
            pr, pi, pn = p[:n1h], p[n1h:2 * n1h], p[2 * n1h:2 * n1h + 1]
            re, im = pr * twr[:n1h] - pi * twi[:n1h], pr * twi[:n1h] + pi * twr[:n1h]
            re_n, im_n = pn * twr[n1h:n1h + 1], pn * twi[n1h:n1h + 1]
            for c, (j, i) in enumerate(cols):
                are_ref[i, pl.ds(j, n1h, stride=pitch), :] = lane_block(re, c)
                aim_ref[i, pl.ds(j, n1h, stride=pitch), :] = lane_block(im, c)
                are_ref[i, pl.ds(n1h * pitch + j, 1), :] = lane_block(re_n, c)
                aim_ref[i, pl.ds(n1h * pitch + j, 1), :] = lane_block(im_n, c)
            return carry

        lax.fori_loop(0, n2 // 2, stage1, 0, unroll=unroll)

        def slabs(ks):
            rows = [pl.ds(pl.multiple_of(k * pitch, SUBLANES), n2) for k in ks]
            a = _stack_bf16(lanes([are_ref[i, r, :] for r in rows for i in range(nb)]),
                            lanes([aim_ref[i, r, :] for r in rows for i in range(nb)]))
            x = _dot(f2, a)
            xr, xi = x[:n2], x[n2:]
            per_slab = lambda ref: lanes([wide(ref[k], nb) for k in ks])
            kr, ki = per_slab(kr_ref).astype(F32), per_slab(ki_ref).astype(F32)
            y = _dot(f2inv, _stack_bf16(xr * kr - xi * ki, xr * ki + xi * kr))
            br, bi = y[:n2], y[n2:]
            twr, twi = per_slab(twri_ref), per_slab(twii_ref)
            ore, oim = br * twr + bi * twi, bi * twr - br * twi
            for s, r in enumerate(rows):
                for i in range(nb):
                    are_ref[i, r, :] = lane_block(ore, s * nb + i)
                    aim_ref[i, r, :] = lane_block(oim, s * nb + i)

        def slab(k, carry):
            slabs((k,))
            return carry

        lax.fori_loop(0, n1h + 1, slab, 0, unroll=4)

        def stage1_inv(jp, carry):
            cols = [(pl.ds(2 * jp + q, n1h, stride=pitch), 2 * jp + q, i) for q in range(2) for i in range(nb)]
            spec = _stack_bf16(lanes([are_ref[i, r, :] for r, _, i in cols]),
                               lanes([aim_ref[i, r, :] for r, _, i in cols]))
            nyq = lanes([are_ref[i, pl.ds(n1h * pitch + j, 1), :] for _, j, i in cols])
            y = _dot(ginv, spec) + nyq_w * nyq
            x = lanes([src[i, r, :] for r, _, i in cols])
            g = lanes([gate[i, r, :] for r, _, i in cols])
            out = g * (y + x * skip)
            for c, (r, _, i) in enumerate(cols):
                dst[i, r, :] = lane_block(out, c)
            return carry

        lax.fori_loop(0, n2 // 2, stage1_inv, 0, unroll=unroll)

    def unpitch(n, carry):
        for i in range(nb):
            o_ref[i, pl.ds(pl.multiple_of(n * n2, n2), n2), :] = (
                tin_ref[0, i, pl.ds(pl.multiple_of(n * pitch, SUBLANES), n2), :])
        return carry

    lax.fori_loop(0, n1h, unpitch, 0)


def _hyena_fused(plan, z, col0, conv_w, conv_b, kr, ki, skip, *, nb):
    b, seq_len, _ = z.shape
    ch = skip.shape[1]
    n1h, n2, kh = plan.n1 // 2, plan.n2, plan.kh
    n_cb = ch // LANES
    cb0 = col0 // LANES
    part = lambda p: pl.BlockSpec((nb, seq_len, LANES), lambda c, bb: (bb, 0, cb0 + p * n_cb + c))
    taps = lambda p: pl.BlockSpec((conv_w.shape[0], LANES), lambda c, bb: (0, p * n_cb + c))
    bias = lambda p: pl.BlockSpec((1, LANES), lambda c, bb: (0, p * n_cb + c))
    once = pl.Buffered(1)
    filt = lambda o: pl.BlockSpec((kh, n2, LANES), lambda c, bb: (0, 0, o * n_cb + c), pipeline_mode=once)
    full = lambda a: pl.BlockSpec(a.shape, lambda c, bb, _n=a.ndim: (0,) * _n, pipeline_mode=once)
    tables = (plan.f1, plan.twr_fwd, plan.twi_fwd, plan.twr_inv, plan.twi_inv, plan.f2, plan.f2inv, plan.ginv)
    pitch = n2 + SUBLANES
    spec_scratch = pltpu.VMEM((nb, kh * pitch, LANES), F32)
    return pl.pallas_call(
        functools.partial(_hyena_fused_kernel, n1h=n1h, n2=n2, unroll=min(n2 // 2, 8)),
        grid=(n_cb, b // nb),
        in_specs=[part(0), part(1), part(2), taps(0), taps(1), taps(2), bias(0), bias(1), bias(2),
                  pl.BlockSpec((skip.shape[0], 1, LANES), lambda c, bb: (0, 0, c)),
                  filt(0), filt(0), filt(1), filt(1)] + [full(a) for a in tables],
        out_specs=pl.BlockSpec((nb, seq_len, LANES), lambda c, bb: (bb, 0, c)),
        out_shape=jax.ShapeDtypeStruct((b, seq_len, ch), F32),
        scratch_shapes=[spec_scratch, spec_scratch, pltpu.VMEM((nb, n1h * pitch, LANES), F32),
                        pltpu.VMEM((3, nb, n1h * pitch, LANES), F32)],
        compiler_params=_params(("parallel", "arbitrary"), 56),
        name="hyena_fused",
    )(z, z, z, conv_w, conv_w, conv_w, conv_b, conv_b, conv_b,
      skip.reshape(skip.shape[0], 1, ch), kr, ki, kr, ki, *tables)


def _dft_shape(seq_len):
    n2 = 128
    return 2 * seq_len // n2, n2


def _block_cols(n2, rows, target_bytes=1 << 20):
    g = max(SUBLANES, min(n2, target_bytes // (rows * LANES * 4)) // SUBLANES * SUBLANES)
    while n2 % g:
        g -= SUBLANES
    return g


def _slabs_per_step(kh, slab_bytes, target_bytes=3 << 20):
    ks = max(1, min(kh, target_bytes // slab_bytes))
    while kh % ks:
        ks -= 1
    return ks


def _hyena(plan, z, col0, conv_w, conv_b, filt, skip):
    b, seq_len, _ = z.shape
    n_orders, ch = skip.shape
    g = _block_cols(plan.n2, plan.n1 // 2)
    ks_filt = _slabs_per_step(plan.kh, 2 * plan.n2 * LANES * 4, target_bytes=6 << 20)
    kr, ki = _filt_spec(plan, filt, n_orders, ks=ks_filt)
    nb = 2 if b % 2 == 0 else 1
    fused_bytes = 15 * nb * seq_len * LANES * 4
    if n_orders == 2 and fused_bytes <= FUSED_VMEM_BUDGET:
        return _hyena_fused(plan, z, col0, conv_w, conv_b, kr, ki, skip, nb=nb)
    uc = _sconv(z, conv_w, conv_b, rows=min(seq_len, 2048), col0=col0, n_parts=n_orders + 1)
    skip3 = skip.reshape(n_orders, 1, ch)
    cur, cur_part = uc, 0
    for order in range(n_orders):
        bre, bim = _conv_front(plan, cur, cur_part, kr, ki, order, ks=ks_filt)
        cur = _dft_inv(plan, bre, bim, cur, cur_part, uc, order + 1, skip3, order, g=g)
        cur_part = 0
    return cur[0]


def _trunk(x3, p, plan, filt):
    b, seq_len, d = x3.shape
    t = b * seq_len
    x = x3.reshape(t, d)
    x, z = _ffn_in(x, p["g1pre"], p["g1post"], p["w1gu"], p["w1d"], p["gmix"], p["w_in"], tm=512)
    z = z.reshape(b, seq_len, -1)
    d_pool = p["pool_w"].shape[0] * LANES
    y_pool = _pool(z, p["pool_w"], p["pool_scale"], rows=min(seq_len, 512))
    y_hy = _hyena(plan, z, d_pool, p["conv_w"], p["conv_b"], filt, p["skip"])
    x = _ffn_out(x, y_pool.reshape(t, -1), y_hy.reshape(t, -1), p["gpool"], p["ghy"], p["w_out"],
                 p["gmixpost"], p["g2pre"], p["g2post"], p["w2gu"], p["w2d"], tm=512)
    return x.reshape(b, seq_len, d)


def kernel(x_prompt, x_sample, ffn1_norm_pre, ffn1_norm_post, ffn1_w_gate_up, ffn1_w_down, mix_norm_pre, w_in, pool_w_map, pool_scale, hyena_conv_w, hyena_conv_b, filt_w_first, filt_b_first, filt_w_hidden, filt_b_hidden, filt_w_last, filt_freq, hyena_skip, pool_out_norm, hyena_out_norm, w_out, mix_norm_post, ffn2_norm_pre, ffn2_norm_post, ffn2_w_gate_up, ffn2_w_down):
    assert ffn1_norm_pre.shape[0] == 1, "single-layer trunk"
    row = lambda a: a[0].reshape(1, -1)
    p = dict(
        g1pre=row(ffn1_norm_pre), g1post=row(ffn1_norm_post),
        w1gu=ffn1_w_gate_up[0].astype(BF16), w1d=ffn1_w_down[0].astype(BF16),
        gmix=row(mix_norm_pre), w_in=w_in[0].astype(BF16),
        pool_w=pool_w_map[0].astype(BF16), pool_scale=row(pool_scale),
        conv_w=hyena_conv_w[0], conv_b=row(hyena_conv_b),
        skip=hyena_skip[0], gpool=row(pool_out_norm), ghy=row(hyena_out_norm),
        w_out=w_out[0].astype(BF16), gmixpost=row(mix_norm_post),
        g2pre=row(ffn2_norm_pre), g2post=row(ffn2_norm_post),
        w2gu=ffn2_w_gate_up[0].astype(BF16), w2d=ffn2_w_down[0].astype(BF16),
    )
    outs = []
    for x3 in (x_prompt, x_sample):
        seq_len = x3.shape[1]
        plan = _Plan(seq_len, *_dft_shape(seq_len))
        filt = _filt_gen(seq_len, filt_w_first[0], filt_b_first[0], filt_w_hidden[0],
                         filt_b_hidden[0], filt_w_last[0], filt_freq[0], rows=min(seq_len, 512),
                         n2=plan.n2)
        outs.append(_trunk(x3, p, plan, filt))
    return tuple(outs)
```

```python
import functools
import math

import jax
import jax.numpy as jnp
import numpy as np
from jax import lax
from jax.experimental import pallas as pl
from jax.experimental.pallas import tpu as pltpu

F32 = jnp.float32
BF16 = jnp.bfloat16
EPS = 1e-6
LANES = 128
SUBLANES = 8
MXU_DIM = 256
POOL_WINDOWS = (2, 4, 8, 16)
HALO = 8
DECAY_TARGET = 1e-2
FAST_DECAY_PCT = 0.3
SLOW_DECAY_PCT = 1.5
MAX_DECAY = math.log(DECAY_TARGET) / FAST_DECAY_PCT
MIN_DECAY = math.log(DECAY_TARGET) / SLOW_DECAY_PCT
HIGHEST = lax.Precision.HIGHEST
FUSED_VMEM_BUDGET = 32 << 20


def _params(sem, vmem_mib):
    return pltpu.CompilerParams(dimension_semantics=sem, vmem_limit_bytes=vmem_mib << 20)


def _rms(x, g):
    inv = lax.rsqrt(jnp.mean(x * x, axis=-1, keepdims=True) + EPS)
    return (x * inv) * g


def _dot(a, b):
    return jnp.dot(a, b, preferred_element_type=F32)


def _dot3(a, b):
    a_hi, b_hi = a.astype(BF16), b.astype(BF16)
    a_lo = (a - a_hi.astype(F32)).astype(BF16)
    b_lo = (b - b_hi.astype(F32)).astype(BF16)
    return _dot(a_hi, b_hi) + _dot(a_hi, b_lo) + _dot(a_lo, b_hi)


def _lanes(parts):
    return parts[0] if len(parts) == 1 else jnp.concatenate(parts, axis=1)


def _lane_block(a, i):
    return a[:, i * LANES:(i + 1) * LANES]


def _ff_chunks(d_ff, n_chunks=2):
    if d_ff % MXU_DIM:
        return (0, d_ff)
    tiles = d_ff // MXU_DIM
    return tuple(MXU_DIM * ((tiles * k + n_chunks - 1) // n_chunks) for k in range(n_chunks + 1))


def _half_step_ffn(x, gpre, gpost, wgu_ref, wd_ref):
    d_ff = wd_ref.shape[0]
    h = _rms(x, gpre).astype(BF16)
    acc = None
    bounds = _ff_chunks(d_ff)
    for lo, hi in zip(bounds[:-1], bounds[1:]):
        gate = _dot(h, wgu_ref[:, lo:hi])
        up = _dot(h, wgu_ref[:, d_ff + lo:d_ff + hi])
        act = (gate * jax.nn.sigmoid(gate) * up).astype(BF16)
        part = _dot(act, wd_ref[lo:hi, :])
        acc = part if acc is None else acc + part
    return x + 0.5 * _rms(acc, gpost)


def _ffn_in_kernel(x_ref, gpre_ref, gpost_ref, wgu_ref, wd_ref, gmix_ref, win_ref, o_ref, z_ref):
    x = _half_step_ffn(x_ref[...], gpre_ref[...], gpost_ref[...], wgu_ref, wd_ref)
    o_ref[...] = x
    z_ref[...] = _dot(_rms(x, gmix_ref[...]).astype(BF16), win_ref[...])


def _ffn_in(x, g_pre, g_post, wgu, wd, g_mix, w_in, *, tm):
    t, d = x.shape
    d_ff, d_in = wd.shape[0], w_in.shape[1]
    const = lambda i: (0, 0)
    row = lambda i: (i, 0)
    once = pl.Buffered(1)
    vec = pl.BlockSpec((1, d), const)
    return pl.pallas_call(
        _ffn_in_kernel,
        grid=(t // tm,),
        in_specs=[
            pl.BlockSpec((tm, d), row), vec, vec,
            pl.BlockSpec((d, 2 * d_ff), const, pipeline_mode=once),
            pl.BlockSpec((d_ff, d), const, pipeline_mode=once),
            vec,
            pl.BlockSpec((d, d_in), const, pipeline_mode=once),
        ],
        out_specs=[pl.BlockSpec((tm, d), row), pl.BlockSpec((tm, d_in), row)],
        out_shape=[jax.ShapeDtypeStruct((t, d), F32), jax.ShapeDtypeStruct((t, d_in), F32)],
        compiler_params=_params(("parallel",), 56),
        name="ffn_in",
    )(x, g_pre, g_post, wgu, wd, g_mix, w_in)


def _ffn_out_kernel(x_ref, yp_ref, yh_ref, gp_ref, gh_ref, wout_ref, gmix_ref,
                    gpre_ref, gpost_ref, wgu_ref, wd_ref, o_ref):
    d_pool = yp_ref.shape[1]
    yp = _rms(yp_ref[...], gp_ref[...]).astype(BF16)
    yh = _rms(yh_ref[...], gh_ref[...]).astype(BF16)
    y = _dot(yp, wout_ref[:d_pool, :]) + _dot(yh, wout_ref[d_pool:, :])
    x = x_ref[...] + _rms(y, gmix_ref[...])
    o_ref[...] = _half_step_ffn(x, gpre_ref[...], gpost_ref[...], wgu_ref, wd_ref)


def _ffn_out(x, yp, yh, gp, gh, w_out, g_mix, g_pre, g_post, wgu, wd, *, tm):
    t, d = x.shape
    d_ff, dp, dh = wd.shape[0], yp.shape[1], yh.shape[1]
    const = lambda i: (0, 0)
    row = lambda i: (i, 0)
    once = pl.Buffered(1)
    vec = lambda n: pl.BlockSpec((1, n), const)
    return pl.pallas_call(
        _ffn_out_kernel,
        grid=(t // tm,),
        in_specs=[
            pl.BlockSpec((tm, d), row), pl.BlockSpec((tm, dp), row), pl.BlockSpec((tm, dh), row),
            vec(dp), vec(dh),
            pl.BlockSpec((dp + dh, d), const, pipeline_mode=once),
            vec(d), vec(d), vec(d),
            pl.BlockSpec((d, 2 * d_ff), const, pipeline_mode=once),
            pl.BlockSpec((d_ff, d), const, pipeline_mode=once),
        ],
        out_specs=pl.BlockSpec((tm, d), row),
        out_shape=jax.ShapeDtypeStruct((t, d), F32),
        compiler_params=_params(("parallel",), 56),
        name="ffn_out",
    )(x, yp, yh, gp, gh, w_out, g_mix, g_pre, g_post, wgu, wd)


def _with_halo(prev_ref, main_ref, next_ref, i, n_tiles):
    prev = jnp.where(i > 0, prev_ref[0], 0.0)
    nxt = jnp.where(i < n_tiles - 1, next_ref[0], 0.0)
    return jnp.concatenate([prev, main_ref[0], nxt], axis=0)


def _pool_kernel(prev_ref, main_ref, next_ref, wmap_ref, scale_ref, o_ref, *, seq_len, n_tiles):
    i = pl.program_id(1)
    rows = main_ref.shape[1]
    n_ext = rows + 2 * HALO
    ext_all = _with_halo(prev_ref, main_ref, next_ref, i, n_tiles)
    pos = i * rows + lax.broadcasted_iota(jnp.int32, (rows, LANES), 0)
    for grp, window in enumerate(POOL_WINDOWS):
        lanes = slice(grp * LANES, (grp + 1) * LANES)
        ext = ext_all[:, lanes]
        ssum, w = ext + pltpu.roll(ext, 1, 0), 2
        while w < window:
            ssum, w = pltpu.roll(ssum, w // 2, 0) + pltpu.roll(ssum, n_ext - w // 2, 0), 2 * w
        u = main_ref[0, :, lanes]
        lo = jnp.clip(pos - window // 2, 0, seq_len)
        hi = jnp.clip(pos + (window - window // 2), 0, seq_len)
        d = ssum[HALO:HALO + rows] / (hi - lo).astype(F32) - u
        y = _dot(d.astype(BF16), wmap_ref[grp])
        o_ref[0, :, lanes] = y * scale_ref[:, lanes]


def _halo_specs(rows, width, seq_len, col_of):
    blocks_per_tile = rows // HALO
    last = seq_len // HALO - 1
    prev = pl.BlockSpec((1, HALO, width),
                        lambda b, i, c: (b, jnp.maximum(i * blocks_per_tile - 1, 0), col_of(c)))
    main = pl.BlockSpec((1, rows, width), lambda b, i, c: (b, i, col_of(c)))
    nxt = pl.BlockSpec((1, HALO, width),
                       lambda b, i, c: (b, jnp.minimum((i + 1) * blocks_per_tile, last), col_of(c)))
    return [prev, main, nxt]


def _pool(z, wmap, scale, *, rows):
    b, seq_len, _ = z.shape
    assert wmap.shape[0] == len(POOL_WINDOWS) and max(POOL_WINDOWS) <= 2 * HALO
    width = wmap.shape[0] * LANES
    n_tiles = seq_len // rows
    return pl.pallas_call(
        functools.partial(_pool_kernel, seq_len=seq_len, n_tiles=n_tiles),
        grid=(b, n_tiles, 1),
        in_specs=_halo_specs(rows, width, seq_len, lambda c: 0) + [
            pl.BlockSpec(wmap.shape, lambda b, i, c: (0, 0, 0)),
            pl.BlockSpec((1, width), lambda b, i, c: (0, 0)),
        ],
        out_specs=pl.BlockSpec((1, rows, width), lambda b, i, c: (b, i, 0)),
        out_shape=jax.ShapeDtypeStruct((b, seq_len, width), F32),
        compiler_params=_params(("parallel", "parallel", "parallel"), 40),
        name="pool",
    )(z, z, z, wmap, scale)


def _sconv_kernel(prev_ref, main_ref, next_ref, w_ref, b_ref, o_ref, *, n_tiles):
    i = pl.program_id(1)
    rows = main_ref.shape[1]
    ext = _with_halo(prev_ref, main_ref, next_ref, i, n_tiles)
    n_ext = rows + 2 * HALO
    before = pltpu.roll(ext, 1, 0)[HALO:HALO + rows]
    after = pltpu.roll(ext, n_ext - 1, 0)[HALO:HALO + rows]
    w = w_ref[...]
    out = before * w[0:1] + main_ref[0] * w[1:2] + after * w[2:3]
    o_ref[0, 0] = out + b_ref[...]


def _sconv(z, w, bias, *, rows, col0, n_parts):
    b, seq_len, _ = z.shape
    n_tiles = seq_len // rows
    width = w.shape[1] // n_parts
    assert col0 % width == 0
    return pl.pallas_call(
        functools.partial(_sconv_kernel, n_tiles=n_tiles),
        grid=(b, n_tiles, n_parts),
        in_specs=_halo_specs(rows, width, seq_len, lambda c: c + col0 // width) + [
            pl.BlockSpec((w.shape[0], width), lambda b, i, c: (0, c)),
            pl.BlockSpec((1, width), lambda b, i, c: (0, c)),
        ],
        out_specs=pl.BlockSpec((1, 1, rows, width), lambda b, i, c: (c, b, i, 0)),
        out_shape=jax.ShapeDtypeStruct((n_parts, b, seq_len, width), F32),
        compiler_params=_params(("parallel", "parallel", "parallel"), 40),
        name="sconv",
    )(z, z, z, w, bias)


def _filt_gen_kernel(bands_ref, wt_ref, wc_ref, ws_ref, b1_ref, wh_ref, bh_ref, freq_ref,
                     wl_ref, delta_ref, o_ref, *, seq_len, d_ch):
    rows = o_ref.shape[0] * (o_ref.shape[1] - SUBLANES)
    base = pl.program_id(0) * rows
    m_lane = (base + lax.broadcasted_iota(jnp.int32, (1, rows), 1)).astype(F32)
    t_lane = m_lane / (seq_len - 1.0)
    ang = (bands_ref[...] * (2.0 * math.pi / seq_len)) * m_lane
    freq = freq_ref[...]
    pre = (jnp.dot(wc_ref[...], jnp.cos(ang), precision=HIGHEST, preferred_element_type=F32)
           + jnp.dot(ws_ref[...], -jnp.sin(ang), precision=HIGHEST, preferred_element_type=F32)
           + wt_ref[...] * t_lane + b1_ref[...])
    h = jnp.sin(freq * pre)
    for layer in range(wh_ref.shape[0]):
        pre = jnp.dot(wh_ref[layer], h, precision=HIGHEST, preferred_element_type=F32) + bh_ref[layer]
        h = jnp.sin(freq * pre)
    out = _dot3(h.T, wl_ref[...])
    m_row = base + lax.broadcasted_iota(jnp.int32, (rows, d_ch), 0)
    t_row = m_row.astype(F32) / (seq_len - 1.0)
    decay = jnp.exp(-t_row * jnp.abs(delta_ref[...]))
    decay_bwd = jnp.where(m_row == 0, 0.0, decay)
    n_chunks, pitch, _ = o_ref.shape
    n2 = rows // n_chunks
    for q in range(out.shape[1] // d_ch):
        dq = decay_bwd if q % 2 == 1 else decay
        val = out[:, q * d_ch:(q + 1) * d_ch] * dq
        for n in range(n_chunks):
            o_ref[n, :n2, q * d_ch:(q + 1) * d_ch] = val[n * n2:(n + 1) * n2]
    o_ref[:, n2:, :] = jnp.zeros((n_chunks, pitch - n2, o_ref.shape[2]), F32)


def _filt_gen(seq_len, w_first, b_first, w_hidden, b_hidden, w_last, freq, *, rows, n2):
    pos_bands = (w_first.shape[0] - 1) // 2
    hidden = w_first.shape[1]
    n_cols = w_last.shape[1]
    d_ch = n_cols // 4
    bands = jnp.linspace(1e-4, pos_bands - 1, pos_bands, dtype=F32).reshape(pos_bands, 1)
    deltas = jnp.linspace(MIN_DECAY, MAX_DECAY, d_ch, dtype=F32).reshape(1, d_ch)
    w1t = w_first.T
    args = (bands, w1t[:, 0:1], w1t[:, 1:1 + pos_bands], w1t[:, 1 + pos_bands:],
            b_first.reshape(hidden, 1), jnp.swapaxes(w_hidden, 1, 2),
            b_hidden.reshape(b_hidden.shape[0], hidden, 1), freq.reshape(hidden, 1),
            w_last, deltas)
    full = lambda a: pl.BlockSpec(a.shape, lambda i, _n=a.ndim: (0,) * _n)
    return pl.pallas_call(
        functools.partial(_filt_gen_kernel, seq_len=seq_len, d_ch=d_ch),
        grid=(seq_len // rows,),
        in_specs=[full(a) for a in args],
        out_specs=pl.BlockSpec((rows // n2, n2 + SUBLANES, n_cols), lambda i: (i, 0, 0)),
        out_shape=jax.ShapeDtypeStruct((seq_len // n2, n2 + SUBLANES, n_cols), F32),
        compiler_params=_params(("parallel",), 40),
        name="filt_gen",
    )(*args)


class _Plan:
    def __init__(self, seq_len, n1, n2):
        assert n1 * n2 == 2 * seq_len and n1 % (2 * SUBLANES) == 0 and n2 % SUBLANES == 0
        self.seq_len, self.n1, self.n2 = seq_len, n1, n2
        n = n1 * n2
        n1h = n1 // 2
        self.kh = kh = n1h + 1
        k1 = np.arange(kh, dtype=np.float64)[:, None]
        m1 = np.arange(n1h, dtype=np.float64)[None, :]
        th1 = 2.0 * np.pi * k1 * m1 / n1
        f1 = np.zeros((n1 + SUBLANES, n1h))
        f1[:n1h] = np.cos(th1[:n1h]); f1[n1h:n1] = -np.sin(th1[:n1h]); f1[n1] = np.cos(th1[n1h])
        self.f1 = jnp.asarray(f1, F32)
        m2 = np.arange(n2, dtype=np.float64)[None, :]
        tht = 2.0 * np.pi * k1 * m2 / n
        twr, twi = np.cos(tht), -np.sin(tht)
        rep = lambda a: np.repeat(a[:, :, None], LANES, axis=2)
        self.twr_inv, self.twi_inv = jnp.asarray(rep(twr), F32), jnp.asarray(rep(twi), F32)
        pad = np.zeros((n2, SUBLANES - 1))
        self.twr_fwd = jnp.asarray(rep(np.concatenate([twr.T, pad], 1)), F32)
        self.twi_fwd = jnp.asarray(rep(np.concatenate([twi.T, pad], 1)), F32)
        k2 = np.arange(n2, dtype=np.float64)[:, None]
        th2 = 2.0 * np.pi * k2 * m2 / n2
        f2r, f2i = np.cos(th2), -np.sin(th2)
        self.f2 = jnp.asarray(np.block([[f2r, -f2i], [f2i, f2r]]), F32)
        self.f2inv = jnp.asarray(np.block([[f2r, f2i], [-f2i, f2r]]), F32)
        wgt = np.where(k1[:n1h] == 0, 1.0, 2.0) / n
        ginv = np.concatenate([(wgt * np.cos(th1[:n1h])).T, (-wgt * np.sin(th1[:n1h])).T], axis=1)
        self.ginv = jnp.asarray(ginv, F32)
        k1f = np.arange(kh, dtype=np.float64)[:, None]
        full = lambda cols: 2.0 * np.pi * k1f * cols[None, :] / n1
        stage1 = lambda th: np.concatenate(
            [np.cos(th[:n1h]), -np.sin(th[:n1h]), np.cos(th[n1h:]), np.zeros((SUBLANES - 1, th.shape[1]))], 0)
        fwd_cols = np.arange(n1h, dtype=np.float64)
        f1a = stage1(full(np.concatenate([fwd_cols, n1 - 1 - fwd_cols])))
        f1b = stage1(full(np.concatenate([fwd_cols, (n1 - fwd_cols) % n1])))
        f1b[:, n1h] = 0.0
        self.f1_filt, self.f1_filt0 = jnp.asarray(f1a, F32), jnp.asarray(f1b, F32)
        rows = lambda t: rep(np.concatenate([t.T, np.zeros((t.shape[1], SUBLANES - 1))], 1))
        coarse, fine = tht[:, ::SUBLANES], tht[:, :SUBLANES]
        self.tw_coarse = (jnp.asarray(rows(np.cos(coarse)), F32), jnp.asarray(rows(-np.sin(coarse)), F32))
        self.tw_fine = (jnp.asarray(rows(np.cos(fine)), F32), jnp.asarray(rows(-np.sin(fine)), F32))
        self.tw_slab_coarse = (jnp.asarray(rep(np.cos(coarse)), F32), jnp.asarray(rep(-np.sin(coarse)), F32))
        self.tw_slab_fine = (jnp.asarray(rep(np.cos(fine)), F32), jnp.asarray(rep(-np.sin(fine)), F32))


def _stack_bf16(re, im):
    return jnp.concatenate([re.astype(BF16), im.astype(BF16)], axis=0)


def _filt_spec_kernel(hf_ref, hba_ref, hbb_ref, f1_ref, f1j0_ref, twcr_ref, twci_ref, twfr_ref, twfi_ref, f2_ref,
                      kr_ref, ki_ref, are_ref, aim_ref, *, n1h, n2, jg):
    pitch = n2 + SUBLANES
    n_j = n2 // jg
    ks = kr_ref.shape[0]
    phase = pl.program_id(1)
    hf, hba, hbb = (r.reshape(n1h * jg, LANES) for r in (hf_ref, hba_ref, hbb_ref))

    @pl.when(phase < n_j)
    def _():
        f1 = f1_ref[...].astype(BF16)
        f1_first = jnp.where(phase == 0, f1j0_ref[...], f1_ref[...]).astype(BF16)
        j0 = phase * jg
        a0 = lax.shift_right_logical(j0, 3)

        def stage1(mat, ts):
            rows = lambda ref, start: ref[pl.ds(start, n1h, stride=jg), :]
            x = _lanes([jnp.concatenate([rows(hf, t), rows(hbb, 0) if t == 0 else rows(hba, jg - t)], axis=0)
                        for t in ts])
            p = _dot(mat, x.astype(BF16))
            cr = _lanes([twcr_ref[a0 + t // SUBLANES] for t in ts])
            ci = _lanes([twci_ref[a0 + t // SUBLANES] for t in ts])
            fr = _lanes([twfr_ref[t % SUBLANES] for t in ts])
            fi = _lanes([twfi_ref[t % SUBLANES] for t in ts])
            twr, twi = cr * fr - ci * fi, cr * fi + ci * fr
            pr, pi, pn = p[:n1h], p[n1h:2 * n1h], p[2 * n1h:2 * n1h + 1]
            re, im = pr * twr[:n1h] - pi * twi[:n1h], pr * twi[:n1h] + pi * twr[:n1h]
            re_n, im_n = pn * twr[n1h:n1h + 1], pn * twi[n1h:n1h + 1]
            for i, t in enumerate(ts):
                are_ref[pl.ds(j0 + t, n1h, stride=pitch), :] = _lane_block(re, i)
                aim_ref[pl.ds(j0 + t, n1h, stride=pitch), :] = _lane_block(im, i)
                are_ref[pl.ds(n1h * pitch + j0 + t, 1), :] = _lane_block(re_n, i)
                aim_ref[pl.ds(n1h * pitch + j0 + t, 1), :] = _lane_block(im_n, i)

        stage1(f1_first, (0,))
        for t in range(1, jg - 1, 2):
            stage1(f1, (t, t + 1))
        stage1(f1, (jg - 1,))

    @pl.when(phase >= n_j)
    def _():
        f2 = f2_ref[...].astype(BF16)
        kc = phase - n_j

        def slabs(first, count):
            rows = [pl.ds(pl.multiple_of((kc * ks + first + i) * pitch, SUBLANES), n2) for i in range(count)]
            x = _dot(f2, _stack_bf16(_lanes([are_ref[r, :] for r in rows]), _lanes([aim_ref[r, :] for r in rows])))
            for i in range(count):
                kr_ref[first + i] = _lane_block(x[:n2], i).astype(kr_ref.dtype)
                ki_ref[first + i] = _lane_block(x[n2:], i).astype(ki_ref.dtype)

        def pair(p, carry):
            slabs(2 * p, 2)
            return carry

        lax.fori_loop(0, ks // 2, pair, 0, unroll=4)
        if ks % 2:
            slabs(ks - 1, 1)


def _filt_spec(plan, filt, n_orders, *, ks, jg=16):
    n1h, n2, kh = plan.n1 // 2, plan.n2, plan.kh
    cols = filt.shape[2]
    n_cb = cols // (2 * n_orders * LANES)
    n_j = n2 // jg
    pitch = n2 + SUBLANES
    col = lambda c, back: (c // n_cb) * 2 * n_cb + back * n_cb + c % n_cb
    step = lambda s: jnp.minimum(s, n_j - 1)
    seq = lambda back, blk: pl.BlockSpec((n1h, jg, LANES), lambda c, s: (0, blk(step(s)), col(c, back)))
    full = lambda a: pl.BlockSpec(a.shape, lambda c, s, _n=a.ndim: (0,) * _n, pipeline_mode=pl.Buffered(1))
    tables = (plan.f1_filt, plan.f1_filt0, *plan.tw_coarse, *plan.tw_fine, plan.f2)
    out = jax.ShapeDtypeStruct((kh, n2, cols // 2), BF16)
    scratch = pltpu.VMEM((kh * pitch, LANES), F32)
    return pl.pallas_call(
        functools.partial(_filt_spec_kernel, n1h=n1h, n2=n2, jg=jg),
        grid=(n_orders * n_cb, n_j + kh // ks),
        in_specs=[seq(0, lambda s: s), seq(1, lambda s: n_j - 1 - s), seq(1, lambda s: (n_j - s) % n_j)]
        + [full(a) for a in tables],
        out_specs=[pl.BlockSpec((ks, n2, LANES), lambda c, s: (jnp.maximum(s - n_j, 0), 0, c))] * 2,
        out_shape=[out, out],
        scratch_shapes=[scratch, scratch],
        compiler_params=_params(("parallel", "arbitrary"), 48),
        name="filt_spec",
    )(filt, filt, filt, *tables)


def _conv_front_kernel(x_ref, kr_ref, ki_ref, f1_ref, twcr_ref, twci_ref, twfr_ref, twfi_ref,
                       icr_ref, ici_ref, ifr_ref, ifi_ref, f2_ref, f2inv_ref,
                       ore_ref, oim_ref, are_ref, aim_ref, *, n1h, n2, unroll):
    pitch = n2 + SUBLANES
    ks = kr_ref.shape[0]
    kc = pl.program_id(2)
    x2 = x_ref.at[0, 0]

    @pl.when(kc == 0)
    def _():
        f1 = f1_ref[...].astype(BF16)

        def body(jp, carry):
            js = (2 * jp, 2 * jp + 1)
            a = lax.shift_right_logical(jp, 2)
            bs = [2 * jnp.bitwise_and(jp, SUBLANES // 2 - 1) + q for q in range(2)]
            xb = _lanes([x2[pl.ds(j, n1h, stride=n2), :] for j in js]).astype(BF16)
            p = _dot(f1, xb)
            cr, ci = jnp.tile(twcr_ref[a], (1, 2)), jnp.tile(twci_ref[a], (1, 2))
            fr, fi = _lanes([twfr_ref[b] for b in bs]), _lanes([twfi_ref[b] for b in bs])
            twr, twi = cr * fr - ci * fi, cr * fi + ci * fr
            pr, pi, pn = p[:n1h], p[n1h:2 * n1h], p[2 * n1h:2 * n1h + 1]
            re, im = pr * twr[:n1h] - pi * twi[:n1h], pr * twi[:n1h] + pi * twr[:n1h]
            re_n, im_n = pn * twr[n1h:n1h + 1], pn * twi[n1h:n1h + 1]
            for i, j in enumerate(js):
                are_ref[pl.ds(j, n1h, stride=pitch), :] = _lane_block(re, i)
                aim_ref[pl.ds(j, n1h, stride=pitch), :] = _lane_block(im, i)
                are_ref[pl.ds(n1h * pitch + j, 1), :] = _lane_block(re_n, i)
                aim_ref[pl.ds(n1h * pitch + j, 1), :] = _lane_block(im_n, i)
            return carry

        lax.fori_loop(0, n2 // 2, body, 0, unroll=unroll)

    f2, f2inv = f2_ref[...].astype(BF16), f2inv_ref[...].astype(BF16)

    def slab_twiddle(k):
        cr, ci, fr, fi = icr_ref[k], ici_ref[k], ifr_ref[k], ifi_ref[k]
        pieces = [(cr[a:a + 1] * fr - ci[a:a + 1] * fi, cr[a:a + 1] * fi + ci[a:a + 1] * fr)
                  for a in range(n2 // SUBLANES)]
        return (jnp.concatenate([p[0] for p in pieces], axis=0), jnp.concatenate([p[1] for p in pieces], axis=0))

    def slabs(first, count):
        ks_abs = [kc * ks + first + i for i in range(count)]
        rows = [pl.ds(pl.multiple_of(k * pitch, SUBLANES), n2) for k in ks_abs]
        x = _dot(f2, _stack_bf16(_lanes([are_ref[r, :] for r in rows]), _lanes([aim_ref[r, :] for r in rows])))
        xr, xi = x[:n2], x[n2:]
        kr = _lanes([kr_ref[first + i] for i in range(count)]).astype(F32)
        ki = _lanes([ki_ref[first + i] for i in range(count)]).astype(F32)
        y = _dot(f2inv, _stack_bf16(xr * kr - xi * ki, xr * ki + xi * kr))
        br, bi = y[:n2], y[n2:]
        tw = [slab_twiddle(k) for k in ks_abs]
        twr, twi = _lanes([t[0] for t in tw]), _lanes([t[1] for t in tw])
        ore, oim = br * twr + bi * twi, bi * twr - br * twi
        for i in range(count):
            ore_ref[0, first + i] = _lane_block(ore, i)
            oim_ref[0, first + i] = _lane_block(oim, i)

    def pair(p, carry):
        slabs(2 * p, 2)
        return carry

    lax.fori_loop(0, ks // 2, pair, 0, unroll=4)
    if ks % 2:
        slabs(ks - 1, 1)


def _conv_front(plan, x4, part, kr, ki, order, *, ks):
    _, b, seq_len, ch = x4.shape
    n1h, n2, kh = plan.n1 // 2, plan.n2, plan.kh
    n_cb = ch // LANES
    pitch = n2 + SUBLANES
    once = pl.Buffered(1)
    full = lambda a: pl.BlockSpec(a.shape, lambda bb, c, k, _n=a.ndim: (0,) * _n, pipeline_mode=once)
    filt = pl.BlockSpec((ks, n2, LANES), lambda bb, c, k: (k, 0, order * n_cb + c))
    data = pl.BlockSpec((1, ks, n2, LANES), lambda bb, c, k: (bb, k, 0, c))
    tables = (plan.f1, *plan.tw_coarse, *plan.tw_fine, *plan.tw_slab_coarse, *plan.tw_slab_fine,
              plan.f2, plan.f2inv)
    out = jax.ShapeDtypeStruct((b, kh, n2, ch), F32)
    scratch = pltpu.VMEM((kh * pitch, LANES), F32)
    return pl.pallas_call(
        functools.partial(_conv_front_kernel, n1h=n1h, n2=n2, unroll=4),
        grid=(b, n_cb, kh // ks),
        in_specs=[pl.BlockSpec((1, 1, seq_len, LANES), lambda bb, c, k: (part, bb, 0, c), pipeline_mode=once),
                  filt, filt] + [full(a) for a in tables],
        out_specs=[data, data],
        out_shape=[out, out],
        scratch_shapes=[scratch, scratch],
        compiler_params=_params(("parallel", "parallel", "arbitrary"), 56),
        name="conv_front",
    )(x4, kr, ki, *tables)


def _dft_inv_kernel(bre_ref, bim_ref, x_ref, gate_ref, skip_ref, ginv_ref, o_ref, *, unroll, scale):
    _, _, n1h, g, _ = o_ref.shape
    b_re = bre_ref.at[0].reshape((n1h + 1) * g, LANES)
    b_im = bim_ref.at[0].reshape((n1h + 1) * g, LANES)
    x2 = x_ref.at[0, 0].reshape(n1h * g, LANES)
    gate2 = gate_ref.at[0, 0].reshape(n1h * g, LANES)
    o2 = o_ref.at[0, 0].reshape(n1h * g, LANES)
    ginv = ginv_ref[...].astype(BF16)
    skip = skip_ref[0]
    n1 = lax.broadcasted_iota(jnp.int32, (n1h, 2 * LANES), 0)
    nyq_w = jnp.where(jnp.bitwise_and(n1, 1) == 0, scale, -scale)
    skip = jnp.tile(skip, (1, 2))

    def body(jp, carry):
        rows = [pl.ds(2 * jp + i, n1h, stride=g) for i in range(2)]
        spec = _stack_bf16(_lanes([b_re[r, :] for r in rows]), _lanes([b_im[r, :] for r in rows]))
        nyq = _lanes([b_re[pl.ds(n1h * g + 2 * jp + i, 1), :] for i in range(2)])
        y = _dot(ginv, spec) + nyq_w * nyq
        out = _lanes([gate2[r, :] for r in rows]) * (y + _lanes([x2[r, :] for r in rows]) * skip)
        for i, r in enumerate(rows):
            o2[r, :] = _lane_block(out, i)
        return carry

    lax.fori_loop(0, g // 2, body, 0, unroll=unroll)


def _dft_inv(plan, bre, bim, x4, x_part, gate4, gate_part, skip, order, *, g):
    b, seq_len, ch = x4.shape[1:]
    n1h, n2, kh = plan.n1 // 2, plan.n2, plan.kh
    xv = x4.reshape(x4.shape[0], b, n1h, n2, ch)
    gv = gate4.reshape(gate4.shape[0], b, n1h, n2, ch)
    spec = pl.BlockSpec((1, kh, g, LANES), lambda j, bb, c: (bb, 0, j, c))
    time = lambda part: pl.BlockSpec((1, 1, n1h, g, LANES), lambda j, bb, c: (part, bb, 0, j, c))
    out = pl.pallas_call(
        functools.partial(_dft_inv_kernel, unroll=min(g // 2, 8), scale=1.0 / (plan.n1 * n2)),
        grid=(n2 // g, b, ch // LANES),
        in_specs=[
            spec, spec, time(x_part), time(gate_part),
            pl.BlockSpec((1, 1, LANES), lambda j, bb, c: (order, 0, c)),
            pl.BlockSpec(plan.ginv.shape, lambda j, bb, c: (0, 0)),
        ],
        out_specs=time(0),
        out_shape=jax.ShapeDtypeStruct((1, b, n1h, n2, ch), F32),
        compiler_params=_params(("parallel", "parallel", "parallel"), 48),
        name="dft_inv",
    )(bre, bim, xv, gv, skip, plan.ginv)
    return out.reshape(1, b, seq_len, ch)


def _hyena_fused_kernel(zv_ref, zx1_ref, zx2_ref, wv_ref, wx1_ref, wx2_ref, bv_ref, bx1_ref, bx2_ref,
                        skip_ref, kr0_ref, ki0_ref, kr1_ref, ki1_ref,
                        f1_ref, twrf_ref, twif_ref, twri_ref, twii_ref, f2_ref, f2inv_ref, ginv_ref,
                        o_ref, are_ref, aim_ref, cur_ref, tin_ref, *, n1h, n2, unroll):
    nb = o_ref.shape[0]
    pitch = n2 + SUBLANES
    f1 = f1_ref[...].astype(BF16)
    f2, f2inv = f2_ref[...].astype(BF16), f2inv_ref[...].astype(BF16)
    ginv = ginv_ref[...].astype(BF16)
    scale = 1.0 / (2 * n1h * n2)
    n1 = lax.broadcasted_iota(jnp.int32, (n1h, 2 * nb * LANES), 0)
    nyq_w = jnp.where(jnp.bitwise_and(n1, 1) == 0, scale, -scale)
    lanes, lane_block = _lanes, _lane_block
    wide = lambda t, reps: t if reps == 1 else jnp.tile(t, (1, reps))

    row = lax.broadcasted_iota(jnp.int32, (n2, LANES), 0)
    for part, (z_ref, w_ref, b_ref) in enumerate(((zv_ref, wv_ref, bv_ref), (zx1_ref, wx1_ref, bx1_ref),
                                                   (zx2_ref, wx2_ref, bx2_ref))):
        w, bias = w_ref[...], b_ref[...]
        for i in range(nb):
            for n in range(n1h):
                lo = n * n2
                mid = z_ref[i, lo:lo + n2, :]
                if n > 0:
                    before = z_ref[i, lo - 1:lo + n2 - 1, :]
                else:
                    before = jnp.where(row == 0, 0.0, pltpu.roll(mid, 1, 0))
                if n < n1h - 1:
                    after = z_ref[i, lo + 1:lo + n2 + 1, :]
                else:
                    after = jnp.where(row == n2 - 1, 0.0, pltpu.roll(mid, n2 - 1, 0))
                tin_ref[part, i, n * pitch:n * pitch + n2, :] = (
                    before * w[0:1] + mid * w[1:2] + after * w[2:3] + bias)
    stages = ((tin_ref.at[0], tin_ref.at[1], cur_ref, kr0_ref, ki0_ref),
              (cur_ref, tin_ref.at[2], tin_ref.at[0], kr1_ref, ki1_ref))
    for order, (src, gate, dst, kr_ref, ki_ref) in enumerate(stages):
        skip = wide(skip_ref[order], 2 * nb)

        def stage1(jp, carry):
            cols = [(2 * jp + q, i) for q in range(2) for i in range(nb)]
            xb = lanes([src[i, pl.ds(j, n1h, stride=pitch), :] for j, i in cols]).astype(BF16)
            p = _dot(f1, xb)
            twr = lanes([wide(twrf_ref[2 * jp + q], nb) for q in range(2)])
            twi = lanes([wide(twif_ref[2 * jp + q], nb) for q in range(2)])
            pr, pi, pn = p[:n1h], p[n1h:2 * n1h], p[2 * n1h:2 * n1h + 1]
            re, im = pr * twr[:n1h] - pi * twi[:n1h], pr * twi[:n1h] + pi * twr[:n1h]
            re_n, im_n = pn * twr[n1h:n1h + 1], pn * twi[n1h:n1h + 1]
            for c, (j, i) in enumerate(cols):
                are_ref[i, pl.ds(j, n1h, stride=pitch), :] = lane_block(re, c)
                aim_ref[i, pl.ds(j, n1h, stride=pitch), :] = lane_block(im, c)
                are_ref[i, pl.ds(n1h * pitch + j, 1), :] = lane_block(re_n, c)
                aim_ref[i, pl.ds(n1h * pitch + j, 1), :] = lane_block(im_n, c)
            return carry

        lax.fori_loop(0, n2 // 2, stage1, 0, unroll=unroll)

        def slabs(ks):
            rows = [pl.ds(pl.multiple_of(k * pitch, SUBLANES), n2) for k in ks]
            a = _stack_bf16(lanes([are_ref[i, r, :] for r in rows for i in range(nb)]),
                            lanes([aim_ref[i, r, :] for r in rows for i in range(nb)]))
            x = _dot(f2, a)
            xr, xi = x[:n2], x[n2:]
            per_slab = lambda ref: lanes([wide(ref[k], nb) for k in ks])
            kr, ki = per_slab(kr_ref).astype(F32), per_slab(ki_ref).astype(F32)
            y = _dot(f2inv, _stack_bf16(xr * kr - xi * ki, xr * ki + xi * kr))
            br, bi = y[:n2], y[n2:]
            twr, twi = per_slab(twri_ref), per_slab(twii_ref)
            ore, oim = br * twr + bi * twi, bi * twr - br * twi
            for s, r in enumerate(rows):
                for i in range(nb):
                    are_ref[i, r, :] = lane_block(ore, s * nb + i)
                    aim_ref[i, r, :] = lane_block(oim, s * nb + i)

        def slab(k, carry):
            slabs((k,))
            return carry

        lax.fori_loop(0, n1h + 1, slab, 0, unroll=8)

        def stage1_inv(jp, carry):
            cols = [(pl.ds(2 * jp + q, n1h, stride=pitch), 2 * jp + q, i) for q in range(2) for i in range(nb)]
            spec = _stack_bf16(lanes([are_ref[i, r, :] for r, _, i in cols]),
                               lanes([aim_ref[i, r, :] for r, _, i in cols]))
            nyq = lanes([are_ref[i, pl.ds(n1h * pitch + j, 1), :] for _, j, i in cols])
            y = _dot(ginv, spec) + nyq_w * nyq
            x = lanes([src[i, r, :] for r, _, i in cols])
            g = lanes([gate[i, r, :] for r, _, i in cols])
            out = g * (y + x * skip)
            for c, (r, _, i) in enumerate(cols):
                dst[i, r, :] = lane_block(out, c)
            return carry

        lax.fori_loop(0, n2 // 2, stage1_inv, 0, unroll=unroll)

    def unpitch(n, carry):
        for i in range(nb):
            o_ref[i, pl.ds(pl.multiple_of(n * n2, n2), n2), :] = (
                tin_ref[0, i, pl.ds(pl.multiple_of(n * pitch, SUBLANES), n2), :])
        return carry

    lax.fori_loop(0, n1h, unpitch, 0)


def _hyena_fused(plan, z, col0, conv_w, conv_b, kr, ki, skip, *, nb):
    b, seq_len, _ = z.shape
    ch = skip.shape[1]
    n1h, n2, kh = plan.n1 // 2, plan.n2, plan.kh
    n_cb = ch // LANES
    cb0 = col0 // LANES
    part = lambda p: pl.BlockSpec((nb, seq_len, LANES), lambda c, bb: (bb, 0, cb0 + p * n_cb + c))
    taps = lambda p: pl.BlockSpec((conv_w.shape[0], LANES), lambda c, bb: (0, p * n_cb + c))
    bias = lambda p: pl.BlockSpec((1, LANES), lambda c, bb: (0, p * n_cb + c))
    once = pl.Buffered(1)
    filt = lambda o: pl.BlockSpec((kh, n2, LANES), lambda c, bb: (0, 0, o * n_cb + c), pipeline_mode=once)
    full = lambda a: pl.BlockSpec(a.shape, lambda c, bb, _n=a.ndim: (0,) * _n, pipeline_mode=once)
    tables = (plan.f1, plan.twr_fwd, plan.twi_fwd, plan.twr_inv, plan.twi_inv, plan.f2, plan.f2inv, plan.ginv)
    pitch = n2 + SUBLANES
    spec_scratch = pltpu.VMEM((nb, kh * pitch, LANES), F32)
    return pl.pallas_call(
        functools.partial(_hyena_fused_kernel, n1h=n1h, n2=n2, unroll=min(n2 // 2, 16)),
        grid=(n_cb, b // nb),
        in_specs=[part(0), part(1), part(2), taps(0), taps(1), taps(2), bias(0), bias(1), bias(2),
                  pl.BlockSpec((skip.shape[0], 1, LANES), lambda c, bb: (0, 0, c)),
                  filt(0), filt(0), filt(1), filt(1)] + [full(a) for a in tables],
        out_specs=pl.BlockSpec((nb, seq_len, LANES), lambda c, bb: (bb, 0, c)),
        out_shape=jax.ShapeDtypeStruct((b, seq_len, ch), F32),
        scratch_shapes=[spec_scratch, spec_scratch, pltpu.VMEM((nb, n1h * pitch, LANES), F32),
                        pltpu.VMEM((3, nb, n1h * pitch, LANES), F32)],
        compiler_params=_params(("parallel", "arbitrary"), 56),
        name="hyena_fused",
    )(z, z, z, conv_w, conv_w, conv_w, conv_b, conv_b, conv_b,
      skip.reshape(skip.shape[0], 1, ch), kr, ki, kr, ki, *tables)


def _dft_shape(seq_len):
    n2 = 128
    return 2 * seq_len // n2, n2


def _block_cols(n2, rows, target_bytes=1 << 20):
    g = max(SUBLANES, min(n2, target_bytes // (rows * LANES * 4)) // SUBLANES * SUBLANES)
    while n2 % g:
        g -= SUBLANES
    return g


def _slabs_per_step(kh, slab_bytes, target_bytes=3 << 20):
    ks = max(1, min(kh, target_bytes // slab_bytes))
    while kh % ks:
        ks -= 1
    return ks


def _hyena(plan, z, col0, conv_w, conv_b, filt, skip):
    b, seq_len, _ = z.shape
    n_orders, ch = skip.shape
    g = _block_cols(plan.n2, plan.n1 // 2)
    ks_filt = _slabs_per_step(plan.kh, 2 * plan.n2 * LANES * 4, target_bytes=6 << 20)
    kr, ki = _filt_spec(plan, filt, n_orders, ks=ks_filt)
    nb = 2 if b % 2 == 0 else 1
    fused_bytes = 15 * nb * seq_len * LANES * 4
    if n_orders == 2 and fused_bytes <= FUSED_VMEM_BUDGET:
        return _hyena_fused(plan, z, col0, conv_w, conv_b, kr, ki, skip, nb=nb)
    uc = _sconv(z, conv_w, conv_b, rows=min(seq_len, 2048), col0=col0, n_parts=n_orders + 1)
    skip3 = skip.reshape(n_orders, 1, ch)
    cur, cur_part = uc, 0
    for order in range(n_orders):
        bre, bim = _conv_front(plan, cur, cur_part, kr, ki, order, ks=ks_filt)
        cur = _dft_inv(plan, bre, bim, cur, cur_part, uc, order + 1, skip3, order, g=g)
        cur_part = 0
    return cur[0]


def _trunk(x3, p, plan, filt):
    b, seq_len, d = x3.shape
    t = b * seq_len
    x = x3.reshape(t, d)
    x, z = _ffn_in(x, p["g1pre"], p["g1post"], p["w1gu"], p["w1d"], p["gmix"], p["w_in"], tm=512)
    z = z.reshape(b, seq_len, -1)
    d_pool = p["pool_w"].shape[0] * LANES
    y_pool = _pool(z, p["pool_w"], p["pool_scale"], rows=min(seq_len, 2048))
    y_hy = _hyena(plan, z, d_pool, p["conv_w"], p["conv_b"], filt, p["skip"])
    x = _ffn_out(x, y_pool.reshape(t, -1), y_hy.reshape(t, -1), p["gpool"], p["ghy"], p["w_out"],
                 p["gmixpost"], p["g2pre"], p["g2post"], p["w2gu"], p["w2d"], tm=512)
    return x.reshape(b, seq_len, d)


def kernel(x_prompt, x_sample, ffn1_norm_pre, ffn1_norm_post, ffn1_w_gate_up, ffn1_w_down, mix_norm_pre, w_in, pool_w_map, pool_scale, hyena_conv_w, hyena_conv_b, filt_w_first, filt_b_first, filt_w_hidden, filt_b_hidden, filt_w_last, filt_freq, hyena_skip, pool_out_norm, hyena_out_norm, w_out, mix_norm_post, ffn2_norm_pre, ffn2_norm_post, ffn2_w_gate_up, ffn2_w_down):
    assert ffn1_norm_pre.shape[0] == 1, "single-layer trunk"
    row = lambda a: a[0].reshape(1, -1)
    p = dict(
        g1pre=row(ffn1_norm_pre), g1post=row(ffn1_norm_post),
        w1gu=ffn1_w_gate_up[0].astype(BF16), w1d=ffn1_w_down[0].astype(BF16),
        gmix=row(mix_norm_pre), w_in=w_in[0].astype(BF16),
        pool_w=pool_w_map[0].astype(BF16), pool_scale=row(pool_scale),
        conv_w=hyena_conv_w[0], conv_b=row(hyena_conv_b),
        skip=hyena_skip[0], gpool=row(pool_out_norm), ghy=row(hyena_out_norm),
        w_out=w_out[0].astype(BF16), gmixpost=row(mix_norm_post),
        g2pre=row(ffn2_norm_pre), g2post=row(ffn2_norm_post),
        w2gu=ffn2_w_gate_up[0].astype(BF16), w2d=ffn2_w_down[0].astype(BF16),
    )
    outs = []
    for x3 in (x_prompt, x_sample):
        seq_len = x3.shape[1]
        plan = _Plan(seq_len, *_dft_shape(seq_len))
        filt = _filt_gen(seq_len, filt_w_first[0], filt_b_first[0], filt_w_hidden[0],
                         filt_b_hidden[0], filt_w_last[0], filt_freq[0], rows=min(seq_len, 512),
                         n2=plan.n2)
        outs.append(_trunk(x3, p, plan, filt))
    return tuple(outs)
```

```python
import functools
import math

import jax
import jax.numpy as jnp
import numpy as np
from jax import lax
from jax.experimental import pallas as pl
from jax.experimental.pallas import tpu as pltpu

F32 = jnp.float32
BF16 = jnp.bfloat16
EPS = 1e-6
LANES = 128
SUBLANES = 8
MXU_DIM = 256
POOL_WINDOWS = (2, 4, 8, 16)
HALO = 8
DECAY_TARGET = 1e-2
FAST_DECAY_PCT = 0.3
SLOW_DECAY_PCT = 1.5
MAX_DECAY = math.log(DECAY_TARGET) / FAST_DECAY_PCT
MIN_DECAY = math.log(DECAY_TARGET) / SLOW_DECAY_PCT
HIGHEST = lax.Precision.HIGHEST
FUSED_VMEM_BUDGET = 32 << 20


def _params(sem, vmem_mib):
    return pltpu.CompilerParams(dimension_semantics=sem, vmem_limit_bytes=vmem_mib << 20)


def _rms(x, g):
    inv = lax.rsqrt(jnp.mean(x * x, axis=-1, keepdims=True) + EPS)
    return (x * inv) * g


def _dot(a, b):
    return jnp.dot(a, b, preferred_element_type=F32)


def _dot3(a, b):
    a_hi, b_hi = a.astype(BF16), b.astype(BF16)
    a_lo = (a - a_hi.astype(F32)).astype(BF16)
    b_lo = (b - b_hi.astype(F32)).astype(BF16)
    return _dot(a_hi, b_hi) + _dot(a_hi, b_lo) + _dot(a_lo, b_hi)


def _lanes(parts):
    return parts[0] if len(parts) == 1 else jnp.concatenate(parts, axis=1)


def _lane_block(a, i):
    return a[:, i * LANES:(i + 1) * LANES]


def _ff_chunks(d_ff, n_chunks=2):
    if d_ff % MXU_DIM:
        return (0, d_ff)
    tiles = d_ff // MXU_DIM
    return tuple(MXU_DIM * ((tiles * k + n_chunks - 1) // n_chunks) for k in range(n_chunks + 1))


def _half_step_ffn(x, gpre, gpost, wgu_ref, wd_ref):
    d_ff = wd_ref.shape[0]
    h = _rms(x, gpre).astype(BF16)
    acc = None
    bounds = _ff_chunks(d_ff)
    for lo, hi in zip(bounds[:-1], bounds[1:]):
        gate = _dot(h, wgu_ref[:, lo:hi])
        up = _dot(h, wgu_ref[:, d_ff + lo:d_ff + hi])
        act = (gate * jax.nn.sigmoid(gate) * up).astype(BF16)
        part = _dot(act, wd_ref[lo:hi, :])
        acc = part if acc is None else acc + part
    return x + 0.5 * _rms(acc, gpost)


def _ffn_in_kernel(x_ref, gpre_ref, gpost_ref, wgu_ref, wd_ref, gmix_ref, win_ref, o_ref, z_ref):
    x = _half_step_ffn(x_ref[...], gpre_ref[...], gpost_ref[...], wgu_ref, wd_ref)
    o_ref[...] = x
    z_ref[...] = _dot(_rms(x, gmix_ref[...]).astype(BF16), win_ref[...])


def _ffn_in(x, g_pre, g_post, wgu, wd, g_mix, w_in, *, tm):
    t, d = x.shape
    d_ff, d_in = wd.shape[0], w_in.shape[1]
    const = lambda i: (0, 0)
    row = lambda i: (i, 0)
    once = pl.Buffered(1)
    vec = pl.BlockSpec((1, d), const)
    return pl.pallas_call(
        _ffn_in_kernel,
        grid=(t // tm,),
        in_specs=[
            pl.BlockSpec((tm, d), row), vec, vec,
            pl.BlockSpec((d, 2 * d_ff), const, pipeline_mode=once),
            pl.BlockSpec((d_ff, d), const, pipeline_mode=once),
            vec,
            pl.BlockSpec((d, d_in), const, pipeline_mode=once),
        ],
        out_specs=[pl.BlockSpec((tm, d), row), pl.BlockSpec((tm, d_in), row)],
        out_shape=[jax.ShapeDtypeStruct((t, d), F32), jax.ShapeDtypeStruct((t, d_in), F32)],
        compiler_params=_params(("parallel",), 56),
        name="ffn_in",
    )(x, g_pre, g_post, wgu, wd, g_mix, w_in)


def _ffn_out_kernel(x_ref, yp_ref, yh_ref, gp_ref, gh_ref, wout_ref, gmix_ref,
                    gpre_ref, gpost_ref, wgu_ref, wd_ref, o_ref):
    d_pool = yp_ref.shape[1]
    yp = _rms(yp_ref[...], gp_ref[...]).astype(BF16)
    yh = _rms(yh_ref[...], gh_ref[...]).astype(BF16)
    y = _dot(yp, wout_ref[:d_pool, :]) + _dot(yh, wout_ref[d_pool:, :])
    x = x_ref[...] + _rms(y, gmix_ref[...])
    o_ref[...] = _half_step_ffn(x, gpre_ref[...], gpost_ref[...], wgu_ref, wd_ref)


def _ffn_out(x, yp, yh, gp, gh, w_out, g_mix, g_pre, g_post, wgu, wd, *, tm):
    t, d = x.shape
    d_ff, dp, dh = wd.shape[0], yp.shape[1], yh.shape[1]
    const = lambda i: (0, 0)
    row = lambda i: (i, 0)
    once = pl.Buffered(1)
    vec = lambda n: pl.BlockSpec((1, n), const)
    return pl.pallas_call(
        _ffn_out_kernel,
        grid=(t // tm,),
        in_specs=[
            pl.BlockSpec((tm, d), row), pl.BlockSpec((tm, dp), row), pl.BlockSpec((tm, dh), row),
            vec(dp), vec(dh),
            pl.BlockSpec((dp + dh, d), const, pipeline_mode=once),
            vec(d), vec(d), vec(d),
            pl.BlockSpec((d, 2 * d_ff), const, pipeline_mode=once),
            pl.BlockSpec((d_ff, d), const, pipeline_mode=once),
        ],
        out_specs=pl.BlockSpec((tm, d), row),
        out_shape=jax.ShapeDtypeStruct((t, d), F32),
        compiler_params=_params(("parallel",), 56),
        name="ffn_out",
    )(x, yp, yh, gp, gh, w_out, g_mix, g_pre, g_post, wgu, wd)


def _with_halo(prev_ref, main_ref, next_ref, i, n_tiles):
    prev = jnp.where(i > 0, prev_ref[0], 0.0)
    nxt = jnp.where(i < n_tiles - 1, next_ref[0], 0.0)
    return jnp.concatenate([prev, main_ref[0], nxt], axis=0)


def _pool_kernel(prev_ref, main_ref, next_ref, wmap_ref, scale_ref, o_ref, *, seq_len, n_tiles):
    i = pl.program_id(1)
    rows = main_ref.shape[1]
    n_ext = rows + 2 * HALO
    ext_all = _with_halo(prev_ref, main_ref, next_ref, i, n_tiles)
    pos = i * rows + lax.broadcasted_iota(jnp.int32, (rows, LANES), 0)
    for grp, window in enumerate(POOL_WINDOWS):
        lanes = slice(grp * LANES, (grp + 1) * LANES)
        ext = ext_all[:, lanes]
        ssum, w = ext + pltpu.roll(ext, 1, 0), 2
        while w < window:
            ssum, w = pltpu.roll(ssum, w // 2, 0) + pltpu.roll(ssum, n_ext - w // 2, 0), 2 * w
        u = main_ref[0, :, lanes]
        lo = jnp.clip(pos - window // 2, 0, seq_len)
        hi = jnp.clip(pos + (window - window // 2), 0, seq_len)
        d = ssum[HALO:HALO + rows] / (hi - lo).astype(F32) - u
        y = _dot(d.astype(BF16), wmap_ref[grp])
        o_ref[0, :, lanes] = y * scale_ref[:, lanes]


def _halo_specs(rows, width, seq_len, col_of):
    blocks_per_tile = rows // HALO
    last = seq_len // HALO - 1
    prev = pl.BlockSpec((1, HALO, width),
                        lambda b, i, c: (b, jnp.maximum(i * blocks_per_tile - 1, 0), col_of(c)))
    main = pl.BlockSpec((1, rows, width), lambda b, i, c: (b, i, col_of(c)))
    nxt = pl.BlockSpec((1, HALO, width),
                       lambda b, i, c: (b, jnp.minimum((i + 1) * blocks_per_tile, last), col_of(c)))
    return [prev, main, nxt]


def _pool(z, wmap, scale, *, rows):
    b, seq_len, _ = z.shape
    assert wmap.shape[0] == len(POOL_WINDOWS) and max(POOL_WINDOWS) <= 2 * HALO
    width = wmap.shape[0] * LANES
    n_tiles = seq_len // rows
    return pl.pallas_call(
        functools.partial(_pool_kernel, seq_len=seq_len, n_tiles=n_tiles),
        grid=(b, n_tiles, 1),
        in_specs=_halo_specs(rows, width, seq_len, lambda c: 0) + [
            pl.BlockSpec(wmap.shape, lambda b, i, c: (0, 0, 0)),
            pl.BlockSpec((1, width), lambda b, i, c: (0, 0)),
        ],
        out_specs=pl.BlockSpec((1, rows, width), lambda b, i, c: (b, i, 0)),
        out_shape=jax.ShapeDtypeStruct((b, seq_len, width), F32),
        compiler_params=_params(("parallel", "parallel", "parallel"), 40),
        name="pool",
    )(z, z, z, wmap, scale)


def _sconv_kernel(prev_ref, main_ref, next_ref, w_ref, b_ref, o_ref, *, n_tiles):
    i = pl.program_id(1)
    rows = main_ref.shape[1]
    ext = _with_halo(prev_ref, main_ref, next_ref, i, n_tiles)
    n_ext = rows + 2 * HALO
    before = pltpu.roll(ext, 1, 0)[HALO:HALO + rows]
    after = pltpu.roll(ext, n_ext - 1, 0)[HALO:HALO + rows]
    w = w_ref[...]
    out = before * w[0:1] + main_ref[0] * w[1:2] + after * w[2:3]
    o_ref[0, 0] = out + b_ref[...]


def _sconv(z, w, bias, *, rows, col0, n_parts):
    b, seq_len, _ = z.shape
    n_tiles = seq_len // rows
    width = w.shape[1] // n_parts
    assert col0 % width == 0
    return pl.pallas_call(
        functools.partial(_sconv_kernel, n_tiles=n_tiles),
        grid=(b, n_tiles, n_parts),
        in_specs=_halo_specs(rows, width, seq_len, lambda c: c + col0 // width) + [
            pl.BlockSpec((w.shape[0], width), lambda b, i, c: (0, c)),
            pl.BlockSpec((1, width), lambda b, i, c: (0, c)),
        ],
        out_specs=pl.BlockSpec((1, 1, rows, width), lambda b, i, c: (c, b, i, 0)),
        out_shape=jax.ShapeDtypeStruct((n_parts, b, seq_len, width), F32),
        compiler_params=_params(("parallel", "parallel", "parallel"), 40),
        name="sconv",
    )(z, z, z, w, bias)


def _filt_gen_kernel(bands_ref, wt_ref, wc_ref, ws_ref, b1_ref, wh_ref, bh_ref, freq_ref,
                     wl_ref, delta_ref, o_ref, *, seq_len, d_ch):
    rows = o_ref.shape[0] * (o_ref.shape[1] - SUBLANES)
    base = pl.program_id(0) * rows
    m_lane = (base + lax.broadcasted_iota(jnp.int32, (1, rows), 1)).astype(F32)
    t_lane = m_lane / (seq_len - 1.0)
    ang = (bands_ref[...] * (2.0 * math.pi / seq_len)) * m_lane
    freq = freq_ref[...]
    pre = (jnp.dot(wc_ref[...], jnp.cos(ang), precision=HIGHEST, preferred_element_type=F32)
           + jnp.dot(ws_ref[...], -jnp.sin(ang), precision=HIGHEST, preferred_element_type=F32)
           + wt_ref[...] * t_lane + b1_ref[...])
    h = jnp.sin(freq * pre)
    for layer in range(wh_ref.shape[0]):
        pre = jnp.dot(wh_ref[layer], h, precision=HIGHEST, preferred_element_type=F32) + bh_ref[layer]
        h = jnp.sin(freq * pre)
    out = _dot3(h.T, wl_ref[...])
    m_row = base + lax.broadcasted_iota(jnp.int32, (rows, d_ch), 0)
    t_row = m_row.astype(F32) / (seq_len - 1.0)
    decay = jnp.exp(-t_row * jnp.abs(delta_ref[...]))
    decay_bwd = jnp.where(m_row == 0, 0.0, decay)
    n_chunks, pitch, _ = o_ref.shape
    n2 = rows // n_chunks
    for q in range(out.shape[1] // d_ch):
        dq = decay_bwd if q % 2 == 1 else decay
        val = out[:, q * d_ch:(q + 1) * d_ch] * dq
        for n in range(n_chunks):
            o_ref[n, :n2, q * d_ch:(q + 1) * d_ch] = val[n * n2:(n + 1) * n2]
    o_ref[:, n2:, :] = jnp.zeros((n_chunks, pitch - n2, o_ref.shape[2]), F32)


def _filt_gen(seq_len, w_first, b_first, w_hidden, b_hidden, w_last, freq, *, rows, n2):
    pos_bands = (w_first.shape[0] - 1) // 2
    hidden = w_first.shape[1]
    n_cols = w_last.shape[1]
    d_ch = n_cols // 4
    bands = jnp.linspace(1e-4, pos_bands - 1, pos_bands, dtype=F32).reshape(pos_bands, 1)
    deltas = jnp.linspace(MIN_DECAY, MAX_DECAY, d_ch, dtype=F32).reshape(1, d_ch)
    w1t = w_first.T
    args = (bands, w1t[:, 0:1], w1t[:, 1:1 + pos_bands], w1t[:, 1 + pos_bands:],
            b_first.reshape(hidden, 1), jnp.swapaxes(w_hidden, 1, 2),
            b_hidden.reshape(b_hidden.shape[0], hidden, 1), freq.reshape(hidden, 1),
            w_last, deltas)
    full = lambda a: pl.BlockSpec(a.shape, lambda i, _n=a.ndim: (0,) * _n)
    return pl.pallas_call(
        functools.partial(_filt_gen_kernel, seq_len=seq_len, d_ch=d_ch),
        grid=(seq_len // rows,),
        in_specs=[full(a) for a in args],
        out_specs=pl.BlockSpec((rows // n2, n2 + SUBLANES, n_cols), lambda i: (i, 0, 0)),
        out_shape=jax.ShapeDtypeStruct((seq_len // n2, n2 + SUBLANES, n_cols), F32),
        compiler_params=_params(("parallel",), 40),
        name="filt_gen",
    )(*args)


class _Plan:
    def __init__(self, seq_len, n1, n2):
        assert n1 * n2 == 2 * seq_len and n1 % (2 * SUBLANES) == 0 and n2 % SUBLANES == 0
        self.seq_len, self.n1, self.n2 = seq_len, n1, n2
        n = n1 * n2
        n1h = n1 // 2
        self.kh = kh = n1h + 1
        k1 = np.arange(kh, dtype=np.float64)[:, None]
        m1 = np.arange(n1h, dtype=np.float64)[None, :]
        th1 = 2.0 * np.pi * k1 * m1 / n1
        f1 = np.zeros((n1 + SUBLANES, n1h))
        f1[:n1h] = np.cos(th1[:n1h]); f1[n1h:n1] = -np.sin(th1[:n1h]); f1[n1] = np.cos(th1[n1h])
        self.f1 = jnp.asarray(f1, F32)
        m2 = np.arange(n2, dtype=np.float64)[None, :]
        tht = 2.0 * np.pi * k1 * m2 / n
        twr, twi = np.cos(tht), -np.sin(tht)
        rep = lambda a: np.repeat(a[:, :, None], LANES, axis=2)
        self.twr_inv, self.twi_inv = jnp.asarray(rep(twr), F32), jnp.asarray(rep(twi), F32)
        pad = np.zeros((n2, SUBLANES - 1))
        self.twr_fwd = jnp.asarray(rep(np.concatenate([twr.T, pad], 1)), F32)
        self.twi_fwd = jnp.asarray(rep(np.concatenate([twi.T, pad], 1)), F32)
        k2 = np.arange(n2, dtype=np.float64)[:, None]
        th2 = 2.0 * np.pi * k2 * m2 / n2
        f2r, f2i = np.cos(th2), -np.sin(th2)
        self.f2 = jnp.asarray(np.block([[f2r, -f2i], [f2i, f2r]]), F32)
        self.f2inv = jnp.asarray(np.block([[f2r, f2i], [-f2i, f2r]]), F32)
        wgt = np.where(k1[:n1h] == 0, 1.0, 2.0) / n
        ginv = np.concatenate([(wgt * np.cos(th1[:n1h])).T, (-wgt * np.sin(th1[:n1h])).T], axis=1)
        self.ginv = jnp.asarray(ginv, F32)
        k1f = np.arange(kh, dtype=np.float64)[:, None]
        full = lambda cols: 2.0 * np.pi * k1f * cols[None, :] / n1
        stage1 = lambda th: np.concatenate(
            [np.cos(th[:n1h]), -np.sin(th[:n1h]), np.cos(th[n1h:]), np.zeros((SUBLANES - 1, th.shape[1]))], 0)
        fwd_cols = np.arange(n1h, dtype=np.float64)
        f1a = stage1(full(np.concatenate([fwd_cols, n1 - 1 - fwd_cols])))
        f1b = stage1(full(np.concatenate([fwd_cols, (n1 - fwd_cols) % n1])))
        f1b[:, n1h] = 0.0
        self.f1_filt, self.f1_filt0 = jnp.asarray(f1a, F32), jnp.asarray(f1b, F32)
        rows = lambda t: rep(np.concatenate([t.T, np.zeros((t.shape[1], SUBLANES - 1))], 1))
        coarse, fine = tht[:, ::SUBLANES], tht[:, :SUBLANES]
        self.tw_coarse = (jnp.asarray(rows(np.cos(coarse)), F32), jnp.asarray(rows(-np.sin(coarse)), F32))
        self.tw_fine = (jnp.asarray(rows(np.cos(fine)), F32), jnp.asarray(rows(-np.sin(fine)), F32))
        self.tw_slab_coarse = (jnp.asarray(rep(np.cos(coarse)), F32), jnp.asarray(rep(-np.sin(coarse)), F32))
        self.tw_slab_fine = (jnp.asarray(rep(np.cos(fine)), F32), jnp.asarray(rep(-np.sin(fine)), F32))


def _stack_bf16(re, im):
    return jnp.concatenate([re.astype(BF16), im.astype(BF16)], axis=0)


def _filt_spec_kernel(hf_ref, hba_ref, hbb_ref, f1_ref, f1j0_ref, twcr_ref, twci_ref, twfr_ref, twfi_ref, f2_ref,
                      kr_ref, ki_ref, are_ref, aim_ref, *, n1h, n2, jg):
    pitch = n2 + SUBLANES
    n_j = n2 // jg
    ks = kr_ref.shape[0]
    phase = pl.program_id(1)
    hf, hba, hbb = (r.reshape(n1h * jg, LANES) for r in (hf_ref, hba_ref, hbb_ref))

    @pl.when(phase < n_j)
    def _():
        f1 = f1_ref[...].astype(BF16)
        f1_first = jnp.where(phase == 0, f1j0_ref[...], f1_ref[...]).astype(BF16)
        j0 = phase * jg
        a0 = lax.shift_right_logical(j0, 3)

        def stage1(mat, ts):
            rows = lambda ref, start: ref[pl.ds(start, n1h, stride=jg), :]
            x = _lanes([jnp.concatenate([rows(hf, t), rows(hbb, 0) if t == 0 else rows(hba, jg - t)], axis=0)
                        for t in ts])
            p = _dot(mat, x.astype(BF16))
            cr = _lanes([twcr_ref[a0 + t // SUBLANES] for t in ts])
            ci = _lanes([twci_ref[a0 + t // SUBLANES] for t in ts])
            fr = _lanes([twfr_ref[t % SUBLANES] for t in ts])
            fi = _lanes([twfi_ref[t % SUBLANES] for t in ts])
            twr, twi = cr * fr - ci * fi, cr * fi + ci * fr
            pr, pi, pn = p[:n1h], p[n1h:2 * n1h], p[2 * n1h:2 * n1h + 1]
            re, im = pr * twr[:n1h] - pi * twi[:n1h], pr * twi[:n1h] + pi * twr[:n1h]
            re_n, im_n = pn * twr[n1h:n1h + 1], pn * twi[n1h:n1h + 1]
            for i, t in enumerate(ts):
                are_ref[pl.ds(j0 + t, n1h, stride=pitch), :] = _lane_block(re, i)
                aim_ref[pl.ds(j0 + t, n1h, stride=pitch), :] = _lane_block(im, i)
                are_ref[pl.ds(n1h * pitch + j0 + t, 1), :] = _lane_block(re_n, i)
                aim_ref[pl.ds(n1h * pitch + j0 + t, 1), :] = _lane_block(im_n, i)

        stage1(f1_first, (0,))
        for t in range(1, jg - 1, 2):
            stage1(f1, (t, t + 1))
        stage1(f1, (jg - 1,))

    @pl.when(phase >= n_j)
    def _():
        f2 = f2_ref[...].astype(BF16)
        kc = phase - n_j

        def slabs(first, count):
            rows = [pl.ds(pl.multiple_of((kc * ks + first + i) * pitch, SUBLANES), n2) for i in range(count)]
            x = _dot(f2, _stack_bf16(_lanes([are_ref[r, :] for r in rows]), _lanes([aim_ref[r, :] for r in rows])))
            for i in range(count):
                kr_ref[first + i] = _lane_block(x[:n2], i).astype(kr_ref.dtype)
                ki_ref[first + i] = _lane_block(x[n2:], i).astype(ki_ref.dtype)

        def pair(p, carry):
            slabs(2 * p, 2)
            return carry

        lax.fori_loop(0, ks // 2, pair, 0, unroll=8)
        if ks % 2:
            slabs(ks - 1, 1)


def _filt_spec(plan, filt, n_orders, *, ks, jg=16):
    n1h, n2, kh = plan.n1 // 2, plan.n2, plan.kh
    cols = filt.shape[2]
    n_cb = cols // (2 * n_orders * LANES)
    n_j = n2 // jg
    pitch = n2 + SUBLANES
    col = lambda c, back: (c // n_cb) * 2 * n_cb + back * n_cb + c % n_cb
    step = lambda s: jnp.minimum(s, n_j - 1)
    seq = lambda back, blk: pl.BlockSpec((n1h, jg, LANES), lambda c, s: (0, blk(step(s)), col(c, back)))
    full = lambda a: pl.BlockSpec(a.shape, lambda c, s, _n=a.ndim: (0,) * _n, pipeline_mode=pl.Buffered(1))
    tables = (plan.f1_filt, plan.f1_filt0, *plan.tw_coarse, *plan.tw_fine, plan.f2)
    out = jax.ShapeDtypeStruct((kh, n2, cols // 2), BF16)
    scratch = pltpu.VMEM((kh * pitch, LANES), F32)
    return pl.pallas_call(
        functools.partial(_filt_spec_kernel, n1h=n1h, n2=n2, jg=jg),
        grid=(n_orders * n_cb, n_j + kh // ks),
        in_specs=[seq(0, lambda s: s), seq(1, lambda s: n_j - 1 - s), seq(1, lambda s: (n_j - s) % n_j)]
        + [full(a) for a in tables],
        out_specs=[pl.BlockSpec((ks, n2, LANES), lambda c, s: (jnp.maximum(s - n_j, 0), 0, c))] * 2,
        out_shape=[out, out],
        scratch_shapes=[scratch, scratch],
        compiler_params=_params(("parallel", "arbitrary"), 48),
        name="filt_spec",
    )(filt, filt, filt, *tables)


def _conv_front_kernel(x_ref, kr_ref, ki_ref, f1_ref, twcr_ref, twci_ref, twfr_ref, twfi_ref,
                       icr_ref, ici_ref, ifr_ref, ifi_ref, f2_ref, f2inv_ref,
                       ore_ref, oim_ref, are_ref, aim_ref, *, n1h, n2, unroll):
    pitch = n2 + SUBLANES
    ks = kr_ref.shape[0]
    kc = pl.program_id(2)
    x2 = x_ref.at[0, 0]

    @pl.when(kc == 0)
    def _():
        f1 = f1_ref[...].astype(BF16)

        def body(jp, carry):
            js = (2 * jp, 2 * jp + 1)
            a = lax.shift_right_logical(jp, 2)
            bs = [2 * jnp.bitwise_and(jp, SUBLANES // 2 - 1) + q for q in range(2)]
            xb = _lanes([x2[pl.ds(j, n1h, stride=n2), :] for j in js]).astype(BF16)
            p = _dot(f1, xb)
            cr, ci = jnp.tile(twcr_ref[a], (1, 2)), jnp.tile(twci_ref[a], (1, 2))
            fr, fi = _lanes([twfr_ref[b] for b in bs]), _lanes([twfi_ref[b] for b in bs])
            twr, twi = cr * fr - ci * fi, cr * fi + ci * fr
            pr, pi, pn = p[:n1h], p[n1h:2 * n1h], p[2 * n1h:2 * n1h + 1]
            re, im = pr * twr[:n1h] - pi * twi[:n1h], pr * twi[:n1h] + pi * twr[:n1h]
            re_n, im_n = pn * twr[n1h:n1h + 1], pn * twi[n1h:n1h + 1]
            for i, j in enumerate(js):
                are_ref[pl.ds(j, n1h, stride=pitch), :] = _lane_block(re, i)
                aim_ref[pl.ds(j, n1h, stride=pitch), :] = _lane_block(im, i)
                are_ref[pl.ds(n1h * pitch + j, 1), :] = _lane_block(re_n, i)
                aim_ref[pl.ds(n1h * pitch + j, 1), :] = _lane_block(im_n, i)
            return carry

        lax.fori_loop(0, n2 // 2, body, 0, unroll=unroll)

    f2, f2inv = f2_ref[...].astype(BF16), f2inv_ref[...].astype(BF16)

    def slab_twiddle(k):
        cr, ci, fr, fi = icr_ref[k], ici_ref[k], ifr_ref[k], ifi_ref[k]
        pieces = [(cr[a:a + 1] * fr - ci[a:a + 1] * fi, cr[a:a + 1] * fi + ci[a:a + 1] * fr)
                  for a in range(n2 // SUBLANES)]
        return (jnp.concatenate([p[0] for p in pieces], axis=0), jnp.concatenate([p[1] for p in pieces], axis=0))

    def slabs(first, count):
        ks_abs = [kc * ks + first + i for i in range(count)]
        rows = [pl.ds(pl.multiple_of(k * pitch, SUBLANES), n2) for k in ks_abs]
        x = _dot(f2, _stack_bf16(_lanes([are_ref[r, :] for r in rows]), _lanes([aim_ref[r, :] for r in rows])))
        xr, xi = x[:n2], x[n2:]
        kr = _lanes([kr_ref[first + i] for i in range(count)]).astype(F32)
        ki = _lanes([ki_ref[first + i] for i in range(count)]).astype(F32)
        y = _dot(f2inv, _stack_bf16(xr * kr - xi * ki, xr * ki + xi * kr))
        br, bi = y[:n2], y[n2:]
        tw = [slab_twiddle(k) for k in ks_abs]
        twr, twi = _lanes([t[0] for t in tw]), _lanes([t[1] for t in tw])
        ore, oim = br * twr + bi * twi, bi * twr - br * twi
        for i in range(count):
            ore_ref[0, first + i] = _lane_block(ore, i)
            oim_ref[0, first + i] = _lane_block(oim, i)

    def pair(p, carry):
        slabs(2 * p, 2)
        return carry

    lax.fori_loop(0, ks // 2, pair, 0, unroll=8)
    if ks % 2:
        slabs(ks - 1, 1)


def _conv_front(plan, x4, part, kr, ki, order, *, ks):
    _, b, seq_len, ch = x4.shape
    n1h, n2, kh = plan.n1 // 2, plan.n2, plan.kh
    n_cb = ch // LANES
    pitch = n2 + SUBLANES
    once = pl.Buffered(1)
    full = lambda a: pl.BlockSpec(a.shape, lambda bb, c, k, _n=a.ndim: (0,) * _n, pipeline_mode=once)
    filt = pl.BlockSpec((ks, n2, LANES), lambda bb, c, k: (k, 0, order * n_cb + c))
    data = pl.BlockSpec((1, ks, n2, LANES), lambda bb, c, k: (bb, k, 0, c))
    tables = (plan.f1, *plan.tw_coarse, *plan.tw_fine, *plan.tw_slab_coarse, *plan.tw_slab_fine,
              plan.f2, plan.f2inv)
    out = jax.ShapeDtypeStruct((b, kh, n2, ch), F32)
    scratch = pltpu.VMEM((kh * pitch, LANES), F32)
    return pl.pallas_call(
        functools.partial(_conv_front_kernel, n1h=n1h, n2=n2, unroll=4),
        grid=(b, n_cb, kh // ks),
        in_specs=[pl.BlockSpec((1, 1, seq_len, LANES), lambda bb, c, k: (part, bb, 0, c), pipeline_mode=once),
                  filt, filt] + [full(a) for a in tables],
        out_specs=[data, data],
        out_shape=[out, out],
        scratch_shapes=[scratch, scratch],
        compiler_params=_params(("parallel", "parallel", "arbitrary"), 56),
        name="conv_front",
    )(x4, kr, ki, *tables)


def _dft_inv_kernel(bre_ref, bim_ref, x_ref, gate_ref, skip_ref, ginv_ref, o_ref, *, unroll, scale):
    _, _, n1h, g, _ = o_ref.shape
    b_re = bre_ref.at[0].reshape((n1h + 1) * g, LANES)
    b_im = bim_ref.at[0].reshape((n1h + 1) * g, LANES)
    x2 = x_ref.at[0, 0].reshape(n1h * g, LANES)
    gate2 = gate_ref.at[0, 0].reshape(n1h * g, LANES)
    o2 = o_ref.at[0, 0].reshape(n1h * g, LANES)
    ginv = ginv_ref[...].astype(BF16)
    skip = skip_ref[0]
    n1 = lax.broadcasted_iota(jnp.int32, (n1h, 2 * LANES), 0)
    nyq_w = jnp.where(jnp.bitwise_and(n1, 1) == 0, scale, -scale)
    skip = jnp.tile(skip, (1, 2))

    def body(jp, carry):
        rows = [pl.ds(2 * jp + i, n1h, stride=g) for i in range(2)]
        spec = _stack_bf16(_lanes([b_re[r, :] for r in rows]), _lanes([b_im[r, :] for r in rows]))
        nyq = _lanes([b_re[pl.ds(n1h * g + 2 * jp + i, 1), :] for i in range(2)])
        y = _dot(ginv, spec) + nyq_w * nyq
        out = _lanes([gate2[r, :] for r in rows]) * (y + _lanes([x2[r, :] for r in rows]) * skip)
        for i, r in enumerate(rows):
            o2[r, :] = _lane_block(out, i)
        return carry

    lax.fori_loop(0, g // 2, body, 0, unroll=unroll)


def _dft_inv(plan, bre, bim, x4, x_part, gate4, gate_part, skip, order, *, g):
    b, seq_len, ch = x4.shape[1:]
    n1h, n2, kh = plan.n1 // 2, plan.n2, plan.kh
    xv = x4.reshape(x4.shape[0], b, n1h, n2, ch)
    gv = gate4.reshape(gate4.shape[0], b, n1h, n2, ch)
    spec = pl.BlockSpec((1, kh, g, LANES), lambda j, bb, c: (bb, 0, j, c))
    time = lambda part: pl.BlockSpec((1, 1, n1h, g, LANES), lambda j, bb, c: (part, bb, 0, j, c))
    out = pl.pallas_call(
        functools.partial(_dft_inv_kernel, unroll=min(g // 2, 8), scale=1.0 / (plan.n1 * n2)),
        grid=(n2 // g, b, ch // LANES),
        in_specs=[
            spec, spec, time(x_part), time(gate_part),
            pl.BlockSpec((1, 1, LANES), lambda j, bb, c: (order, 0, c)),
            pl.BlockSpec(plan.ginv.shape, lambda j, bb, c: (0, 0)),
        ],
        out_specs=time(0),
        out_shape=jax.ShapeDtypeStruct((1, b, n1h, n2, ch), F32),
        compiler_params=_params(("parallel", "parallel", "parallel"), 48),
        name="dft_inv",
    )(bre, bim, xv, gv, skip, plan.ginv)
    return out.reshape(1, b, seq_len, ch)


def _hyena_fused_kernel(zv_ref, zx1_ref, zx2_ref, wv_ref, wx1_ref, wx2_ref, bv_ref, bx1_ref, bx2_ref,
                        skip_ref, kr0_ref, ki0_ref, kr1_ref, ki1_ref,
                        f1_ref, twrf_ref, twif_ref, twri_ref, twii_ref, f2_ref, f2inv_ref, ginv_ref,
                        o_ref, are_ref, aim_ref, cur_ref, tin_ref, *, n1h, n2, unroll):
    nb = o_ref.shape[0]
    pitch = n2 + SUBLANES
    f1 = f1_ref[...].astype(BF16)
    f2, f2inv = f2_ref[...].astype(BF16), f2inv_ref[...].astype(BF16)
    ginv = ginv_ref[...].astype(BF16)
    scale = 1.0 / (2 * n1h * n2)
    n1 = lax.broadcasted_iota(jnp.int32, (n1h, 2 * nb * LANES), 0)
    nyq_w = jnp.where(jnp.bitwise_and(n1, 1) == 0, scale, -scale)
    lanes, lane_block = _lanes, _lane_block
    wide = lambda t, reps: t if reps == 1 else jnp.tile(t, (1, reps))

    row = lax.broadcasted_iota(jnp.int32, (n2, LANES), 0)
    for part, (z_ref, w_ref, b_ref) in enumerate(((zv_ref, wv_ref, bv_ref), (zx1_ref, wx1_ref, bx1_ref),
                                                   (zx2_ref, wx2_ref, bx2_ref))):
        w, bias = w_ref[...], b_ref[...]
        for i in range(nb):
            for n in range(n1h):
                lo = n * n2
                mid = z_ref[i, lo:lo + n2, :]
                if n > 0:
                    before = z_ref[i, lo - 1:lo + n2 - 1, :]
                else:
                    before = jnp.where(row == 0, 0.0, pltpu.roll(mid, 1, 0))
                if n < n1h - 1:
                    after = z_ref[i, lo + 1:lo + n2 + 1, :]
                else:
                    after = jnp.where(row == n2 - 1, 0.0, pltpu.roll(mid, n2 - 1, 0))
                tin_ref[part, i, n * pitch:n * pitch + n2, :] = (
                    before * w[0:1] + mid * w[1:2] + after * w[2:3] + bias)
    stages = ((tin_ref.at[0], tin_ref.at[1], cur_ref, kr0_ref, ki0_ref),
              (cur_ref, tin_ref.at[2], tin_ref.at[0], kr1_ref, ki1_ref))
    for order, (src, gate, dst, kr_ref, ki_ref) in enumerate(stages):
        skip = wide(skip_ref[order], 2 * nb)

        def stage1(jp, carry):
            cols = [(2 * jp + q, i) for q in range(2) for i in range(nb)]
            xb = lanes([src[i, pl.ds(j, n1h, stride=pitch), :] for j, i in cols]).astype(BF16)
            p = _dot(f1, xb)
            twr = lanes([wide(twrf_ref[2 * jp + q], nb) for q in range(2)])
            twi = lanes([wide(twif_ref[2 * jp + q], nb) for q in range(2)])
            pr, pi, pn = p[:n1h], p[n1h:2 * n1h], p[2 * n1h:2 * n1h + 1]
            re, im = pr * twr[:n1h] - pi * twi[:n1h], pr * twi[:n1h] + pi * twr[:n1h]
            re_n, im_n = pn * twr[n1h:n1h + 1], pn * twi[n1h:n1h + 1]
            for c, (j, i) in enumerate(cols):
                are_ref[i, pl.ds(j, n1h, stride=pitch), :] = lane_block(re, c)
                aim_ref[i, pl.ds(j, n1h, stride=pitch), :] = lane_block(im, c)
                are_ref[i, pl.ds(n1h * pitch + j, 1), :] = lane_block(re_n, c)
                aim_ref[i, pl.ds(n1h * pitch + j, 1), :] = lane_block(im_n, c)
            return carry

        lax.fori_loop(0, n2 // 2, stage1, 0, unroll=unroll)

        def slabs(ks):
            rows = [pl.ds(pl.multiple_of(k * pitch, SUBLANES), n2) for k in ks]
            a = _stack_bf16(lanes([are_ref[i, r, :] for r in rows for i in range(nb)]),
                            lanes([aim_ref[i, r, :] for r in rows for i in range(nb)]))
            x = _dot(f2, a)
            xr, xi = x[:n2], x[n2:]
            per_slab = lambda ref: lanes([wide(ref[k], nb) for k in ks])
            kr, ki = per_slab(kr_ref).astype(F32), per_slab(ki_ref).astype(F32)
            y = _dot(f2inv, _stack_bf16(xr * kr - xi * ki, xr * ki + xi * kr))
            br, bi = y[:n2], y[n2:]
            twr, twi = per_slab(twri_ref), per_slab(twii_ref)
            ore, oim = br * twr + bi * twi, bi * twr - br * twi
            for s, r in enumerate(rows):
                for i in range(nb):
                    are_ref[i, r, :] = lane_block(ore, s * nb + i)
                    aim_ref[i, r, :] = lane_block(oim, s * nb + i)

        def slab(k, carry):
            slabs((k,))
            return carry

        lax.fori_loop(0, n1h + 1, slab, 0, unroll=True)

        def stage1_inv(jp, carry):
            cols = [(pl.ds(2 * jp + q, n1h, stride=pitch), 2 * jp + q, i) for q in range(2) for i in range(nb)]
            spec = _stack_bf16(lanes([are_ref[i, r, :] for r, _, i in cols]),
                               lanes([aim_ref[i, r, :] for r, _, i in cols]))
            nyq = lanes([are_ref[i, pl.ds(n1h * pitch + j, 1), :] for _, j, i in cols])
            y = _dot(ginv, spec) + nyq_w * nyq
            x = lanes([src[i, r, :] for r, _, i in cols])
            g = lanes([gate[i, r, :] for r, _, i in cols])
            out = g * (y + x * skip)
            for c, (r, _, i) in enumerate(cols):
                dst[i, r, :] = lane_block(out, c)
            return carry

        lax.fori_loop(0, n2 // 2, stage1_inv, 0, unroll=unroll)

    def unpitch(n, carry):
        for i in range(nb):
            o_ref[i, pl.ds(pl.multiple_of(n * n2, n2), n2), :] = (
                tin_ref[0, i, pl.ds(pl.multiple_of(n * pitch, SUBLANES), n2), :])
        return carry

    lax.fori_loop(0, n1h, unpitch, 0)


def _hyena_fused(plan, z, col0, conv_w, conv_b, kr, ki, skip, *, nb):
    b, seq_len, _ = z.shape
    ch = skip.shape[1]
    n1h, n2, kh = plan.n1 // 2, plan.n2, plan.kh
    n_cb = ch // LANES
    cb0 = col0 // LANES
    part = lambda p: pl.BlockSpec((nb, seq_len, LANES), lambda c, bb: (bb, 0, cb0 + p * n_cb + c))
    taps = lambda p: pl.BlockSpec((conv_w.shape[0], LANES), lambda c, bb: (0, p * n_cb + c))
    bias = lambda p: pl.BlockSpec((1, LANES), lambda c, bb: (0, p * n_cb + c))
    once = pl.Buffered(1)
    filt = lambda o: pl.BlockSpec((kh, n2, LANES), lambda c, bb: (0, 0, o * n_cb + c), pipeline_mode=once)
    full = lambda a: pl.BlockSpec(a.shape, lambda c, bb, _n=a.ndim: (0,) * _n, pipeline_mode=once)
    tables = (plan.f1, plan.twr_fwd, plan.twi_fwd, plan.twr_inv, plan.twi_inv, plan.f2, plan.f2inv, plan.ginv)
    pitch = n2 + SUBLANES
    spec_scratch = pltpu.VMEM((nb, kh * pitch, LANES), F32)
    return pl.pallas_call(
        functools.partial(_hyena_fused_kernel, n1h=n1h, n2=n2, unroll=min(n2 // 2, 32)),
        grid=(n_cb, b // nb),
        in_specs=[part(0), part(1), part(2), taps(0), taps(1), taps(2), bias(0), bias(1), bias(2),
                  pl.BlockSpec((skip.shape[0], 1, LANES), lambda c, bb: (0, 0, c)),
                  filt(0), filt(0), filt(1), filt(1)] + [full(a) for a in tables],
        out_specs=pl.BlockSpec((nb, seq_len, LANES), lambda c, bb: (bb, 0, c)),
        out_shape=jax.ShapeDtypeStruct((b, seq_len, ch), F32),
        scratch_shapes=[spec_scratch, spec_scratch, pltpu.VMEM((nb, n1h * pitch, LANES), F32),
                        pltpu.VMEM((3, nb, n1h * pitch, LANES), F32)],
        compiler_params=_params(("parallel", "arbitrary"), 56),
        name="hyena_fused",
    )(z, z, z, conv_w, conv_w, conv_w, conv_b, conv_b, conv_b,
      skip.reshape(skip.shape[0], 1, ch), kr, ki, kr, ki, *tables)


def _dft_shape(seq_len):
    n2 = 128
    return 2 * seq_len // n2, n2


def _block_cols(n2, rows, target_bytes=1 << 20):
    g = max(SUBLANES, min(n2, target_bytes // (rows * LANES * 4)) // SUBLANES * SUBLANES)
    while n2 % g:
        g -= SUBLANES
    return g


def _slabs_per_step(kh, slab_bytes, target_bytes=3 << 20):
    ks = max(1, min(kh, target_bytes // slab_bytes))
    while kh % ks:
        ks -= 1
    return ks


def _hyena(plan, z, col0, conv_w, conv_b, filt, skip):
    b, seq_len, _ = z.shape
    n_orders, ch = skip.shape
    g = _block_cols(plan.n2, plan.n1 // 2)
    ks_filt = _slabs_per_step(plan.kh, 2 * plan.n2 * LANES * 4, target_bytes=6 << 20)
    kr, ki = _filt_spec(plan, filt, n_orders, ks=ks_filt)
    nb = 2 if b % 2 == 0 else 1
    fused_bytes = 15 * nb * seq_len * LANES * 4
    if n_orders == 2 and fused_bytes <= FUSED_VMEM_BUDGET:
        return _hyena_fused(plan, z, col0, conv_w, conv_b, kr, ki, skip, nb=nb)
    uc = _sconv(z, conv_w, conv_b, rows=min(seq_len, 2048), col0=col0, n_parts=n_orders + 1)
    skip3 = skip.reshape(n_orders, 1, ch)
    cur, cur_part = uc, 0
    for order in range(n_orders):
        bre, bim = _conv_front(plan, cur, cur_part, kr, ki, order, ks=ks_filt)
        cur = _dft_inv(plan, bre, bim, cur, cur_part, uc, order + 1, skip3, order, g=g)
        cur_part = 0
    return cur[0]


def _trunk(x3, p, plan, filt):
    b, seq_len, d = x3.shape
    t = b * seq_len
    x = x3.reshape(t, d)
    x, z = _ffn_in(x, p["g1pre"], p["g1post"], p["w1gu"], p["w1d"], p["gmix"], p["w_in"], tm=512)
    z = z.reshape(b, seq_len, -1)
    d_pool = p["pool_w"].shape[0] * LANES
    y_pool = _pool(z, p["pool_w"], p["pool_scale"], rows=min(seq_len, 2048))
    y_hy = _hyena(plan, z, d_pool, p["conv_w"], p["conv_b"], filt, p["skip"])
    x = _ffn_out(x, y_pool.reshape(t, -1), y_hy.reshape(t, -1), p["gpool"], p["ghy"], p["w_out"],
                 p["gmixpost"], p["g2pre"], p["g2post"], p["w2gu"], p["w2d"], tm=512)
    return x.reshape(b, seq_len, d)


def kernel(x_prompt, x_sample, ffn1_norm_pre, ffn1_norm_post, ffn1_w_gate_up, ffn1_w_down, mix_norm_pre, w_in, pool_w_map, pool_scale, hyena_conv_w, hyena_conv_b, filt_w_first, filt_b_first, filt_w_hidden, filt_b_hidden, filt_w_last, filt_freq, hyena_skip, pool_out_norm, hyena_out_norm, w_out, mix_norm_post, ffn2_norm_pre, ffn2_norm_post, ffn2_w_gate_up, ffn2_w_down):
    assert ffn1_norm_pre.shape[0] == 1, "single-layer trunk"
    row = lambda a: a[0].reshape(1, -1)
    p = dict(
        g1pre=row(ffn1_norm_pre), g1post=row(ffn1_norm_post),
        w1gu=ffn1_w_gate_up[0].astype(BF16), w1d=ffn1_w_down[0].astype(BF16),
        gmix=row(mix_norm_pre), w_in=w_in[0].astype(BF16),
        pool_w=pool_w_map[0].astype(BF16), pool_scale=row(pool_scale),
        conv_w=hyena_conv_w[0], conv_b=row(hyena_conv_b),
        skip=hyena_skip[0], gpool=row(pool_out_norm), ghy=row(hyena_out_norm),
        w_out=w_out[0].astype(BF16), gmixpost=row(mix_norm_post),
        g2pre=row(ffn2_norm_pre), g2post=row(ffn2_norm_post),
        w2gu=ffn2_w_gate_up[0].astype(BF16), w2d=ffn2_w_down[0].astype(BF16),
    )
    outs = []
    for x3 in (x_prompt, x_sample):
        seq_len = x3.shape[1]
        plan = _Plan(seq_len, *_dft_shape(seq_len))
        filt = _filt_gen(seq_len, filt_w_first[0], filt_b_first[0], filt_w_hidden[0],
                         filt_b_hidden[0], filt_w_last[0], filt_freq[0], rows=min(seq_len, 512),
                         n2=plan.n2)
        outs.append(_trunk(x3, p, plan, filt))
    return tuple(outs)
```

```python
import functools
import math

import jax
import jax.numpy as jnp
import numpy as np
from jax import lax
from jax.experimental import pallas as pl
from jax.experimental.pallas import tpu as pltpu

F32 = jnp.float32
BF16 = jnp.bfloat16
EPS = 1e-6
LANES = 128
SUBLANES = 8
MXU_DIM = 256
POOL_WINDOWS = (2, 4, 8, 16)
HALO = 8
DECAY_TARGET = 1e-2
FAST_DECAY_PCT = 0.3
SLOW_DECAY_PCT = 1.5
MAX_DECAY = math.log(DECAY_TARGET) / FAST_DECAY_PCT
MIN_DECAY = math.log(DECAY_TARGET) / SLOW_DECAY_PCT
FUSED_VMEM_BUDGET = 32 << 20


def _params(sem, vmem_mib):
    return pltpu.CompilerParams(dimension_semantics=sem, vmem_limit_bytes=vmem_mib << 20)


def _rms(x, g):
    inv = lax.rsqrt(jnp.mean(x * x, axis=-1, keepdims=True) + EPS)
    return (x * inv) * g


def _dot(a, b):
    return jnp.dot(a, b, preferred_element_type=F32)


def _split_bf16(a):
    hi = a.astype(BF16)
    return hi, (a - hi.astype(F32)).astype(BF16)


def _dot_split(a, b):
    (a_hi, a_lo), (b_hi, b_lo) = _split_bf16(a), _split_bf16(b)
    return _dot(jnp.concatenate([a_hi, a_hi, a_lo], axis=1), jnp.concatenate([b_hi, b_lo, b_hi], axis=0))


def _lanes(parts):
    return parts[0] if len(parts) == 1 else jnp.concatenate(parts, axis=1)


def _lane_block(a, i):
    return a[:, i * LANES:(i + 1) * LANES]


def _ff_chunks(d_ff, n_chunks=2):
    if d_ff % MXU_DIM:
        return (0, d_ff)
    tiles = d_ff // MXU_DIM
    return tuple(MXU_DIM * ((tiles * k + n_chunks - 1) // n_chunks) for k in range(n_chunks + 1))


def _half_step_ffn(x, gpre, gpost, wgu_ref, wd_ref):
    d_ff = wd_ref.shape[0]
    h = _rms(x, gpre).astype(BF16)
    acc = None
    bounds = _ff_chunks(d_ff)
    for lo, hi in zip(bounds[:-1], bounds[1:]):
        gate = _dot(h, wgu_ref[:, lo:hi])
        up = _dot(h, wgu_ref[:, d_ff + lo:d_ff + hi])
        act = (gate * jax.nn.sigmoid(gate) * up).astype(BF16)
        part = _dot(act, wd_ref[lo:hi, :])
        acc = part if acc is None else acc + part
    return x + 0.5 * _rms(acc, gpost)


def _ffn_in_kernel(x_ref, gpre_ref, gpost_ref, wgu_ref, wd_ref, gmix_ref, win_ref, o_ref, z_ref):
    x = _half_step_ffn(x_ref[...], gpre_ref[...], gpost_ref[...], wgu_ref, wd_ref)
    o_ref[...] = x
    z_ref[...] = _dot(_rms(x, gmix_ref[...]).astype(BF16), win_ref[...])


def _ffn_in(x, g_pre, g_post, wgu, wd, g_mix, w_in, *, tm):
    t, d = x.shape
    d_ff, d_in = wd.shape[0], w_in.shape[1]
    const = lambda i: (0, 0)
    row = lambda i: (i, 0)
    once = pl.Buffered(1)
    vec = pl.BlockSpec((1, d), const)
    return pl.pallas_call(
        _ffn_in_kernel,
        grid=(t // tm,),
        in_specs=[
            pl.BlockSpec((tm, d), row), vec, vec,
            pl.BlockSpec((d, 2 * d_ff), const, pipeline_mode=once),
            pl.BlockSpec((d_ff, d), const, pipeline_mode=once),
            vec,
            pl.BlockSpec((d, d_in), const, pipeline_mode=once),
        ],
        out_specs=[pl.BlockSpec((tm, d), row), pl.BlockSpec((tm, d_in), row)],
        out_shape=[jax.ShapeDtypeStruct((t, d), F32), jax.ShapeDtypeStruct((t, d_in), F32)],
        compiler_params=_params(("parallel",), 56),
        name="ffn_in",
    )(x, g_pre, g_post, wgu, wd, g_mix, w_in)


def _ffn_out_kernel(x_ref, yp_ref, yh_ref, gp_ref, gh_ref, wout_ref, gmix_ref,
                    gpre_ref, gpost_ref, wgu_ref, wd_ref, o_ref):
    d_pool = yp_ref.shape[1]
    yp = _rms(yp_ref[...], gp_ref[...]).astype(BF16)
    yh = _rms(yh_ref[...], gh_ref[...]).astype(BF16)
    y = _dot(yp, wout_ref[:d_pool, :]) + _dot(yh, wout_ref[d_pool:, :])
    x = x_ref[...] + _rms(y, gmix_ref[...])
    o_ref[...] = _half_step_ffn(x, gpre_ref[...], gpost_ref[...], wgu_ref, wd_ref)


def _ffn_out(x, yp, yh, gp, gh, w_out, g_mix, g_pre, g_post, wgu, wd, *, tm):
    t, d = x.shape
    d_ff, dp, dh = wd.shape[0], yp.shape[1], yh.shape[1]
    const = lambda i: (0, 0)
    row = lambda i: (i, 0)
    once = pl.Buffered(1)
    vec = lambda n: pl.BlockSpec((1, n), const)
    return pl.pallas_call(
        _ffn_out_kernel,
        grid=(t // tm,),
        in_specs=[
            pl.BlockSpec((tm, d), row), pl.BlockSpec((tm, dp), row), pl.BlockSpec((tm, dh), row),
            vec(dp), vec(dh),
            pl.BlockSpec((dp + dh, d), const, pipeline_mode=once),
            vec(d), vec(d), vec(d),
            pl.BlockSpec((d, 2 * d_ff), const, pipeline_mode=once),
            pl.BlockSpec((d_ff, d), const, pipeline_mode=once),
        ],
        out_specs=pl.BlockSpec((tm, d), row),
        out_shape=jax.ShapeDtypeStruct((t, d), F32),
        compiler_params=_params(("parallel",), 56),
        name="ffn_out",
    )(x, yp, yh, gp, gh, w_out, g_mix, g_pre, g_post, wgu, wd)


def _with_halo(prev_ref, main_ref, next_ref, i, n_tiles):
    prev = jnp.where(i > 0, prev_ref[0], 0.0)
    nxt = jnp.where(i < n_tiles - 1, next_ref[0], 0.0)
    return jnp.concatenate([prev, main_ref[0], nxt], axis=0)


def _pool_kernel(prev_ref, main_ref, next_ref, wmap_ref, scale_ref, o_ref, *, seq_len, n_tiles):
    i = pl.program_id(1)
    rows = main_ref.shape[1]
    n_ext = rows + 2 * HALO
    ext_all = _with_halo(prev_ref, main_ref, next_ref, i, n_tiles)
    pos = i * rows + lax.broadcasted_iota(jnp.int32, (rows, LANES), 0)
    for grp, window in enumerate(POOL_WINDOWS):
        lanes = slice(grp * LANES, (grp + 1) * LANES)
        ext = ext_all[:, lanes]
        ssum, w = ext + pltpu.roll(ext, 1, 0), 2
        while w < window:
            ssum, w = pltpu.roll(ssum, w // 2, 0) + pltpu.roll(ssum, n_ext - w // 2, 0), 2 * w
        u = main_ref[0, :, lanes]
        lo = jnp.clip(pos - window // 2, 0, seq_len)
        hi = jnp.clip(pos + (window - window // 2), 0, seq_len)
        d = ssum[HALO:HALO + rows] / (hi - lo).astype(F32) - u
        y = _dot(d.astype(BF16), wmap_ref[grp])
        o_ref[0, :, lanes] = y * scale_ref[:, lanes]


def _halo_specs(rows, width, seq_len, col_of):
    blocks_per_tile = rows // HALO
    last = seq_len // HALO - 1
    prev = pl.BlockSpec((1, HALO, width),
                        lambda b, i, c: (b, jnp.maximum(i * blocks_per_tile - 1, 0), col_of(c)))
    main = pl.BlockSpec((1, rows, width), lambda b, i, c: (b, i, col_of(c)))
    nxt = pl.BlockSpec((1, HALO, width),
                       lambda b, i, c: (b, jnp.minimum((i + 1) * blocks_per_tile, last), col_of(c)))
    return [prev, main, nxt]


def _pool(z, wmap, scale, *, rows):
    b, seq_len, _ = z.shape
    assert wmap.shape[0] == len(POOL_WINDOWS) and max(POOL_WINDOWS) <= 2 * HALO
    width = wmap.shape[0] * LANES
    n_tiles = seq_len // rows
    return pl.pallas_call(
        functools.partial(_pool_kernel, seq_len=seq_len, n_tiles=n_tiles),
        grid=(b, n_tiles, 1),
        in_specs=_halo_specs(rows, width, seq_len, lambda c: 0) + [
            pl.BlockSpec(wmap.shape, lambda b, i, c: (0, 0, 0)),
            pl.BlockSpec((1, width), lambda b, i, c: (0, 0)),
        ],
        out_specs=pl.BlockSpec((1, rows, width), lambda b, i, c: (b, i, 0)),
        out_shape=jax.ShapeDtypeStruct((b, seq_len, width), F32),
        compiler_params=_params(("parallel", "parallel", "parallel"), 40),
        name="pool",
    )(z, z, z, wmap, scale)


def _sconv_kernel(prev_ref, main_ref, next_ref, w_ref, b_ref, o_ref, *, n_tiles):
    i = pl.program_id(1)
    rows = main_ref.shape[1]
    ext = _with_halo(prev_ref, main_ref, next_ref, i, n_tiles)
    n_ext = rows + 2 * HALO
    before = pltpu.roll(ext, 1, 0)[HALO:HALO + rows]
    after = pltpu.roll(ext, n_ext - 1, 0)[HALO:HALO + rows]
    w = w_ref[...]
    out = before * w[0:1] + main_ref[0] * w[1:2] + after * w[2:3]
    o_ref[0, 0] = out + b_ref[...]


def _sconv(z, w, bias, *, rows, col0, n_parts):
    b, seq_len, _ = z.shape
    n_tiles = seq_len // rows
    width = w.shape[1] // n_parts
    assert col0 % width == 0
    return pl.pallas_call(
        functools.partial(_sconv_kernel, n_tiles=n_tiles),
        grid=(b, n_tiles, n_parts),
        in_specs=_halo_specs(rows, width, seq_len, lambda c: c + col0 // width) + [
            pl.BlockSpec((w.shape[0], width), lambda b, i, c: (0, c)),
            pl.BlockSpec((1, width), lambda b, i, c: (0, c)),
        ],
        out_specs=pl.BlockSpec((1, 1, rows, width), lambda b, i, c: (c, b, i, 0)),
        out_shape=jax.ShapeDtypeStruct((n_parts, b, seq_len, width), F32),
        compiler_params=_params(("parallel", "parallel", "parallel"), 40),
        name="sconv",
    )(z, z, z, w, bias)


def _filt_gen_kernel(bands_ref, wt_ref, wc_ref, ws_ref, b1_ref, wh_ref, bh_ref, freq_ref,
                     wl_ref, delta_ref, o_ref, *, seq_len, d_ch):
    rows = o_ref.shape[0] * (o_ref.shape[1] - SUBLANES)
    base = pl.program_id(0) * rows
    m_lane = (base + lax.broadcasted_iota(jnp.int32, (1, rows), 1)).astype(F32)
    t_lane = m_lane / (seq_len - 1.0)
    ang = (bands_ref[...] * (2.0 * math.pi / seq_len)) * m_lane
    freq = freq_ref[...]
    pre = (_dot_split(wc_ref[...], jnp.cos(ang)) + _dot_split(ws_ref[...], -jnp.sin(ang))
           + wt_ref[...] * t_lane + b1_ref[...])
    h = jnp.sin(freq * pre)
    for layer in range(wh_ref.shape[0]):
        pre = _dot_split(wh_ref[layer], h) + bh_ref[layer]
        h = jnp.sin(freq * pre)
    (h_hi, h_lo), (w_hi, w_lo) = _split_bf16(h), _split_bf16(wl_ref[...])
    out = _dot(jnp.concatenate([h_hi, h_hi, h_lo], axis=0).T,
               jnp.concatenate([w_hi, w_lo, w_hi], axis=0))
    m_row = base + lax.broadcasted_iota(jnp.int32, (rows, d_ch), 0)
    t_row = m_row.astype(F32) / (seq_len - 1.0)
    decay = jnp.exp(-t_row * jnp.abs(delta_ref[...]))
    decay_bwd = jnp.where(m_row == 0, 0.0, decay)
    n_chunks, pitch, _ = o_ref.shape
    n2 = rows // n_chunks
    for q in range(out.shape[1] // d_ch):
        dq = decay_bwd if q % 2 == 1 else decay
        val = out[:, q * d_ch:(q + 1) * d_ch] * dq
        for n in range(n_chunks):
            o_ref[n, :n2, q * d_ch:(q + 1) * d_ch] = val[n * n2:(n + 1) * n2]
    o_ref[:, n2:, :] = jnp.zeros((n_chunks, pitch - n2, o_ref.shape[2]), F32)


def _filt_gen(seq_len, w_first, b_first, w_hidden, b_hidden, w_last, freq, *, rows, n2):
    pos_bands = (w_first.shape[0] - 1) // 2
    hidden = w_first.shape[1]
    n_cols = w_last.shape[1]
    d_ch = n_cols // 4
    bands = jnp.linspace(1e-4, pos_bands - 1, pos_bands, dtype=F32).reshape(pos_bands, 1)
    deltas = jnp.linspace(MIN_DECAY, MAX_DECAY, d_ch, dtype=F32).reshape(1, d_ch)
    w1t = w_first.T
    args = (bands, w1t[:, 0:1], w1t[:, 1:1 + pos_bands], w1t[:, 1 + pos_bands:],
            b_first.reshape(hidden, 1), jnp.swapaxes(w_hidden, 1, 2),
            b_hidden.reshape(b_hidden.shape[0], hidden, 1), freq.reshape(hidden, 1),
            w_last, deltas)
    full = lambda a: pl.BlockSpec(a.shape, lambda i, _n=a.ndim: (0,) * _n)
    return pl.pallas_call(
        functools.partial(_filt_gen_kernel, seq_len=seq_len, d_ch=d_ch),
        grid=(seq_len // rows,),
        in_specs=[full(a) for a in args],
        out_specs=pl.BlockSpec((rows // n2, n2 + SUBLANES, n_cols), lambda i: (i, 0, 0)),
        out_shape=jax.ShapeDtypeStruct((seq_len // n2, n2 + SUBLANES, n_cols), F32),
        compiler_params=_params(("parallel",), 40),
        name="filt_gen",
    )(*args)


class _Plan:
    def __init__(self, seq_len, n1, n2):
        assert n1 * n2 == 2 * seq_len and n1 % (2 * SUBLANES) == 0 and n2 % SUBLANES == 0
        self.seq_len, self.n1, self.n2 = seq_len, n1, n2
        n = n1 * n2
        n1h = n1 // 2
        self.kh = kh = n1h + 1
        k1 = np.arange(kh, dtype=np.float64)[:, None]
        m1 = np.arange(n1h, dtype=np.float64)[None, :]
        th1 = 2.0 * np.pi * k1 * m1 / n1
        f1 = np.zeros((n1 + SUBLANES, n1h))
        f1[:n1h] = np.cos(th1[:n1h]); f1[n1h:n1] = -np.sin(th1[:n1h]); f1[n1] = np.cos(th1[n1h])
        self.f1 = jnp.asarray(f1, F32)
        m2 = np.arange(n2, dtype=np.float64)[None, :]
        tht = 2.0 * np.pi * k1 * m2 / n
        twr, twi = np.cos(tht), -np.sin(tht)
        rep = lambda a: np.repeat(a[:, :, None], LANES, axis=2)
        self.twr_inv, self.twi_inv = jnp.asarray(rep(twr), F32), jnp.asarray(rep(twi), F32)
        pad = np.zeros((n2, SUBLANES - 1))
        self.twr_fwd = jnp.asarray(rep(np.concatenate([twr.T, pad], 1)), F32)
        self.twi_fwd = jnp.asarray(rep(np.concatenate([twi.T, pad], 1)), F32)
        k2 = np.arange(n2, dtype=np.float64)[:, None]
        th2 = 2.0 * np.pi * k2 * m2 / n2
        f2r, f2i = np.cos(th2), -np.sin(th2)
        self.f2 = jnp.asarray(np.block([[f2r, -f2i], [f2i, f2r]]), F32)
        self.f2inv = jnp.asarray(np.block([[f2r, f2i], [-f2i, f2r]]), F32)
        wgt = np.where(k1[:n1h] == 0, 1.0, 2.0) / n
        ginv = np.concatenate([(wgt * np.cos(th1[:n1h])).T, (-wgt * np.sin(th1[:n1h])).T], axis=1)
        self.ginv = jnp.asarray(ginv, F32)
        k1f = np.arange(kh, dtype=np.float64)[:, None]
        full = lambda cols: 2.0 * np.pi * k1f * cols[None, :] / n1
        stage1 = lambda th: np.concatenate(
            [np.cos(th[:n1h]), -np.sin(th[:n1h]), np.cos(th[n1h:]), np.zeros((SUBLANES - 1, th.shape[1]))], 0)
        fwd_cols = np.arange(n1h, dtype=np.float64)
        f1a = stage1(full(np.concatenate([fwd_cols, n1 - 1 - fwd_cols])))
        f1b = stage1(full(np.concatenate([fwd_cols, (n1 - fwd_cols) % n1])))
        f1b[:, n1h] = 0.0
        self.f1_filt, self.f1_filt0 = jnp.asarray(f1a, F32), jnp.asarray(f1b, F32)
        rows = lambda t: rep(np.concatenate([t.T, np.zeros((t.shape[1], SUBLANES - 1))], 1))
        coarse, fine = tht[:, ::SUBLANES], tht[:, :SUBLANES]
        self.tw_coarse = (jnp.asarray(rows(np.cos(coarse)), F32), jnp.asarray(rows(-np.sin(coarse)), F32))
        self.tw_fine = (jnp.asarray(rows(np.cos(fine)), F32), jnp.asarray(rows(-np.sin(fine)), F32))
        self.tw_slab_coarse = (jnp.asarray(rep(np.cos(coarse)), F32), jnp.asarray(rep(-np.sin(coarse)), F32))
        self.tw_slab_fine = (jnp.asarray(rep(np.cos(fine)), F32), jnp.asarray(rep(-np.sin(fine)), F32))


def _stack_bf16(re, im):
    return jnp.concatenate([re.astype(BF16), im.astype(BF16)], axis=0)


def _filt_spec_kernel(hf_ref, hba_ref, hbb_ref, f1_ref, f1j0_ref, twcr_ref, twci_ref, twfr_ref, twfi_ref, f2_ref,
                      kr_ref, ki_ref, are_ref, aim_ref, *, n1h, n2, jg):
    pitch = n2 + SUBLANES
    n_j = n2 // jg
    ks = kr_ref.shape[0]
    phase = pl.program_id(1)
    hf, hba, hbb = (r.reshape(n1h * jg, LANES) for r in (hf_ref, hba_ref, hbb_ref))

    @pl.when(phase < n_j)
    def _():
        f1 = f1_ref[...].astype(BF16)
        f1_first = jnp.where(phase == 0, f1j0_ref[...], f1_ref[...]).astype(BF16)
        j0 = phase * jg
        a0 = lax.shift_right_logical(j0, 3)

        def stage1(mat, ts):
            rows = lambda ref, start: ref[pl.ds(start, n1h, stride=jg), :]
            x = _lanes([jnp.concatenate([rows(hf, t), rows(hbb, 0) if t == 0 else rows(hba, jg - t)], axis=0)
                        for t in ts])
            p = _dot(mat, x.astype(BF16))
            cr = _lanes([twcr_ref[a0 + t // SUBLANES] for t in ts])
            ci = _lanes([twci_ref[a0 + t // SUBLANES] for t in ts])
            fr = _lanes([twfr_ref[t % SUBLANES] for t in ts])
            fi = _lanes([twfi_ref[t % SUBLANES] for t in ts])
            twr, twi = cr * fr - ci * fi, cr * fi + ci * fr
            pr, pi, pn = p[:n1h], p[n1h:2 * n1h], p[2 * n1h:2 * n1h + 1]
            re, im = pr * twr[:n1h] - pi * twi[:n1h], pr * twi[:n1h] + pi * twr[:n1h]
            re_n, im_n = pn * twr[n1h:n1h + 1], pn * twi[n1h:n1h + 1]
            for i, t in enumerate(ts):
                are_ref[pl.ds(j0 + t, n1h, stride=pitch), :] = _lane_block(re, i)
                aim_ref[pl.ds(j0 + t, n1h, stride=pitch), :] = _lane_block(im, i)
                are_ref[pl.ds(n1h * pitch + j0 + t, 1), :] = _lane_block(re_n, i)
                aim_ref[pl.ds(n1h * pitch + j0 + t, 1), :] = _lane_block(im_n, i)

        stage1(f1_first, (0,))
        for t in range(1, jg - 1, 2):
            stage1(f1, (t, t + 1))
        stage1(f1, (jg - 1,))

    @pl.when(phase >= n_j)
    def _():
        f2 = f2_ref[...].astype(BF16)
        kc = phase - n_j

        def slabs(first, count):
            rows = [pl.ds(pl.multiple_of((kc * ks + first + i) * pitch, SUBLANES), n2) for i in range(count)]
            x = _dot(f2, _stack_bf16(_lanes([are_ref[r, :] for r in rows]), _lanes([aim_ref[r, :] for r in rows])))
            for i in range(count):
                kr_ref[first + i] = _lane_block(x[:n2], i).astype(kr_ref.dtype)
                ki_ref[first + i] = _lane_block(x[n2:], i).astype(ki_ref.dtype)

        def pair(p, carry):
            slabs(2 * p, 2)
            return carry

        lax.fori_loop(0, ks // 2, pair, 0, unroll=8)
        if ks % 2:
            slabs(ks - 1, 1)


def _filt_spec(plan, filt, n_orders, *, ks, jg=16):
    n1h, n2, kh = plan.n1 // 2, plan.n2, plan.kh
    cols = filt.shape[2]
    n_cb = cols // (2 * n_orders * LANES)
    n_j = n2 // jg
    pitch = n2 + SUBLANES
    col = lambda c, back: (c // n_cb) * 2 * n_cb + back * n_cb + c % n_cb
    step = lambda s: jnp.minimum(s, n_j - 1)
    seq = lambda back, blk: pl.BlockSpec((n1h, jg, LANES), lambda c, s: (0, blk(step(s)), col(c, back)))
    full = lambda a: pl.BlockSpec(a.shape, lambda c, s, _n=a.ndim: (0,) * _n, pipeline_mode=pl.Buffered(1))
    tables = (plan.f1_filt, plan.f1_filt0, *plan.tw_coarse, *plan.tw_fine, plan.f2)
    out = jax.ShapeDtypeStruct((kh, n2, cols // 2), BF16)
    scratch = pltpu.VMEM((kh * pitch, LANES), F32)
    return pl.pallas_call(
        functools.partial(_filt_spec_kernel, n1h=n1h, n2=n2, jg=jg),
        grid=(n_orders * n_cb, n_j + kh // ks),
        in_specs=[seq(0, lambda s: s), seq(1, lambda s: n_j - 1 - s), seq(1, lambda s: (n_j - s) % n_j)]
        + [full(a) for a in tables],
        out_specs=[pl.BlockSpec((ks, n2, LANES), lambda c, s: (jnp.maximum(s - n_j, 0), 0, c))] * 2,
        out_shape=[out, out],
        scratch_shapes=[scratch, scratch],
        compiler_params=_params(("parallel", "arbitrary"), 48),
        name="filt_spec",
    )(filt, filt, filt, *tables)


def _conv_front_kernel(x_ref, kr_ref, ki_ref, f1_ref, twcr_ref, twci_ref, twfr_ref, twfi_ref,
                       icr_ref, ici_ref, ifr_ref, ifi_ref, f2_ref, f2inv_ref,
                       ore_ref, oim_ref, are_ref, aim_ref, *, n1h, n2, unroll):
    pitch = n2 + SUBLANES
    ks = kr_ref.shape[0]
    kc = pl.program_id(2)
    x2 = x_ref.at[0, 0]

    @pl.when(kc == 0)
    def _():
        f1 = f1_ref[...].astype(BF16)

        def body(jp, carry):
            js = (2 * jp, 2 * jp + 1)
            a = lax.shift_right_logical(jp, 2)
            bs = [2 * jnp.bitwise_and(jp, SUBLANES // 2 - 1) + q for q in range(2)]
            xb = _lanes([x2[pl.ds(j, n1h, stride=n2), :] for j in js]).astype(BF16)
            p = _dot(f1, xb)
            cr, ci = jnp.tile(twcr_ref[a], (1, 2)), jnp.tile(twci_ref[a], (1, 2))
            fr, fi = _lanes([twfr_ref[b] for b in bs]), _lanes([twfi_ref[b] for b in bs])
            twr, twi = cr * fr - ci * fi, cr * fi + ci * fr
            pr, pi, pn = p[:n1h], p[n1h:2 * n1h], p[2 * n1h:2 * n1h + 1]
            re, im = pr * twr[:n1h] - pi * twi[:n1h], pr * twi[:n1h] + pi * twr[:n1h]
            re_n, im_n = pn * twr[n1h:n1h + 1], pn * twi[n1h:n1h + 1]
            for i, j in enumerate(js):
                are_ref[pl.ds(j, n1h, stride=pitch), :] = _lane_block(re, i)
                aim_ref[pl.ds(j, n1h, stride=pitch), :] = _lane_block(im, i)
                are_ref[pl.ds(n1h * pitch + j, 1), :] = _lane_block(re_n, i)
                aim_ref[pl.ds(n1h * pitch + j, 1), :] = _lane_block(im_n, i)
            return carry

        lax.fori_loop(0, n2 // 2, body, 0, unroll=unroll)

    f2, f2inv = f2_ref[...].astype(BF16), f2inv_ref[...].astype(BF16)

    def slab_twiddle(k):
        cr, ci, fr, fi = icr_ref[k], ici_ref[k], ifr_ref[k], ifi_ref[k]
        pieces = [(cr[a:a + 1] * fr - ci[a:a + 1] * fi, cr[a:a + 1] * fi + ci[a:a + 1] * fr)
                  for a in range(n2 // SUBLANES)]
        return (jnp.concatenate([p[0] for p in pieces], axis=0), jnp.concatenate([p[1] for p in pieces], axis=0))

    def slabs(first, count):
        ks_abs = [kc * ks + first + i for i in range(count)]
        rows = [pl.ds(pl.multiple_of(k * pitch, SUBLANES), n2) for k in ks_abs]
        x = _dot(f2, _stack_bf16(_lanes([are_ref[r, :] for r in rows]), _lanes([aim_ref[r, :] for r in rows])))
        xr, xi = x[:n2], x[n2:]
        kr = _lanes([kr_ref[first + i] for i in range(count)]).astype(F32)
        ki = _lanes([ki_ref[first + i] for i in range(count)]).astype(F32)
        y = _dot(f2inv, _stack_bf16(xr * kr - xi * ki, xr * ki + xi * kr))
        br, bi = y[:n2], y[n2:]
        tw = [slab_twiddle(k) for k in ks_abs]
        twr, twi = _lanes([t[0] for t in tw]), _lanes([t[1] for t in tw])
        ore, oim = br * twr + bi * twi, bi * twr - br * twi
        for i in range(count):
            ore_ref[0, first + i] = _lane_block(ore, i)
            oim_ref[0, first + i] = _lane_block(oim, i)

    def pair(p, carry):
        slabs(2 * p, 2)
        return carry

    lax.fori_loop(0, ks // 2, pair, 0, unroll=8)
    if ks % 2:
        slabs(ks - 1, 1)


def _conv_front(plan, x4, part, kr, ki, order, *, ks):
    _, b, seq_len, ch = x4.shape
    n1h, n2, kh = plan.n1 // 2, plan.n2, plan.kh
    n_cb = ch // LANES
    pitch = n2 + SUBLANES
    once = pl.Buffered(1)
    full = lambda a: pl.BlockSpec(a.shape, lambda bb, c, k, _n=a.ndim: (0,) * _n, pipeline_mode=once)
    filt = pl.BlockSpec((ks, n2, LANES), lambda bb, c, k: (k, 0, order * n_cb + c))
    data = pl.BlockSpec((1, ks, n2, LANES), lambda bb, c, k: (bb, k, 0, c))
    tables = (plan.f1, *plan.tw_coarse, *plan.tw_fine, *plan.tw_slab_coarse, *plan.tw_slab_fine,
              plan.f2, plan.f2inv)
    out = jax.ShapeDtypeStruct((b, kh, n2, ch), F32)
    scratch = pltpu.VMEM((kh * pitch, LANES), F32)
    return pl.pallas_call(
        functools.partial(_conv_front_kernel, n1h=n1h, n2=n2, unroll=4),
        grid=(b, n_cb, kh // ks),
        in_specs=[pl.BlockSpec((1, 1, seq_len, LANES), lambda bb, c, k: (part, bb, 0, c), pipeline_mode=once),
                  filt, filt] + [full(a) for a in tables],
        out_specs=[data, data],
        out_shape=[out, out],
        scratch_shapes=[scratch, scratch],
        compiler_params=_params(("parallel", "parallel", "arbitrary"), 56),
        name="conv_front",
    )(x4, kr, ki, *tables)


def _dft_inv_kernel(bre_ref, bim_ref, x_ref, gate_ref, skip_ref, ginv_ref, o_ref, *, unroll, scale):
    _, _, n1h, g, _ = o_ref.shape
    b_re = bre_ref.at[0].reshape((n1h + 1) * g, LANES)
    b_im = bim_ref.at[0].reshape((n1h + 1) * g, LANES)
    x2 = x_ref.at[0, 0].reshape(n1h * g, LANES)
    gate2 = gate_ref.at[0, 0].reshape(n1h * g, LANES)
    o2 = o_ref.at[0, 0].reshape(n1h * g, LANES)
    ginv = ginv_ref[...].astype(BF16)
    skip = skip_ref[0]
    n1 = lax.broadcasted_iota(jnp.int32, (n1h, 2 * LANES), 0)
    nyq_w = jnp.where(jnp.bitwise_and(n1, 1) == 0, scale, -scale)
    skip = jnp.tile(skip, (1, 2))

    def body(jp, carry):
        rows = [pl.ds(2 * jp + i, n1h, stride=g) for i in range(2)]
        spec = _stack_bf16(_lanes([b_re[r, :] for r in rows]), _lanes([b_im[r, :] for r in rows]))
        nyq = _lanes([b_re[pl.ds(n1h * g + 2 * jp + i, 1), :] for i in range(2)])
        y = _dot(ginv, spec) + nyq_w * nyq
        out = _lanes([gate2[r, :] for r in rows]) * (y + _lanes([x2[r, :] for r in rows]) * skip)
        for i, r in enumerate(rows):
            o2[r, :] = _lane_block(out, i)
        return carry

    lax.fori_loop(0, g // 2, body, 0, unroll=unroll)


def _dft_inv(plan, bre, bim, x4, x_part, gate4, gate_part, skip, order, *, g):
    b, seq_len, ch = x4.shape[1:]
    n1h, n2, kh = plan.n1 // 2, plan.n2, plan.kh
    xv = x4.reshape(x4.shape[0], b, n1h, n2, ch)
    gv = gate4.reshape(gate4.shape[0], b, n1h, n2, ch)
    spec = pl.BlockSpec((1, kh, g, LANES), lambda j, bb, c: (bb, 0, j, c))
    time = lambda part: pl.BlockSpec((1, 1, n1h, g, LANES), lambda j, bb, c: (part, bb, 0, j, c))
    out = pl.pallas_call(
        functools.partial(_dft_inv_kernel, unroll=min(g // 2, 8), scale=1.0 / (plan.n1 * n2)),
        grid=(n2 // g, b, ch // LANES),
        in_specs=[
            spec, spec, time(x_part), time(gate_part),
            pl.BlockSpec((1, 1, LANES), lambda j, bb, c: (order, 0, c)),
            pl.BlockSpec(plan.ginv.shape, lambda j, bb, c: (0, 0)),
        ],
        out_specs=time(0),
        out_shape=jax.ShapeDtypeStruct((1, b, n1h, n2, ch), F32),
        compiler_params=_params(("parallel", "parallel", "parallel"), 48),
        name="dft_inv",
    )(bre, bim, xv, gv, skip, plan.ginv)
    return out.reshape(1, b, seq_len, ch)


def _hyena_fused_kernel(zv_ref, zx1_ref, zx2_ref, wv_ref, wx1_ref, wx2_ref, bv_ref, bx1_ref, bx2_ref,
                        skip_ref, kr0_ref, ki0_ref, kr1_ref, ki1_ref,
                        f1_ref, twrf_ref, twif_ref, twri_ref, twii_ref, f2_ref, f2inv_ref, ginv_ref,
                        o_ref, are_ref, aim_ref, cur_ref, tin_ref, *, n1h, n2, unroll):
    nb = o_ref.shape[0]
    pitch = n2 + SUBLANES
    f1 = f1_ref[...].astype(BF16)
    f2, f2inv = f2_ref[...].astype(BF16), f2inv_ref[...].astype(BF16)
    ginv = ginv_ref[...].astype(BF16)
    scale = 1.0 / (2 * n1h * n2)
    n1 = lax.broadcasted_iota(jnp.int32, (n1h, 2 * nb * LANES), 0)
    nyq_w = jnp.where(jnp.bitwise_and(n1, 1) == 0, scale, -scale)
    lanes, lane_block = _lanes, _lane_block
    wide = lambda t, reps: t if reps == 1 else jnp.tile(t, (1, reps))

    row = lax.broadcasted_iota(jnp.int32, (n2, LANES), 0)
    for part, (z_ref, w_ref, b_ref) in enumerate(((zv_ref, wv_ref, bv_ref), (zx1_ref, wx1_ref, bx1_ref),
                                                   (zx2_ref, wx2_ref, bx2_ref))):
        w, bias = w_ref[...], b_ref[...]
        for i in range(nb):
            for n in range(n1h):
                lo = n * n2
                mid = z_ref[i, lo:lo + n2, :]
                if n > 0:
                    before = z_ref[i, lo - 1:lo + n2 - 1, :]
                else:
                    before = jnp.where(row == 0, 0.0, pltpu.roll(mid, 1, 0))
                if n < n1h - 1:
                    after = z_ref[i, lo + 1:lo + n2 + 1, :]
                else:
                    after = jnp.where(row == n2 - 1, 0.0, pltpu.roll(mid, n2 - 1, 0))
                tin_ref[part, i, n * pitch:n * pitch + n2, :] = (
                    before * w[0:1] + mid * w[1:2] + after * w[2:3] + bias)
    stages = ((tin_ref.at[0], tin_ref.at[1], cur_ref, kr0_ref, ki0_ref),
              (cur_ref, tin_ref.at[2], tin_ref.at[0], kr1_ref, ki1_ref))
    for order, (src, gate, dst, kr_ref, ki_ref) in enumerate(stages):
        skip = wide(skip_ref[order], 2 * nb)

        def stage1(jp, carry):
            cols = [(2 * jp + q, i) for q in range(2) for i in range(nb)]
            xb = lanes([src[i, pl.ds(j, n1h, stride=pitch), :] for j, i in cols]).astype(BF16)
            p = _dot(f1, xb)
            twr = lanes([wide(twrf_ref[2 * jp + q], nb) for q in range(2)])
            twi = lanes([wide(twif_ref[2 * jp + q], nb) for q in range(2)])
            pr, pi, pn = p[:n1h], p[n1h:2 * n1h], p[2 * n1h:2 * n1h + 1]
            re, im = pr * twr[:n1h] - pi * twi[:n1h], pr * twi[:n1h] + pi * twr[:n1h]
            re_n, im_n = pn * twr[n1h:n1h + 1], pn * twi[n1h:n1h + 1]
            for c, (j, i) in enumerate(cols):
                are_ref[i, pl.ds(j, n1h, stride=pitch), :] = lane_block(re, c)
                aim_ref[i, pl.ds(j, n1h, stride=pitch), :] = lane_block(im, c)
                are_ref[i, pl.ds(n1h * pitch + j, 1), :] = lane_block(re_n, c)
                aim_ref[i, pl.ds(n1h * pitch + j, 1), :] = lane_block(im_n, c)
            return carry

        lax.fori_loop(0, n2 // 2, stage1, 0, unroll=unroll)

        def slabs(ks):
            rows = [pl.ds(pl.multiple_of(k * pitch, SUBLANES), n2) for k in ks]
            a = _stack_bf16(lanes([are_ref[i, r, :] for r in rows for i in range(nb)]),
                            lanes([aim_ref[i, r, :] for r in rows for i in range(nb)]))
            x = _dot(f2, a)
            xr, xi = x[:n2], x[n2:]
            per_slab = lambda ref: lanes([wide(ref[k], nb) for k in ks])
            kr, ki = per_slab(kr_ref).astype(F32), per_slab(ki_ref).astype(F32)
            y = _dot(f2inv, _stack_bf16(xr * kr - xi * ki, xr * ki + xi * kr))
            br, bi = y[:n2], y[n2:]
            twr, twi = per_slab(twri_ref), per_slab(twii_ref)
            ore, oim = br * twr + bi * twi, bi * twr - br * twi
            for s, r in enumerate(rows):
                for i in range(nb):
                    are_ref[i, r, :] = lane_block(ore, s * nb + i)
                    aim_ref[i, r, :] = lane_block(oim, s * nb + i)

        def slab(k, carry):
            slabs((k,))
            return carry

        lax.fori_loop(0, n1h + 1, slab, 0, unroll=True)

        def stage1_inv(jp, carry):
            cols = [(pl.ds(2 * jp + q, n1h, stride=pitch), 2 * jp + q, i) for q in range(2) for i in range(nb)]
            spec = _stack_bf16(lanes([are_ref[i, r, :] for r, _, i in cols]),
                               lanes([aim_ref[i, r, :] for r, _, i in cols]))
            nyq = lanes([are_ref[i, pl.ds(n1h * pitch + j, 1), :] for _, j, i in cols])
            y = _dot(ginv, spec) + nyq_w * nyq
            x = lanes([src[i, r, :] for r, _, i in cols])
            g = lanes([gate[i, r, :] for r, _, i in cols])
            out = g * (y + x * skip)
            for c, (r, _, i) in enumerate(cols):
                dst[i, r, :] = lane_block(out, c)
            return carry

        lax.fori_loop(0, n2 // 2, stage1_inv, 0, unroll=unroll)

    def unpitch(n, carry):
        for i in range(nb):
            o_ref[i, pl.ds(pl.multiple_of(n * n2, n2), n2), :] = (
                tin_ref[0, i, pl.ds(pl.multiple_of(n * pitch, SUBLANES), n2), :])
        return carry

    lax.fori_loop(0, n1h, unpitch, 0)


def _hyena_fused(plan, z, col0, conv_w, conv_b, kr, ki, skip, *, nb):
    b, seq_len, _ = z.shape
    ch = skip.shape[1]
    n1h, n2, kh = plan.n1 // 2, plan.n2, plan.kh
    n_cb = ch // LANES
    cb0 = col0 // LANES
    part = lambda p: pl.BlockSpec((nb, seq_len, LANES), lambda c, bb: (bb, 0, cb0 + p * n_cb + c))
    taps = lambda p: pl.BlockSpec((conv_w.shape[0], LANES), lambda c, bb: (0, p * n_cb + c))
    bias = lambda p: pl.BlockSpec((1, LANES), lambda c, bb: (0, p * n_cb + c))
    once = pl.Buffered(1)
    filt = lambda o: pl.BlockSpec((kh, n2, LANES), lambda c, bb: (0, 0, o * n_cb + c), pipeline_mode=once)
    full = lambda a: pl.BlockSpec(a.shape, lambda c, bb, _n=a.ndim: (0,) * _n, pipeline_mode=once)
    tables = (plan.f1, plan.twr_fwd, plan.twi_fwd, plan.twr_inv, plan.twi_inv, plan.f2, plan.f2inv, plan.ginv)
    pitch = n2 + SUBLANES
    spec_scratch = pltpu.VMEM((nb, kh * pitch, LANES), F32)
    return pl.pallas_call(
        functools.partial(_hyena_fused_kernel, n1h=n1h, n2=n2, unroll=min(n2 // 2, 32)),
        grid=(n_cb, b // nb),
        in_specs=[part(0), part(1), part(2), taps(0), taps(1), taps(2), bias(0), bias(1), bias(2),
                  pl.BlockSpec((skip.shape[0], 1, LANES), lambda c, bb: (0, 0, c)),
                  filt(0), filt(0), filt(1), filt(1)] + [full(a) for a in tables],
        out_specs=pl.BlockSpec((nb, seq_len, LANES), lambda c, bb: (bb, 0, c)),
        out_shape=jax.ShapeDtypeStruct((b, seq_len, ch), F32),
        scratch_shapes=[spec_scratch, spec_scratch, pltpu.VMEM((nb, n1h * pitch, LANES), F32),
                        pltpu.VMEM((3, nb, n1h * pitch, LANES), F32)],
        compiler_params=_params(("parallel", "arbitrary"), 56),
        name="hyena_fused",
    )(z, z, z, conv_w, conv_w, conv_w, conv_b, conv_b, conv_b,
      skip.reshape(skip.shape[0], 1, ch), kr, ki, kr, ki, *tables)


def _dft_shape(seq_len):
    n2 = 128
    return 2 * seq_len // n2, n2


def _block_cols(n2, rows, target_bytes=1 << 20):
    g = max(SUBLANES, min(n2, target_bytes // (rows * LANES * 4)) // SUBLANES * SUBLANES)
    while n2 % g:
        g -= SUBLANES
    return g


def _slabs_per_step(kh, slab_bytes, target_bytes=3 << 20):
    ks = max(1, min(kh, target_bytes // slab_bytes))
    while kh % ks:
        ks -= 1
    return ks


def _hyena(plan, z, col0, conv_w, conv_b, filt, skip):
    b, seq_len, _ = z.shape
    n_orders, ch = skip.shape
    g = _block_cols(plan.n2, plan.n1 // 2)
    ks_filt = _slabs_per_step(plan.kh, 2 * plan.n2 * LANES * 4, target_bytes=6 << 20)
    kr, ki = _filt_spec(plan, filt, n_orders, ks=ks_filt)
    nb = 2 if b % 2 == 0 else 1
    fused_bytes = 15 * nb * seq_len * LANES * 4
    if n_orders == 2 and fused_bytes <= FUSED_VMEM_BUDGET:
        return _hyena_fused(plan, z, col0, conv_w, conv_b, kr, ki, skip, nb=nb)
    uc = _sconv(z, conv_w, conv_b, rows=min(seq_len, 2048), col0=col0, n_parts=n_orders + 1)
    skip3 = skip.reshape(n_orders, 1, ch)
    cur, cur_part = uc, 0
    for order in range(n_orders):
        bre, bim = _conv_front(plan, cur, cur_part, kr, ki, order, ks=ks_filt)
        cur = _dft_inv(plan, bre, bim, cur, cur_part, uc, order + 1, skip3, order, g=g)
        cur_part = 0
    return cur[0]


def _trunk(x3, p, plan, filt):
    b, seq_len, d = x3.shape
    t = b * seq_len
    x = x3.reshape(t, d)
    x, z = _ffn_in(x, p["g1pre"], p["g1post"], p["w1gu"], p["w1d"], p["gmix"], p["w_in"], tm=512)
    z = z.reshape(b, seq_len, -1)
    d_pool = p["pool_w"].shape[0] * LANES
    y_pool = _pool(z, p["pool_w"], p["pool_scale"], rows=min(seq_len, 2048))
    y_hy = _hyena(plan, z, d_pool, p["conv_w"], p["conv_b"], filt, p["skip"])
    x = _ffn_out(x, y_pool.reshape(t, -1), y_hy.reshape(t, -1), p["gpool"], p["ghy"], p["w_out"],
                 p["gmixpost"], p["g2pre"], p["g2post"], p["w2gu"], p["w2d"], tm=512)
    return x.reshape(b, seq_len, d)


def kernel(x_prompt, x_sample, ffn1_norm_pre, ffn1_norm_post, ffn1_w_gate_up, ffn1_w_down, mix_norm_pre, w_in, pool_w_map, pool_scale, hyena_conv_w, hyena_conv_b, filt_w_first, filt_b_first, filt_w_hidden, filt_b_hidden, filt_w_last, filt_freq, hyena_skip, pool_out_norm, hyena_out_norm, w_out, mix_norm_post, ffn2_norm_pre, ffn2_norm_post, ffn2_w_gate_up, ffn2_w_down):
    assert ffn1_norm_pre.shape[0] == 1, "single-layer trunk"
    row = lambda a: a[0].reshape(1, -1)
    p = dict(
        g1pre=row(ffn1_norm_pre), g1post=row(ffn1_norm_post),
        w1gu=ffn1_w_gate_up[0].astype(BF16), w1d=ffn1_w_down[0].astype(BF16),
        gmix=row(mix_norm_pre), w_in=w_in[0].astype(BF16),
        pool_w=pool_w_map[0].astype(BF16), pool_scale=row(pool_scale),
        conv_w=hyena_conv_w[0], conv_b=row(hyena_conv_b),
        skip=hyena_skip[0], gpool=row(pool_out_norm), ghy=row(hyena_out_norm),
        w_out=w_out[0].astype(BF16), gmixpost=row(mix_norm_post),
        g2pre=row(ffn2_norm_pre), g2post=row(ffn2_norm_post),
        w2gu=ffn2_w_gate_up[0].astype(BF16), w2d=ffn2_w_down[0].astype(BF16),
    )
    outs = []
    for x3 in (x_prompt, x_sample):
        seq_len = x3.shape[1]
        plan = _Plan(seq_len, *_dft_shape(seq_len))
        filt = _filt_gen(seq_len, filt_w_first[0], filt_b_first[0], filt_w_hidden[0],
                         filt_b_hidden[0], filt_w_last[0], filt_freq[0], rows=min(seq_len, 512),
                         n2=plan.n2)
        outs.append(_trunk(x3, p, plan, filt))
    return tuple(outs)
```

```python
import functools
import math

import jax
import jax.numpy as jnp
import numpy as np
from jax import lax
from jax.experimental import pallas as pl
from jax.experimental.pallas import tpu as pltpu

F32 = jnp.float32
BF16 = jnp.bfloat16
EPS = 1e-6
LANES = 128
SUBLANES = 8
MXU_DIM = 256
POOL_WINDOWS = (2, 4, 8, 16)
HALO = 8
DECAY_TARGET = 1e-2
FAST_DECAY_PCT = 0.3
SLOW_DECAY_PCT = 1.5
MAX_DECAY = math.log(DECAY_TARGET) / FAST_DECAY_PCT
MIN_DECAY = math.log(DECAY_TARGET) / SLOW_DECAY_PCT
FUSED_VMEM_BUDGET = 32 << 20


def _params(sem, vmem_mib):
    return pltpu.CompilerParams(dimension_semantics=sem, vmem_limit_bytes=vmem_mib << 20)


def _rms(x, g):
    inv = lax.rsqrt(jnp.mean(x * x, axis=-1, keepdims=True) + EPS)
    return (x * inv) * g


def _dot(a, b):
    return jnp.dot(a, b, preferred_element_type=F32)


def _split_bf16(a):
    hi = a.astype(BF16)
    return hi, (a - hi.astype(F32)).astype(BF16)


def _dot_split(a, b):
    (a_hi, a_lo), (b_hi, b_lo) = _split_bf16(a), _split_bf16(b)
    return _dot(jnp.concatenate([a_hi, a_hi, a_lo], axis=1), jnp.concatenate([b_hi, b_lo, b_hi], axis=0))


def _lanes(parts):
    return parts[0] if len(parts) == 1 else jnp.concatenate(parts, axis=1)


def _lane_block(a, i):
    return a[:, i * LANES:(i + 1) * LANES]


def _ff_chunks(d_ff, n_chunks=2):
    if d_ff % MXU_DIM:
        return (0, d_ff)
    tiles = d_ff // MXU_DIM
    return tuple(MXU_DIM * ((tiles * k + n_chunks - 1) // n_chunks) for k in range(n_chunks + 1))


def _half_step_ffn(x, gpre, gpost, wgu_ref, wd_ref):
    d_ff = wd_ref.shape[0]
    h = _rms(x, gpre).astype(BF16)
    acc = None
    bounds = _ff_chunks(d_ff)
    for lo, hi in zip(bounds[:-1], bounds[1:]):
        gate = _dot(h, wgu_ref[:, lo:hi])
        up = _dot(h, wgu_ref[:, d_ff + lo:d_ff + hi])
        act = (gate * jax.nn.sigmoid(gate) * up).astype(BF16)
        part = _dot(act, wd_ref[lo:hi, :])
        acc = part if acc is None else acc + part
    return x + 0.5 * _rms(acc, gpost)


def _ffn_in_kernel(x_ref, gpre_ref, gpost_ref, wgu_ref, wd_ref, gmix_ref, win_ref, o_ref, z_ref):
    x = _half_step_ffn(x_ref[...], gpre_ref[...], gpost_ref[...], wgu_ref, wd_ref)
    o_ref[...] = x
    z_ref[...] = _dot(_rms(x, gmix_ref[...]).astype(BF16), win_ref[...])


def _ffn_in(x, g_pre, g_post, wgu, wd, g_mix, w_in, *, tm):
    t, d = x.shape
    d_ff, d_in = wd.shape[0], w_in.shape[1]
    const = lambda i: (0, 0)
    row = lambda i: (i, 0)
    once = pl.Buffered(1)
    vec = pl.BlockSpec((1, d), const)
    return pl.pallas_call(
        _ffn_in_kernel,
        grid=(t // tm,),
        in_specs=[
            pl.BlockSpec((tm, d), row), vec, vec,
            pl.BlockSpec((d, 2 * d_ff), const, pipeline_mode=once),
            pl.BlockSpec((d_ff, d), const, pipeline_mode=once),
            vec,
            pl.BlockSpec((d, d_in), const, pipeline_mode=once),
        ],
        out_specs=[pl.BlockSpec((tm, d), row), pl.BlockSpec((tm, d_in), row)],
        out_shape=[jax.ShapeDtypeStruct((t, d), F32), jax.ShapeDtypeStruct((t, d_in), F32)],
        compiler_params=_params(("parallel",), 56),
        name="ffn_in",
    )(x, g_pre, g_post, wgu, wd, g_mix, w_in)


def _ffn_out_kernel(x_ref, yp_ref, yh_ref, gp_ref, gh_ref, wout_ref, gmix_ref,
                    gpre_ref, gpost_ref, wgu_ref, wd_ref, o_ref):
    d_pool = yp_ref.shape[1]
    yp = _rms(yp_ref[...], gp_ref[...]).astype(BF16)
    yh = _rms(yh_ref[...], gh_ref[...]).astype(BF16)
    y = _dot(yp, wout_ref[:d_pool, :]) + _dot(yh, wout_ref[d_pool:, :])
    x = x_ref[...] + _rms(y, gmix_ref[...])
    o_ref[...] = _half_step_ffn(x, gpre_ref[...], gpost_ref[...], wgu_ref, wd_ref)


def _ffn_out(x, yp, yh, gp, gh, w_out, g_mix, g_pre, g_post, wgu, wd, *, tm):
    t, d = x.shape
    d_ff, dp, dh = wd.shape[0], yp.shape[1], yh.shape[1]
    const = lambda i: (0, 0)
    row = lambda i: (i, 0)
    once = pl.Buffered(1)
    vec = lambda n: pl.BlockSpec((1, n), const)
    return pl.pallas_call(
        _ffn_out_kernel,
        grid=(t // tm,),
        in_specs=[
            pl.BlockSpec((tm, d), row), pl.BlockSpec((tm, dp), row), pl.BlockSpec((tm, dh), row),
            vec(dp), vec(dh),
            pl.BlockSpec((dp + dh, d), const, pipeline_mode=once),
            vec(d), vec(d), vec(d),
            pl.BlockSpec((d, 2 * d_ff), const, pipeline_mode=once),
            pl.BlockSpec((d_ff, d), const, pipeline_mode=once),
        ],
        out_specs=pl.BlockSpec((tm, d), row),
        out_shape=jax.ShapeDtypeStruct((t, d), F32),
        compiler_params=_params(("parallel",), 56),
        name="ffn_out",
    )(x, yp, yh, gp, gh, w_out, g_mix, g_pre, g_post, wgu, wd)


def _with_halo(prev_ref, main_ref, next_ref, i, n_tiles):
    prev = jnp.where(i > 0, prev_ref[0], 0.0)
    nxt = jnp.where(i < n_tiles - 1, next_ref[0], 0.0)
    return jnp.concatenate([prev, main_ref[0], nxt], axis=0)


def _pool_kernel(prev_ref, main_ref, next_ref, wmap_ref, scale_ref, o_ref, *, seq_len, n_tiles):
    i = pl.program_id(1)
    rows = main_ref.shape[1]
    n_ext = rows + 2 * HALO
    ext_all = _with_halo(prev_ref, main_ref, next_ref, i, n_tiles)
    pos = i * rows + lax.broadcasted_iota(jnp.int32, (rows, LANES), 0)
    for grp, window in enumerate(POOL_WINDOWS):
        lanes = slice(grp * LANES, (grp + 1) * LANES)
        ext = ext_all[:, lanes]
        ssum, w = ext + pltpu.roll(ext, 1, 0), 2
        while w < window:
            ssum, w = pltpu.roll(ssum, w // 2, 0) + pltpu.roll(ssum, n_ext - w // 2, 0), 2 * w
        u = main_ref[0, :, lanes]
        lo = jnp.clip(pos - window // 2, 0, seq_len)
        hi = jnp.clip(pos + (window - window // 2), 0, seq_len)
        d = ssum[HALO:HALO + rows] / (hi - lo).astype(F32) - u
        y = _dot(d.astype(BF16), wmap_ref[grp])
        o_ref[0, :, lanes] = y * scale_ref[:, lanes]


def _halo_specs(rows, width, seq_len, col_of):
    blocks_per_tile = rows // HALO
    last = seq_len // HALO - 1
    prev = pl.BlockSpec((1, HALO, width),
                        lambda b, i, c: (b, jnp.maximum(i * blocks_per_tile - 1, 0), col_of(c)))
    main = pl.BlockSpec((1, rows, width), lambda b, i, c: (b, i, col_of(c)))
    nxt = pl.BlockSpec((1, HALO, width),
                       lambda b, i, c: (b, jnp.minimum((i + 1) * blocks_per_tile, last), col_of(c)))
    return [prev, main, nxt]


def _pool(z, wmap, scale, *, rows):
    b, seq_len, _ = z.shape
    assert wmap.shape[0] == len(POOL_WINDOWS) and max(POOL_WINDOWS) <= 2 * HALO
    width = wmap.shape[0] * LANES
    n_tiles = seq_len // rows
    return pl.pallas_call(
        functools.partial(_pool_kernel, seq_len=seq_len, n_tiles=n_tiles),
        grid=(b, n_tiles, 1),
        in_specs=_halo_specs(rows, width, seq_len, lambda c: 0) + [
            pl.BlockSpec(wmap.shape, lambda b, i, c: (0, 0, 0)),
            pl.BlockSpec((1, width), lambda b, i, c: (0, 0)),
        ],
        out_specs=pl.BlockSpec((1, rows, width), lambda b, i, c: (b, i, 0)),
        out_shape=jax.ShapeDtypeStruct((b, seq_len, width), F32),
        compiler_params=_params(("parallel", "parallel", "parallel"), 40),
        name="pool",
    )(z, z, z, wmap, scale)


def _sconv_kernel(prev_ref, main_ref, next_ref, w_ref, b_ref, o_ref, *, n_tiles):
    i = pl.program_id(1)
    rows = main_ref.shape[1]
    ext = _with_halo(prev_ref, main_ref, next_ref, i, n_tiles)
    n_ext = rows + 2 * HALO
    before = pltpu.roll(ext, 1, 0)[HALO:HALO + rows]
    after = pltpu.roll(ext, n_ext - 1, 0)[HALO:HALO + rows]
    w = w_ref[...]
    out = before * w[0:1] + main_ref[0] * w[1:2] + after * w[2:3]
    o_ref[0, 0] = out + b_ref[...]


def _sconv(z, w, bias, *, rows, col0, n_parts):
    b, seq_len, _ = z.shape
    n_tiles = seq_len // rows
    width = w.shape[1] // n_parts
    assert col0 % width == 0
    return pl.pallas_call(
        functools.partial(_sconv_kernel, n_tiles=n_tiles),
        grid=(b, n_tiles, n_parts),
        in_specs=_halo_specs(rows, width, seq_len, lambda c: c + col0 // width) + [
            pl.BlockSpec((w.shape[0], width), lambda b, i, c: (0, c)),
            pl.BlockSpec((1, width), lambda b, i, c: (0, c)),
        ],
        out_specs=pl.BlockSpec((1, 1, rows, width), lambda b, i, c: (c, b, i, 0)),
        out_shape=jax.ShapeDtypeStruct((n_parts, b, seq_len, width), F32),
        compiler_params=_params(("parallel", "parallel", "parallel"), 40),
        name="sconv",
    )(z, z, z, w, bias)


def _filt_gen_kernel(bands_ref, wt_ref, wc_ref, ws_ref, b1_ref, wh_ref, bh_ref, freq_ref,
                     wl_ref, delta_ref, o_ref, *, seq_len, d_ch):
    rows = o_ref.shape[0] * (o_ref.shape[1] - SUBLANES)
    base = pl.program_id(0) * rows
    m_lane = (base + lax.broadcasted_iota(jnp.int32, (1, rows), 1)).astype(F32)
    t_lane = m_lane / (seq_len - 1.0)
    ang = (bands_ref[...] * (2.0 * math.pi / seq_len)) * m_lane
    freq = freq_ref[...]
    pre = (_dot_split(wc_ref[...], jnp.cos(ang)) + _dot_split(ws_ref[...], -jnp.sin(ang))
           + wt_ref[...] * t_lane + b1_ref[...])
    h = jnp.sin(freq * pre)
    for layer in range(wh_ref.shape[0]):
        pre = _dot_split(wh_ref[layer], h) + bh_ref[layer]
        h = jnp.sin(freq * pre)
    (h_hi, h_lo), (w_hi, w_lo) = _split_bf16(h), _split_bf16(wl_ref[...])
    out = _dot(jnp.concatenate([h_hi, h_hi, h_lo], axis=0).T,
               jnp.concatenate([w_hi, w_lo, w_hi], axis=0))
    m_row = base + lax.broadcasted_iota(jnp.int32, (rows, d_ch), 0)
    t_row = m_row.astype(F32) / (seq_len - 1.0)
    decay = jnp.exp(-t_row * jnp.abs(delta_ref[...]))
    decay_bwd = jnp.where(m_row == 0, 0.0, decay)
    n_chunks, pitch, _ = o_ref.shape
    n2 = rows // n_chunks
    for q in range(out.shape[1] // d_ch):
        dq = decay_bwd if q % 2 == 1 else decay
        val = out[:, q * d_ch:(q + 1) * d_ch] * dq
        for n in range(n_chunks):
            o_ref[n, :n2, q * d_ch:(q + 1) * d_ch] = val[n * n2:(n + 1) * n2]
    o_ref[:, n2:, :] = jnp.zeros((n_chunks, pitch - n2, o_ref.shape[2]), F32)


def _filt_gen(seq_len, w_first, b_first, w_hidden, b_hidden, w_last, freq, *, rows, n2):
    pos_bands = (w_first.shape[0] - 1) // 2
    hidden = w_first.shape[1]
    n_cols = w_last.shape[1]
    d_ch = n_cols // 4
    bands = jnp.linspace(1e-4, pos_bands - 1, pos_bands, dtype=F32).reshape(pos_bands, 1)
    deltas = jnp.linspace(MIN_DECAY, MAX_DECAY, d_ch, dtype=F32).reshape(1, d_ch)
    w1t = w_first.T
    args = (bands, w1t[:, 0:1], w1t[:, 1:1 + pos_bands], w1t[:, 1 + pos_bands:],
            b_first.reshape(hidden, 1), jnp.swapaxes(w_hidden, 1, 2),
            b_hidden.reshape(b_hidden.shape[0], hidden, 1), freq.reshape(hidden, 1),
            w_last, deltas)
    full = lambda a: pl.BlockSpec(a.shape, lambda i, _n=a.ndim: (0,) * _n)
    return pl.pallas_call(
        functools.partial(_filt_gen_kernel, seq_len=seq_len, d_ch=d_ch),
        grid=(seq_len // rows,),
        in_specs=[full(a) for a in args],
        out_specs=pl.BlockSpec((rows // n2, n2 + SUBLANES, n_cols), lambda i: (i, 0, 0)),
        out_shape=jax.ShapeDtypeStruct((seq_len // n2, n2 + SUBLANES, n_cols), F32),
        compiler_params=_params(("parallel",), 40),
        name="filt_gen",
    )(*args)


class _Plan:
    def __init__(self, seq_len, n1, n2):
        assert n1 * n2 == 2 * seq_len and n1 % (2 * SUBLANES) == 0 and n2 % SUBLANES == 0
        self.seq_len, self.n1, self.n2 = seq_len, n1, n2
        n = n1 * n2
        n1h = n1 // 2
        self.kh = kh = n1h + 1
        k1 = np.arange(kh, dtype=np.float64)[:, None]
        m1 = np.arange(n1h, dtype=np.float64)[None, :]
        th1 = 2.0 * np.pi * k1 * m1 / n1
        f1 = np.zeros((n1 + SUBLANES, n1h))
        f1[:n1h] = np.cos(th1[:n1h]); f1[n1h:n1] = -np.sin(th1[:n1h]); f1[n1] = np.cos(th1[n1h])
        self.f1 = jnp.asarray(f1, F32)
        m2 = np.arange(n2, dtype=np.float64)[None, :]
        tht = 2.0 * np.pi * k1 * m2 / n
        twr, twi = np.cos(tht), -np.sin(tht)
        rep = lambda a: np.repeat(a[:, :, None], LANES, axis=2)
        self.twr_inv, self.twi_inv = jnp.asarray(rep(twr), F32), jnp.asarray(rep(twi), F32)
        pad = np.zeros((n2, SUBLANES - 1))
        self.twr_fwd = jnp.asarray(rep(np.concatenate([twr.T, pad], 1)), F32)
        self.twi_fwd = jnp.asarray(rep(np.concatenate([twi.T, pad], 1)), F32)
        k2 = np.arange(n2, dtype=np.float64)[:, None]
        th2 = 2.0 * np.pi * k2 * m2 / n2
        f2r, f2i = np.cos(th2), -np.sin(th2)
        self.f2 = jnp.asarray(np.block([[f2r, -f2i], [f2i, f2r]]), F32)
        self.f2inv = jnp.asarray(np.block([[f2r, f2i], [-f2i, f2r]]), F32)
        wgt = np.where(k1[:n1h] == 0, 1.0, 2.0) / n
        ginv = np.concatenate([(wgt * np.cos(th1[:n1h])).T, (-wgt * np.sin(th1[:n1h])).T], axis=1)
        self.ginv = jnp.asarray(ginv, F32)
        k1f = np.arange(kh, dtype=np.float64)[:, None]
        full = lambda cols: 2.0 * np.pi * k1f * cols[None, :] / n1
        stage1 = lambda th: np.concatenate(
            [np.cos(th[:n1h]), -np.sin(th[:n1h]), np.cos(th[n1h:]), np.zeros((SUBLANES - 1, th.shape[1]))], 0)
        fwd_cols = np.arange(n1h, dtype=np.float64)
        f1a = stage1(full(np.concatenate([fwd_cols, n1 - 1 - fwd_cols])))
        f1b = stage1(full(np.concatenate([fwd_cols, (n1 - fwd_cols) % n1])))
        f1b[:, n1h] = 0.0
        self.f1_filt, self.f1_filt0 = jnp.asarray(f1a, F32), jnp.asarray(f1b, F32)
        rows = lambda t: rep(np.concatenate([t.T, np.zeros((t.shape[1], SUBLANES - 1))], 1))
        coarse, fine = tht[:, ::SUBLANES], tht[:, :SUBLANES]
        self.tw_coarse = (jnp.asarray(rows(np.cos(coarse)), F32), jnp.asarray(rows(-np.sin(coarse)), F32))
        self.tw_fine = (jnp.asarray(rows(np.cos(fine)), F32), jnp.asarray(rows(-np.sin(fine)), F32))
        self.tw_slab_coarse = (jnp.asarray(rep(np.cos(coarse)), F32), jnp.asarray(rep(-np.sin(coarse)), F32))
        self.tw_slab_fine = (jnp.asarray(rep(np.cos(fine)), F32), jnp.asarray(rep(-np.sin(fine)), F32))


def _stack_bf16(re, im):
    return jnp.concatenate([re.astype(BF16), im.astype(BF16)], axis=0)


def _filt_spec_kernel(hf_ref, hba_ref, hbb_ref, f1_ref, f1j0_ref, twcr_ref, twci_ref, twfr_ref, twfi_ref, f2_ref,
                      kr_ref, ki_ref, are_ref, aim_ref, *, n1h, n2, jg):
    pitch = n2 + SUBLANES
    n_j = n2 // jg
    ks = kr_ref.shape[0]
    phase = pl.program_id(1)
    hf, hba, hbb = (r.reshape(n1h * jg, LANES) for r in (hf_ref, hba_ref, hbb_ref))

    @pl.when(phase < n_j)
    def _():
        f1 = f1_ref[...].astype(BF16)
        f1_first = jnp.where(phase == 0, f1j0_ref[...], f1_ref[...]).astype(BF16)
        j0 = phase * jg
        a0 = lax.shift_right_logical(j0, 3)

        def stage1(mat, ts):
            rows = lambda ref, start: ref[pl.ds(start, n1h, stride=jg), :]
            x = _lanes([jnp.concatenate([rows(hf, t), rows(hbb, 0) if t == 0 else rows(hba, jg - t)], axis=0)
                        for t in ts])
            p = _dot(mat, x.astype(BF16))
            cr = _lanes([twcr_ref[a0 + t // SUBLANES] for t in ts])
            ci = _lanes([twci_ref[a0 + t // SUBLANES] for t in ts])
            fr = _lanes([twfr_ref[t % SUBLANES] for t in ts])
            fi = _lanes([twfi_ref[t % SUBLANES] for t in ts])
            twr, twi = cr * fr - ci * fi, cr * fi + ci * fr
            pr, pi, pn = p[:n1h], p[n1h:2 * n1h], p[2 * n1h:2 * n1h + 1]
            re, im = pr * twr[:n1h] - pi * twi[:n1h], pr * twi[:n1h] + pi * twr[:n1h]
            re_n, im_n = pn * twr[n1h:n1h + 1], pn * twi[n1h:n1h + 1]
            for i, t in enumerate(ts):
                are_ref[pl.ds(j0 + t, n1h, stride=pitch), :] = _lane_block(re, i)
                aim_ref[pl.ds(j0 + t, n1h, stride=pitch), :] = _lane_block(im, i)
                are_ref[pl.ds(n1h * pitch + j0 + t, 1), :] = _lane_block(re_n, i)
                aim_ref[pl.ds(n1h * pitch + j0 + t, 1), :] = _lane_block(im_n, i)

        stage1(f1_first, (0,))
        for t in range(1, jg - 1, 2):
            stage1(f1, (t, t + 1))
        stage1(f1, (jg - 1,))

    @pl.when(phase >= n_j)
    def _():
        f2 = f2_ref[...].astype(BF16)
        kc = phase - n_j

        def slabs(first, count):
            rows = [pl.ds(pl.multiple_of((kc * ks + first + i) * pitch, SUBLANES), n2) for i in range(count)]
            x = _dot(f2, _stack_bf16(_lanes([are_ref[r, :] for r in rows]), _lanes([aim_ref[r, :] for r in rows])))
            for i in range(count):
                kr_ref[first + i] = _lane_block(x[:n2], i).astype(kr_ref.dtype)
                ki_ref[first + i] = _lane_block(x[n2:], i).astype(ki_ref.dtype)

        def pair(p, carry):
            slabs(2 * p, 2)
            return carry

        lax.fori_loop(0, ks // 2, pair, 0, unroll=8)
        if ks % 2:
            slabs(ks - 1, 1)


def _filt_spec(plan, filt, n_orders, *, ks):
    n1h, n2, kh = plan.n1 // 2, plan.n2, plan.kh
    cols = filt.shape[2]
    n_cb = cols // (2 * n_orders * LANES)
    jg = _block_cols(n2, n1h)
    n_j = n2 // jg
    pitch = n2 + SUBLANES
    col = lambda c, back: (c // n_cb) * 2 * n_cb + back * n_cb + c % n_cb
    step = lambda s: jnp.minimum(s, n_j - 1)
    seq = lambda back, blk: pl.BlockSpec((n1h, jg, LANES), lambda c, s: (0, blk(step(s)), col(c, back)))
    full = lambda a: pl.BlockSpec(a.shape, lambda c, s, _n=a.ndim: (0,) * _n, pipeline_mode=pl.Buffered(1))
    tables = (plan.f1_filt, plan.f1_filt0, *plan.tw_coarse, *plan.tw_fine, plan.f2)
    out = jax.ShapeDtypeStruct((kh, n2, cols // 2), BF16)
    scratch = pltpu.VMEM((kh * pitch, LANES), F32)
    return pl.pallas_call(
        functools.partial(_filt_spec_kernel, n1h=n1h, n2=n2, jg=jg),
        grid=(n_orders * n_cb, n_j + kh // ks),
        in_specs=[seq(0, lambda s: s), seq(1, lambda s: n_j - 1 - s), seq(1, lambda s: (n_j - s) % n_j)]
        + [full(a) for a in tables],
        out_specs=[pl.BlockSpec((ks, n2, LANES), lambda c, s: (jnp.maximum(s - n_j, 0), 0, c))] * 2,
        out_shape=[out, out],
        scratch_shapes=[scratch, scratch],
        compiler_params=_params(("parallel", "arbitrary"), 48),
        name="filt_spec",
    )(filt, filt, filt, *tables)


def _conv_front_kernel(x_ref, kr_ref, ki_ref, f1_ref, twcr_ref, twci_ref, twfr_ref, twfi_ref,
                       icr_ref, ici_ref, ifr_ref, ifi_ref, f2_ref, f2inv_ref,
                       ore_ref, oim_ref, are_ref, aim_ref, *, n1h, n2, unroll):
    pitch = n2 + SUBLANES
    ks = kr_ref.shape[0]
    kc = pl.program_id(2)
    x2 = x_ref.at[0, 0]

    @pl.when(kc == 0)
    def _():
        f1 = f1_ref[...].astype(BF16)

        def body(jp, carry):
            js = (2 * jp, 2 * jp + 1)
            a = lax.shift_right_logical(jp, 2)
            bs = [2 * jnp.bitwise_and(jp, SUBLANES // 2 - 1) + q for q in range(2)]
            xb = _lanes([x2[pl.ds(j, n1h, stride=n2), :] for j in js]).astype(BF16)
            p = _dot(f1, xb)
            cr, ci = jnp.tile(twcr_ref[a], (1, 2)), jnp.tile(twci_ref[a], (1, 2))
            fr, fi = _lanes([twfr_ref[b] for b in bs]), _lanes([twfi_ref[b] for b in bs])
            twr, twi = cr * fr - ci * fi, cr * fi + ci * fr
            pr, pi, pn = p[:n1h], p[n1h:2 * n1h], p[2 * n1h:2 * n1h + 1]
            re, im = pr * twr[:n1h] - pi * twi[:n1h], pr * twi[:n1h] + pi * twr[:n1h]
            re_n, im_n = pn * twr[n1h:n1h + 1], pn * twi[n1h:n1h + 1]
            for i, j in enumerate(js):
                are_ref[pl.ds(j, n1h, stride=pitch), :] = _lane_block(re, i)
                aim_ref[pl.ds(j, n1h, stride=pitch), :] = _lane_block(im, i)
                are_ref[pl.ds(n1h * pitch + j, 1), :] = _lane_block(re_n, i)
                aim_ref[pl.ds(n1h * pitch + j, 1), :] = _lane_block(im_n, i)
            return carry

        lax.fori_loop(0, n2 // 2, body, 0, unroll=unroll)

    f2, f2inv = f2_ref[...].astype(BF16), f2inv_ref[...].astype(BF16)

    def slab_twiddle(k):
        cr, ci, fr, fi = icr_ref[k], ici_ref[k], ifr_ref[k], ifi_ref[k]
        pieces = [(cr[a:a + 1] * fr - ci[a:a + 1] * fi, cr[a:a + 1] * fi + ci[a:a + 1] * fr)
                  for a in range(n2 // SUBLANES)]
        return (jnp.concatenate([p[0] for p in pieces], axis=0), jnp.concatenate([p[1] for p in pieces], axis=0))

    def slabs(first, count):
        ks_abs = [kc * ks + first + i for i in range(count)]
        rows = [pl.ds(pl.multiple_of(k * pitch, SUBLANES), n2) for k in ks_abs]
        x = _dot(f2, _stack_bf16(_lanes([are_ref[r, :] for r in rows]), _lanes([aim_ref[r, :] for r in rows])))
        xr, xi = x[:n2], x[n2:]
        kr = _lanes([kr_ref[first + i] for i in range(count)]).astype(F32)
        ki = _lanes([ki_ref[first + i] for i in range(count)]).astype(F32)
        y = _dot(f2inv, _stack_bf16(xr * kr - xi * ki, xr * ki + xi * kr))
        br, bi = y[:n2], y[n2:]
        tw = [slab_twiddle(k) for k in ks_abs]
        twr, twi = _lanes([t[0] for t in tw]), _lanes([t[1] for t in tw])
        ore, oim = br * twr + bi * twi, bi * twr - br * twi
        for i in range(count):
            ore_ref[0, first + i] = _lane_block(ore, i)
            oim_ref[0, first + i] = _lane_block(oim, i)

    def pair(p, carry):
        slabs(2 * p, 2)
        return carry

    lax.fori_loop(0, ks // 2, pair, 0, unroll=8)
    if ks % 2:
        slabs(ks - 1, 1)


def _conv_front(plan, x4, part, kr, ki, order, *, ks):
    _, b, seq_len, ch = x4.shape
    n1h, n2, kh = plan.n1 // 2, plan.n2, plan.kh
    n_cb = ch // LANES
    pitch = n2 + SUBLANES
    once = pl.Buffered(1)
    full = lambda a: pl.BlockSpec(a.shape, lambda bb, c, k, _n=a.ndim: (0,) * _n, pipeline_mode=once)
    filt = pl.BlockSpec((ks, n2, LANES), lambda bb, c, k: (k, 0, order * n_cb + c))
    data = pl.BlockSpec((1, ks, n2, LANES), lambda bb, c, k: (bb, k, 0, c))
    tables = (plan.f1, *plan.tw_coarse, *plan.tw_fine, *plan.tw_slab_coarse, *plan.tw_slab_fine,
              plan.f2, plan.f2inv)
    out = jax.ShapeDtypeStruct((b, kh, n2, ch), F32)
    scratch = pltpu.VMEM((kh * pitch, LANES), F32)
    return pl.pallas_call(
        functools.partial(_conv_front_kernel, n1h=n1h, n2=n2, unroll=4),
        grid=(b, n_cb, kh // ks),
        in_specs=[pl.BlockSpec((1, 1, seq_len, LANES), lambda bb, c, k: (part, bb, 0, c)),
                  filt, filt] + [full(a) for a in tables],
        out_specs=[data, data],
        out_shape=[out, out],
        scratch_shapes=[scratch, scratch],
        compiler_params=_params(("parallel", "parallel", "arbitrary"), 60),
        name="conv_front",
    )(x4, kr, ki, *tables)


def _dft_inv_kernel(bre_ref, bim_ref, x_ref, gate_ref, skip_ref, ginv_ref, o_ref, *, unroll, scale):
    _, _, n1h, g, _ = o_ref.shape
    b_re = bre_ref.at[0].reshape((n1h + 1) * g, LANES)
    b_im = bim_ref.at[0].reshape((n1h + 1) * g, LANES)
    x2 = x_ref.at[0, 0].reshape(n1h * g, LANES)
    gate2 = gate_ref.at[0, 0].reshape(n1h * g, LANES)
    o2 = o_ref.at[0, 0].reshape(n1h * g, LANES)
    ginv = ginv_ref[...].astype(BF16)
    skip = skip_ref[0]
    n1 = lax.broadcasted_iota(jnp.int32, (n1h, 2 * LANES), 0)
    nyq_w = jnp.where(jnp.bitwise_and(n1, 1) == 0, scale, -scale)
    skip = jnp.tile(skip, (1, 2))

    def body(jp, carry):
        rows = [pl.ds(2 * jp + i, n1h, stride=g) for i in range(2)]
        spec = _stack_bf16(_lanes([b_re[r, :] for r in rows]), _lanes([b_im[r, :] for r in rows]))
        nyq = _lanes([b_re[pl.ds(n1h * g + 2 * jp + i, 1), :] for i in range(2)])
        y = _dot(ginv, spec) + nyq_w * nyq
        out = _lanes([gate2[r, :] for r in rows]) * (y + _lanes([x2[r, :] for r in rows]) * skip)
        for i, r in enumerate(rows):
            o2[r, :] = _lane_block(out, i)
        return carry

    lax.fori_loop(0, g // 2, body, 0, unroll=unroll)


def _dft_inv(plan, bre, bim, x4, x_part, gate4, gate_part, skip, order, *, g):
    b, seq_len, ch = x4.shape[1:]
    n1h, n2, kh = plan.n1 // 2, plan.n2, plan.kh
    xv = x4.reshape(x4.shape[0], b, n1h, n2, ch)
    gv = gate4.reshape(gate4.shape[0], b, n1h, n2, ch)
    spec = pl.BlockSpec((1, kh, g, LANES), lambda j, bb, c: (bb, 0, j, c))
    time = lambda part: pl.BlockSpec((1, 1, n1h, g, LANES), lambda j, bb, c: (part, bb, 0, j, c))
    out = pl.pallas_call(
        functools.partial(_dft_inv_kernel, unroll=min(g // 2, 8), scale=1.0 / (plan.n1 * n2)),
        grid=(n2 // g, b, ch // LANES),
        in_specs=[
            spec, spec, time(x_part), time(gate_part),
            pl.BlockSpec((1, 1, LANES), lambda j, bb, c: (order, 0, c)),
            pl.BlockSpec(plan.ginv.shape, lambda j, bb, c: (0, 0)),
        ],
        out_specs=time(0),
        out_shape=jax.ShapeDtypeStruct((1, b, n1h, n2, ch), F32),
        compiler_params=_params(("parallel", "parallel", "parallel"), 48),
        name="dft_inv",
    )(bre, bim, xv, gv, skip, plan.ginv)
    return out.reshape(1, b, seq_len, ch)


def _hyena_fused_kernel(zv_ref, zx1_ref, zx2_ref, wv_ref, wx1_ref, wx2_ref, bv_ref, bx1_ref, bx2_ref,
                        skip_ref, kr0_ref, ki0_ref, kr1_ref, ki1_ref,
                        f1_ref, twrf_ref, twif_ref, twri_ref, twii_ref, f2_ref, f2inv_ref, ginv_ref,
                        o_ref, are_ref, aim_ref, cur_ref, tin_ref, *, n1h, n2, unroll):
    nb = o_ref.shape[0]
    pitch = n2 + SUBLANES
    f1 = f1_ref[...].astype(BF16)
    f2, f2inv = f2_ref[...].astype(BF16), f2inv_ref[...].astype(BF16)
    ginv = ginv_ref[...].astype(BF16)
    scale = 1.0 / (2 * n1h * n2)
    n1 = lax.broadcasted_iota(jnp.int32, (n1h, 2 * nb * LANES), 0)
    nyq_w = jnp.where(jnp.bitwise_and(n1, 1) == 0, scale, -scale)
    lanes, lane_block = _lanes, _lane_block
    wide = lambda t, reps: t if reps == 1 else jnp.tile(t, (1, reps))

    row = lax.broadcasted_iota(jnp.int32, (n2, LANES), 0)
    for part, (z_ref, w_ref, b_ref) in enumerate(((zv_ref, wv_ref, bv_ref), (zx1_ref, wx1_ref, bx1_ref),
                                                   (zx2_ref, wx2_ref, bx2_ref))):
        w, bias = w_ref[...], b_ref[...]
        for i in range(nb):
            for n in range(n1h):
                lo = n * n2
                mid = z_ref[i, lo:lo + n2, :]
                if n > 0:
                    before = z_ref[i, lo - 1:lo + n2 - 1, :]
                else:
                    before = jnp.where(row == 0, 0.0, pltpu.roll(mid, 1, 0))
                if n < n1h - 1:
                    after = z_ref[i, lo + 1:lo + n2 + 1, :]
                else:
                    after = jnp.where(row == n2 - 1, 0.0, pltpu.roll(mid, n2 - 1, 0))
                tin_ref[part, i, n * pitch:n * pitch + n2, :] = (
                    before * w[0:1] + mid * w[1:2] + after * w[2:3] + bias)
    stages = ((tin_ref.at[0], tin_ref.at[1], cur_ref, kr0_ref, ki0_ref),
              (cur_ref, tin_ref.at[2], tin_ref.at[0], kr1_ref, ki1_ref))
    for order, (src, gate, dst, kr_ref, ki_ref) in enumerate(stages):
        skip = wide(skip_ref[order], 2 * nb)

        def stage1(jp, carry):
            cols = [(2 * jp + q, i) for q in range(2) for i in range(nb)]
            xb = lanes([src[i, pl.ds(j, n1h, stride=pitch), :] for j, i in cols]).astype(BF16)
            p = _dot(f1, xb)
            twr = lanes([wide(twrf_ref[2 * jp + q], nb) for q in range(2)])
            twi = lanes([wide(twif_ref[2 * jp + q], nb) for q in range(2)])
            pr, pi, pn = p[:n1h], p[n1h:2 * n1h], p[2 * n1h:2 * n1h + 1]
            re, im = pr * twr[:n1h] - pi * twi[:n1h], pr * twi[:n1h] + pi * twr[:n1h]
            re_n, im_n = pn * twr[n1h:n1h + 1], pn * twi[n1h:n1h + 1]
            for c, (j, i) in enumerate(cols):
                are_ref[i, pl.ds(j, n1h, stride=pitch), :] = lane_block(re, c)
                aim_ref[i, pl.ds(j, n1h, stride=pitch), :] = lane_block(im, c)
                are_ref[i, pl.ds(n1h * pitch + j, 1), :] = lane_block(re_n, c)
                aim_ref[i, pl.ds(n1h * pitch + j, 1), :] = lane_block(im_n, c)
            return carry

        lax.fori_loop(0, n2 // 2, stage1, 0, unroll=unroll)

        def slabs(ks):
            rows = [pl.ds(pl.multiple_of(k * pitch, SUBLANES), n2) for k in ks]
            a = _stack_bf16(lanes([are_ref[i, r, :] for r in rows for i in range(nb)]),
                            lanes([aim_ref[i, r, :] for r in rows for i in range(nb)]))
            x = _dot(f2, a)
            xr, xi = x[:n2], x[n2:]
            per_slab = lambda ref: lanes([wide(ref[k], nb) for k in ks])
            kr, ki = per_slab(kr_ref).astype(F32), per_slab(ki_ref).astype(F32)
            y = _dot(f2inv, _stack_bf16(xr * kr - xi * ki, xr * ki + xi * kr))
            br, bi = y[:n2], y[n2:]
            twr, twi = per_slab(twri_ref), per_slab(twii_ref)
            ore, oim = br * twr + bi * twi, bi * twr - br * twi
            for s, r in enumerate(rows):
                for i in range(nb):
                    are_ref[i, r, :] = lane_block(ore, s * nb + i)
                    aim_ref[i, r, :] = lane_block(oim, s * nb + i)

        def slab(k, carry):
            slabs((k,))
            return carry

        lax.fori_loop(0, n1h + 1, slab, 0, unroll=True)

        def stage1_inv(jp, carry):
            cols = [(pl.ds(2 * jp + q, n1h, stride=pitch), 2 * jp + q, i) for q in range(2) for i in range(nb)]
            spec = _stack_bf16(lanes([are_ref[i, r, :] for r, _, i in cols]),
                               lanes([aim_ref[i, r, :] for r, _, i in cols]))
            nyq = lanes([are_ref[i, pl.ds(n1h * pitch + j, 1), :] for _, j, i in cols])
            y = _dot(ginv, spec) + nyq_w * nyq
            x = lanes([src[i, r, :] for r, _, i in cols])
            g = lanes([gate[i, r, :] for r, _, i in cols])
            out = g * (y + x * skip)
            for c, (r, _, i) in enumerate(cols):
                dst[i, r, :] = lane_block(out, c)
            return carry

        lax.fori_loop(0, n2 // 2, stage1_inv, 0, unroll=unroll)

    def unpitch(n, carry):
        for i in range(nb):
            o_ref[i, pl.ds(pl.multiple_of(n * n2, n2), n2), :] = (
                tin_ref[0, i, pl.ds(pl.multiple_of(n * pitch, SUBLANES), n2), :])
        return carry

    lax.fori_loop(0, n1h, unpitch, 0)


def _hyena_fused(plan, z, col0, conv_w, conv_b, kr, ki, skip, *, nb):
    b, seq_len, _ = z.shape
    ch = skip.shape[1]
    n1h, n2, kh = plan.n1 // 2, plan.n2, plan.kh
    n_cb = ch // LANES
    cb0 = col0 // LANES
    part = lambda p: pl.BlockSpec((nb, seq_len, LANES), lambda c, bb: (bb, 0, cb0 + p * n_cb + c))
    taps = lambda p: pl.BlockSpec((conv_w.shape[0], LANES), lambda c, bb: (0, p * n_cb + c))
    bias = lambda p: pl.BlockSpec((1, LANES), lambda c, bb: (0, p * n_cb + c))
    once = pl.Buffered(1)
    filt = lambda o: pl.BlockSpec((kh, n2, LANES), lambda c, bb: (0, 0, o * n_cb + c), pipeline_mode=once)
    full = lambda a: pl.BlockSpec(a.shape, lambda c, bb, _n=a.ndim: (0,) * _n, pipeline_mode=once)
    tables = (plan.f1, plan.twr_fwd, plan.twi_fwd, plan.twr_inv, plan.twi_inv, plan.f2, plan.f2inv, plan.ginv)
    pitch = n2 + SUBLANES
    spec_scratch = pltpu.VMEM((nb, kh * pitch, LANES), F32)
    return pl.pallas_call(
        functools.partial(_hyena_fused_kernel, n1h=n1h, n2=n2, unroll=min(n2 // 2, 32)),
        grid=(n_cb, b // nb),
        in_specs=[part(0), part(1), part(2), taps(0), taps(1), taps(2), bias(0), bias(1), bias(2),
                  pl.BlockSpec((skip.shape[0], 1, LANES), lambda c, bb: (0, 0, c)),
                  filt(0), filt(0), filt(1), filt(1)] + [full(a) for a in tables],
        out_specs=pl.BlockSpec((nb, seq_len, LANES), lambda c, bb: (bb, 0, c)),
        out_shape=jax.ShapeDtypeStruct((b, seq_len, ch), F32),
        scratch_shapes=[spec_scratch, spec_scratch, pltpu.VMEM((nb, n1h * pitch, LANES), F32),
                        pltpu.VMEM((3, nb, n1h * pitch, LANES), F32)],
        compiler_params=_params(("parallel", "arbitrary"), 56),
        name="hyena_fused",
    )(z, z, z, conv_w, conv_w, conv_w, conv_b, conv_b, conv_b,
      skip.reshape(skip.shape[0], 1, ch), kr, ki, kr, ki, *tables)


def _dft_shape(seq_len):
    n2 = 128
    return 2 * seq_len // n2, n2


def _block_cols(n2, rows, target_bytes=1 << 20):
    g = max(SUBLANES, min(n2, target_bytes // (rows * LANES * 4)) // SUBLANES * SUBLANES)
    while n2 % g:
        g -= SUBLANES
    return g


def _slabs_per_step(kh, slab_bytes, target_bytes=3 << 20):
    ks = max(1, min(kh, target_bytes // slab_bytes))
    while kh % ks:
        ks -= 1
    return ks


def _hyena(plan, z, col0, conv_w, conv_b, filt, skip):
    b, seq_len, _ = z.shape
    n_orders, ch = skip.shape
    g = _block_cols(plan.n2, plan.n1 // 2)
    ks_filt = _slabs_per_step(plan.kh, 2 * plan.n2 * LANES * 4, target_bytes=6 << 20)
    kr, ki = _filt_spec(plan, filt, n_orders, ks=ks_filt)
    nb = 2 if b % 2 == 0 else 1
    fused_bytes = 15 * nb * seq_len * LANES * 4
    if n_orders == 2 and fused_bytes <= FUSED_VMEM_BUDGET:
        return _hyena_fused(plan, z, col0, conv_w, conv_b, kr, ki, skip, nb=nb)
    uc = _sconv(z, conv_w, conv_b, rows=min(seq_len, 2048), col0=col0, n_parts=n_orders + 1)
    skip3 = skip.reshape(n_orders, 1, ch)
    cur, cur_part = uc, 0
    for order in range(n_orders):
        bre, bim = _conv_front(plan, cur, cur_part, kr, ki, order, ks=ks_filt)
        cur = _dft_inv(plan, bre, bim, cur, cur_part, uc, order + 1, skip3, order, g=g)
        cur_part = 0
    return cur[0]


def _trunk(x3, p, plan, filt):
    b, seq_len, d = x3.shape
    t = b * seq_len
    x = x3.reshape(t, d)
    x, z = _ffn_in(x, p["g1pre"], p["g1post"], p["w1gu"], p["w1d"], p["gmix"], p["w_in"], tm=512)
    z = z.reshape(b, seq_len, -1)
    d_pool = p["pool_w"].shape[0] * LANES
    y_pool = _pool(z, p["pool_w"], p["pool_scale"], rows=min(seq_len, 2048))
    y_hy = _hyena(plan, z, d_pool, p["conv_w"], p["conv_b"], filt, p["skip"])
    x = _ffn_out(x, y_pool.reshape(t, -1), y_hy.reshape(t, -1), p["gpool"], p["ghy"], p["w_out"],
                 p["gmixpost"], p["g2pre"], p["g2post"], p["w2gu"], p["w2d"], tm=512)
    return x.reshape(b, seq_len, d)


def kernel(x_prompt, x_sample, ffn1_norm_pre, ffn1_norm_post, ffn1_w_gate_up, ffn1_w_down, mix_norm_pre, w_in, pool_w_map, pool_scale, hyena_conv_w, hyena_conv_b, filt_w_first, filt_b_first, filt_w_hidden, filt_b_hidden, filt_w_last, filt_freq, hyena_skip, pool_out_norm, hyena_out_norm, w_out, mix_norm_post, ffn2_norm_pre, ffn2_norm_post, ffn2_w_gate_up, ffn2_w_down):
    assert ffn1_norm_pre.shape[0] == 1, "single-layer trunk"
    row = lambda a: a[0].reshape(1, -1)
    p = dict(
        g1pre=row(ffn1_norm_pre), g1post=row(ffn1_norm_post),
        w1gu=ffn1_w_gate_up[0].astype(BF16), w1d=ffn1_w_down[0].astype(BF16),
        gmix=row(mix_norm_pre), w_in=w_in[0].astype(BF16),
        pool_w=pool_w_map[0].astype(BF16), pool_scale=row(pool_scale),
        conv_w=hyena_conv_w[0], conv_b=row(hyena_conv_b),
        skip=hyena_skip[0], gpool=row(pool_out_norm), ghy=row(hyena_out_norm),
        w_out=w_out[0].astype(BF16), gmixpost=row(mix_norm_post),
        g2pre=row(ffn2_norm_pre), g2post=row(ffn2_norm_post),
        w2gu=ffn2_w_gate_up[0].astype(BF16), w2d=ffn2_w_down[0].astype(BF16),
    )
    outs = []
    for x3 in (x_prompt, x_sample):
        seq_len = x3.shape[1]
        plan = _Plan(seq_len, *_dft_shape(seq_len))
        filt = _filt_gen(seq_len, filt_w_first[0], filt_b_first[0], filt_w_hidden[0],
                         filt_b_hidden[0], filt_w_last[0], filt_freq[0], rows=min(seq_len, 512),
                         n2=plan.n2)
        outs.append(_trunk(x3, p, plan, filt))
    return tuple(outs)
```

```python
import functools
import math

import jax
import jax.numpy as jnp
import numpy as np
from jax import lax
from jax.experimental import pallas as pl
from jax.experimental.pallas import tpu as pltpu

F32 = jnp.float32
BF16 = jnp.bfloat16
EPS = 1e-6
LANES = 128
SUBLANES = 8
MXU_DIM = 256
POOL_WINDOWS = (2, 4, 8, 16)
HALO = 8
DECAY_TARGET = 1e-2
FAST_DECAY_PCT = 0.3
SLOW_DECAY_PCT = 1.5
MAX_DECAY = math.log(DECAY_TARGET) / FAST_DECAY_PCT
MIN_DECAY = math.log(DECAY_TARGET) / SLOW_DECAY_PCT
FUSED_VMEM_BUDGET = 32 << 20


def _params(sem, vmem_mib):
    return pltpu.CompilerParams(dimension_semantics=sem, vmem_limit_bytes=vmem_mib << 20)


def _rms(x, g):
    inv = lax.rsqrt(jnp.mean(x * x, axis=-1, keepdims=True) + EPS)
    return (x * inv) * g


def _dot(a, b):
    return jnp.dot(a, b, preferred_element_type=F32)


def _split_bf16(a):
    hi = a.astype(BF16)
    return hi, (a - hi.astype(F32)).astype(BF16)


def _dot_split(a, b):
    (a_hi, a_lo), (b_hi, b_lo) = _split_bf16(a), _split_bf16(b)
    return _dot(jnp.concatenate([a_hi, a_hi, a_lo], axis=1), jnp.concatenate([b_hi, b_lo, b_hi], axis=0))


def _lanes(parts):
    return parts[0] if len(parts) == 1 else jnp.concatenate(parts, axis=1)


def _lane_block(a, i):
    return a[:, i * LANES:(i + 1) * LANES]


def _ff_chunks(d_ff, n_chunks=2):
    if d_ff % MXU_DIM:
        return (0, d_ff)
    tiles = d_ff // MXU_DIM
    return tuple(MXU_DIM * ((tiles * k + n_chunks - 1) // n_chunks) for k in range(n_chunks + 1))


def _half_step_ffn(x, gpre, gpost, wgu_ref, wd_ref):
    d_ff = wd_ref.shape[0]
    h = _rms(x, gpre).astype(BF16)
    acc = None
    bounds = _ff_chunks(d_ff)
    for lo, hi in zip(bounds[:-1], bounds[1:]):
        gate = _dot(h, wgu_ref[:, lo:hi])
        up = _dot(h, wgu_ref[:, d_ff + lo:d_ff + hi])
        act = (gate * jax.nn.sigmoid(gate) * up).astype(BF16)
        part = _dot(act, wd_ref[lo:hi, :])
        acc = part if acc is None else acc + part
    return x + 0.5 * _rms(acc, gpost)


def _ffn_in_kernel(x_ref, gpre_ref, gpost_ref, wgu_ref, wd_ref, gmix_ref, win_ref, o_ref, z_ref):
    x = _half_step_ffn(x_ref[...], gpre_ref[...], gpost_ref[...], wgu_ref, wd_ref)
    o_ref[...] = x
    z_ref[...] = _dot(_rms(x, gmix_ref[...]).astype(BF16), win_ref[...])


def _ffn_in(x, g_pre, g_post, wgu, wd, g_mix, w_in, *, tm):
    t, d = x.shape
    d_ff, d_in = wd.shape[0], w_in.shape[1]
    const = lambda i: (0, 0)
    row = lambda i: (i, 0)
    once = pl.Buffered(1)
    vec = pl.BlockSpec((1, d), const)
    return pl.pallas_call(
        _ffn_in_kernel,
        grid=(t // tm,),
        in_specs=[
            pl.BlockSpec((tm, d), row), vec, vec,
            pl.BlockSpec((d, 2 * d_ff), const, pipeline_mode=once),
            pl.BlockSpec((d_ff, d), const, pipeline_mode=once),
            vec,
            pl.BlockSpec((d, d_in), const, pipeline_mode=once),
        ],
        out_specs=[pl.BlockSpec((tm, d), row), pl.BlockSpec((tm, d_in), row)],
        out_shape=[jax.ShapeDtypeStruct((t, d), F32), jax.ShapeDtypeStruct((t, d_in), F32)],
        compiler_params=_params(("parallel",), 56),
        name="ffn_in",
    )(x, g_pre, g_post, wgu, wd, g_mix, w_in)


def _ffn_out_kernel(x_ref, yp_ref, yh_ref, gp_ref, gh_ref, wout_ref, gmix_ref,
                    gpre_ref, gpost_ref, wgu_ref, wd_ref, o_ref):
    d_pool = yp_ref.shape[1]
    yp = _rms(yp_ref[...], gp_ref[...]).astype(BF16)
    yh = _rms(yh_ref[...], gh_ref[...]).astype(BF16)
    y = _dot(yp, wout_ref[:d_pool, :]) + _dot(yh, wout_ref[d_pool:, :])
    x = x_ref[...] + _rms(y, gmix_ref[...])
    o_ref[...] = _half_step_ffn(x, gpre_ref[...], gpost_ref[...], wgu_ref, wd_ref)


def _ffn_out(x, yp, yh, gp, gh, w_out, g_mix, g_pre, g_post, wgu, wd, *, tm):
    t, d = x.shape
    d_ff, dp, dh = wd.shape[0], yp.shape[1], yh.shape[1]
    const = lambda i: (0, 0)
    row = lambda i: (i, 0)
    once = pl.Buffered(1)
    vec = lambda n: pl.BlockSpec((1, n), const)
    return pl.pallas_call(
        _ffn_out_kernel,
        grid=(t // tm,),
        in_specs=[
            pl.BlockSpec((tm, d), row), pl.BlockSpec((tm, dp), row), pl.BlockSpec((tm, dh), row),
            vec(dp), vec(dh),
            pl.BlockSpec((dp + dh, d), const, pipeline_mode=once),
            vec(d), vec(d), vec(d),
            pl.BlockSpec((d, 2 * d_ff), const, pipeline_mode=once),
            pl.BlockSpec((d_ff, d), const, pipeline_mode=once),
        ],
        out_specs=pl.BlockSpec((tm, d), row),
        out_shape=jax.ShapeDtypeStruct((t, d), F32),
        compiler_params=_params(("parallel",), 56),
        name="ffn_out",
    )(x, yp, yh, gp, gh, w_out, g_mix, g_pre, g_post, wgu, wd)


def _with_halo(prev_ref, main_ref, next_ref, i, n_tiles):
    prev = jnp.where(i > 0, prev_ref[0], 0.0)
    nxt = jnp.where(i < n_tiles - 1, next_ref[0], 0.0)
    return jnp.concatenate([prev, main_ref[0], nxt], axis=0)


def _pool_kernel(prev_ref, main_ref, next_ref, wmap_ref, scale_ref, o_ref, *, seq_len, n_tiles):
    i = pl.program_id(1)
    rows = main_ref.shape[1]
    n_ext = rows + 2 * HALO
    ext_all = _with_halo(prev_ref, main_ref, next_ref, i, n_tiles)
    pos = i * rows + lax.broadcasted_iota(jnp.int32, (rows, LANES), 0)
    for grp, window in enumerate(POOL_WINDOWS):
        lanes = slice(grp * LANES, (grp + 1) * LANES)
        ext = ext_all[:, lanes]
        ssum, w = ext + pltpu.roll(ext, 1, 0), 2
        while w < window:
            ssum, w = pltpu.roll(ssum, w // 2, 0) + pltpu.roll(ssum, n_ext - w // 2, 0), 2 * w
        u = main_ref[0, :, lanes]
        lo = jnp.clip(pos - window // 2, 0, seq_len)
        hi = jnp.clip(pos + (window - window // 2), 0, seq_len)
        d = ssum[HALO:HALO + rows] / (hi - lo).astype(F32) - u
        y = _dot(d.astype(BF16), wmap_ref[grp])
        o_ref[0, :, lanes] = y * scale_ref[:, lanes]


def _halo_specs(rows, width, seq_len, col_of):
    blocks_per_tile = rows // HALO
    last = seq_len // HALO - 1
    prev = pl.BlockSpec((1, HALO, width),
                        lambda b, i, c: (b, jnp.maximum(i * blocks_per_tile - 1, 0), col_of(c)))
    main = pl.BlockSpec((1, rows, width), lambda b, i, c: (b, i, col_of(c)))
    nxt = pl.BlockSpec((1, HALO, width),
                       lambda b, i, c: (b, jnp.minimum((i + 1) * blocks_per_tile, last), col_of(c)))
    return [prev, main, nxt]


def _pool(z, wmap, scale, *, rows):
    b, seq_len, _ = z.shape
    assert wmap.shape[0] == len(POOL_WINDOWS) and max(POOL_WINDOWS) <= 2 * HALO
    width = wmap.shape[0] * LANES
    n_tiles = seq_len // rows
    return pl.pallas_call(
        functools.partial(_pool_kernel, seq_len=seq_len, n_tiles=n_tiles),
        grid=(b, n_tiles, 1),
        in_specs=_halo_specs(rows, width, seq_len, lambda c: 0) + [
            pl.BlockSpec(wmap.shape, lambda b, i, c: (0, 0, 0)),
            pl.BlockSpec((1, width), lambda b, i, c: (0, 0)),
        ],
        out_specs=pl.BlockSpec((1, rows, width), lambda b, i, c: (b, i, 0)),
        out_shape=jax.ShapeDtypeStruct((b, seq_len, width), F32),
        compiler_params=_params(("parallel", "parallel", "parallel"), 40),
        name="pool",
    )(z, z, z, wmap, scale)


def _sconv_kernel(prev_ref, main_ref, next_ref, w_ref, b_ref, o_ref, *, n_tiles):
    i = pl.program_id(1)
    rows = main_ref.shape[1]
    ext = _with_halo(prev_ref, main_ref, next_ref, i, n_tiles)
    n_ext = rows + 2 * HALO
    before = pltpu.roll(ext, 1, 0)[HALO:HALO + rows]
    after = pltpu.roll(ext, n_ext - 1, 0)[HALO:HALO + rows]
    w = w_ref[...]
    out = before * w[0:1] + main_ref[0] * w[1:2] + after * w[2:3] + b_ref[...]
    n_chunks, pitch, width = o_ref.shape[2:]
    n2 = rows // n_chunks
    for n in range(n_chunks):
        o_ref[0, 0, n, :n2, :] = out[n * n2:(n + 1) * n2]
    o_ref[0, 0, :, n2:, :] = jnp.zeros((n_chunks, pitch - n2, width), F32)


def _sconv(z, w, bias, *, rows, col0, n_parts, n2):
    b, seq_len, _ = z.shape
    n_tiles = seq_len // rows
    width = w.shape[1] // n_parts
    assert col0 % width == 0
    return pl.pallas_call(
        functools.partial(_sconv_kernel, n_tiles=n_tiles),
        grid=(b, n_tiles, n_parts),
        in_specs=_halo_specs(rows, width, seq_len, lambda c: c + col0 // width) + [
            pl.BlockSpec((w.shape[0], width), lambda b, i, c: (0, c)),
            pl.BlockSpec((1, width), lambda b, i, c: (0, c)),
        ],
        out_specs=pl.BlockSpec((1, 1, rows // n2, n2 + SUBLANES, width), lambda b, i, c: (c, b, i, 0, 0)),
        out_shape=jax.ShapeDtypeStruct((n_parts, b, seq_len // n2, n2 + SUBLANES, width), F32),
        compiler_params=_params(("parallel", "parallel", "parallel"), 40),
        name="sconv",
    )(z, z, z, w, bias)


def _filt_gen_kernel(bands_ref, wt_ref, wc_ref, ws_ref, b1_ref, wh_ref, bh_ref, freq_ref,
                     wl_ref, delta_ref, o_ref, *, seq_len, d_ch):
    rows = o_ref.shape[0] * (o_ref.shape[1] - SUBLANES)
    base = pl.program_id(0) * rows
    m_lane = (base + lax.broadcasted_iota(jnp.int32, (1, rows), 1)).astype(F32)
    t_lane = m_lane / (seq_len - 1.0)
    ang = (bands_ref[...] * (2.0 * math.pi / seq_len)) * m_lane
    freq = freq_ref[...]
    pre = (_dot_split(wc_ref[...], jnp.cos(ang)) + _dot_split(ws_ref[...], -jnp.sin(ang))
           + wt_ref[...] * t_lane + b1_ref[...])
    h = jnp.sin(freq * pre)
    for layer in range(wh_ref.shape[0]):
        pre = _dot_split(wh_ref[layer], h) + bh_ref[layer]
        h = jnp.sin(freq * pre)
    (h_hi, h_lo), (w_hi, w_lo) = _split_bf16(h), _split_bf16(wl_ref[...])
    out = _dot(jnp.concatenate([h_hi, h_hi, h_lo], axis=0).T,
               jnp.concatenate([w_hi, w_lo, w_hi], axis=0))
    m_row = base + lax.broadcasted_iota(jnp.int32, (rows, d_ch), 0)
    t_row = m_row.astype(F32) / (seq_len - 1.0)
    decay = jnp.exp(-t_row * jnp.abs(delta_ref[...]))
    decay_bwd = jnp.where(m_row == 0, 0.0, decay)
    n_chunks, pitch, _ = o_ref.shape
    n2 = rows // n_chunks
    for q in range(out.shape[1] // d_ch):
        dq = decay_bwd if q % 2 == 1 else decay
        val = out[:, q * d_ch:(q + 1) * d_ch] * dq
        for n in range(n_chunks):
            o_ref[n, :n2, q * d_ch:(q + 1) * d_ch] = val[n * n2:(n + 1) * n2]
    o_ref[:, n2:, :] = jnp.zeros((n_chunks, pitch - n2, o_ref.shape[2]), F32)


def _filt_gen(seq_len, w_first, b_first, w_hidden, b_hidden, w_last, freq, *, rows, n2):
    pos_bands = (w_first.shape[0] - 1) // 2
    hidden = w_first.shape[1]
    n_cols = w_last.shape[1]
    d_ch = n_cols // 4
    bands = jnp.linspace(1e-4, pos_bands - 1, pos_bands, dtype=F32).reshape(pos_bands, 1)
    deltas = jnp.linspace(MIN_DECAY, MAX_DECAY, d_ch, dtype=F32).reshape(1, d_ch)
    w1t = w_first.T
    args = (bands, w1t[:, 0:1], w1t[:, 1:1 + pos_bands], w1t[:, 1 + pos_bands:],
            b_first.reshape(hidden, 1), jnp.swapaxes(w_hidden, 1, 2),
            b_hidden.reshape(b_hidden.shape[0], hidden, 1), freq.reshape(hidden, 1),
            w_last, deltas)
    full = lambda a: pl.BlockSpec(a.shape, lambda i, _n=a.ndim: (0,) * _n)
    return pl.pallas_call(
        functools.partial(_filt_gen_kernel, seq_len=seq_len, d_ch=d_ch),
        grid=(seq_len // rows,),
        in_specs=[full(a) for a in args],
        out_specs=pl.BlockSpec((rows // n2, n2 + SUBLANES, n_cols), lambda i: (i, 0, 0)),
        out_shape=jax.ShapeDtypeStruct((seq_len // n2, n2 + SUBLANES, n_cols), F32),
        compiler_params=_params(("parallel",), 40),
        name="filt_gen",
    )(*args)


class _Plan:
    def __init__(self, seq_len, n1, n2):
        assert n1 * n2 == 2 * seq_len and n1 % (2 * SUBLANES) == 0 and n2 % SUBLANES == 0
        self.seq_len, self.n1, self.n2 = seq_len, n1, n2
        n = n1 * n2
        n1h = n1 // 2
        self.kh = kh = n1h + 1
        k1 = np.arange(kh, dtype=np.float64)[:, None]
        m1 = np.arange(n1h, dtype=np.float64)[None, :]
        th1 = 2.0 * np.pi * k1 * m1 / n1
        f1 = np.zeros((n1 + SUBLANES, n1h))
        f1[:n1h] = np.cos(th1[:n1h]); f1[n1h:n1] = -np.sin(th1[:n1h]); f1[n1] = np.cos(th1[n1h])
        self.f1 = jnp.asarray(f1, F32)
        m2 = np.arange(n2, dtype=np.float64)[None, :]
        tht = 2.0 * np.pi * k1 * m2 / n
        twr, twi = np.cos(tht), -np.sin(tht)
        rep = lambda a: np.repeat(a[:, :, None], LANES, axis=2)
        self.twr_inv, self.twi_inv = jnp.asarray(rep(twr), F32), jnp.asarray(rep(twi), F32)
        pad = np.zeros((n2, SUBLANES - 1))
        self.twr_fwd = jnp.asarray(rep(np.concatenate([twr.T, pad], 1)), F32)
        self.twi_fwd = jnp.asarray(rep(np.concatenate([twi.T, pad], 1)), F32)
        k2 = np.arange(n2, dtype=np.float64)[:, None]
        th2 = 2.0 * np.pi * k2 * m2 / n2
        f2r, f2i = np.cos(th2), -np.sin(th2)
        self.f2 = jnp.asarray(np.block([[f2r, -f2i], [f2i, f2r]]), F32)
        self.f2inv = jnp.asarray(np.block([[f2r, f2i], [-f2i, f2r]]), F32)
        wgt = np.where(k1[:n1h] == 0, 1.0, 2.0) / n
        ginv = np.concatenate([(wgt * np.cos(th1[:n1h])).T, (-wgt * np.sin(th1[:n1h])).T], axis=1)
        self.ginv = jnp.asarray(ginv, F32)
        k1f = np.arange(kh, dtype=np.float64)[:, None]
        full = lambda cols: 2.0 * np.pi * k1f * cols[None, :] / n1
        stage1 = lambda th: np.concatenate(
            [np.cos(th[:n1h]), -np.sin(th[:n1h]), np.cos(th[n1h:]), np.zeros((SUBLANES - 1, th.shape[1]))], 0)
        fwd_cols = np.arange(n1h, dtype=np.float64)
        f1a = stage1(full(np.concatenate([fwd_cols, n1 - 1 - fwd_cols])))
        f1b = stage1(full(np.concatenate([fwd_cols, (n1 - fwd_cols) % n1])))
        f1b[:, n1h] = 0.0
        self.f1_filt, self.f1_filt0 = jnp.asarray(f1a, F32), jnp.asarray(f1b, F32)
        rows = lambda t: rep(np.concatenate([t.T, np.zeros((t.shape[1], SUBLANES - 1))], 1))
        coarse, fine = tht[:, ::SUBLANES], tht[:, :SUBLANES]
        self.tw_coarse = (jnp.asarray(rows(np.cos(coarse)), F32), jnp.asarray(rows(-np.sin(coarse)), F32))
        self.tw_fine = (jnp.asarray(rows(np.cos(fine)), F32), jnp.asarray(rows(-np.sin(fine)), F32))
        self.tw_slab_coarse = (jnp.asarray(rep(np.cos(coarse)), F32), jnp.asarray(rep(-np.sin(coarse)), F32))
        self.tw_slab_fine = (jnp.asarray(rep(np.cos(fine)), F32), jnp.asarray(rep(-np.sin(fine)), F32))


def _stack_bf16(re, im):
    return jnp.concatenate([re.astype(BF16), im.astype(BF16)], axis=0)


def _filt_spec_kernel(hf_ref, hba_ref, hbb_ref, f1_ref, f1j0_ref, twcr_ref, twci_ref, twfr_ref, twfi_ref, f2_ref,
                      kr_ref, ki_ref, are_ref, aim_ref, *, n1h, n2, jg):
    pitch = n2 + SUBLANES
    n_j = n2 // jg
    ks = kr_ref.shape[0]
    phase = pl.program_id(1)
    hf, hba, hbb = (r.reshape(n1h * jg, LANES) for r in (hf_ref, hba_ref, hbb_ref))

    @pl.when(phase < n_j)
    def _():
        f1 = f1_ref[...].astype(BF16)
        f1_first = jnp.where(phase == 0, f1j0_ref[...], f1_ref[...]).astype(BF16)
        j0 = phase * jg
        a0 = lax.shift_right_logical(j0, 3)

        def stage1(mat, ts):
            rows = lambda ref, start: ref[pl.ds(start, n1h, stride=jg), :]
            x = _lanes([jnp.concatenate([rows(hf, t), rows(hbb, 0) if t == 0 else rows(hba, jg - t)], axis=0)
                        for t in ts])
            p = _dot(mat, x.astype(BF16))
            cr = _lanes([twcr_ref[a0 + t // SUBLANES] for t in ts])
            ci = _lanes([twci_ref[a0 + t // SUBLANES] for t in ts])
            fr = _lanes([twfr_ref[t % SUBLANES] for t in ts])
            fi = _lanes([twfi_ref[t % SUBLANES] for t in ts])
            twr, twi = cr * fr - ci * fi, cr * fi + ci * fr
            pr, pi, pn = p[:n1h], p[n1h:2 * n1h], p[2 * n1h:2 * n1h + 1]
            re, im = pr * twr[:n1h] - pi * twi[:n1h], pr * twi[:n1h] + pi * twr[:n1h]
            re_n, im_n = pn * twr[n1h:n1h + 1], pn * twi[n1h:n1h + 1]
            for i, t in enumerate(ts):
                are_ref[pl.ds(j0 + t, n1h, stride=pitch), :] = _lane_block(re, i)
                aim_ref[pl.ds(j0 + t, n1h, stride=pitch), :] = _lane_block(im, i)
                are_ref[pl.ds(n1h * pitch + j0 + t, 1), :] = _lane_block(re_n, i)
                aim_ref[pl.ds(n1h * pitch + j0 + t, 1), :] = _lane_block(im_n, i)

        stage1(f1_first, (0,))
        for t in range(1, jg - 1, 2):
            stage1(f1, (t, t + 1))
        stage1(f1, (jg - 1,))

    @pl.when(phase >= n_j)
    def _():
        f2 = f2_ref[...].astype(BF16)
        kc = phase - n_j

        def slabs(first, count):
            rows = [pl.ds(pl.multiple_of((kc * ks + first + i) * pitch, SUBLANES), n2) for i in range(count)]
            x = _dot(f2, _stack_bf16(_lanes([are_ref[r, :] for r in rows]), _lanes([aim_ref[r, :] for r in rows])))
            for i in range(count):
                kr_ref[first + i] = _lane_block(x[:n2], i).astype(kr_ref.dtype)
                ki_ref[first + i] = _lane_block(x[n2:], i).astype(ki_ref.dtype)

        def pair(p, carry):
            slabs(2 * p, 2)
            return carry

        lax.fori_loop(0, ks // 2, pair, 0, unroll=8)
        if ks % 2:
            slabs(ks - 1, 1)


def _filt_spec(plan, filt, n_orders, *, ks):
    n1h, n2, kh = plan.n1 // 2, plan.n2, plan.kh
    cols = filt.shape[2]
    n_cb = cols // (2 * n_orders * LANES)
    jg = _block_cols(n2, n1h)
    n_j = n2 // jg
    pitch = n2 + SUBLANES
    col = lambda c, back: (c // n_cb) * 2 * n_cb + back * n_cb + c % n_cb
    step = lambda s: jnp.minimum(s, n_j - 1)
    seq = lambda back, blk: pl.BlockSpec((n1h, jg, LANES), lambda c, s: (0, blk(step(s)), col(c, back)))
    full = lambda a: pl.BlockSpec(a.shape, lambda c, s, _n=a.ndim: (0,) * _n, pipeline_mode=pl.Buffered(1))
    tables = (plan.f1_filt, plan.f1_filt0, *plan.tw_coarse, *plan.tw_fine, plan.f2)
    out = jax.ShapeDtypeStruct((kh, n2, cols // 2), BF16)
    scratch = pltpu.VMEM((kh * pitch, LANES), F32)
    return pl.pallas_call(
        functools.partial(_filt_spec_kernel, n1h=n1h, n2=n2, jg=jg),
        grid=(n_orders * n_cb, n_j + kh // ks),
        in_specs=[seq(0, lambda s: s), seq(1, lambda s: n_j - 1 - s), seq(1, lambda s: (n_j - s) % n_j)]
        + [full(a) for a in tables],
        out_specs=[pl.BlockSpec((ks, n2, LANES), lambda c, s: (jnp.maximum(s - n_j, 0), 0, c))] * 2,
        out_shape=[out, out],
        scratch_shapes=[scratch, scratch],
        compiler_params=_params(("parallel", "arbitrary"), 48),
        name="filt_spec",
    )(filt, filt, filt, *tables)


def _conv_front_kernel(x_ref, kr_ref, ki_ref, f1_ref, twcr_ref, twci_ref, twfr_ref, twfi_ref,
                       icr_ref, ici_ref, ifr_ref, ifi_ref, f2_ref, f2inv_ref,
                       ore_ref, oim_ref, are_ref, aim_ref, *, n1h, n2, unroll):
    pitch = n2 + SUBLANES
    ks = kr_ref.shape[0]
    kc = pl.program_id(2)
    x2 = x_ref.at[0, 0].reshape(n1h * pitch, LANES)

    @pl.when(kc == 0)
    def _():
        f1 = f1_ref[...].astype(BF16)

        def body(jp, carry):
            js = (2 * jp, 2 * jp + 1)
            a = lax.shift_right_logical(jp, 2)
            bs = [2 * jnp.bitwise_and(jp, SUBLANES // 2 - 1) + q for q in range(2)]
            xb = _lanes([x2[pl.ds(j, n1h, stride=pitch), :] for j in js]).astype(BF16)
            p = _dot(f1, xb)
            cr, ci = jnp.tile(twcr_ref[a], (1, 2)), jnp.tile(twci_ref[a], (1, 2))
            fr, fi = _lanes([twfr_ref[b] for b in bs]), _lanes([twfi_ref[b] for b in bs])
            twr, twi = cr * fr - ci * fi, cr * fi + ci * fr
            pr, pi, pn = p[:n1h], p[n1h:2 * n1h], p[2 * n1h:2 * n1h + 1]
            re, im = pr * twr[:n1h] - pi * twi[:n1h], pr * twi[:n1h] + pi * twr[:n1h]
            re_n, im_n = pn * twr[n1h:n1h + 1], pn * twi[n1h:n1h + 1]
            for i, j in enumerate(js):
                are_ref[pl.ds(j, n1h, stride=pitch), :] = _lane_block(re, i)
                aim_ref[pl.ds(j, n1h, stride=pitch), :] = _lane_block(im, i)
                are_ref[pl.ds(n1h * pitch + j, 1), :] = _lane_block(re_n, i)
                aim_ref[pl.ds(n1h * pitch + j, 1), :] = _lane_block(im_n, i)
            return carry

        lax.fori_loop(0, n2 // 2, body, 0, unroll=unroll)

    f2, f2inv = f2_ref[...].astype(BF16), f2inv_ref[...].astype(BF16)

    def slab_twiddle(k):
        cr, ci, fr, fi = icr_ref[k], ici_ref[k], ifr_ref[k], ifi_ref[k]
        pieces = [(cr[a:a + 1] * fr - ci[a:a + 1] * fi, cr[a:a + 1] * fi + ci[a:a + 1] * fr)
                  for a in range(n2 // SUBLANES)]
        return (jnp.concatenate([p[0] for p in pieces], axis=0), jnp.concatenate([p[1] for p in pieces], axis=0))

    def slabs(first, count):
        ks_abs = [kc * ks + first + i for i in range(count)]
        rows = [pl.ds(pl.multiple_of(k * pitch, SUBLANES), n2) for k in ks_abs]
        x = _dot(f2, _stack_bf16(_lanes([are_ref[r, :] for r in rows]), _lanes([aim_ref[r, :] for r in rows])))
        xr, xi = x[:n2], x[n2:]
        kr = _lanes([kr_ref[first + i] for i in range(count)]).astype(F32)
        ki = _lanes([ki_ref[first + i] for i in range(count)]).astype(F32)
        y = _dot(f2inv, _stack_bf16(xr * kr - xi * ki, xr * ki + xi * kr))
        br, bi = y[:n2], y[n2:]
        tw = [slab_twiddle(k) for k in ks_abs]
        twr, twi = _lanes([t[0] for t in tw]), _lanes([t[1] for t in tw])
        ore, oim = br * twr + bi * twi, bi * twr - br * twi
        for i in range(count):
            ore_ref[0, first + i] = _lane_block(ore, i)
            oim_ref[0, first + i] = _lane_block(oim, i)

    def pair(p, carry):
        slabs(2 * p, 2)
        return carry

    lax.fori_loop(0, ks // 2, pair, 0, unroll=8)
    if ks % 2:
        slabs(ks - 1, 1)


def _conv_front(plan, x4, part, kr, ki, order, *, ks):
    b, ch = x4.shape[1], x4.shape[-1]
    n1h, n2, kh = plan.n1 // 2, plan.n2, plan.kh
    n_cb = ch // LANES
    pitch = n2 + SUBLANES
    once = pl.Buffered(1)
    full = lambda a: pl.BlockSpec(a.shape, lambda bb, c, k, _n=a.ndim: (0,) * _n, pipeline_mode=once)
    filt = pl.BlockSpec((ks, n2, LANES), lambda bb, c, k: (k, 0, order * n_cb + c))
    data = pl.BlockSpec((1, ks, n2, LANES), lambda bb, c, k: (bb, k, 0, c))
    tables = (plan.f1, *plan.tw_coarse, *plan.tw_fine, *plan.tw_slab_coarse, *plan.tw_slab_fine,
              plan.f2, plan.f2inv)
    out = jax.ShapeDtypeStruct((b, kh, n2, ch), F32)
    scratch = pltpu.VMEM((kh * pitch, LANES), F32)
    return pl.pallas_call(
        functools.partial(_conv_front_kernel, n1h=n1h, n2=n2, unroll=4),
        grid=(b, n_cb, kh // ks),
        in_specs=[pl.BlockSpec((1, 1, n1h, pitch, LANES), lambda bb, c, k: (part, bb, 0, 0, c)),
                  filt, filt] + [full(a) for a in tables],
        out_specs=[data, data],
        out_shape=[out, out],
        scratch_shapes=[scratch, scratch],
        compiler_params=_params(("parallel", "parallel", "arbitrary"), 60),
        name="conv_front",
    )(x4, kr, ki, *tables)


def _dft_inv_kernel(bre_ref, bim_ref, x_ref, gate_ref, skip_ref, ginv_ref, o_ref, *, unroll, scale, n_j):
    _, _, n1h, g, _ = o_ref.shape
    b_re = bre_ref.at[0].reshape((n1h + 1) * g, LANES)
    b_im = bim_ref.at[0].reshape((n1h + 1) * g, LANES)
    x2 = x_ref.at[0, 0].reshape(n1h * g, LANES)
    gate2 = gate_ref.at[0, 0].reshape(n1h * g, LANES)
    o2 = o_ref.at[0, 0].reshape(n1h * g, LANES)
    ginv = ginv_ref[...].astype(BF16)
    skip = skip_ref[0]
    n1 = lax.broadcasted_iota(jnp.int32, (n1h, 2 * LANES), 0)
    nyq_w = jnp.where(jnp.bitwise_and(n1, 1) == 0, scale, -scale)
    skip = jnp.tile(skip, (1, 2))

    def body(jp, carry):
        rows = [pl.ds(2 * jp + i, n1h, stride=g) for i in range(2)]
        spec = _stack_bf16(_lanes([b_re[r, :] for r in rows]), _lanes([b_im[r, :] for r in rows]))
        nyq = _lanes([b_re[pl.ds(n1h * g + 2 * jp + i, 1), :] for i in range(2)])
        y = _dot(ginv, spec) + nyq_w * nyq
        out = _lanes([gate2[r, :] for r in rows]) * (y + _lanes([x2[r, :] for r in rows]) * skip)
        for i, r in enumerate(rows):
            o2[r, :] = _lane_block(out, i)
        return carry

    @pl.when(pl.program_id(0) < n_j)
    def _():
        lax.fori_loop(0, g // 2, body, 0, unroll=unroll)

    @pl.when(pl.program_id(0) >= n_j)
    def _():
        o_ref[...] = jnp.zeros(o_ref.shape, o_ref.dtype)


def _dft_inv(plan, bre, bim, x4, x_part, gate4, gate_part, skip, order, *, g, pitched_out):
    b, ch = x4.shape[1], x4.shape[-1]
    n1h, n2, kh = plan.n1 // 2, plan.n2, plan.kh
    n_j = n2 // g
    rows_out = n2 + SUBLANES if pitched_out else n2
    blk = lambda j: jnp.minimum(j, n_j - 1)
    spec = pl.BlockSpec((1, kh, g, LANES), lambda j, bb, c: (bb, 0, blk(j), c))
    time = lambda part: pl.BlockSpec((1, 1, n1h, g, LANES), lambda j, bb, c: (part, bb, 0, blk(j), c))
    return pl.pallas_call(
        functools.partial(_dft_inv_kernel, unroll=min(g // 2, 8), scale=1.0 / (plan.n1 * n2), n_j=n_j),
        grid=(n_j + (1 if pitched_out else 0), b, ch // LANES),
        in_specs=[
            spec, spec, time(x_part), time(gate_part),
            pl.BlockSpec((1, 1, LANES), lambda j, bb, c: (order, 0, c)),
            pl.BlockSpec(plan.ginv.shape, lambda j, bb, c: (0, 0)),
        ],
        out_specs=pl.BlockSpec((1, 1, n1h, g, LANES), lambda j, bb, c: (0, bb, 0, j, c)),
        out_shape=jax.ShapeDtypeStruct((1, b, n1h, rows_out, ch), F32),
        compiler_params=_params(("parallel", "parallel", "parallel"), 48),
        name="dft_inv",
    )(bre, bim, x4, gate4, skip, plan.ginv)


def _hyena_fused_kernel(zv_ref, zx1_ref, zx2_ref, wv_ref, wx1_ref, wx2_ref, bv_ref, bx1_ref, bx2_ref,
                        skip_ref, kr0_ref, ki0_ref, kr1_ref, ki1_ref,
                        f1_ref, twrf_ref, twif_ref, twri_ref, twii_ref, f2_ref, f2inv_ref, ginv_ref,
                        o_ref, are_ref, aim_ref, cur_ref, tin_ref, *, n1h, n2, unroll):
    nb = o_ref.shape[0]
    pitch = n2 + SUBLANES
    f1 = f1_ref[...].astype(BF16)
    f2, f2inv = f2_ref[...].astype(BF16), f2inv_ref[...].astype(BF16)
    ginv = ginv_ref[...].astype(BF16)
    scale = 1.0 / (2 * n1h * n2)
    n1 = lax.broadcasted_iota(jnp.int32, (n1h, 2 * nb * LANES), 0)
    nyq_w = jnp.where(jnp.bitwise_and(n1, 1) == 0, scale, -scale)
    lanes, lane_block = _lanes, _lane_block
    wide = lambda t, reps: t if reps == 1 else jnp.tile(t, (1, reps))

    row = lax.broadcasted_iota(jnp.int32, (n2, LANES), 0)
    for part, (z_ref, w_ref, b_ref) in enumerate(((zv_ref, wv_ref, bv_ref), (zx1_ref, wx1_ref, bx1_ref),
                                                   (zx2_ref, wx2_ref, bx2_ref))):
        w, bias = w_ref[...], b_ref[...]
        for i in range(nb):
            for n in range(n1h):
                lo = n * n2
                mid = z_ref[i, lo:lo + n2, :]
                if n > 0:
                    before = z_ref[i, lo - 1:lo + n2 - 1, :]
                else:
                    before = jnp.where(row == 0, 0.0, pltpu.roll(mid, 1, 0))
                if n < n1h - 1:
                    after = z_ref[i, lo + 1:lo + n2 + 1, :]
                else:
                    after = jnp.where(row == n2 - 1, 0.0, pltpu.roll(mid, n2 - 1, 0))
                tin_ref[part, i, n * pitch:n * pitch + n2, :] = (
                    before * w[0:1] + mid * w[1:2] + after * w[2:3] + bias)
    stages = ((tin_ref.at[0], tin_ref.at[1], cur_ref, kr0_ref, ki0_ref),
              (cur_ref, tin_ref.at[2], tin_ref.at[0], kr1_ref, ki1_ref))
    for order, (src, gate, dst, kr_ref, ki_ref) in enumerate(stages):
        skip = wide(skip_ref[order], 2 * nb)

        def stage1(jp, carry):
            cols = [(2 * jp + q, i) for q in range(2) for i in range(nb)]
            xb = lanes([src[i, pl.ds(j, n1h, stride=pitch), :] for j, i in cols]).astype(BF16)
            p = _dot(f1, xb)
            twr = lanes([wide(twrf_ref[2 * jp + q], nb) for q in range(2)])
            twi = lanes([wide(twif_ref[2 * jp + q], nb) for q in range(2)])
            pr, pi, pn = p[:n1h], p[n1h:2 * n1h], p[2 * n1h:2 * n1h + 1]
            re, im = pr * twr[:n1h] - pi * twi[:n1h], pr * twi[:n1h] + pi * twr[:n1h]
            re_n, im_n = pn * twr[n1h:n1h + 1], pn * twi[n1h:n1h + 1]
            for c, (j, i) in enumerate(cols):
                are_ref[i, pl.ds(j, n1h, stride=pitch), :] = lane_block(re, c)
                aim_ref[i, pl.ds(j, n1h, stride=pitch), :] = lane_block(im, c)
                are_ref[i, pl.ds(n1h * pitch + j, 1), :] = lane_block(re_n, c)
                aim_ref[i, pl.ds(n1h * pitch + j, 1), :] = lane_block(im_n, c)
            return carry

        lax.fori_loop(0, n2 // 2, stage1, 0, unroll=unroll)

        def slabs(ks):
            rows = [pl.ds(pl.multiple_of(k * pitch, SUBLANES), n2) for k in ks]
            a = _stack_bf16(lanes([are_ref[i, r, :] for r in rows for i in range(nb)]),
                            lanes([aim_ref[i, r, :] for r in rows for i in range(nb)]))
            x = _dot(f2, a)
            xr, xi = x[:n2], x[n2:]
            per_slab = lambda ref: lanes([wide(ref[k], nb) for k in ks])
            kr, ki = per_slab(kr_ref).astype(F32), per_slab(ki_ref).astype(F32)
            y = _dot(f2inv, _stack_bf16(xr * kr - xi * ki, xr * ki + xi * kr))
            br, bi = y[:n2], y[n2:]
            twr, twi = per_slab(twri_ref), per_slab(twii_ref)
            ore, oim = br * twr + bi * twi, bi * twr - br * twi
            for s, r in enumerate(rows):
                for i in range(nb):
                    are_ref[i, r, :] = lane_block(ore, s * nb + i)
                    aim_ref[i, r, :] = lane_block(oim, s * nb + i)

        def slab(k, carry):
            slabs((k,))
            return carry

        lax.fori_loop(0, n1h + 1, slab, 0, unroll=True)

        def stage1_inv(jp, carry):
            cols = [(pl.ds(2 * jp + q, n1h, stride=pitch), 2 * jp + q, i) for q in range(2) for i in range(nb)]
            spec = _stack_bf16(lanes([are_ref[i, r, :] for r, _, i in cols]),
                               lanes([aim_ref[i, r, :] for r, _, i in cols]))
            nyq = lanes([are_ref[i, pl.ds(n1h * pitch + j, 1), :] for _, j, i in cols])
            y = _dot(ginv, spec) + nyq_w * nyq
            x = lanes([src[i, r, :] for r, _, i in cols])
            g = lanes([gate[i, r, :] for r, _, i in cols])
            out = g * (y + x * skip)
            for c, (r, _, i) in enumerate(cols):
                dst[i, r, :] = lane_block(out, c)
            return carry

        lax.fori_loop(0, n2 // 2, stage1_inv, 0, unroll=unroll)

    def unpitch(n, carry):
        for i in range(nb):
            o_ref[i, pl.ds(pl.multiple_of(n * n2, n2), n2), :] = (
                tin_ref[0, i, pl.ds(pl.multiple_of(n * pitch, SUBLANES), n2), :])
        return carry

    lax.fori_loop(0, n1h, unpitch, 0)


def _hyena_fused(plan, z, col0, conv_w, conv_b, kr, ki, skip, *, nb):
    b, seq_len, _ = z.shape
    ch = skip.shape[1]
    n1h, n2, kh = plan.n1 // 2, plan.n2, plan.kh
    n_cb = ch // LANES
    cb0 = col0 // LANES
    part = lambda p: pl.BlockSpec((nb, seq_len, LANES), lambda c, bb: (bb, 0, cb0 + p * n_cb + c))
    taps = lambda p: pl.BlockSpec((conv_w.shape[0], LANES), lambda c, bb: (0, p * n_cb + c))
    bias = lambda p: pl.BlockSpec((1, LANES), lambda c, bb: (0, p * n_cb + c))
    once = pl.Buffered(1)
    filt = lambda o: pl.BlockSpec((kh, n2, LANES), lambda c, bb: (0, 0, o * n_cb + c), pipeline_mode=once)
    full = lambda a: pl.BlockSpec(a.shape, lambda c, bb, _n=a.ndim: (0,) * _n, pipeline_mode=once)
    tables = (plan.f1, plan.twr_fwd, plan.twi_fwd, plan.twr_inv, plan.twi_inv, plan.f2, plan.f2inv, plan.ginv)
    pitch = n2 + SUBLANES
    spec_scratch = pltpu.VMEM((nb, kh * pitch, LANES), F32)
    return pl.pallas_call(
        functools.partial(_hyena_fused_kernel, n1h=n1h, n2=n2, unroll=min(n2 // 2, 32)),
        grid=(n_cb, b // nb),
        in_specs=[part(0), part(1), part(2), taps(0), taps(1), taps(2), bias(0), bias(1), bias(2),
                  pl.BlockSpec((skip.shape[0], 1, LANES), lambda c, bb: (0, 0, c)),
                  filt(0), filt(0), filt(1), filt(1)] + [full(a) for a in tables],
        out_specs=pl.BlockSpec((nb, seq_len, LANES), lambda c, bb: (bb, 0, c)),
        out_shape=jax.ShapeDtypeStruct((b, seq_len, ch), F32),
        scratch_shapes=[spec_scratch, spec_scratch, pltpu.VMEM((nb, n1h * pitch, LANES), F32),
                        pltpu.VMEM((3, nb, n1h * pitch, LANES), F32)],
        compiler_params=_params(("parallel", "arbitrary"), 56),
        name="hyena_fused",
    )(z, z, z, conv_w, conv_w, conv_w, conv_b, conv_b, conv_b,
      skip.reshape(skip.shape[0], 1, ch), kr, ki, kr, ki, *tables)


def _dft_shape(seq_len):
    n2 = 128
    return 2 * seq_len // n2, n2


def _block_cols(n2, rows, target_bytes=1 << 20):
    g = max(SUBLANES, min(n2, target_bytes // (rows * LANES * 4)) // SUBLANES * SUBLANES)
    while n2 % g:
        g -= SUBLANES
    return g


def _slabs_per_step(kh, slab_bytes, target_bytes=3 << 20):
    ks = max(1, min(kh, target_bytes // slab_bytes))
    while kh % ks:
        ks -= 1
    return ks


def _hyena(plan, z, col0, conv_w, conv_b, filt, skip):
    b, seq_len, _ = z.shape
    n_orders, ch = skip.shape
    g = _block_cols(plan.n2, plan.n1 // 2)
    ks_filt = _slabs_per_step(plan.kh, 2 * plan.n2 * LANES * 4, target_bytes=6 << 20)
    kr, ki = _filt_spec(plan, filt, n_orders, ks=ks_filt)
    nb = 2 if b % 2 == 0 else 1
    fused_bytes = 15 * nb * seq_len * LANES * 4
    if n_orders == 2 and fused_bytes <= FUSED_VMEM_BUDGET:
        return _hyena_fused(plan, z, col0, conv_w, conv_b, kr, ki, skip, nb=nb)
    uc = _sconv(z, conv_w, conv_b, rows=min(seq_len, 2048), col0=col0, n_parts=n_orders + 1, n2=plan.n2)
    skip3 = skip.reshape(n_orders, 1, ch)
    cur, cur_part = uc, 0
    for order in range(n_orders):
        bre, bim = _conv_front(plan, cur, cur_part, kr, ki, order, ks=ks_filt)
        cur = _dft_inv(plan, bre, bim, cur, cur_part, uc, order + 1, skip3, order, g=g,
                       pitched_out=order < n_orders - 1)
        cur_part = 0
    return cur.reshape(b, seq_len, ch)


def _trunk(x3, p, plan, filt):
    b, seq_len, d = x3.shape
    t = b * seq_len
    x = x3.reshape(t, d)
    x, z = _ffn_in(x, p["g1pre"], p["g1post"], p["w1gu"], p["w1d"], p["gmix"], p["w_in"], tm=512)
    z = z.reshape(b, seq_len, -1)
    d_pool = p["pool_w"].shape[0] * LANES
    y_pool = _pool(z, p["pool_w"], p["pool_scale"], rows=min(seq_len, 2048))
    y_hy = _hyena(plan, z, d_pool, p["conv_w"], p["conv_b"], filt, p["skip"])
    x = _ffn_out(x, y_pool.reshape(t, -1), y_hy.reshape(t, -1), p["gpool"], p["ghy"], p["w_out"],
                 p["gmixpost"], p["g2pre"], p["g2post"], p["w2gu"], p["w2d"], tm=512)
    return x.reshape(b, seq_len, d)


def kernel(x_prompt, x_sample, ffn1_norm_pre, ffn1_norm_post, ffn1_w_gate_up, ffn1_w_down, mix_norm_pre, w_in, pool_w_map, pool_scale, hyena_conv_w, hyena_conv_b, filt_w_first, filt_b_first, filt_w_hidden, filt_b_hidden, filt_w_last, filt_freq, hyena_skip, pool_out_norm, hyena_out_norm, w_out, mix_norm_post, ffn2_norm_pre, ffn2_norm_post, ffn2_w_gate_up, ffn2_w_down):
    assert ffn1_norm_pre.shape[0] == 1, "single-layer trunk"
    row = lambda a: a[0].reshape(1, -1)
    p = dict(
        g1pre=row(ffn1_norm_pre), g1post=row(ffn1_norm_post),
        w1gu=ffn1_w_gate_up[0].astype(BF16), w1d=ffn1_w_down[0].astype(BF16),
        gmix=row(mix_norm_pre), w_in=w_in[0].astype(BF16),
        pool_w=pool_w_map[0].astype(BF16), pool_scale=row(pool_scale),
        conv_w=hyena_conv_w[0], conv_b=row(hyena_conv_b),
        skip=hyena_skip[0], gpool=row(pool_out_norm), ghy=row(hyena_out_norm),
        w_out=w_out[0].astype(BF16), gmixpost=row(mix_norm_post),
        g2pre=row(ffn2_norm_pre), g2post=row(ffn2_norm_post),
        w2gu=ffn2_w_gate_up[0].astype(BF16), w2d=ffn2_w_down[0].astype(BF16),
    )
    outs = []
    for x3 in (x_prompt, x_sample):
        seq_len = x3.shape[1]
        plan = _Plan(seq_len, *_dft_shape(seq_len))
        filt = _filt_gen(seq_len, filt_w_first[0], filt_b_first[0], filt_w_hidden[0],
                         filt_b_hidden[0], filt_w_last[0], filt_freq[0], rows=min(seq_len, 512),
                         n2=plan.n2)
        outs.append(_trunk(x3, p, plan, filt))
    return tuple(outs)
```

```python
import functools
import math

import jax
import jax.numpy as jnp
import numpy as np
from jax import lax
from jax.experimental import pallas as pl
from jax.experimental.pallas import tpu as pltpu

F32 = jnp.float32
BF16 = jnp.bfloat16
EPS = 1e-6
LANES = 128
SUBLANES = 8
MXU_DIM = 256
POOL_WINDOWS = (2, 4, 8, 16)
HALO = 8
DECAY_TARGET = 1e-2
FAST_DECAY_PCT = 0.3
SLOW_DECAY_PCT = 1.5
MAX_DECAY = math.log(DECAY_TARGET) / FAST_DECAY_PCT
MIN_DECAY = math.log(DECAY_TARGET) / SLOW_DECAY_PCT
FUSED_VMEM_BUDGET = 32 << 20


def _params(sem, vmem_mib):
    return pltpu.CompilerParams(dimension_semantics=sem, vmem_limit_bytes=vmem_mib << 20)


def _rms(x, g):
    inv = lax.rsqrt(jnp.mean(x * x, axis=-1, keepdims=True) + EPS)
    return (x * inv) * g


def _dot(a, b):
    return jnp.dot(a, b, preferred_element_type=F32)


def _split_bf16(a):
    hi = a.astype(BF16)
    return hi, (a - hi.astype(F32)).astype(BF16)


def _dot_split(a, b):
    (a_hi, a_lo), (b_hi, b_lo) = _split_bf16(a), _split_bf16(b)
    return _dot(jnp.concatenate([a_hi, a_hi, a_lo], axis=1), jnp.concatenate([b_hi, b_lo, b_hi], axis=0))


def _lanes(parts):
    return parts[0] if len(parts) == 1 else jnp.concatenate(parts, axis=1)


def _lane_block(a, i):
    return a[:, i * LANES:(i + 1) * LANES]


def _ff_chunks(d_ff, n_chunks=2):
    if d_ff % MXU_DIM:
        return (0, d_ff)
    tiles = d_ff // MXU_DIM
    return tuple(MXU_DIM * ((tiles * k + n_chunks - 1) // n_chunks) for k in range(n_chunks + 1))


def _half_step_ffn(x, gpre, gpost, wgu_ref, wd_ref):
    d_ff = wd_ref.shape[0]
    h = _rms(x, gpre).astype(BF16)
    acc = None
    bounds = _ff_chunks(d_ff)
    for lo, hi in zip(bounds[:-1], bounds[1:]):
        gate = _dot(h, wgu_ref[:, lo:hi])
        up = _dot(h, wgu_ref[:, d_ff + lo:d_ff + hi])
        act = (gate * jax.nn.sigmoid(gate) * up).astype(BF16)
        part = _dot(act, wd_ref[lo:hi, :])
        acc = part if acc is None else acc + part
    return x + 0.5 * _rms(acc, gpost)


def _ffn_in_kernel(x_ref, gpre_ref, gpost_ref, wgu_ref, wd_ref, gmix_ref, win_ref, o_ref, z_ref):
    x = _half_step_ffn(x_ref[...], gpre_ref[...], gpost_ref[...], wgu_ref, wd_ref)
    o_ref[...] = x
    z_ref[...] = _dot(_rms(x, gmix_ref[...]).astype(BF16), win_ref[...])


def _ffn_in(x, g_pre, g_post, wgu, wd, g_mix, w_in, *, tm):
    t, d = x.shape
    d_ff, d_in = wd.shape[0], w_in.shape[1]
    const = lambda i: (0, 0)
    row = lambda i: (i, 0)
    once = pl.Buffered(1)
    vec = pl.BlockSpec((1, d), const)
    return pl.pallas_call(
        _ffn_in_kernel,
        grid=(t // tm,),
        in_specs=[
            pl.BlockSpec((tm, d), row), vec, vec,
            pl.BlockSpec((d, 2 * d_ff), const, pipeline_mode=once),
            pl.BlockSpec((d_ff, d), const, pipeline_mode=once),
            vec,
            pl.BlockSpec((d, d_in), const, pipeline_mode=once),
        ],
        out_specs=[pl.BlockSpec((tm, d), row), pl.BlockSpec((tm, d_in), row)],
        out_shape=[jax.ShapeDtypeStruct((t, d), F32), jax.ShapeDtypeStruct((t, d_in), F32)],
        compiler_params=_params(("parallel",), 56),
        name="ffn_in",
    )(x, g_pre, g_post, wgu, wd, g_mix, w_in)


def _ffn_out_kernel(x_ref, yp_ref, yh_ref, gp_ref, gh_ref, wout_ref, gmix_ref,
                    gpre_ref, gpost_ref, wgu_ref, wd_ref, o_ref):
    d_pool = yp_ref.shape[1]
    yp = _rms(yp_ref[...], gp_ref[...]).astype(BF16)
    yh = _rms(yh_ref[...], gh_ref[...]).astype(BF16)
    y = _dot(yp, wout_ref[:d_pool, :]) + _dot(yh, wout_ref[d_pool:, :])
    x = x_ref[...] + _rms(y, gmix_ref[...])
    o_ref[...] = _half_step_ffn(x, gpre_ref[...], gpost_ref[...], wgu_ref, wd_ref)


def _ffn_out(x, yp, yh, gp, gh, w_out, g_mix, g_pre, g_post, wgu, wd, *, tm):
    t, d = x.shape
    d_ff, dp, dh = wd.shape[0], yp.shape[1], yh.shape[1]
    const = lambda i: (0, 0)
    row = lambda i: (i, 0)
    once = pl.Buffered(1)
    vec = lambda n: pl.BlockSpec((1, n), const)
    return pl.pallas_call(
        _ffn_out_kernel,
        grid=(t // tm,),
        in_specs=[
            pl.BlockSpec((tm, d), row), pl.BlockSpec((tm, dp), row), pl.BlockSpec((tm, dh), row),
            vec(dp), vec(dh),
            pl.BlockSpec((dp + dh, d), const, pipeline_mode=once),
            vec(d), vec(d), vec(d),
            pl.BlockSpec((d, 2 * d_ff), const, pipeline_mode=once),
            pl.BlockSpec((d_ff, d), const, pipeline_mode=once),
        ],
        out_specs=pl.BlockSpec((tm, d), row),
        out_shape=jax.ShapeDtypeStruct((t, d), F32),
        compiler_params=_params(("parallel",), 56),
        name="ffn_out",
    )(x, yp, yh, gp, gh, w_out, g_mix, g_pre, g_post, wgu, wd)


def _with_halo(prev_ref, main_ref, next_ref, i, n_tiles):
    prev = jnp.where(i > 0, prev_ref[0], 0.0)
    nxt = jnp.where(i < n_tiles - 1, next_ref[0], 0.0)
    return jnp.concatenate([prev, main_ref[0], nxt], axis=0)


def _pool_kernel(prev_ref, main_ref, next_ref, wmap_ref, scale_ref, o_ref, *, seq_len, n_tiles):
    i = pl.program_id(1)
    rows = main_ref.shape[1]
    n_ext = rows + 2 * HALO
    ext_all = _with_halo(prev_ref, main_ref, next_ref, i, n_tiles)
    pos = i * rows + lax.broadcasted_iota(jnp.int32, (rows, LANES), 0)
    for grp, window in enumerate(POOL_WINDOWS):
        lanes = slice(grp * LANES, (grp + 1) * LANES)
        ext = ext_all[:, lanes]
        ssum, w = ext + pltpu.roll(ext, 1, 0), 2
        while w < window:
            ssum, w = pltpu.roll(ssum, w // 2, 0) + pltpu.roll(ssum, n_ext - w // 2, 0), 2 * w
        u = main_ref[0, :, lanes]
        lo = jnp.clip(pos - window // 2, 0, seq_len)
        hi = jnp.clip(pos + (window - window // 2), 0, seq_len)
        d = ssum[HALO:HALO + rows] / (hi - lo).astype(F32) - u
        y = _dot(d.astype(BF16), wmap_ref[grp])
        o_ref[0, :, lanes] = y * scale_ref[:, lanes]


def _halo_specs(rows, width, seq_len, col_of):
    blocks_per_tile = rows // HALO
    last = seq_len // HALO - 1
    prev = pl.BlockSpec((1, HALO, width),
                        lambda b, i, c: (b, jnp.maximum(i * blocks_per_tile - 1, 0), col_of(c)))
    main = pl.BlockSpec((1, rows, width), lambda b, i, c: (b, i, col_of(c)))
    nxt = pl.BlockSpec((1, HALO, width),
                       lambda b, i, c: (b, jnp.minimum((i + 1) * blocks_per_tile, last), col_of(c)))
    return [prev, main, nxt]


def _pool(z, wmap, scale, *, rows):
    b, seq_len, _ = z.shape
    assert wmap.shape[0] == len(POOL_WINDOWS) and max(POOL_WINDOWS) <= 2 * HALO
    width = wmap.shape[0] * LANES
    n_tiles = seq_len // rows
    return pl.pallas_call(
        functools.partial(_pool_kernel, seq_len=seq_len, n_tiles=n_tiles),
        grid=(b, n_tiles, 1),
        in_specs=_halo_specs(rows, width, seq_len, lambda c: 0) + [
            pl.BlockSpec(wmap.shape, lambda b, i, c: (0, 0, 0)),
            pl.BlockSpec((1, width), lambda b, i, c: (0, 0)),
        ],
        out_specs=pl.BlockSpec((1, rows, width), lambda b, i, c: (b, i, 0)),
        out_shape=jax.ShapeDtypeStruct((b, seq_len, width), F32),
        compiler_params=_params(("parallel", "parallel", "parallel"), 40),
        name="pool",
    )(z, z, z, wmap, scale)


def _sconv_kernel(prev_ref, main_ref, next_ref, w_ref, b_ref, o_ref, *, n_tiles):
    i = pl.program_id(1)
    rows = main_ref.shape[1]
    ext = _with_halo(prev_ref, main_ref, next_ref, i, n_tiles)
    n_ext = rows + 2 * HALO
    before = pltpu.roll(ext, 1, 0)[HALO:HALO + rows]
    after = pltpu.roll(ext, n_ext - 1, 0)[HALO:HALO + rows]
    w = w_ref[...]
    out = before * w[0:1] + main_ref[0] * w[1:2] + after * w[2:3] + b_ref[...]
    n_chunks, pitch, width = o_ref.shape[2:]
    n2 = rows // n_chunks
    for n in range(n_chunks):
        o_ref[0, 0, n, :n2, :] = out[n * n2:(n + 1) * n2]
    o_ref[0, 0, :, n2:, :] = jnp.zeros((n_chunks, pitch - n2, width), F32)


def _sconv(z, w, bias, *, rows, col0, n_parts, n2):
    b, seq_len, _ = z.shape
    n_tiles = seq_len // rows
    width = w.shape[1] // n_parts
    assert col0 % width == 0
    return pl.pallas_call(
        functools.partial(_sconv_kernel, n_tiles=n_tiles),
        grid=(b, n_tiles, n_parts),
        in_specs=_halo_specs(rows, width, seq_len, lambda c: c + col0 // width) + [
            pl.BlockSpec((w.shape[0], width), lambda b, i, c: (0, c)),
            pl.BlockSpec((1, width), lambda b, i, c: (0, c)),
        ],
        out_specs=pl.BlockSpec((1, 1, rows // n2, n2 + SUBLANES, width), lambda b, i, c: (c, b, i, 0, 0)),
        out_shape=jax.ShapeDtypeStruct((n_parts, b, seq_len // n2, n2 + SUBLANES, width), F32),
        compiler_params=_params(("parallel", "parallel", "parallel"), 40),
        name="sconv",
    )(z, z, z, w, bias)


def _filt_gen_kernel(bands_ref, wt_ref, wc_ref, ws_ref, b1_ref, wh_ref, bh_ref, freq_ref,
                     wl_ref, delta_ref, o_ref, *, seq_len, d_ch):
    rows = o_ref.shape[0] * (o_ref.shape[1] - SUBLANES)
    base = pl.program_id(0) * rows
    m_lane = (base + lax.broadcasted_iota(jnp.int32, (1, rows), 1)).astype(F32)
    t_lane = m_lane / (seq_len - 1.0)
    ang = (bands_ref[...] * (2.0 * math.pi / seq_len)) * m_lane
    freq = freq_ref[...]
    pre = (_dot_split(wc_ref[...], jnp.cos(ang)) + _dot_split(ws_ref[...], -jnp.sin(ang))
           + wt_ref[...] * t_lane + b1_ref[...])
    h = jnp.sin(freq * pre)
    for layer in range(wh_ref.shape[0]):
        pre = _dot_split(wh_ref[layer], h) + bh_ref[layer]
        h = jnp.sin(freq * pre)
    (h_hi, h_lo), (w_hi, w_lo) = _split_bf16(h), _split_bf16(wl_ref[...])
    out = _dot(jnp.concatenate([h_hi, h_hi, h_lo], axis=0).T,
               jnp.concatenate([w_hi, w_lo, w_hi], axis=0))
    m_row = base + lax.broadcasted_iota(jnp.int32, (rows, d_ch), 0)
    t_row = m_row.astype(F32) / (seq_len - 1.0)
    decay = jnp.exp(-t_row * jnp.abs(delta_ref[...]))
    decay_bwd = jnp.where(m_row == 0, 0.0, decay)
    n_chunks, pitch, _ = o_ref.shape
    n2 = rows // n_chunks
    for q in range(out.shape[1] // d_ch):
        dq = decay_bwd if q % 2 == 1 else decay
        val = out[:, q * d_ch:(q + 1) * d_ch] * dq
        for n in range(n_chunks):
            o_ref[n, :n2, q * d_ch:(q + 1) * d_ch] = val[n * n2:(n + 1) * n2]
    o_ref[:, n2:, :] = jnp.zeros((n_chunks, pitch - n2, o_ref.shape[2]), F32)


def _filt_gen(seq_len, w_first, b_first, w_hidden, b_hidden, w_last, freq, *, rows, n2):
    pos_bands = (w_first.shape[0] - 1) // 2
    hidden = w_first.shape[1]
    n_cols = w_last.shape[1]
    d_ch = n_cols // 4
    bands = jnp.linspace(1e-4, pos_bands - 1, pos_bands, dtype=F32).reshape(pos_bands, 1)
    deltas = jnp.linspace(MIN_DECAY, MAX_DECAY, d_ch, dtype=F32).reshape(1, d_ch)
    w1t = w_first.T
    args = (bands, w1t[:, 0:1], w1t[:, 1:1 + pos_bands], w1t[:, 1 + pos_bands:],
            b_first.reshape(hidden, 1), jnp.swapaxes(w_hidden, 1, 2),
            b_hidden.reshape(b_hidden.shape[0], hidden, 1), freq.reshape(hidden, 1),
            w_last, deltas)
    full = lambda a: pl.BlockSpec(a.shape, lambda i, _n=a.ndim: (0,) * _n)
    return pl.pallas_call(
        functools.partial(_filt_gen_kernel, seq_len=seq_len, d_ch=d_ch),
        grid=(seq_len // rows,),
        in_specs=[full(a) for a in args],
        out_specs=pl.BlockSpec((rows // n2, n2 + SUBLANES, n_cols), lambda i: (i, 0, 0)),
        out_shape=jax.ShapeDtypeStruct((seq_len // n2, n2 + SUBLANES, n_cols), F32),
        compiler_params=_params(("parallel",), 40),
        name="filt_gen",
    )(*args)


class _Plan:
    def __init__(self, seq_len, n1, n2):
        assert n1 * n2 == 2 * seq_len and n1 % (2 * SUBLANES) == 0 and n2 % SUBLANES == 0
        self.seq_len, self.n1, self.n2 = seq_len, n1, n2
        n = n1 * n2
        n1h = n1 // 2
        self.kh = kh = n1h + 1
        k1 = np.arange(kh, dtype=np.float64)[:, None]
        m1 = np.arange(n1h, dtype=np.float64)[None, :]
        th1 = 2.0 * np.pi * k1 * m1 / n1
        f1 = np.zeros((n1 + SUBLANES, n1h))
        f1[:n1h] = np.cos(th1[:n1h]); f1[n1h:n1] = -np.sin(th1[:n1h]); f1[n1] = np.cos(th1[n1h])
        self.f1 = jnp.asarray(f1, F32)
        m2 = np.arange(n2, dtype=np.float64)[None, :]
        tht = 2.0 * np.pi * k1 * m2 / n
        twr, twi = np.cos(tht), -np.sin(tht)
        rep = lambda a: np.repeat(a[:, :, None], LANES, axis=2)
        self.twr_inv, self.twi_inv = jnp.asarray(rep(twr), F32), jnp.asarray(rep(twi), F32)
        pad = np.zeros((n2, SUBLANES - 1))
        self.twr_fwd = jnp.asarray(rep(np.concatenate([twr.T, pad], 1)), F32)
        self.twi_fwd = jnp.asarray(rep(np.concatenate([twi.T, pad], 1)), F32)
        k2 = np.arange(n2, dtype=np.float64)[:, None]
        th2 = 2.0 * np.pi * k2 * m2 / n2
        f2r, f2i = np.cos(th2), -np.sin(th2)
        self.f2 = jnp.asarray(np.block([[f2r, -f2i], [f2i, f2r]]), F32)
        self.f2inv = jnp.asarray(np.block([[f2r, f2i], [-f2i, f2r]]), F32)
        wgt = np.where(k1[:n1h] == 0, 1.0, 2.0) / n
        ginv = np.concatenate([(wgt * np.cos(th1[:n1h])).T, (-wgt * np.sin(th1[:n1h])).T], axis=1)
        self.ginv = jnp.asarray(ginv, F32)
        k1f = np.arange(kh, dtype=np.float64)[:, None]
        full = lambda cols: 2.0 * np.pi * k1f * cols[None, :] / n1
        stage1 = lambda th: np.concatenate(
            [np.cos(th[:n1h]), -np.sin(th[:n1h]), np.cos(th[n1h:]), np.zeros((SUBLANES - 1, th.shape[1]))], 0)
        fwd_cols = np.arange(n1h, dtype=np.float64)
        f1a = stage1(full(np.concatenate([fwd_cols, n1 - 1 - fwd_cols])))
        f1b = stage1(full(np.concatenate([fwd_cols, (n1 - fwd_cols) % n1])))
        f1b[:, n1h] = 0.0
        self.f1_filt, self.f1_filt0 = jnp.asarray(f1a, F32), jnp.asarray(f1b, F32)
        rows = lambda t: rep(np.concatenate([t.T, np.zeros((t.shape[1], SUBLANES - 1))], 1))
        coarse, fine = tht[:, ::SUBLANES], tht[:, :SUBLANES]
        self.tw_coarse = (jnp.asarray(rows(np.cos(coarse)), F32), jnp.asarray(rows(-np.sin(coarse)), F32))
        self.tw_fine = (jnp.asarray(rows(np.cos(fine)), F32), jnp.asarray(rows(-np.sin(fine)), F32))
        self.tw_slab_coarse = (jnp.asarray(rep(np.cos(coarse)), F32), jnp.asarray(rep(-np.sin(coarse)), F32))
        self.tw_slab_fine = (jnp.asarray(rep(np.cos(fine)), F32), jnp.asarray(rep(-np.sin(fine)), F32))


def _stack_bf16(re, im):
    return jnp.concatenate([re.astype(BF16), im.astype(BF16)], axis=0)


def _filt_spec_kernel(hf_ref, hba_ref, hbb_ref, f1_ref, f1j0_ref, twcr_ref, twci_ref, twfr_ref, twfi_ref, f2_ref,
                      kr_ref, ki_ref, are_ref, aim_ref, *, n1h, n2, jg):
    pitch = n2 + SUBLANES
    n_j = n2 // jg
    ks = kr_ref.shape[0]
    phase = pl.program_id(1)
    hf, hba, hbb = (r.reshape(n1h * jg, LANES) for r in (hf_ref, hba_ref, hbb_ref))

    @pl.when(phase < n_j)
    def _():
        f1 = f1_ref[...].astype(BF16)
        f1_first = jnp.where(phase == 0, f1j0_ref[...], f1_ref[...]).astype(BF16)
        j0 = phase * jg
        a0 = lax.shift_right_logical(j0, 3)

        def stage1(mat, ts):
            rows = lambda ref, start: ref[pl.ds(start, n1h, stride=jg), :]
            x = _lanes([jnp.concatenate([rows(hf, t), rows(hbb, 0) if t == 0 else rows(hba, jg - t)], axis=0)
                        for t in ts])
            p = _dot(mat, x.astype(BF16))
            cr = _lanes([twcr_ref[a0 + t // SUBLANES] for t in ts])
            ci = _lanes([twci_ref[a0 + t // SUBLANES] for t in ts])
            fr = _lanes([twfr_ref[t % SUBLANES] for t in ts])
            fi = _lanes([twfi_ref[t % SUBLANES] for t in ts])
            twr, twi = cr * fr - ci * fi, cr * fi + ci * fr
            pr, pi, pn = p[:n1h], p[n1h:2 * n1h], p[2 * n1h:2 * n1h + 1]
            re, im = pr * twr[:n1h] - pi * twi[:n1h], pr * twi[:n1h] + pi * twr[:n1h]
            re_n, im_n = pn * twr[n1h:n1h + 1], pn * twi[n1h:n1h + 1]
            for i, t in enumerate(ts):
                are_ref[pl.ds(j0 + t, n1h, stride=pitch), :] = _lane_block(re, i)
                aim_ref[pl.ds(j0 + t, n1h, stride=pitch), :] = _lane_block(im, i)
                are_ref[pl.ds(n1h * pitch + j0 + t, 1), :] = _lane_block(re_n, i)
                aim_ref[pl.ds(n1h * pitch + j0 + t, 1), :] = _lane_block(im_n, i)

        stage1(f1_first, (0,))
        for t in range(1, jg - 1, 2):
            stage1(f1, (t, t + 1))
        stage1(f1, (jg - 1,))

    @pl.when(phase >= n_j)
    def _():
        f2 = f2_ref[...].astype(BF16)
        kc = phase - n_j

        def slabs(first, count):
            rows = [pl.ds(pl.multiple_of((kc * ks + first + i) * pitch, SUBLANES), n2) for i in range(count)]
            x = _dot(f2, _stack_bf16(_lanes([are_ref[r, :] for r in rows]), _lanes([aim_ref[r, :] for r in rows])))
            for i in range(count):
                kr_ref[first + i] = _lane_block(x[:n2], i).astype(kr_ref.dtype)
                ki_ref[first + i] = _lane_block(x[n2:], i).astype(ki_ref.dtype)

        def pair(p, carry):
            slabs(2 * p, 2)
            return carry

        lax.fori_loop(0, ks // 2, pair, 0, unroll=8)
        if ks % 2:
            slabs(ks - 1, 1)


def _filt_spec(plan, filt, n_orders, *, ks):
    n1h, n2, kh = plan.n1 // 2, plan.n2, plan.kh
    cols = filt.shape[2]
    n_cb = cols // (2 * n_orders * LANES)
    jg = _block_cols(n2, n1h)
    n_j = n2 // jg
    pitch = n2 + SUBLANES
    col = lambda c, back: (c // n_cb) * 2 * n_cb + back * n_cb + c % n_cb
    step = lambda s: jnp.minimum(s, n_j - 1)
    seq = lambda back, blk: pl.BlockSpec((n1h, jg, LANES), lambda c, s: (0, blk(step(s)), col(c, back)))
    full = lambda a: pl.BlockSpec(a.shape, lambda c, s, _n=a.ndim: (0,) * _n, pipeline_mode=pl.Buffered(1))
    tables = (plan.f1_filt, plan.f1_filt0, *plan.tw_coarse, *plan.tw_fine, plan.f2)
    out = jax.ShapeDtypeStruct((kh, n2, cols // 2), BF16)
    scratch = pltpu.VMEM((kh * pitch, LANES), F32)
    return pl.pallas_call(
        functools.partial(_filt_spec_kernel, n1h=n1h, n2=n2, jg=jg),
        grid=(n_orders * n_cb, n_j + kh // ks),
        in_specs=[seq(0, lambda s: s), seq(1, lambda s: n_j - 1 - s), seq(1, lambda s: (n_j - s) % n_j)]
        + [full(a) for a in tables],
        out_specs=[pl.BlockSpec((ks, n2, LANES), lambda c, s: (jnp.maximum(s - n_j, 0), 0, c))] * 2,
        out_shape=[out, out],
        scratch_shapes=[scratch, scratch],
        compiler_params=_params(("parallel", "arbitrary"), 48),
        name="filt_spec",
    )(filt, filt, filt, *tables)


def _conv_front_kernel(x_ref, kr_ref, ki_ref, f1_ref, twcr_ref, twci_ref, twfr_ref, twfi_ref,
                       icr_ref, ici_ref, ifr_ref, ifi_ref, f2_ref, f2inv_ref,
                       ore_ref, oim_ref, are_ref, aim_ref, *, n1h, n2, unroll):
    pitch = n2 + SUBLANES
    ks = kr_ref.shape[0]
    kc = pl.program_id(2)
    x2 = x_ref.at[0, 0].reshape(n1h * pitch, LANES)

    @pl.when(kc == 0)
    def _():
        f1 = f1_ref[...].astype(BF16)

        def body(jp, carry):
            js = (2 * jp, 2 * jp + 1)
            a = lax.shift_right_logical(jp, 2)
            bs = [2 * jnp.bitwise_and(jp, SUBLANES // 2 - 1) + q for q in range(2)]
            xb = _lanes([x2[pl.ds(j, n1h, stride=pitch), :] for j in js]).astype(BF16)
            p = _dot(f1, xb)
            cr, ci = jnp.tile(twcr_ref[a], (1, 2)), jnp.tile(twci_ref[a], (1, 2))
            fr, fi = _lanes([twfr_ref[b] for b in bs]), _lanes([twfi_ref[b] for b in bs])
            twr, twi = cr * fr - ci * fi, cr * fi + ci * fr
            pr, pi, pn = p[:n1h], p[n1h:2 * n1h], p[2 * n1h:2 * n1h + 1]
            re, im = pr * twr[:n1h] - pi * twi[:n1h], pr * twi[:n1h] + pi * twr[:n1h]
            re_n, im_n = pn * twr[n1h:n1h + 1], pn * twi[n1h:n1h + 1]
            for i, j in enumerate(js):
                are_ref[pl.ds(j, n1h, stride=pitch), :] = _lane_block(re, i)
                aim_ref[pl.ds(j, n1h, stride=pitch), :] = _lane_block(im, i)
                are_ref[pl.ds(n1h * pitch + j, 1), :] = _lane_block(re_n, i)
                aim_ref[pl.ds(n1h * pitch + j, 1), :] = _lane_block(im_n, i)
            return carry

        lax.fori_loop(0, n2 // 2, body, 0, unroll=unroll)

    f2, f2inv = f2_ref[...].astype(BF16), f2inv_ref[...].astype(BF16)

    def slab_twiddle(k):
        cr, ci, fr, fi = icr_ref[k], ici_ref[k], ifr_ref[k], ifi_ref[k]
        pieces = [(cr[a:a + 1] * fr - ci[a:a + 1] * fi, cr[a:a + 1] * fi + ci[a:a + 1] * fr)
                  for a in range(n2 // SUBLANES)]
        return (jnp.concatenate([p[0] for p in pieces], axis=0), jnp.concatenate([p[1] for p in pieces], axis=0))

    def slabs(first, count):
        ks_abs = [kc * ks + first + i for i in range(count)]
        rows = [pl.ds(pl.multiple_of(k * pitch, SUBLANES), n2) for k in ks_abs]
        x = _dot(f2, _stack_bf16(_lanes([are_ref[r, :] for r in rows]), _lanes([aim_ref[r, :] for r in rows])))
        xr, xi = x[:n2], x[n2:]
        kr = _lanes([kr_ref[first + i] for i in range(count)]).astype(F32)
        ki = _lanes([ki_ref[first + i] for i in range(count)]).astype(F32)
        y = _dot(f2inv, _stack_bf16(xr * kr - xi * ki, xr * ki + xi * kr))
        br, bi = y[:n2], y[n2:]
        tw = [slab_twiddle(k) for k in ks_abs]
        twr, twi = _lanes([t[0] for t in tw]), _lanes([t[1] for t in tw])
        ore, oim = br * twr + bi * twi, bi * twr - br * twi
        for i in range(count):
            ore_ref[0, first + i] = _lane_block(ore, i)
            oim_ref[0, first + i] = _lane_block(oim, i)

    def pair(p, carry):
        slabs(2 * p, 2)
        return carry

    lax.fori_loop(0, ks // 2, pair, 0, unroll=8)
    if ks % 2:
        slabs(ks - 1, 1)


def _conv_front(plan, x4, part, kr, ki, order, *, ks):
    b, ch = x4.shape[1], x4.shape[-1]
    n1h, n2, kh = plan.n1 // 2, plan.n2, plan.kh
    n_cb = ch // LANES
    pitch = n2 + SUBLANES
    once = pl.Buffered(1)
    full = lambda a: pl.BlockSpec(a.shape, lambda bb, c, k, _n=a.ndim: (0,) * _n, pipeline_mode=once)
    filt = pl.BlockSpec((ks, n2, LANES), lambda bb, c, k: (k, 0, order * n_cb + c))
    data = pl.BlockSpec((1, ks, n2, LANES), lambda bb, c, k: (bb, k, 0, c))
    tables = (plan.f1, *plan.tw_coarse, *plan.tw_fine, *plan.tw_slab_coarse, *plan.tw_slab_fine,
              plan.f2, plan.f2inv)
    out = jax.ShapeDtypeStruct((b, kh, n2, ch), F32)
    scratch = pltpu.VMEM((kh * pitch, LANES), F32)
    return pl.pallas_call(
        functools.partial(_conv_front_kernel, n1h=n1h, n2=n2, unroll=4),
        grid=(b, n_cb, kh // ks),
        in_specs=[pl.BlockSpec((1, 1, n1h, pitch, LANES), lambda bb, c, k: (part, bb, 0, 0, c)),
                  filt, filt] + [full(a) for a in tables],
        out_specs=[data, data],
        out_shape=[out, out],
        scratch_shapes=[scratch, scratch],
        compiler_params=_params(("parallel", "parallel", "arbitrary"), 60),
        name="conv_front",
    )(x4, kr, ki, *tables)


def _dft_inv_kernel(bre_ref, bim_ref, x_ref, gate_ref, skip_ref, ginv_ref, o_ref, *, unroll, scale, n_j):
    _, _, n1h, g, _ = o_ref.shape
    b_re = bre_ref.at[0].reshape((n1h + 1) * g, LANES)
    b_im = bim_ref.at[0].reshape((n1h + 1) * g, LANES)
    x2 = x_ref.at[0, 0].reshape(n1h * g, LANES)
    gate2 = gate_ref.at[0, 0].reshape(n1h * g, LANES)
    o2 = o_ref.at[0, 0].reshape(n1h * g, LANES)
    ginv = ginv_ref[...].astype(BF16)
    skip = skip_ref[0]
    n1 = lax.broadcasted_iota(jnp.int32, (n1h, 2 * LANES), 0)
    nyq_w = jnp.where(jnp.bitwise_and(n1, 1) == 0, scale, -scale)
    skip = jnp.tile(skip, (1, 2))

    def body(jp, carry):
        rows = [pl.ds(2 * jp + i, n1h, stride=g) for i in range(2)]
        spec = _stack_bf16(_lanes([b_re[r, :] for r in rows]), _lanes([b_im[r, :] for r in rows]))
        nyq = _lanes([b_re[pl.ds(n1h * g + 2 * jp + i, 1), :] for i in range(2)])
        y = _dot(ginv, spec) + nyq_w * nyq
        out = _lanes([gate2[r, :] for r in rows]) * (y + _lanes([x2[r, :] for r in rows]) * skip)
        for i, r in enumerate(rows):
            o2[r, :] = _lane_block(out, i)
        return carry

    @pl.when(pl.program_id(2) < n_j)
    def _():
        lax.fori_loop(0, g // 2, body, 0, unroll=unroll)

    @pl.when(pl.program_id(2) >= n_j)
    def _():
        o_ref[...] = jnp.zeros(o_ref.shape, o_ref.dtype)


def _dft_inv(plan, bre, bim, x4, x_part, gate4, gate_part, skip, order, *, g, pitched_out):
    b, ch = x4.shape[1], x4.shape[-1]
    n1h, n2, kh = plan.n1 // 2, plan.n2, plan.kh
    n_j = n2 // g
    rows_out = n2 + SUBLANES if pitched_out else n2
    blk = lambda j: jnp.minimum(j, n_j - 1)
    spec = pl.BlockSpec((1, kh, g, LANES), lambda bb, c, j: (bb, 0, blk(j), c))
    time = lambda part: pl.BlockSpec((1, 1, n1h, g, LANES), lambda bb, c, j: (part, bb, 0, blk(j), c))
    return pl.pallas_call(
        functools.partial(_dft_inv_kernel, unroll=min(g // 2, 8), scale=1.0 / (plan.n1 * n2), n_j=n_j),
        grid=(b, ch // LANES, n_j + (1 if pitched_out else 0)),
        in_specs=[
            spec, spec, time(x_part), time(gate_part),
            pl.BlockSpec((1, 1, LANES), lambda bb, c, j: (order, 0, c)),
            pl.BlockSpec(plan.ginv.shape, lambda bb, c, j: (0, 0)),
        ],
        out_specs=pl.BlockSpec((1, 1, n1h, g, LANES), lambda bb, c, j: (0, bb, 0, j, c)),
        out_shape=jax.ShapeDtypeStruct((1, b, n1h, rows_out, ch), F32),
        compiler_params=_params(("parallel", "parallel", "parallel"), 48),
        name="dft_inv",
    )(bre, bim, x4, gate4, skip, plan.ginv)


def _hyena_fused_kernel(zv_ref, zx1_ref, zx2_ref, wv_ref, wx1_ref, wx2_ref, bv_ref, bx1_ref, bx2_ref,
                        skip_ref, kr0_ref, ki0_ref, kr1_ref, ki1_ref,
                        f1_ref, twrf_ref, twif_ref, twri_ref, twii_ref, f2_ref, f2inv_ref, ginv_ref,
                        o_ref, are_ref, aim_ref, cur_ref, tin_ref, *, n1h, n2, unroll):
    nb = o_ref.shape[0]
    pitch = n2 + SUBLANES
    f1 = f1_ref[...].astype(BF16)
    f2, f2inv = f2_ref[...].astype(BF16), f2inv_ref[...].astype(BF16)
    ginv = ginv_ref[...].astype(BF16)
    scale = 1.0 / (2 * n1h * n2)
    n1 = lax.broadcasted_iota(jnp.int32, (n1h, 2 * nb * LANES), 0)
    nyq_w = jnp.where(jnp.bitwise_and(n1, 1) == 0, scale, -scale)
    lanes, lane_block = _lanes, _lane_block
    wide = lambda t, reps: t if reps == 1 else jnp.tile(t, (1, reps))

    row = lax.broadcasted_iota(jnp.int32, (n2, LANES), 0)
    for part, (z_ref, w_ref, b_ref) in enumerate(((zv_ref, wv_ref, bv_ref), (zx1_ref, wx1_ref, bx1_ref),
                                                   (zx2_ref, wx2_ref, bx2_ref))):
        w, bias = w_ref[...], b_ref[...]
        for i in range(nb):
            for n in range(n1h):
                lo = n * n2
                mid = z_ref[i, lo:lo + n2, :]
                if n > 0:
                    before = z_ref[i, lo - 1:lo + n2 - 1, :]
                else:
                    before = jnp.where(row == 0, 0.0, pltpu.roll(mid, 1, 0))
                if n < n1h - 1:
                    after = z_ref[i, lo + 1:lo + n2 + 1, :]
                else:
                    after = jnp.where(row == n2 - 1, 0.0, pltpu.roll(mid, n2 - 1, 0))
                tin_ref[part, i, n * pitch:n * pitch + n2, :] = (
                    before * w[0:1] + mid * w[1:2] + after * w[2:3] + bias)
    stages = ((tin_ref.at[0], tin_ref.at[1], cur_ref, kr0_ref, ki0_ref),
              (cur_ref, tin_ref.at[2], tin_ref.at[0], kr1_ref, ki1_ref))
    for order, (src, gate, dst, kr_ref, ki_ref) in enumerate(stages):
        skip = wide(skip_ref[order], 2 * nb)

        def stage1(jp, carry):
            cols = [(2 * jp + q, i) for q in range(2) for i in range(nb)]
            xb = lanes([src[i, pl.ds(j, n1h, stride=pitch), :] for j, i in cols]).astype(BF16)
            p = _dot(f1, xb)
            twr = lanes([wide(twrf_ref[2 * jp + q], nb) for q in range(2)])
            twi = lanes([wide(twif_ref[2 * jp + q], nb) for q in range(2)])
            pr, pi, pn = p[:n1h], p[n1h:2 * n1h], p[2 * n1h:2 * n1h + 1]
            re, im = pr * twr[:n1h] - pi * twi[:n1h], pr * twi[:n1h] + pi * twr[:n1h]
            re_n, im_n = pn * twr[n1h:n1h + 1], pn * twi[n1h:n1h + 1]
            for c, (j, i) in enumerate(cols):
                are_ref[i, pl.ds(j, n1h, stride=pitch), :] = lane_block(re, c)
                aim_ref[i, pl.ds(j, n1h, stride=pitch), :] = lane_block(im, c)
                are_ref[i, pl.ds(n1h * pitch + j, 1), :] = lane_block(re_n, c)
                aim_ref[i, pl.ds(n1h * pitch + j, 1), :] = lane_block(im_n, c)
            return carry

        lax.fori_loop(0, n2 // 2, stage1, 0, unroll=unroll)

        def slabs(ks):
            rows = [pl.ds(pl.multiple_of(k * pitch, SUBLANES), n2) for k in ks]
            a = _stack_bf16(lanes([are_ref[i, r, :] for r in rows for i in range(nb)]),
                            lanes([aim_ref[i, r, :] for r in rows for i in range(nb)]))
            x = _dot(f2, a)
            xr, xi = x[:n2], x[n2:]
            per_slab = lambda ref: lanes([wide(ref[k], nb) for k in ks])
            kr, ki = per_slab(kr_ref).astype(F32), per_slab(ki_ref).astype(F32)
            y = _dot(f2inv, _stack_bf16(xr * kr - xi * ki, xr * ki + xi * kr))
            br, bi = y[:n2], y[n2:]
            twr, twi = per_slab(twri_ref), per_slab(twii_ref)
            ore, oim = br * twr + bi * twi, bi * twr - br * twi
            for s, r in enumerate(rows):
                for i in range(nb):
                    are_ref[i, r, :] = lane_block(ore, s * nb + i)
                    aim_ref[i, r, :] = lane_block(oim, s * nb + i)

        def slab(k, carry):
            slabs((k,))
            return carry

        lax.fori_loop(0, n1h + 1, slab, 0, unroll=True)

        def stage1_inv(jp, carry):
            cols = [(pl.ds(2 * jp + q, n1h, stride=pitch), 2 * jp + q, i) for q in range(2) for i in range(nb)]
            spec = _stack_bf16(lanes([are_ref[i, r, :] for r, _, i in cols]),
                               lanes([aim_ref[i, r, :] for r, _, i in cols]))
            nyq = lanes([are_ref[i, pl.ds(n1h * pitch + j, 1), :] for _, j, i in cols])
            y = _dot(ginv, spec) + nyq_w * nyq
            x = lanes([src[i, r, :] for r, _, i in cols])
            g = lanes([gate[i, r, :] for r, _, i in cols])
            out = g * (y + x * skip)
            for c, (r, _, i) in enumerate(cols):
                dst[i, r, :] = lane_block(out, c)
            return carry

        lax.fori_loop(0, n2 // 2, stage1_inv, 0, unroll=unroll)

    def unpitch(n, carry):
        for i in range(nb):
            o_ref[i, pl.ds(pl.multiple_of(n * n2, n2), n2), :] = (
                tin_ref[0, i, pl.ds(pl.multiple_of(n * pitch, SUBLANES), n2), :])
        return carry

    lax.fori_loop(0, n1h, unpitch, 0)


def _hyena_fused(plan, z, col0, conv_w, conv_b, kr, ki, skip, *, nb):
    b, seq_len, _ = z.shape
    ch = skip.shape[1]
    n1h, n2, kh = plan.n1 // 2, plan.n2, plan.kh
    n_cb = ch // LANES
    cb0 = col0 // LANES
    part = lambda p: pl.BlockSpec((nb, seq_len, LANES), lambda c, bb: (bb, 0, cb0 + p * n_cb + c))
    taps = lambda p: pl.BlockSpec((conv_w.shape[0], LANES), lambda c, bb: (0, p * n_cb + c))
    bias = lambda p: pl.BlockSpec((1, LANES), lambda c, bb: (0, p * n_cb + c))
    once = pl.Buffered(1)
    filt = lambda o: pl.BlockSpec((kh, n2, LANES), lambda c, bb: (0, 0, o * n_cb + c), pipeline_mode=once)
    full = lambda a: pl.BlockSpec(a.shape, lambda c, bb, _n=a.ndim: (0,) * _n, pipeline_mode=once)
    tables = (plan.f1, plan.twr_fwd, plan.twi_fwd, plan.twr_inv, plan.twi_inv, plan.f2, plan.f2inv, plan.ginv)
    pitch = n2 + SUBLANES
    spec_scratch = pltpu.VMEM((nb, kh * pitch, LANES), F32)
    return pl.pallas_call(
        functools.partial(_hyena_fused_kernel, n1h=n1h, n2=n2, unroll=min(n2 // 2, 32)),
        grid=(n_cb, b // nb),
        in_specs=[part(0), part(1), part(2), taps(0), taps(1), taps(2), bias(0), bias(1), bias(2),
                  pl.BlockSpec((skip.shape[0], 1, LANES), lambda c, bb: (0, 0, c)),
                  filt(0), filt(0), filt(1), filt(1)] + [full(a) for a in tables],
        out_specs=pl.BlockSpec((nb, seq_len, LANES), lambda c, bb: (bb, 0, c)),
        out_shape=jax.ShapeDtypeStruct((b, seq_len, ch), F32),
        scratch_shapes=[spec_scratch, spec_scratch, pltpu.VMEM((nb, n1h * pitch, LANES), F32),
                        pltpu.VMEM((3, nb, n1h * pitch, LANES), F32)],
        compiler_params=_params(("parallel", "arbitrary"), 56),
        name="hyena_fused",
    )(z, z, z, conv_w, conv_w, conv_w, conv_b, conv_b, conv_b,
      skip.reshape(skip.shape[0], 1, ch), kr, ki, kr, ki, *tables)


def _dft_shape(seq_len):
    n2 = 128
    return 2 * seq_len // n2, n2


def _block_cols(n2, rows, target_bytes=1 << 20):
    g = max(SUBLANES, min(n2, target_bytes // (rows * LANES * 4)) // SUBLANES * SUBLANES)
    while n2 % g:
        g -= SUBLANES
    return g


def _slabs_per_step(kh, slab_bytes, target_bytes=3 << 20):
    ks = max(1, min(kh, target_bytes // slab_bytes))
    while kh % ks:
        ks -= 1
    return ks


def _hyena(plan, z, col0, conv_w, conv_b, filt, skip):
    b, seq_len, _ = z.shape
    n_orders, ch = skip.shape
    g = _block_cols(plan.n2, plan.n1 // 2)
    ks_filt = _slabs_per_step(plan.kh, 2 * plan.n2 * LANES * 4, target_bytes=6 << 20)
    kr, ki = _filt_spec(plan, filt, n_orders, ks=ks_filt)
    nb = 2 if b % 2 == 0 else 1
    fused_bytes = 15 * nb * seq_len * LANES * 4
    if n_orders == 2 and fused_bytes <= FUSED_VMEM_BUDGET:
        return _hyena_fused(plan, z, col0, conv_w, conv_b, kr, ki, skip, nb=nb)
    uc = _sconv(z, conv_w, conv_b, rows=min(seq_len, 2048), col0=col0, n_parts=n_orders + 1, n2=plan.n2)
    skip3 = skip.reshape(n_orders, 1, ch)
    cur, cur_part = uc, 0
    for order in range(n_orders):
        bre, bim = _conv_front(plan, cur, cur_part, kr, ki, order, ks=ks_filt)
        cur = _dft_inv(plan, bre, bim, cur, cur_part, uc, order + 1, skip3, order, g=g,
                       pitched_out=order < n_orders - 1)
        cur_part = 0
    return cur.reshape(b, seq_len, ch)


def _trunk(x3, p, plan, filt):
    b, seq_len, d = x3.shape
    t = b * seq_len
    x = x3.reshape(t, d)
    x, z = _ffn_in(x, p["g1pre"], p["g1post"], p["w1gu"], p["w1d"], p["gmix"], p["w_in"], tm=512)
    z = z.reshape(b, seq_len, -1)
    d_pool = p["pool_w"].shape[0] * LANES
    y_pool = _pool(z, p["pool_w"], p["pool_scale"], rows=min(seq_len, 2048))
    y_hy = _hyena(plan, z, d_pool, p["conv_w"], p["conv_b"], filt, p["skip"])
    x = _ffn_out(x, y_pool.reshape(t, -1), y_hy.reshape(t, -1), p["gpool"], p["ghy"], p["w_out"],
                 p["gmixpost"], p["g2pre"], p["g2post"], p["w2gu"], p["w2d"], tm=512)
    return x.reshape(b, seq_len, d)


def kernel(x_prompt, x_sample, ffn1_norm_pre, ffn1_norm_post, ffn1_w_gate_up, ffn1_w_down, mix_norm_pre, w_in, pool_w_map, pool_scale, hyena_conv_w, hyena_conv_b, filt_w_first, filt_b_first, filt_w_hidden, filt_b_hidden, filt_w_last, filt_freq, hyena_skip, pool_out_norm, hyena_out_norm, w_out, mix_norm_post, ffn2_norm_pre, ffn2_norm_post, ffn2_w_gate_up, ffn2_w_down):
    assert ffn1_norm_pre.shape[0] == 1, "single-layer trunk"
    row = lambda a: a[0].reshape(1, -1)
    p = dict(
        g1pre=row(ffn1_norm_pre), g1post=row(ffn1_norm_post),
        w1gu=ffn1_w_gate_up[0].astype(BF16), w1d=ffn1_w_down[0].astype(BF16),
        gmix=row(mix_norm_pre), w_in=w_in[0].astype(BF16),
        pool_w=pool_w_map[0].astype(BF16), pool_scale=row(pool_scale),
        conv_w=hyena_conv_w[0], conv_b=row(hyena_conv_b),
        skip=hyena_skip[0], gpool=row(pool_out_norm), ghy=row(hyena_out_norm),
        w_out=w_out[0].astype(BF16), gmixpost=row(mix_norm_post),
        g2pre=row(ffn2_norm_pre), g2post=row(ffn2_norm_post),
        w2gu=ffn2_w_gate_up[0].astype(BF16), w2d=ffn2_w_down[0].astype(BF16),
    )
    outs = []
    for x3 in (x_prompt, x_sample):
        seq_len = x3.shape[1]
        plan = _Plan(seq_len, *_dft_shape(seq_len))
        filt = _filt_gen(seq_len, filt_w_first[0], filt_b_first[0], filt_w_hidden[0],
                         filt_b_hidden[0], filt_w_last[0], filt_freq[0], rows=min(seq_len, 512),
                         n2=plan.n2)
        outs.append(_trunk(x3, p, plan, filt))
    return tuple(outs)
```
